```python
import math
import numpy as np
import jax
import jax.numpy as jnp
from jax import lax

D_MODEL = 1024
BATCH = 8
SEQ = 4096
DEPTH = 1
DEC_BATCH = 32
DEC_SEQ = 4
PAST_LEN = 16384
PAGE_SIZE = 128

ATT_GROUPS = ((128, 1), (512, 4), (2048, 16))
N_GROUPS = 3
H_G = 4
HD = 64
STEPS = 128
D_ATT = N_GROUPS * H_G * HD
H_D = 4
DK = 128
DV = 128
D_DK = H_D * DK
D_DV = H_D * DV
D_CONV = 2 * D_DK + D_DV
CONV_W = 4
CHUNK = 64
N_EXP = 32
TOP_K = 4
D_FF = D_MODEL
SWIGLU_LIMIT = 7.0
SWIGLU_ALPHA = 1.702
MOE_BLK = 128
DN_ALPHA = (2.0 * DEPTH) ** 0.25
DN_BETA = (8.0 * DEPTH) ** -0.25
LN_EPS = 1e-5
NORM_EPS = 1e-6
IN_SIZES = (D_ATT, D_ATT, D_ATT, D_DK, D_DK, D_DV, D_DV, H_D, H_D, D_MODEL, D_MODEL)
D_IN = sum(IN_SIZES)

kernel_name = 'hybrid_dilated_attn_gated_deltanet_moe_step'

F32 = jnp.float32


def layer_norm(x, g, b):
    xf = x.astype(F32)
    mu = xf.mean(-1, keepdims=True)
    var = jnp.square(xf - mu).mean(-1, keepdims=True)
    return ((xf - mu) * lax.rsqrt(var + LN_EPS) * g.astype(F32) + b.astype(F32)).astype(x.dtype)


def alibi_slopes():
    return 2.0 ** (-8.0 * jnp.arange(1, H_G + 1, dtype=F32) / H_G)


def dilated_prompt(q, k, v, dil, slopes):
    b, t, h, e = q.shape
    span = dil * STEPS
    tp = -(-t // span) * span
    L = tp // dil
    nb = L // STEPS

    def to_res(a):
        a = jnp.pad(a, ((0, 0), (0, tp - t), (0, 0), (0, 0)))
        return a.reshape(b, L, dil, h, e).transpose(0, 2, 1, 3, 4).reshape(b, dil, nb, STEPS, h, e)

    def band(a):
        prev = jnp.pad(a[:, :, :-1], ((0, 0), (0, 0), (1, 0), (0, 0), (0, 0), (0, 0)))
        return jnp.concatenate([prev, a], axis=3)

    qb = to_res(q)
    kk = band(to_res(k))
    vv = band(to_res(v))
    scores = jnp.einsum('brnqhe,brnkhe->brnhqk', qb, kk, preferred_element_type=F32) * (HD ** -0.5)
    steps = jnp.arange(STEPS)[:, None] + STEPS - jnp.arange(2 * STEPS)[None, :]
    first = (jnp.arange(nb)[:, None, None] == 0) & (jnp.arange(2 * STEPS)[None, None, :] < STEPS)
    valid = (steps >= 0) & (steps <= STEPS) & ~first
    bias = -slopes[:, None, None] * (dil * steps).astype(F32)
    scores = jnp.where(valid[None, None, :, None], scores + bias, -jnp.inf)
    mx = scores.max(-1, keepdims=True)
    p = jnp.exp(scores - mx)
    den = p.sum(-1, keepdims=True)
    out = jnp.einsum('brnhqk,brnkhe->brnqhe', p, vv.astype(F32)) / den.transpose(0, 1, 2, 4, 3, 5)
    lse = (mx + jnp.log(den))[..., 0].transpose(0, 1, 2, 4, 3)
    out = out.reshape(b, dil, L, h, e).transpose(0, 2, 1, 3, 4).reshape(b, tp, h, e)[:, :t]
    lse = lse.reshape(b, dil, L, h).transpose(0, 2, 1, 3).reshape(b, tp, h)[:, :t]
    return out, lse


def dilated_sample(q, k, v, buf, dil, slopes):
    b, s, h, e = q.shape
    lw = buf.shape[1]
    kk = jnp.concatenate([buf[:, :, 0], k], axis=1)
    vv = jnp.concatenate([buf[:, :, 1], v], axis=1)
    steps = jnp.arange(STEPS + 1)
    idx = lw + jnp.arange(s)[:, None] - dil * steps[None, :]
    valid = idx >= 0
    idx = jnp.maximum(idx, 0)
    kg = kk[:, idx]
    vg = vv[:, idx]
    scores = jnp.einsum('bshe,bsmhe->bhsm', q, kg, preferred_element_type=F32) * (HD ** -0.5)
    scores = scores - slopes[:, None, None] * (dil * steps).astype(F32)
    scores = jnp.where(valid, scores, -jnp.inf)
    mx = scores.max(-1, keepdims=True)
    p = jnp.exp(scores - mx)
    den = p.sum(-1, keepdims=True)
    out = jnp.einsum('bhsm,bsmhe->bshe', p, vg.astype(F32)) / den.transpose(0, 2, 1, 3)
    lse = (mx + jnp.log(den))[..., 0].transpose(0, 2, 1)
    return out, lse


def short_conv(u, buf, w):
    t = u.shape[1]
    ext = jnp.concatenate([buf.astype(u.dtype), u], axis=1)
    out = ext[:, 0:t] * w[0]
    for i in range(1, CONV_W):
        out = out + ext[:, i:i + t] * w[i]
    return jax.nn.silu(out), ext[:, -(CONV_W - 1):]


def gated_delta_chunked(q, k, v, g, beta, s0):
    b, t, h, _ = q.shape
    tp = -(-t // CHUNK) * CHUNK
    nc = tp // CHUNK

    def blk(a):
        a = jnp.pad(a, [(0, 0), (0, tp - t)] + [(0, 0)] * (a.ndim - 2))
        return jnp.moveaxis(a.reshape((b, nc, CHUNK) + a.shape[2:]), 3, 1)

    q, k, v, g, beta = blk(q), blk(k), blk(v), blk(g), blk(beta)
    gc = jnp.cumsum(g, axis=-1)
    ii = jnp.arange(CHUNK)
    tril = ii[:, None] >= ii[None, :]
    decay = jnp.where(tril, jnp.exp(jnp.where(tril, gc[..., :, None] - gc[..., None, :], 0.0)), 0.0)
    kb = k * beta[..., None]
    lmat = jnp.where(ii[:, None] > ii[None, :], jnp.einsum('bhnie,bhnje->bhnij', kb, k) * decay, 0.0)
    eye = jnp.eye(CHUNK, dtype=F32)
    tinv = lax.linalg.triangular_solve(lmat + eye, jnp.broadcast_to(eye, lmat.shape),
                                       left_side=True, lower=True, unit_diagonal=True)
    u = tinv @ (v * beta[..., None])
    w = tinv @ (kb * jnp.exp(gc)[..., None])
    aqk = jnp.einsum('bhnie,bhnje->bhnij', q, k) * decay
    qd = q * jnp.exp(gc)[..., None]
    glast = gc[..., -1]
    kd = k * jnp.exp(glast[..., None] - gc)[..., None]
    xs = tuple(jnp.moveaxis(a, 2, 0) for a in (u, w, aqk, qd, kd, glast))

    def step(s, inp):
        u_c, w_c, a_c, qd_c, kd_c, gl_c = inp
        vn = u_c - w_c @ s
        o = qd_c @ s + a_c @ vn
        s = s * jnp.exp(gl_c)[..., None, None] + jnp.einsum('bhce,bhcv->bhev', kd_c, vn)
        return s, o

    s, o = lax.scan(step, s0, xs)
    o = jnp.moveaxis(jnp.moveaxis(o, 0, 2), 1, 3).reshape(b, tp, h, DV)[:, :t]
    return o, s


def token_mixers(x, kv_bufs, conv_buf, s0, w_in, b_gate, conv_w, a_log, dt_bias, delta_norm_w,
                 w_branch_attn, w_branch_delta, w_out):
    b, t, _ = x.shape
    dt_ = x.dtype
    proj = jnp.einsum('btd,dc->btc', x, w_in)
    cuts = np.cumsum(IN_SIZES)[:-1].tolist()
    aq, ak, av, dq, dk, dv, dz, beta_l, a_l, ga_l, gd_l = jnp.split(proj, cuts, axis=-1)

    slopes = alibi_slopes()
    aq = aq.reshape(b, t, N_GROUPS, H_G, HD)
    ak = ak.reshape(b, t, N_GROUPS, H_G, HD)
    av = av.reshape(b, t, N_GROUPS, H_G, HD)
    outs, lses, kv_new = [], [], []
    for gi, (win, dil) in enumerate(ATT_GROUPS):
        qg, kg, vg = aq[:, :, gi], ak[:, :, gi], av[:, :, gi]
        if kv_bufs is None:
            o, l = dilated_prompt(qg, kg, vg, dil, slopes)
            kv_new.append(jnp.stack([kg[:, -win:], vg[:, -win:]], axis=2))
        else:
            o, l = dilated_sample(qg, kg, vg, kv_bufs[gi], dil, slopes)
            kv_new.append(jnp.stack([kg, vg], axis=2))
        outs.append(o)
        lses.append(l)
    wts = jax.nn.softmax(jnp.stack(lses), axis=0)
    o_att = jnp.einsum('gbth,gbthe->bthe', wts, jnp.stack(outs)).reshape(b, t, H_G * HD).astype(dt_)

    if conv_buf is None:
        conv_buf = jnp.zeros((b, CONV_W - 1, D_CONV), dt_)
        s0 = jnp.zeros((b, H_D, DK, DV), F32)
    cqkv, conv_new = short_conv(jnp.concatenate([dq, dk, dv], axis=-1), conv_buf, conv_w)
    cq, ck, cv = jnp.split(cqkv.astype(F32), [D_DK, 2 * D_DK], axis=-1)
    cq = cq.reshape(b, t, H_D, DK)
    ck = ck.reshape(b, t, H_D, DK)
    cv = cv.reshape(b, t, H_D, DV)
    cq = cq * lax.rsqrt(jnp.sum(cq * cq, -1, keepdims=True) + NORM_EPS) * (DK ** -0.5)
    ck = ck * lax.rsqrt(jnp.sum(ck * ck, -1, keepdims=True) + NORM_EPS)
    beta = jax.nn.sigmoid(beta_l.astype(F32))
    g = -jnp.exp(a_log.astype(F32)) * jax.nn.softplus(a_l.astype(F32) + dt_bias.astype(F32))
    od, s_new = gated_delta_chunked(cq, ck, cv, g, beta, s0.astype(F32))
    od = od * lax.rsqrt(jnp.mean(od * od, -1, keepdims=True) + NORM_EPS) * delta_norm_w.astype(F32)
    o_del = (od.reshape(b, t, D_DV) * jax.nn.silu(dz.astype(F32))).astype(dt_)

    ga = jax.nn.sigmoid((ga_l + b_gate[:D_MODEL]).astype(F32))
    gd = jax.nn.sigmoid((gd_l + b_gate[D_MODEL:]).astype(F32))
    merged = (ga * jnp.einsum('btc,cd->btd', o_att, w_branch_attn, preferred_element_type=F32)
              + gd * jnp.einsum('btc,cd->btd', o_del, w_branch_delta, preferred_element_type=F32))
    mix = jnp.einsum('btc,cd->btd', merged.astype(dt_), w_out)
    return mix, kv_new, conv_new, s_new


def moe_ffn(h, router_w, router_b, w_gu, b_gu, w_down, b_down):
    shp = h.shape
    xt = h.reshape(-1, D_MODEL)
    n = xt.shape[0]
    logits = jnp.einsum('nd,de->ne', xt, router_w, preferred_element_type=F32) + router_b.astype(F32)
    top_v, top_i = lax.top_k(logits, TOP_K)
    gate = jax.nn.softmax(top_v, axis=-1)
    nk = n * TOP_K
    a_exp = top_i.reshape(nk)
    a_tok = jnp.arange(nk, dtype=jnp.int32) // TOP_K
    a_gate = gate.reshape(nk)
    order = jnp.argsort(a_exp)
    s_exp, s_tok, s_gate = a_exp[order], a_tok[order], a_gate[order]
    counts = jnp.zeros((N_EXP,), jnp.int32).at[a_exp].add(1)
    padded = (counts + MOE_BLK - 1) // MOE_BLK * MOE_BLK
    pad_end = jnp.cumsum(padded)
    pad_start = pad_end - padded
    grp_start = jnp.cumsum(counts) - counts
    dest = pad_start[s_exp] + jnp.arange(nk, dtype=jnp.int32) - grp_start[s_exp]
    n_blk = -(-(nk + N_EXP * (MOE_BLK - 1)) // MOE_BLK)
    n_slot = n_blk * MOE_BLK
    slot_tok = jnp.zeros((n_slot,), jnp.int32).at[dest].set(s_tok)
    slot_gate = jnp.zeros((n_slot,), F32).at[dest].set(s_gate)
    blk_exp = jnp.minimum(jnp.searchsorted(pad_end, jnp.arange(n_blk, dtype=jnp.int32) * MOE_BLK, side='right'),
                          N_EXP - 1)
    xs = xt[slot_tok].reshape(n_blk, MOE_BLK, D_MODEL)

    def expert_block(args):
        xb, e = args
        gu = jnp.einsum('td,df->tf', xb, w_gu[e], preferred_element_type=F32) + b_gu[e].astype(F32)
        gt = jnp.minimum(gu[:, :D_FF], SWIGLU_LIMIT)
        up = jnp.clip(gu[:, D_FF:], -SWIGLU_LIMIT, SWIGLU_LIMIT)
        act = (up + 1.0) * gt * jax.nn.sigmoid(SWIGLU_ALPHA * gt)
        return jnp.einsum('tf,fd->td', act.astype(xb.dtype), w_down[e], preferred_element_type=F32) + b_down[e].astype(F32)

    ys = lax.map(expert_block, (xs, blk_exp))
    y = jax.ops.segment_sum(ys.reshape(n_slot, D_MODEL) * slot_gate[:, None], slot_tok, num_segments=n)
    return y.astype(h.dtype).reshape(shp)


def decoder_layer(x, kv_bufs, conv_buf, s0, w_in, b_gate, conv_w, a_log, dt_bias, delta_norm_w,
                  w_branch_attn, w_branch_delta, w_out, ln1_g, ln1_b, router_w, router_b,
                  w_gu, b_gu, w_down, b_down, ln2_g, ln2_b):
    mix, kv_new, conv_new, s_new = token_mixers(x, kv_bufs, conv_buf, s0, w_in, b_gate, conv_w, a_log, dt_bias,
                                                delta_norm_w, w_branch_attn, w_branch_delta, w_out)
    h = layer_norm(DN_ALPHA * x + mix, ln1_g, ln1_b)
    y = layer_norm(DN_ALPHA * h + moe_ffn(h, router_w, router_b, w_gu, b_gu, w_down, b_down), ln2_g, ln2_b)
    return y, kv_new, conv_new, s_new


def setup_inputs(seed: int = 0) -> dict:
    key = jax.random.key(seed)
    ks = jax.random.split(key, 32)

    def nrm(k, shape, s):
        return s * jax.random.normal(k, shape, F32)

    L = DEPTH
    dt = jnp.exp(jax.random.uniform(ks[9], (L, H_D), F32) * (math.log(0.1) - math.log(1e-3)) + math.log(1e-3))
    return {
        'x_prompt': nrm(ks[0], (BATCH, SEQ, D_MODEL), 1.0),
        'x_sample': nrm(ks[1], (DEC_BATCH, DEC_SEQ, D_MODEL), 1.0),
        'cache_kv_w128': nrm(ks[2], (L, DEC_BATCH, min(128, PAST_LEN), 2, H_G, HD), 1.0),
        'cache_kv_w512': nrm(ks[3], (L, DEC_BATCH, min(512, PAST_LEN), 2, H_G, HD), 1.0),
        'cache_kv_w2048': nrm(ks[4], (L, DEC_BATCH, min(2048, PAST_LEN), 2, H_G, HD), 1.0),
        'state_conv': nrm(ks[5], (L, DEC_BATCH, CONV_W - 1, D_CONV), 1.0),
        'state_delta': nrm(ks[6], (L, DEC_BATCH, H_D, DK, DV), 0.1),
        'w_in': nrm(ks[7], (L, D_MODEL, D_IN), D_MODEL ** -0.5),
        'b_gate': nrm(ks[8], (L, 2 * D_MODEL), 0.02),
        'conv_w': nrm(ks[10], (L, CONV_W, D_CONV), CONV_W ** -0.5),
        'a_log': jnp.log(jax.random.uniform(ks[11], (L, H_D), F32, 1.0, 16.0)),
        'dt_bias': dt + jnp.log(-jnp.expm1(-dt)),
        'delta_norm_w': 1.0 + nrm(ks[12], (L, DV), 0.02),
        'w_branch_attn': nrm(ks[13], (L, H_G * HD, D_MODEL), DN_BETA * (H_G * HD) ** -0.5),
        'w_branch_delta': nrm(ks[14], (L, D_DV, D_MODEL), DN_BETA * D_DV ** -0.5),
        'w_out': nrm(ks[15], (L, D_MODEL, D_MODEL), DN_BETA * D_MODEL ** -0.5),
        'ln1_g': 1.0 + nrm(ks[16], (L, D_MODEL), 0.02),
        'ln1_b': nrm(ks[17], (L, D_MODEL), 0.02),
        'router_w': nrm(ks[18], (L, D_MODEL, N_EXP), D_MODEL ** -0.5),
        'router_b': nrm(ks[19], (L, N_EXP), 0.01),
        'w_gu': nrm(ks[20], (L, N_EXP, D_MODEL, 2 * D_FF), D_MODEL ** -0.5),
        'b_gu': nrm(ks[21], (L, N_EXP, 2 * D_FF), 0.02),
        'w_down': nrm(ks[22], (L, N_EXP, D_FF, D_MODEL), DN_BETA * D_FF ** -0.5),
        'b_down': nrm(ks[23], (L, N_EXP, D_MODEL), 0.02),
        'ln2_g': 1.0 + nrm(ks[24], (L, D_MODEL), 0.02),
        'ln2_b': nrm(ks[25], (L, D_MODEL), 0.02),
    }


def reference(x_prompt, x_sample, cache_kv_w128, cache_kv_w512, cache_kv_w2048, state_conv, state_delta,
              w_in, b_gate, conv_w, a_log, dt_bias, delta_norm_w, w_branch_attn, w_branch_delta, w_out,
              ln1_g, ln1_b, router_w, router_b, w_gu, b_gu, w_down, b_down, ln2_g, ln2_b):
    hp, hs = x_prompt, x_sample
    acc = [[] for _ in range(10)]
    for l in range(DEPTH):
        lw = (w_in[l], b_gate[l], conv_w[l], a_log[l], dt_bias[l], delta_norm_w[l], w_branch_attn[l],
              w_branch_delta[l], w_out[l], ln1_g[l], ln1_b[l], router_w[l], router_b[l], w_gu[l], b_gu[l],
              w_down[l], b_down[l], ln2_g[l], ln2_b[l])
        hp, kv_p, cv_p, s_p = decoder_layer(hp, None, None, None, *lw)
        hs, kv_s, cv_s, s_s = decoder_layer(hs, (cache_kv_w128[l], cache_kv_w512[l], cache_kv_w2048[l]),
                                            state_conv[l], state_delta[l], *lw)
        for lst, val in zip(acc, (kv_p[0], kv_p[1], kv_p[2], cv_p, s_p, kv_s[0], kv_s[1], kv_s[2], cv_s, s_s)):
            lst.append(val)
    (kv128_p, kv512_p, kv2048_p, conv_p, delta_p,
     kv128_s, kv512_s, kv2048_s, conv_s, delta_s) = [jnp.stack(a) for a in acc]
    y_prompt, y_sample = hp, hs
    return (y_prompt, y_sample, kv128_p, kv512_p, kv2048_p, conv_p, delta_p,
            kv128_s, kv512_s, kv2048_s, conv_s, delta_s)
```

```python
import functools
import math

import numpy as np
import jax
import jax.numpy as jnp
from jax import lax
from jax.experimental import pallas as pl
from jax.experimental.pallas import tpu as pltpu

F32 = jnp.float32
BF16 = jnp.bfloat16
HIGHEST = lax.Precision.HIGHEST

D_MODEL = 1024
ATT_DILS = (1, 4, 16)
STEPS = 128
H_G = 4
HD = 64
GW = H_G * HD
D_ATT = len(ATT_DILS) * GW
H_D = 4
DK = 128
DV = 128
D_DK = H_D * DK
D_DV = H_D * DV
D_CONV = 2 * D_DK + D_DV
CONV_W = 4
CHUNK = 64
N_EXP = 32
TOP_K = 4
D_FF = D_MODEL
SWIGLU_LIMIT = 7.0
SWIGLU_ALPHA = 1.702
DN_ALPHA = 2.0 ** 0.25
LN_EPS = 1e-5
NORM_EPS = 1e-6
NEG = -1e30

LANES = 128
SUBLANES = 8
VMEM_LIMIT = 56 * 1024 * 1024
MOE_BM = 256


def _sigmoid(x):
    return 1.0 / (1.0 + jnp.exp(-x))


def _layer_norm(v, g, b):
    mu = jnp.mean(v, axis=-1, keepdims=True)
    d = v - mu
    var = jnp.mean(d * d, axis=-1, keepdims=True)
    return d * lax.rsqrt(var + LN_EPS) * g + b


def _alibi_slopes():
    return 2.0 ** (-8.0 * np.arange(1, H_G + 1, dtype=np.float64) / H_G)


def _inproj_body(x_ref, *refs):
    nw = len(refs) // 2
    x = x_ref[...].astype(BF16)
    for w_ref, o_ref in zip(refs[:nw], refs[nw:]):
        o_ref[...] = jnp.dot(x, w_ref[...], preferred_element_type=F32)


def _in_projection(x2d, ws, tm):
    n = x2d.shape[0]
    widths = [w.shape[1] for w in ws]
    return pl.pallas_call(
        _inproj_body,
        grid=(n // tm,),
        in_specs=[pl.BlockSpec((tm, D_MODEL), lambda i: (i, 0))]
        + [pl.BlockSpec((D_MODEL, c), lambda i: (0, 0), pipeline_mode=pl.Buffered(1)) for c in widths],
        out_specs=[pl.BlockSpec((tm, c), lambda i: (i, 0)) for c in widths],
        out_shape=[jax.ShapeDtypeStruct((n, c), F32) for c in widths],
        compiler_params=pltpu.CompilerParams(dimension_semantics=("parallel",), vmem_limit_bytes=VMEM_LIMIT),
        name="in_projection",
    )(x2d, *ws)


def _prompt_bias_table(dil):
    qi = np.arange(STEPS)[:, None]
    kj = np.arange(2 * STEPS)[None, :]
    steps = qi + STEPS - kj
    valid = (steps >= 0) & (steps <= STEPS)
    slopes = _alibi_slopes()
    bias = -slopes[:, None, None] * (dil * steps)[None].astype(np.float64)
    later = np.where(valid[None], bias, NEG)
    first = np.where((valid & (kj >= STEPS))[None], bias, NEG)
    return jnp.asarray(np.stack([first, later]), F32)


def _attn_prompt_body(q_ref, kp_ref, kc_ref, vp_ref, vc_ref, bias_ref, o_ref, lse_ref):
    later = jnp.minimum(pl.program_id(2), 1)
    q = q_ref[0] * (HD ** -0.5)
    kk = jnp.concatenate([kp_ref[0], kc_ref[0]], axis=0).astype(BF16)
    vv = jnp.concatenate([vp_ref[0], vc_ref[0]], axis=0).astype(BF16)
    lse_ref[0] = jnp.zeros((STEPS, LANES), F32)
    for h in range(H_G):
        cs = slice(h * HD, (h + 1) * HD)
        s = lax.dot_general(q[:, cs].astype(BF16), kk[:, cs], (((1,), (1,)), ((), ())),
                            preferred_element_type=F32)
        s = s + bias_ref[later, h]
        mx = jnp.max(s, axis=-1, keepdims=True)
        p = jnp.exp(s - mx)
        den = jnp.sum(p, axis=-1, keepdims=True)
        o = jnp.dot(p.astype(BF16), vv[:, cs], preferred_element_type=F32)
        o_ref[0, :, cs] = o / den
        lse_ref[0, :, h:h + 1] = mx + jnp.log(den)


def _attn_prompt(att, gi, dil):
    b, t, c = att.shape
    assert t % (dil * STEPS) == 0
    L = t // dil
    nb = L // STEPS
    nblk_c = c // GW
    a = att.reshape(b, L, dil * c)
    qcol, kcol, vcol = gi, D_ATT // GW + gi, 2 * (D_ATT // GW) + gi

    def spec(col, prev):
        if prev:
            return pl.BlockSpec((1, STEPS, GW), lambda bi, r, n: (bi, jnp.maximum(n - 1, 0), r * nblk_c + col))
        return pl.BlockSpec((1, STEPS, GW), lambda bi, r, n: (bi, n, r * nblk_c + col))

    o, lse = pl.pallas_call(
        _attn_prompt_body,
        grid=(b, dil, nb),
        in_specs=[spec(qcol, False), spec(kcol, True), spec(kcol, False), spec(vcol, True), spec(vcol, False),
                  pl.BlockSpec((2, H_G, STEPS, 2 * STEPS), lambda bi, r, n: (0, 0, 0, 0))],
        out_specs=[pl.BlockSpec((1, STEPS, GW), lambda bi, r, n: (bi, n, r)),
                   pl.BlockSpec((1, STEPS, LANES), lambda bi, r, n: (bi, n, r))],
        out_shape=[jax.ShapeDtypeStruct((b, L, dil * GW), F32),
                   jax.ShapeDtypeStruct((b, L, dil * LANES), F32)],
        compiler_params=pltpu.CompilerParams(dimension_semantics=("parallel", "parallel", "arbitrary"),
                                             vmem_limit_bytes=VMEM_LIMIT),
        name=f"attn_prompt_d{dil}",
    )(a, a, a, a, a, _prompt_bias_table(dil))
    return o.reshape(b * t, GW), lse.reshape(b * t, LANES)


KV_ROWS = STEPS + SUBLANES


def _sample_bias_table(s_new):
    slopes = _alibi_slopes()
    tab = np.zeros((len(ATT_DILS), s_new, KV_ROWS, LANES), np.float64)
    tab[:, :, :, :H_G] = NEG
    for gi, dil in enumerate(ATT_DILS):
        for s in range(s_new):
            for j in range(STEPS):
                if dil == 1:
                    m = STEPS + s - j
                    ok = j >= s
                else:
                    m = STEPS - j
                    ok = True
                if ok:
                    tab[gi, s, j, :H_G] = -slopes * dil * m
            for s2 in range(s_new):
                if dil == 1 and s2 <= s:
                    tab[gi, s, STEPS + s2, :H_G] = -slopes * (s - s2)
                elif dil > 1 and s2 == s:
                    tab[gi, s, STEPS + s2, :H_G] = 0.0
    return jnp.asarray(tab, F32)


def _head_indicator():
    e = np.zeros((GW, LANES), np.float32)
    for h in range(H_G):
        e[h * HD:(h + 1) * HD, h] = 1.0
    return jnp.asarray(e), jnp.asarray(e.T.copy())


def _attn_sample_body(new_ref, b0_ref, b1_ref, b2_ref, bias_ref, e_ref, et_ref, o_ref, lse_ref, kall, vall, *, s_new):
    for gi, (dil, buf) in enumerate(zip(ATT_DILS, (b0_ref, b1_ref, b2_ref))):
        kall[STEPS:KV_ROWS, :] = new_ref[0, :, D_ATT + gi * GW:D_ATT + (gi + 1) * GW]
        vall[STEPS:KV_ROWS, :] = new_ref[0, :, 2 * D_ATT + gi * GW:2 * D_ATT + (gi + 1) * GW]
        for s in range(s_new):
            if dil > 1 or s == 0:
                c0 = 0 if dil == 1 else s * 2 * GW
                kall[0:STEPS, :] = buf[0, :, c0:c0 + GW]
                vall[0:STEPS, :] = buf[0, :, c0 + GW:c0 + 2 * GW]
            q = new_ref[0, s:s + 1, gi * GW:(gi + 1) * GW] * (HD ** -0.5)
            sc = jnp.dot(kall[...] * q, e_ref[...], preferred_element_type=F32, precision=HIGHEST)
            sc = sc + bias_ref[gi, s]
            mx = jnp.max(sc, axis=0, keepdims=True)
            p = jnp.exp(sc - mx)
            den = jnp.sum(p, axis=0, keepdims=True)
            pe = jnp.dot(p, et_ref[...], preferred_element_type=F32, precision=HIGHEST)
            num = jnp.sum(pe * vall[...], axis=0, keepdims=True)
            o_ref[0, s:s + 1, gi * GW:(gi + 1) * GW] = num / jnp.sum(pe, axis=0, keepdims=True)
            lse_ref[0, s:s + 1, gi * LANES:(gi + 1) * LANES] = mx + jnp.log(den)


def _attn_sample(att, caches):
    b, s_new, c = att.shape
    assert s_new <= min(SUBLANES, ATT_DILS[1])
    new = jnp.pad(att, ((0, 0), (0, SUBLANES - s_new), (0, 0)))
    bufs, specs = [], []
    for dil, cache in zip(ATT_DILS, caches):
        assert cache.shape[1] == dil * STEPS
        bufs.append(cache.reshape(b, STEPS, dil * 2 * GW))
        specs.append(pl.BlockSpec((1, STEPS, min(dil, s_new) * 2 * GW), lambda bi: (bi, 0, 0)))
    e, et = _head_indicator()
    ng = len(ATT_DILS)
    o, lse = pl.pallas_call(
        functools.partial(_attn_sample_body, s_new=s_new),
        grid=(b,),
        in_specs=[pl.BlockSpec((1, SUBLANES, c), lambda bi: (bi, 0, 0))] + specs
        + [pl.BlockSpec((ng, s_new, KV_ROWS, LANES), lambda bi: (0, 0, 0, 0)),
           pl.BlockSpec((GW, LANES), lambda bi: (0, 0)), pl.BlockSpec((LANES, GW), lambda bi: (0, 0))],
        out_specs=[pl.BlockSpec((1, s_new, D_ATT), lambda bi: (bi, 0, 0)),
                   pl.BlockSpec((1, s_new, ng * LANES), lambda bi: (bi, 0, 0))],
        out_shape=[jax.ShapeDtypeStruct((b, s_new, D_ATT), F32),
                   jax.ShapeDtypeStruct((b, s_new, ng * LANES), F32)],
        scratch_shapes=[pltpu.VMEM((KV_ROWS, GW), F32), pltpu.VMEM((KV_ROWS, GW), F32)],
        compiler_params=pltpu.CompilerParams(dimension_semantics=("parallel",), vmem_limit_bytes=VMEM_LIMIT),
        name="attn_sample",
    )(new, *bufs, _sample_bias_table(s_new), e, et)
    return o.reshape(b * s_new, D_ATT), lse.reshape(b * s_new, ng * LANES)


HIST = SUBLANES


def _delta_body(u_ref, small_ref, z_ref, cbuf_ref, s0_ref, cw_ref, alog_ref, dtb_ref, nw_ref,
                o_ref, sout_ref, S, ext, *, t_valid, nc):
    c = pl.program_id(1)

    @pl.when(c == 0)
    def _():
        S[...] = s0_ref[0]
        ext[0:HIST, :] = cbuf_ref[0]

    ext[HIST:HIST + CHUNK, :] = u_ref[0]
    conv = ext[HIST - 3:HIST - 3 + CHUNK, :] * cw_ref[0:1, :]
    for i in range(1, CONV_W):
        conv = conv + ext[HIST - 3 + i:HIST - 3 + i + CHUNK, :] * cw_ref[i:i + 1, :]
    ext[0:HIST, :] = ext[CHUNK:CHUNK + HIST, :]
    act = conv * _sigmoid(conv)

    sm = small_ref[0]
    bmat = _sigmoid(sm)
    xg = sm + dtb_ref[...]
    gmat = -jnp.exp(alog_ref[...]) * (jnp.maximum(xg, 0.0) + jnp.log1p(jnp.exp(-jnp.abs(xg))))
    if t_valid < CHUNK:
        live = lax.broadcasted_iota(jnp.int32, (CHUNK, 1), 0) < t_valid
        act = jnp.where(live, act, 0.0)
        bmat = jnp.where(live, bmat, 0.0)
        gmat = jnp.where(live, gmat, 0.0)

    ri = lax.broadcasted_iota(jnp.int32, (CHUNK, CHUNK), 0)
    ci = lax.broadcasted_iota(jnp.int32, (CHUNK, CHUNK), 1)
    tril = ri >= ci
    strict = ri > ci
    eye = (ri == ci).astype(F32)
    gc = jnp.dot(tril.astype(F32), gmat, preferred_element_type=F32, precision=HIGHEST)
    gct = gc.T

    for h in range(H_D):
        q = act[:, h * DK:(h + 1) * DK]
        k = act[:, D_DK + h * DK:D_DK + (h + 1) * DK]
        v = act[:, 2 * D_DK + h * DV:2 * D_DK + (h + 1) * DV]
        qn = q * lax.rsqrt(jnp.sum(q * q, axis=-1, keepdims=True) + NORM_EPS) * (DK ** -0.5)
        kn = k * lax.rsqrt(jnp.sum(k * k, axis=-1, keepdims=True) + NORM_EPS)
        beta = bmat[:, h:h + 1]
        gcol = gc[:, H_D + h:H_D + h + 1]
        grow = gct[H_D + h:H_D + h + 1, :]
        glast = gc[CHUNK - 1:CHUNK, H_D + h:H_D + h + 1]
        decay = jnp.where(tril, jnp.exp(jnp.where(tril, gcol - grow, 0.0)), 0.0)
        eg = jnp.exp(gcol)
        kb = kn * beta
        nt = (((1,), (1,)), ((), ()))
        a_kk = lax.dot_general(kb, kn, nt, preferred_element_type=F32, precision=HIGHEST)
        neg_l = jnp.where(strict, -(a_kk * decay), 0.0)
        tinv = eye + neg_l
        pw = neg_l
        for _ in range(int(math.log2(CHUNK)) - 1):
            pw = jnp.dot(pw, pw, preferred_element_type=F32, precision=HIGHEST)
            tinv = tinv + jnp.dot(tinv, pw, preferred_element_type=F32, precision=HIGHEST)
        u = jnp.dot(tinv, v * beta, preferred_element_type=F32, precision=HIGHEST)
        w = jnp.dot(tinv, kb * eg, preferred_element_type=F32, precision=HIGHEST)
        a_qk = lax.dot_general(qn, kn, nt, preferred_element_type=F32, precision=HIGHEST) * decay
        s_h = S[h]
        vn = u - jnp.dot(w, s_h, preferred_element_type=F32, precision=HIGHEST)
        o = (jnp.dot(qn * eg, s_h, preferred_element_type=F32, precision=HIGHEST)
             + jnp.dot(a_qk, vn, preferred_element_type=F32, precision=HIGHEST))
        kd = kn * jnp.exp(glast - gcol)
        S[h] = s_h * jnp.exp(glast) + jnp.dot(kd.T, vn, preferred_element_type=F32, precision=HIGHEST)
        od = o * lax.rsqrt(jnp.mean(o * o, axis=-1, keepdims=True) + NORM_EPS) * nw_ref[...]
        zh = z_ref[0, :, h * DV:(h + 1) * DV]
        o_ref[0, :, h * DV:(h + 1) * DV] = od * (zh * _sigmoid(zh))

    @pl.when(c == nc - 1)
    def _():
        sout_ref[0] = S[...]


def _lane_row(vals, offset):
    row = jnp.zeros((1, LANES), F32)
    return row.at[0, offset:offset + vals.shape[0]].set(vals.astype(F32))


def _delta_net(dqkv, small, z, conv_buf, s0, conv_w, a_log, dt_bias, delta_norm_w):
    b, t, _ = dqkv.shape
    tp = -(-t // CHUNK) * CHUNK
    nc = tp // CHUNK
    if tp != t:
        pad = ((0, 0), (0, tp - t), (0, 0))
        dqkv, small, z = jnp.pad(dqkv, pad), jnp.pad(small, pad), jnp.pad(z, pad)
    cbuf = jnp.pad(conv_buf, ((0, 0), (HIST - (CONV_W - 1), 0), (0, 0)))
    cw = jnp.pad(conv_w, ((0, SUBLANES - CONV_W), (0, 0)))
    full = lambda shape: pl.BlockSpec(shape, lambda bi, ci: (0,) * len(shape))
    o, s_new = pl.pallas_call(
        functools.partial(_delta_body, t_valid=t if nc == 1 else CHUNK, nc=nc),
        grid=(b, nc),
        in_specs=[pl.BlockSpec((1, CHUNK, D_CONV), lambda bi, ci: (bi, ci, 0)),
                  pl.BlockSpec((1, CHUNK, LANES), lambda bi, ci: (bi, ci, 0)),
                  pl.BlockSpec((1, CHUNK, D_DV), lambda bi, ci: (bi, ci, 0)),
                  pl.BlockSpec((1, HIST, D_CONV), lambda bi, ci: (bi, 0, 0)),
                  pl.BlockSpec((1, H_D, DK, DV), lambda bi, ci: (bi, 0, 0, 0)),
                  full((SUBLANES, D_CONV)), full((1, LANES)), full((1, LANES)), full((1, DV))],
        out_specs=[pl.BlockSpec((1, CHUNK, D_DV), lambda bi, ci: (bi, ci, 0)),
                   pl.BlockSpec((1, H_D, DK, DV), lambda bi, ci: (bi, 0, 0, 0))],
        out_shape=[jax.ShapeDtypeStruct((b, tp, D_DV), F32), jax.ShapeDtypeStruct((b, H_D, DK, DV), F32)],
        scratch_shapes=[pltpu.VMEM((H_D, DK, DV), F32), pltpu.VMEM((HIST + CHUNK, D_CONV), F32)],
        compiler_params=pltpu.CompilerParams(dimension_semantics=("parallel", "arbitrary"),
                                             vmem_limit_bytes=VMEM_LIMIT),
        name="delta_net",
    )(dqkv, small, z, cbuf, s0, cw, _lane_row(a_log, H_D), _lane_row(dt_bias, H_D),
      delta_norm_w.reshape(1, DV).astype(F32))
    assert tp == t or nc == 1
    return o[:, :t].reshape(b * t, D_DV), s_new


def _lane_expand(cols, width):
    tm = cols[0].shape[0]
    seg = lax.broadcasted_iota(jnp.int32, (tm, len(cols) * width), 1) // width
    out = jnp.broadcast_to(cols[-1], seg.shape)
    for i in range(len(cols) - 2, -1, -1):
        out = jnp.where(seg == i, cols[i], out)
    return out


def _mix_body(x_ref, gate_ref, o0_ref, o1_ref, o2_ref, l0_ref, l1_ref, l2_ref, od_ref,
              wa_ref, wd_ref, wo_ref, bg_ref, g1_ref, b1_ref, rw_ref, rb_ref,
              h_ref, ti_ref, tg_ref):
    lses = (l0_ref[...], l1_ref[...], l2_ref[...])
    mx = jnp.maximum(jnp.maximum(lses[0], lses[1]), lses[2])
    es = [jnp.exp(l - mx) for l in lses]
    inv = 1.0 / (es[0] + es[1] + es[2])
    o_att = None
    for e_g, o_g in zip(es, (o0_ref, o1_ref, o2_ref)):
        wgt = e_g * inv
        term = _lane_expand([wgt[:, h:h + 1] for h in range(H_G)], HD) * o_g[...]
        o_att = term if o_att is None else o_att + term
    ga = _sigmoid(gate_ref[:, :D_MODEL] + bg_ref[:, :D_MODEL])
    gd = _sigmoid(gate_ref[:, D_MODEL:] + bg_ref[:, D_MODEL:])
    merged = (ga * jnp.dot(o_att.astype(BF16), wa_ref[...], preferred_element_type=F32)
              + gd * jnp.dot(od_ref[...].astype(BF16), wd_ref[...], preferred_element_type=F32))
    mix = jnp.dot(merged.astype(BF16), wo_ref[...], preferred_element_type=F32)
    h = _layer_norm(DN_ALPHA * x_ref[...] + mix, g1_ref[...], b1_ref[...])
    h_ref[...] = h

    logits = jnp.dot(h, rw_ref[...], preferred_element_type=F32, precision=HIGHEST) + rb_ref[...]
    lane = lax.broadcasted_iota(jnp.int32, logits.shape, 1)
    lane_f = lane.astype(F32)
    ti = jnp.zeros(logits.shape, F32)
    tv = jnp.zeros(logits.shape, F32)
    top = None
    for kk in range(TOP_K):
        m = jnp.max(logits, axis=-1, keepdims=True)
        idx = jnp.min(jnp.where(logits == m, lane_f, float(LANES)), axis=-1, keepdims=True)
        if top is None:
            top = m
        ti = jnp.where(lane == kk, idx, ti)
        tv = jnp.where(lane == kk, jnp.exp(m - top), tv)
        logits = jnp.where(lane_f == idx, -jnp.inf, logits)
    ti_ref[...] = ti.astype(jnp.int32)
    tg_ref[...] = tv / jnp.sum(tv, axis=-1, keepdims=True)


def _mix_and_route(x2d, gates, o_groups, lse_groups, o_del, wa, wd, wo, b_gate, ln_g, ln_b, rw, rb, tm):
    n = x2d.shape[0]
    row = lambda c: pl.BlockSpec((tm, c), lambda i: (i, 0))
    full = lambda a: pl.BlockSpec(a.shape, lambda i: (0, 0))
    consts = (wa, wd, wo, b_gate, ln_g, ln_b, rw, rb)
    return pl.pallas_call(
        _mix_body,
        grid=(n // tm,),
        in_specs=[row(D_MODEL), row(2 * D_MODEL), row(GW), row(GW), row(GW), row(LANES), row(LANES), row(LANES),
                  row(D_DV)] + [full(a) for a in consts],
        out_specs=[row(D_MODEL), row(LANES), row(LANES)],
        out_shape=[jax.ShapeDtypeStruct((n, D_MODEL), F32), jax.ShapeDtypeStruct((n, LANES), jnp.int32),
                   jax.ShapeDtypeStruct((n, LANES), F32)],
        compiler_params=pltpu.CompilerParams(dimension_semantics=("parallel",), vmem_limit_bytes=VMEM_LIMIT),
        name="mix_and_route",
    )(x2d, gates, *o_groups, *lse_groups, o_del, *consts)


def _rank_body(ti_ref, rank_ref, cnt_ref, carry):
    i = pl.program_id(0)

    @pl.when(i == 0)
    def _():
        carry[...] = jnp.zeros_like(carry)

    ti = ti_ref[...]
    tm = ti.shape[0]
    lane = lax.broadcasted_iota(jnp.int32, ti.shape, 1)
    sel = [lane == ti[:, kk:kk + 1] for kk in range(TOP_K)]
    hit = sel[0]
    for kk in range(1, TOP_K):
        hit = hit | sel[kk]
    cnt = hit.astype(F32)
    ri = lax.broadcasted_iota(jnp.int32, (tm, tm), 0)
    ci = lax.broadcasted_iota(jnp.int32, (tm, tm), 1)
    incl = jnp.dot((ri >= ci).astype(BF16), cnt.astype(BF16), preferred_element_type=F32)
    before = incl - cnt + carry[...]
    rank = jnp.zeros(ti.shape, jnp.int32)
    for kk in range(TOP_K):
        r = jnp.sum(jnp.where(sel[kk], before, 0.0), axis=-1, keepdims=True)
        rank = jnp.where(lane == kk, r.astype(jnp.int32), rank)
    rank_ref[...] = rank
    carry[...] = carry[...] + incl[tm - 1:tm, :]
    cnt_ref[...] = carry[...]


def _route_ranks(top_i, tm):
    n = top_i.shape[0]
    return pl.pallas_call(
        _rank_body,
        grid=(n // tm,),
        in_specs=[pl.BlockSpec((tm, LANES), lambda i: (i, 0))],
        out_specs=[pl.BlockSpec((tm, LANES), lambda i: (i, 0)), pl.BlockSpec((1, LANES), lambda i: (0, 0))],
        out_shape=[jax.ShapeDtypeStruct((n, LANES), jnp.int32), jax.ShapeDtypeStruct((1, LANES), F32)],
        scratch_shapes=[pltpu.VMEM((1, LANES), F32)],
        compiler_params=pltpu.CompilerParams(dimension_semantics=("arbitrary",), vmem_limit_bytes=VMEM_LIMIT),
        name="route_ranks",
    )(top_i)


def _dispatch_body(dest_ref, h_ref, xs_in, xs_out, sem):
    del xs_in
    tm = h_ref.shape[0]

    def issue(t, carry):
        for kk in range(TOP_K):
            d = dest_ref[t * TOP_K + kk]
            pltpu.make_async_copy(h_ref.at[pl.ds(t, 1)], xs_out.at[pl.ds(d, 1)], sem).start()
        return carry

    lax.fori_loop(0, tm, issue, 0)
    for kk in range(TOP_K):
        pltpu.make_async_copy(h_ref, xs_out.at[pl.ds(0, tm)], sem).wait()


def _dispatch(h, dest_flat, n_slot, tm):
    n = h.shape[0]
    xs0 = jnp.zeros((n_slot, D_MODEL), F32)
    return pl.pallas_call(
        _dispatch_body,
        grid=(n // tm,),
        in_specs=[pl.BlockSpec((tm * TOP_K,), lambda i: (i,), memory_space=pltpu.SMEM),
                  pl.BlockSpec((tm, D_MODEL), lambda i: (i, 0)),
                  pl.BlockSpec(memory_space=pl.ANY)],
        out_specs=pl.BlockSpec(memory_space=pl.ANY),
        out_shape=jax.ShapeDtypeStruct((n_slot, D_MODEL), F32),
        scratch_shapes=[pltpu.SemaphoreType.DMA(())],
        input_output_aliases={2: 0},
        compiler_params=pltpu.CompilerParams(dimension_semantics=("arbitrary",), vmem_limit_bytes=VMEM_LIMIT,
                                             has_side_effects=True),
        name="moe_dispatch",
    )(dest_flat, h, xs0)


def _expert_body(be_ref, nu_ref, x_ref, wgu_ref, bgu_ref, wd_ref, bd_ref, o_ref, wgu_bf, wd_bf):
    i = pl.program_id(0)
    e = be_ref[i]
    prev = be_ref[jnp.maximum(i - 1, 0)]

    @pl.when((i == 0) | (e != prev))
    def _():
        wgu_bf[...] = wgu_ref[0].astype(BF16)
        wd_bf[...] = wd_ref[0].astype(BF16)

    @pl.when(i < nu_ref[0])
    def _():
        gu = jnp.dot(x_ref[...].astype(BF16), wgu_bf[...], preferred_element_type=F32) + bgu_ref[0]
        gt = jnp.minimum(gu[:, :D_FF], SWIGLU_LIMIT)
        up = jnp.clip(gu[:, D_FF:], -SWIGLU_LIMIT, SWIGLU_LIMIT)
        act = (up + 1.0) * gt * _sigmoid(SWIGLU_ALPHA * gt)
        o_ref[...] = jnp.dot(act.astype(BF16), wd_bf[...], preferred_element_type=F32) + bd_ref[0]

    @pl.when(i >= nu_ref[0])
    def _():
        o_ref[...] = jnp.zeros_like(o_ref)


def _experts(xs, blk_exp, n_used, w_gu, b_gu, w_down, b_down):
    n_slot = xs.shape[0]
    n_blk = n_slot // MOE_BM
    rows = lambda i, be, nu: (jnp.minimum(i, nu[0] - 1), 0)
    grid_spec = pltpu.PrefetchScalarGridSpec(
        num_scalar_prefetch=2,
        grid=(n_blk,),
        in_specs=[pl.BlockSpec((MOE_BM, D_MODEL), rows),
                  pl.BlockSpec((1, D_MODEL, 2 * D_FF), lambda i, be, nu: (be[i], 0, 0)),
                  pl.BlockSpec((1, 1, 2 * D_FF), lambda i, be, nu: (be[i], 0, 0)),
                  pl.BlockSpec((1, D_FF, D_MODEL), lambda i, be, nu: (be[i], 0, 0)),
                  pl.BlockSpec((1, 1, D_MODEL), lambda i, be, nu: (be[i], 0, 0))],
        out_specs=pl.BlockSpec((MOE_BM, D_MODEL), lambda i, be, nu: (i, 0)),
        scratch_shapes=[pltpu.VMEM((D_MODEL, 2 * D_FF), BF16), pltpu.VMEM((D_FF, D_MODEL), BF16)],
    )
    return pl.pallas_call(
        _expert_body,
        grid_spec=grid_spec,
        out_shape=jax.ShapeDtypeStruct((n_slot, D_MODEL), F32),
        compiler_params=pltpu.CompilerParams(dimension_semantics=("arbitrary",), vmem_limit_bytes=VMEM_LIMIT),
        name="moe_experts",
    )(blk_exp, n_used, xs, w_gu, b_gu.reshape(N_EXP, 1, 2 * D_FF), w_down, b_down.reshape(N_EXP, 1, D_MODEL))


def _combine_body(dest_ref, gate_ref, h_ref, g_ref, b_ref, ys_ref, y_ref, buf, sem):
    tm = h_ref.shape[0]

    def issue(t, carry):
        for kk in range(TOP_K):
            d = dest_ref[t * TOP_K + kk]
            pltpu.make_async_copy(ys_ref.at[pl.ds(d, 1)], buf.at[kk, pl.ds(t, 1)], sem).start()
        return carry

    lax.fori_loop(0, tm, issue, 0)
    for kk in range(TOP_K):
        pltpu.make_async_copy(ys_ref.at[pl.ds(0, tm)], buf.at[kk], sem).wait()
    gate = gate_ref[...]
    moe = gate[:, 0:1] * buf[0]
    for kk in range(1, TOP_K):
        moe = moe + gate[:, kk:kk + 1] * buf[kk]
    y_ref[...] = _layer_norm(DN_ALPHA * h_ref[...] + moe, g_ref[...], b_ref[...])


def _combine(ys, dest_flat, gate, h, ln_g, ln_b, tm):
    n = h.shape[0]
    return pl.pallas_call(
        _combine_body,
        grid=(n // tm,),
        in_specs=[pl.BlockSpec((tm * TOP_K,), lambda i: (i,), memory_space=pltpu.SMEM),
                  pl.BlockSpec((tm, LANES), lambda i: (i, 0)),
                  pl.BlockSpec((tm, D_MODEL), lambda i: (i, 0)),
                  pl.BlockSpec((1, D_MODEL), lambda i: (0, 0)),
                  pl.BlockSpec((1, D_MODEL), lambda i: (0, 0)),
                  pl.BlockSpec(memory_space=pl.ANY)],
        out_specs=pl.BlockSpec((tm, D_MODEL), lambda i: (i, 0)),
        out_shape=jax.ShapeDtypeStruct((n, D_MODEL), F32),
        scratch_shapes=[pltpu.VMEM((TOP_K, tm, D_MODEL), F32), pltpu.SemaphoreType.DMA(())],
        compiler_params=pltpu.CompilerParams(dimension_semantics=("arbitrary",), vmem_limit_bytes=VMEM_LIMIT),
        name="moe_combine",
    )(dest_flat, gate, h, ln_g, ln_b, ys)


def _moe(h, top_i, top_g, w_gu, b_gu, w_down, b_down, ln_g, ln_b, tm):
    n = h.shape[0]
    rank, counts = _route_ranks(top_i, tm)
    counts = counts[0, :N_EXP].astype(jnp.int32)
    padded = (counts + MOE_BM - 1) // MOE_BM * MOE_BM
    pad_end = jnp.cumsum(padded)
    pad_start = pad_end - padded
    n_blk = -(-(n * TOP_K + N_EXP * (MOE_BM - 1)) // MOE_BM)
    dest = (pad_start[top_i[:, :TOP_K]] + rank[:, :TOP_K]).reshape(n * TOP_K)
    blk_exp = jnp.minimum(jnp.searchsorted(pad_end, jnp.arange(n_blk, dtype=jnp.int32) * MOE_BM, side='right'),
                          N_EXP - 1).astype(jnp.int32)
    n_used = (pad_end[-1:] // MOE_BM).astype(jnp.int32)
    xs = _dispatch(h, dest, n_blk * MOE_BM, tm)
    ys = _experts(xs, blk_exp, n_used, w_gu, b_gu, w_down, b_down)
    return _combine(ys, dest, top_g, h, ln_g, ln_b, tm)


def _pad_cols(a, width, fill=0.0):
    return jnp.pad(a, ((0, 0), (0, width - a.shape[1])), constant_values=fill)


def _decoder_layer(x, caches, conv_buf, s0, w_in, b_gate, conv_w, a_log, dt_bias, delta_norm_w,
                   w_branch_attn, w_branch_delta, w_out, ln1_g, ln1_b, router_w, router_b,
                   w_gu, b_gu, w_down, b_down, ln2_g, ln2_b):
    b, t, _ = x.shape
    n = b * t
    tm = min(256, n)
    x2d = x.reshape(n, D_MODEL)

    c_att, c_dz, c_small = 3 * D_ATT, 3 * D_ATT + D_CONV, 3 * D_ATT + D_CONV + D_DV
    c_gate = c_small + 2 * H_D
    ws = (w_in[:, :c_att], w_in[:, c_att:c_dz], w_in[:, c_dz:c_small],
          _pad_cols(w_in[:, c_small:c_gate], LANES), w_in[:, c_gate:])
    att, dqkv, z, small, gates = _in_projection(x2d, [w.astype(BF16) for w in ws], tm)
    att3 = att.reshape(b, t, 3 * D_ATT)
    dqkv3 = dqkv.reshape(b, t, D_CONV)

    kv_new = []
    if caches is None:
        o_groups, lse_groups = [], []
        for gi, dil in enumerate(ATT_DILS):
            o_g, lse_g = _attn_prompt(att3, gi, dil)
            o_groups.append(o_g)
            lse_groups.append(lse_g)
            win = min(dil * STEPS, t)
            kg = att3[:, t - win:, D_ATT + gi * GW:D_ATT + (gi + 1) * GW].reshape(b, win, H_G, HD)
            vg = att3[:, t - win:, 2 * D_ATT + gi * GW:2 * D_ATT + (gi + 1) * GW].reshape(b, win, H_G, HD)
            kv_new.append(jnp.stack([kg, vg], axis=2))
        conv_buf = jnp.zeros((b, CONV_W - 1, D_CONV), F32)
        s0 = jnp.zeros((b, H_D, DK, DV), F32)
    else:
        o_all, lse_all = _attn_sample(att3, caches)
        o_groups = [o_all[:, gi * GW:(gi + 1) * GW] for gi in range(len(ATT_DILS))]
        lse_groups = [lse_all[:, gi * LANES:(gi + 1) * LANES] for gi in range(len(ATT_DILS))]
        for gi in range(len(ATT_DILS)):
            kg = att3[:, :, D_ATT + gi * GW:D_ATT + (gi + 1) * GW].reshape(b, t, H_G, HD)
            vg = att3[:, :, 2 * D_ATT + gi * GW:2 * D_ATT + (gi + 1) * GW].reshape(b, t, H_G, HD)
            kv_new.append(jnp.stack([kg, vg], axis=2))

    o_del, s_new = _delta_net(dqkv3, small.reshape(b, t, LANES), z.reshape(b, t, D_DV), conv_buf, s0,
                              conv_w, a_log, dt_bias, delta_norm_w)
    conv_new = jnp.concatenate([conv_buf, dqkv3], axis=1)[:, -(CONV_W - 1):] if t < CONV_W - 1 \
        else dqkv3[:, t - (CONV_W - 1):]

    rw = _pad_cols(router_w, LANES)
    rb = _pad_cols(router_b.reshape(1, N_EXP), LANES, fill=NEG)
    h, top_i, top_g = _mix_and_route(
        x2d, gates, o_groups, lse_groups, o_del,
        w_branch_attn.astype(BF16), w_branch_delta.astype(BF16), w_out.astype(BF16),
        b_gate.reshape(1, 2 * D_MODEL), ln1_g.reshape(1, D_MODEL), ln1_b.reshape(1, D_MODEL), rw, rb, tm)
    y = _moe(h, top_i, top_g, w_gu, b_gu, w_down, b_down,
             ln2_g.reshape(1, D_MODEL), ln2_b.reshape(1, D_MODEL), tm)
    return y.reshape(b, t, D_MODEL), kv_new, conv_new, s_new


def kernel(x_prompt, x_sample, cache_kv_w128, cache_kv_w512, cache_kv_w2048, state_conv, state_delta,
           w_in, b_gate, conv_w, a_log, dt_bias, delta_norm_w, w_branch_attn, w_branch_delta, w_out,
           ln1_g, ln1_b, router_w, router_b, w_gu, b_gu, w_down, b_down, ln2_g, ln2_b):
    depth = w_in.shape[0]
    assert depth == 1
    l = 0
    lw = (w_in[l], b_gate[l], conv_w[l], a_log[l], dt_bias[l], delta_norm_w[l], w_branch_attn[l],
          w_branch_delta[l], w_out[l], ln1_g[l], ln1_b[l], router_w[l], router_b[l], w_gu[l], b_gu[l],
          w_down[l], b_down[l], ln2_g[l], ln2_b[l])
    yp, kv_p, cv_p, s_p = _decoder_layer(x_prompt, None, None, None, *lw)
    ys, kv_s, cv_s, s_s = _decoder_layer(x_sample, (cache_kv_w128[l], cache_kv_w512[l], cache_kv_w2048[l]),
                                         state_conv[l], state_delta[l], *lw)
    stk = lambda a: a[None]
    return (yp, ys, stk(kv_p[0]), stk(kv_p[1]), stk(kv_p[2]), stk(cv_p), stk(s_p),
            stk(kv_s[0]), stk(kv_s[1]), stk(kv_s[2]), stk(cv_s), stk(s_s))
```

```python
import functools
import math

import numpy as np
import jax
import jax.numpy as jnp
from jax import lax
from jax.experimental import pallas as pl
from jax.experimental.pallas import tpu as pltpu

F32 = jnp.float32
BF16 = jnp.bfloat16
HIGHEST = lax.Precision.HIGHEST

D_MODEL = 1024
ATT_DILS = (1, 4, 16)
STEPS = 128
H_G = 4
HD = 64
GW = H_G * HD
D_ATT = len(ATT_DILS) * GW
H_D = 4
DK = 128
DV = 128
D_DK = H_D * DK
D_DV = H_D * DV
D_CONV = 2 * D_DK + D_DV
CONV_W = 4
CHUNK = 64
N_EXP = 32
TOP_K = 4
D_FF = D_MODEL
SWIGLU_LIMIT = 7.0
SWIGLU_ALPHA = 1.702
DN_ALPHA = 2.0 ** 0.25
LN_EPS = 1e-5
NORM_EPS = 1e-6
NEG = -1e30

LANES = 128
SUBLANES = 8
VMEM_LIMIT = 56 * 1024 * 1024
MOE_BM = 256


def _sigmoid(x):
    return 1.0 / (1.0 + jnp.exp(-x))


def _layer_norm(v, g, b):
    mu = jnp.mean(v, axis=-1, keepdims=True)
    d = v - mu
    var = jnp.mean(d * d, axis=-1, keepdims=True)
    return d * lax.rsqrt(var + LN_EPS) * g + b


def _alibi_slopes():
    return 2.0 ** (-8.0 * np.arange(1, H_G + 1, dtype=np.float64) / H_G)


def _inproj_body(x_ref, *refs):
    nw = len(refs) // 2
    x = x_ref[...].astype(BF16)
    for w_ref, o_ref in zip(refs[:nw], refs[nw:]):
        o_ref[...] = jnp.dot(x, w_ref[...], preferred_element_type=F32)


def _in_projection(x2d, ws, tm):
    n = x2d.shape[0]
    widths = [w.shape[1] for w in ws]
    return pl.pallas_call(
        _inproj_body,
        grid=(n // tm,),
        in_specs=[pl.BlockSpec((tm, D_MODEL), lambda i: (i, 0))]
        + [pl.BlockSpec((D_MODEL, c), lambda i: (0, 0), pipeline_mode=pl.Buffered(1)) for c in widths],
        out_specs=[pl.BlockSpec((tm, c), lambda i: (i, 0)) for c in widths],
        out_shape=[jax.ShapeDtypeStruct((n, c), F32) for c in widths],
        compiler_params=pltpu.CompilerParams(dimension_semantics=("parallel",), vmem_limit_bytes=VMEM_LIMIT),
        name="in_projection",
    )(x2d, *ws)


def _prompt_bias_table(dil):
    qi = np.arange(STEPS)[:, None]
    kj = np.arange(2 * STEPS)[None, :]
    steps = qi + STEPS - kj
    valid = (steps >= 0) & (steps <= STEPS)
    slopes = _alibi_slopes()
    bias = -slopes[:, None, None] * (dil * steps)[None].astype(np.float64)
    later = np.where(valid[None], bias, NEG)
    first = np.where((valid & (kj >= STEPS))[None], bias, NEG)
    return jnp.asarray(np.stack([first, later]), F32)


def _attn_prompt_body(q_ref, kp_ref, kc_ref, vp_ref, vc_ref, bias_ref, o_ref, lse_ref):
    later = jnp.minimum(pl.program_id(2), 1)
    q = q_ref[0] * (HD ** -0.5)
    kk = jnp.concatenate([kp_ref[0], kc_ref[0]], axis=0).astype(BF16)
    vv = jnp.concatenate([vp_ref[0], vc_ref[0]], axis=0).astype(BF16)
    lse_ref[0] = jnp.zeros((STEPS, LANES), F32)
    for h in range(H_G):
        cs = slice(h * HD, (h + 1) * HD)
        s = lax.dot_general(q[:, cs].astype(BF16), kk[:, cs], (((1,), (1,)), ((), ())),
                            preferred_element_type=F32)
        s = s + bias_ref[later, h]
        mx = jnp.max(s, axis=-1, keepdims=True)
        p = jnp.exp(s - mx)
        den = jnp.sum(p, axis=-1, keepdims=True)
        o = jnp.dot(p.astype(BF16), vv[:, cs], preferred_element_type=F32)
        o_ref[0, :, cs] = o / den
        lse_ref[0, :, h:h + 1] = mx + jnp.log(den)


def _attn_prompt(att, gi, dil):
    b, t, c = att.shape
    assert t % (dil * STEPS) == 0
    L = t // dil
    nb = L // STEPS
    nblk_c = c // GW
    a = att.reshape(b, L, dil * c)
    qcol, kcol, vcol = gi, D_ATT // GW + gi, 2 * (D_ATT // GW) + gi

    def spec(col, prev):
        if prev:
            return pl.BlockSpec((1, STEPS, GW), lambda bi, r, n: (bi, jnp.maximum(n - 1, 0), r * nblk_c + col))
        return pl.BlockSpec((1, STEPS, GW), lambda bi, r, n: (bi, n, r * nblk_c + col))

    o, lse = pl.pallas_call(
        _attn_prompt_body,
        grid=(b, dil, nb),
        in_specs=[spec(qcol, False), spec(kcol, True), spec(kcol, False), spec(vcol, True), spec(vcol, False),
                  pl.BlockSpec((2, H_G, STEPS, 2 * STEPS), lambda bi, r, n: (0, 0, 0, 0))],
        out_specs=[pl.BlockSpec((1, STEPS, GW), lambda bi, r, n: (bi, n, r)),
                   pl.BlockSpec((1, STEPS, LANES), lambda bi, r, n: (bi, n, r))],
        out_shape=[jax.ShapeDtypeStruct((b, L, dil * GW), F32),
                   jax.ShapeDtypeStruct((b, L, dil * LANES), F32)],
        compiler_params=pltpu.CompilerParams(dimension_semantics=("parallel", "parallel", "arbitrary"),
                                             vmem_limit_bytes=VMEM_LIMIT),
        name=f"attn_prompt_d{dil}",
    )(a, a, a, a, a, _prompt_bias_table(dil))
    return o.reshape(b * t, GW), lse.reshape(b * t, LANES)


KV_ROWS = STEPS + SUBLANES


def _sample_bias_table(s_new):
    slopes = _alibi_slopes()
    tab = np.zeros((len(ATT_DILS), s_new, KV_ROWS, LANES), np.float64)
    tab[:, :, :, :H_G] = NEG
    for gi, dil in enumerate(ATT_DILS):
        for s in range(s_new):
            for j in range(STEPS):
                if dil == 1:
                    m = STEPS + s - j
                    ok = j >= s
                else:
                    m = STEPS - j
                    ok = True
                if ok:
                    tab[gi, s, j, :H_G] = -slopes * dil * m
            for s2 in range(s_new):
                if dil == 1 and s2 <= s:
                    tab[gi, s, STEPS + s2, :H_G] = -slopes * (s - s2)
                elif dil > 1 and s2 == s:
                    tab[gi, s, STEPS + s2, :H_G] = 0.0
    return jnp.asarray(tab, F32)


def _head_indicator():
    e = np.zeros((GW, LANES), np.float32)
    for h in range(H_G):
        e[h * HD:(h + 1) * HD, h] = 1.0
    return jnp.asarray(e), jnp.asarray(e.T.copy())


def _attn_sample_body(new_ref, b0_ref, b1_ref, b2_ref, bias_ref, e_ref, et_ref, o_ref, lse_ref, kall, vall, *, s_new):
    for gi, (dil, buf) in enumerate(zip(ATT_DILS, (b0_ref, b1_ref, b2_ref))):
        kall[STEPS:KV_ROWS, :] = new_ref[0, :, D_ATT + gi * GW:D_ATT + (gi + 1) * GW]
        vall[STEPS:KV_ROWS, :] = new_ref[0, :, 2 * D_ATT + gi * GW:2 * D_ATT + (gi + 1) * GW]
        for s in range(s_new):
            if dil > 1 or s == 0:
                c0 = 0 if dil == 1 else s * 2 * GW
                kall[0:STEPS, :] = buf[0, :, c0:c0 + GW]
                vall[0:STEPS, :] = buf[0, :, c0 + GW:c0 + 2 * GW]
            q = new_ref[0, s:s + 1, gi * GW:(gi + 1) * GW] * (HD ** -0.5)
            sc = jnp.dot(kall[...] * q, e_ref[...], preferred_element_type=F32, precision=HIGHEST)
            sc = sc + bias_ref[gi, s]
            mx = jnp.max(sc, axis=0, keepdims=True)
            p = jnp.exp(sc - mx)
            den = jnp.sum(p, axis=0, keepdims=True)
            pe = jnp.dot(p, et_ref[...], preferred_element_type=F32, precision=HIGHEST)
            num = jnp.sum(pe * vall[...], axis=0, keepdims=True)
            o_ref[0, s:s + 1, gi * GW:(gi + 1) * GW] = num / jnp.sum(pe, axis=0, keepdims=True)
            lse_ref[0, s:s + 1, gi * LANES:(gi + 1) * LANES] = mx + jnp.log(den)


def _attn_sample(att, caches):
    b, s_new, c = att.shape
    assert s_new <= min(SUBLANES, ATT_DILS[1])
    new = jnp.pad(att, ((0, 0), (0, SUBLANES - s_new), (0, 0)))
    bufs, specs = [], []
    for dil, cache in zip(ATT_DILS, caches):
        assert cache.shape[1] == dil * STEPS
        bufs.append(cache.reshape(b, STEPS, dil * 2 * GW))
        specs.append(pl.BlockSpec((1, STEPS, min(dil, s_new) * 2 * GW), lambda bi: (bi, 0, 0)))
    e, et = _head_indicator()
    ng = len(ATT_DILS)
    o, lse = pl.pallas_call(
        functools.partial(_attn_sample_body, s_new=s_new),
        grid=(b,),
        in_specs=[pl.BlockSpec((1, SUBLANES, c), lambda bi: (bi, 0, 0))] + specs
        + [pl.BlockSpec((ng, s_new, KV_ROWS, LANES), lambda bi: (0, 0, 0, 0)),
           pl.BlockSpec((GW, LANES), lambda bi: (0, 0)), pl.BlockSpec((LANES, GW), lambda bi: (0, 0))],
        out_specs=[pl.BlockSpec((1, s_new, D_ATT), lambda bi: (bi, 0, 0)),
                   pl.BlockSpec((1, s_new, ng * LANES), lambda bi: (bi, 0, 0))],
        out_shape=[jax.ShapeDtypeStruct((b, s_new, D_ATT), F32),
                   jax.ShapeDtypeStruct((b, s_new, ng * LANES), F32)],
        scratch_shapes=[pltpu.VMEM((KV_ROWS, GW), F32), pltpu.VMEM((KV_ROWS, GW), F32)],
        compiler_params=pltpu.CompilerParams(dimension_semantics=("parallel",), vmem_limit_bytes=VMEM_LIMIT),
        name="attn_sample",
    )(new, *bufs, _sample_bias_table(s_new), e, et)
    return o.reshape(b * s_new, D_ATT), lse.reshape(b * s_new, ng * LANES)


HIST = SUBLANES


def _bdot(a, b):
    return jnp.dot(a.astype(BF16), b.astype(BF16), preferred_element_type=F32)


def _delta_prep_body(u_ref, prev_ref, small_ref, cbuf_ref, cw_ref, alog_ref, dtb_ref,
                     uo_ref, wq_ref, ak_ref, eg_ref, ext, *, t_valid, cp):
    c = pl.program_id(1)
    rows = cp * CHUNK

    @pl.when(c == 0)
    def _():
        ext[0:HIST, :] = cbuf_ref[0]

    @pl.when(c > 0)
    def _():
        ext[0:HIST, :] = prev_ref[0]

    ext[HIST:HIST + rows, :] = u_ref[0]
    conv = ext[HIST - 3:HIST - 3 + rows, :] * cw_ref[0:1, :]
    for i in range(1, CONV_W):
        conv = conv + ext[HIST - 3 + i:HIST - 3 + i + rows, :] * cw_ref[i:i + 1, :]
    act = conv * _sigmoid(conv)

    sm = small_ref[0]
    bmat = _sigmoid(sm)
    xg = sm + dtb_ref[...]
    gmat = -jnp.exp(alog_ref[...]) * (jnp.maximum(xg, 0.0) + jnp.log1p(jnp.exp(-jnp.abs(xg))))
    if t_valid < rows:
        live = lax.broadcasted_iota(jnp.int32, (rows, 1), 0) < t_valid
        act = jnp.where(live, act, 0.0)
        bmat = jnp.where(live, bmat, 0.0)
        gmat = jnp.where(live, gmat, 0.0)

    ri = lax.broadcasted_iota(jnp.int32, (CHUNK, CHUNK), 0)
    ci = lax.broadcasted_iota(jnp.int32, (CHUNK, CHUNK), 1)
    tril = ri >= ci
    strict = ri > ci

    pairs = [(j, h) for j in range(cp) for h in range(H_D)]
    rsl = lambda j: slice(j * CHUNK, (j + 1) * CHUNK)
    gcs, gcts = [], []
    for j in range(cp):
        gc_j = jnp.dot(tril.astype(F32), gmat[rsl(j)], preferred_element_type=F32, precision=HIGHEST)
        gcs.append(gc_j)
        gcts.append(gc_j.T)
    q = jnp.stack([act[rsl(j), h * DK:(h + 1) * DK] for j, h in pairs])
    k = jnp.stack([act[rsl(j), D_DK + h * DK:D_DK + (h + 1) * DK] for j, h in pairs])
    v = jnp.stack([act[rsl(j), 2 * D_DK + h * DV:2 * D_DK + (h + 1) * DV] for j, h in pairs])
    beta = jnp.stack([bmat[rsl(j), h:h + 1] for j, h in pairs])
    gcol = jnp.stack([gcs[j][:, H_D + h:H_D + h + 1] for j, h in pairs])
    grow = jnp.stack([gcts[j][H_D + h:H_D + h + 1, :] for j, h in pairs])
    glast = jnp.stack([gcs[j][CHUNK - 1:CHUNK, H_D + h:H_D + h + 1] for j, h in pairs])

    qn = q * lax.rsqrt(jnp.sum(q * q, axis=-1, keepdims=True) + NORM_EPS) * (DK ** -0.5)
    kn = k * lax.rsqrt(jnp.sum(k * k, axis=-1, keepdims=True) + NORM_EPS)
    decay = jnp.where(tril, jnp.exp(jnp.where(tril, gcol - grow, 0.0)), 0.0)
    eg = jnp.exp(gcol)
    kb = kn * beta
    kn_b = kn.astype(BF16)
    a_kk = jnp.einsum('bik,bjk->bij', kb.astype(BF16), kn_b, preferred_element_type=F32)
    a_qk = jnp.einsum('bik,bjk->bij', qn.astype(BF16), kn_b, preferred_element_type=F32)
    pw = jnp.where(strict, -(a_kk * decay), 0.0)
    tm1 = pw
    for _ in range(int(math.log2(CHUNK)) - 1):
        pw_b = pw.astype(BF16)
        pw = jnp.einsum('bij,bjk->bik', pw_b, pw_b, preferred_element_type=F32)
        tm1 = tm1 + pw + jnp.einsum('bij,bjk->bik', tm1.astype(BF16), pw.astype(BF16), preferred_element_type=F32)
    vb = v * beta
    kbg = kb * eg
    tm1_b = tm1.astype(BF16)
    u = vb + jnp.einsum('bij,bjk->bik', tm1_b, vb.astype(BF16), preferred_element_type=F32)
    w = (kbg + jnp.einsum('bij,bjk->bik', tm1_b, kbg.astype(BF16), preferred_element_type=F32)).astype(BF16)
    qd = (qn * eg).astype(BF16)
    aqd = (a_qk * decay).astype(BF16)
    kd = kn * jnp.exp(glast - gcol)
    egl = jnp.exp(glast)
    for i, (j, h) in enumerate(pairs):
        uo_ref[0, rsl(j), h * DV:(h + 1) * DV] = u[i]
        wq_ref[0, j, h, 0:CHUNK, :] = w[i]
        wq_ref[0, j, h, CHUNK:2 * CHUNK, :] = qd[i]
        ak_ref[0, j, h, 0:CHUNK, :] = aqd[i]
        ak_ref[0, j, h, CHUNK:CHUNK + DK, :] = kd[i].T.astype(BF16)
        eg_ref[0, j, h:h + 1, :] = jnp.broadcast_to(egl[i], (1, LANES))
    for j in range(cp):
        eg_ref[0, j, H_D:SUBLANES, :] = jnp.zeros((SUBLANES - H_D, LANES), F32)


def _delta_rec_body(u_ref, wq_ref, ak_ref, eg_ref, z_ref, s0_ref, nw_ref, o_ref, sout_ref, S, *, bb, nc):
    c = pl.program_id(1)

    @pl.when(c == 0)
    def _():
        S[...] = s0_ref[...]

    for b in range(bb):
        for h in range(H_D):
            cs = slice(h * DV, (h + 1) * DV)
            s_h = S[b, h]
            x = jnp.dot(wq_ref[b, 0, h], s_h.astype(BF16), preferred_element_type=F32)
            vn = u_ref[b, :, cs] - x[0:CHUNK]
            y = jnp.dot(ak_ref[b, 0, h], vn.astype(BF16), preferred_element_type=F32)
            o = x[CHUNK:2 * CHUNK] + y[0:CHUNK]
            S[b, h] = s_h * eg_ref[b, 0, h:h + 1, :] + y[CHUNK:CHUNK + DK]
            od = o * lax.rsqrt(jnp.mean(o * o, axis=-1, keepdims=True) + NORM_EPS) * nw_ref[...]
            zh = z_ref[b, :, cs]
            o_ref[b, :, cs] = od * (zh * _sigmoid(zh))

    @pl.when(c == nc - 1)
    def _():
        sout_ref[...] = S[...]


def _lane_row(vals, offset):
    row = jnp.zeros((1, LANES), F32)
    return row.at[0, offset:offset + vals.shape[0]].set(vals.astype(F32))


def _delta_net(dqkv, small, z, conv_buf, s0, conv_w, a_log, dt_bias, delta_norm_w):
    b, t, _ = dqkv.shape
    tp = -(-t // CHUNK) * CHUNK
    nc = tp // CHUNK
    assert tp == t or nc == 1
    if tp != t:
        pad = ((0, 0), (0, tp - t), (0, 0))
        dqkv, small, z = jnp.pad(dqkv, pad), jnp.pad(small, pad), jnp.pad(z, pad)
    cbuf = jnp.pad(conv_buf, ((0, 0), (HIST - (CONV_W - 1), 0), (0, 0)))
    cw = jnp.pad(conv_w, ((0, SUBLANES - CONV_W), (0, 0)))
    cp = 2 if nc % 2 == 0 else 1
    rows = cp * CHUNK
    full2 = lambda shape: pl.BlockSpec(shape, lambda bi, ci: (0,) * len(shape))
    u, wq, ak, eg = pl.pallas_call(
        functools.partial(_delta_prep_body, t_valid=t if nc == 1 else rows, cp=cp),
        grid=(b, nc // cp),
        in_specs=[pl.BlockSpec((1, rows, D_CONV), lambda bi, ci: (bi, ci, 0)),
                  pl.BlockSpec((1, HIST, D_CONV), lambda bi, ci: (bi, jnp.maximum(ci * (rows // HIST) - 1, 0), 0)),
                  pl.BlockSpec((1, rows, LANES), lambda bi, ci: (bi, ci, 0)),
                  pl.BlockSpec((1, HIST, D_CONV), lambda bi, ci: (bi, 0, 0)),
                  full2((SUBLANES, D_CONV)), full2((1, LANES)), full2((1, LANES))],
        out_specs=[pl.BlockSpec((1, rows, D_DV), lambda bi, ci: (bi, ci, 0)),
                   pl.BlockSpec((1, cp, H_D, 2 * CHUNK, DK), lambda bi, ci: (bi, ci, 0, 0, 0)),
                   pl.BlockSpec((1, cp, H_D, CHUNK + DK, CHUNK), lambda bi, ci: (bi, ci, 0, 0, 0)),
                   pl.BlockSpec((1, cp, SUBLANES, LANES), lambda bi, ci: (bi, ci, 0, 0))],
        out_shape=[jax.ShapeDtypeStruct((b, tp, D_DV), F32),
                   jax.ShapeDtypeStruct((b, nc, H_D, 2 * CHUNK, DK), BF16),
                   jax.ShapeDtypeStruct((b, nc, H_D, CHUNK + DK, CHUNK), BF16),
                   jax.ShapeDtypeStruct((b, nc, SUBLANES, LANES), F32)],
        scratch_shapes=[pltpu.VMEM((HIST + rows, D_CONV), F32)],
        compiler_params=pltpu.CompilerParams(dimension_semantics=("parallel", "parallel"),
                                             vmem_limit_bytes=VMEM_LIMIT),
        name="delta_prep",
    )(dqkv, dqkv, small, cbuf, cw, _lane_row(a_log, H_D), _lane_row(dt_bias, H_D))

    bb = 4 if b % 4 == 0 else 1
    o, s_new = pl.pallas_call(
        functools.partial(_delta_rec_body, bb=bb, nc=nc),
        grid=(b // bb, nc),
        in_specs=[pl.BlockSpec((bb, CHUNK, D_DV), lambda bi, ci: (bi, ci, 0)),
                  pl.BlockSpec((bb, 1, H_D, 2 * CHUNK, DK), lambda bi, ci: (bi, ci, 0, 0, 0)),
                  pl.BlockSpec((bb, 1, H_D, CHUNK + DK, CHUNK), lambda bi, ci: (bi, ci, 0, 0, 0)),
                  pl.BlockSpec((bb, 1, SUBLANES, LANES), lambda bi, ci: (bi, ci, 0, 0)),
                  pl.BlockSpec((bb, CHUNK, D_DV), lambda bi, ci: (bi, ci, 0)),
                  pl.BlockSpec((bb, H_D, DK, DV), lambda bi, ci: (bi, 0, 0, 0)),
                  full2((1, DV))],
        out_specs=[pl.BlockSpec((bb, CHUNK, D_DV), lambda bi, ci: (bi, ci, 0)),
                   pl.BlockSpec((bb, H_D, DK, DV), lambda bi, ci: (bi, 0, 0, 0))],
        out_shape=[jax.ShapeDtypeStruct((b, tp, D_DV), F32), jax.ShapeDtypeStruct((b, H_D, DK, DV), F32)],
        scratch_shapes=[pltpu.VMEM((bb, H_D, DK, DV), F32)],
        compiler_params=pltpu.CompilerParams(dimension_semantics=("parallel", "arbitrary"),
                                             vmem_limit_bytes=VMEM_LIMIT),
        name="delta_recurrence",
    )(u, wq, ak, eg, z, s0, delta_norm_w.reshape(1, DV).astype(F32))
    return o[:, :t].reshape(b * t, D_DV), s_new


def _lane_expand(cols, width):
    tm = cols[0].shape[0]
    seg = lax.broadcasted_iota(jnp.int32, (tm, len(cols) * width), 1) // width
    out = jnp.broadcast_to(cols[-1], seg.shape)
    for i in range(len(cols) - 2, -1, -1):
        out = jnp.where(seg == i, cols[i], out)
    return out


def _mix_body(x_ref, gate_ref, o0_ref, o1_ref, o2_ref, l0_ref, l1_ref, l2_ref, od_ref,
              wa_ref, wd_ref, wo_ref, bg_ref, g1_ref, b1_ref, rw_ref, rb_ref,
              h_ref, ti_ref, tg_ref):
    lses = (l0_ref[...], l1_ref[...], l2_ref[...])
    mx = jnp.maximum(jnp.maximum(lses[0], lses[1]), lses[2])
    es = [jnp.exp(l - mx) for l in lses]
    inv = 1.0 / (es[0] + es[1] + es[2])
    o_att = None
    for e_g, o_g in zip(es, (o0_ref, o1_ref, o2_ref)):
        wgt = e_g * inv
        term = _lane_expand([wgt[:, h:h + 1] for h in range(H_G)], HD) * o_g[...]
        o_att = term if o_att is None else o_att + term
    ga = _sigmoid(gate_ref[:, :D_MODEL] + bg_ref[:, :D_MODEL])
    gd = _sigmoid(gate_ref[:, D_MODEL:] + bg_ref[:, D_MODEL:])
    merged = (ga * jnp.dot(o_att.astype(BF16), wa_ref[...], preferred_element_type=F32)
              + gd * jnp.dot(od_ref[...].astype(BF16), wd_ref[...], preferred_element_type=F32))
    mix = jnp.dot(merged.astype(BF16), wo_ref[...], preferred_element_type=F32)
    h = _layer_norm(DN_ALPHA * x_ref[...] + mix, g1_ref[...], b1_ref[...])
    h_ref[...] = h

    logits = jnp.dot(h, rw_ref[...], preferred_element_type=F32, precision=HIGHEST) + rb_ref[...]
    lane = lax.broadcasted_iota(jnp.int32, logits.shape, 1)
    lane_f = lane.astype(F32)
    ti = jnp.zeros(logits.shape, F32)
    tv = jnp.zeros(logits.shape, F32)
    top = None
    for kk in range(TOP_K):
        m = jnp.max(logits, axis=-1, keepdims=True)
        idx = jnp.min(jnp.where(logits == m, lane_f, float(LANES)), axis=-1, keepdims=True)
        if top is None:
            top = m
        ti = jnp.where(lane == kk, idx, ti)
        tv = jnp.where(lane == kk, jnp.exp(m - top), tv)
        logits = jnp.where(lane_f == idx, -jnp.inf, logits)
    ti_ref[...] = ti.astype(jnp.int32)
    tg_ref[...] = tv / jnp.sum(tv, axis=-1, keepdims=True)


def _mix_and_route(x2d, gates, o_groups, lse_groups, o_del, wa, wd, wo, b_gate, ln_g, ln_b, rw, rb, tm):
    n = x2d.shape[0]
    row = lambda c: pl.BlockSpec((tm, c), lambda i: (i, 0))
    full = lambda a: pl.BlockSpec(a.shape, lambda i: (0, 0))
    consts = (wa, wd, wo, b_gate, ln_g, ln_b, rw, rb)
    return pl.pallas_call(
        _mix_body,
        grid=(n // tm,),
        in_specs=[row(D_MODEL), row(2 * D_MODEL), row(GW), row(GW), row(GW), row(LANES), row(LANES), row(LANES),
                  row(D_DV)] + [full(a) for a in consts],
        out_specs=[row(D_MODEL), row(LANES), row(LANES)],
        out_shape=[jax.ShapeDtypeStruct((n, D_MODEL), F32), jax.ShapeDtypeStruct((n, LANES), jnp.int32),
                   jax.ShapeDtypeStruct((n, LANES), F32)],
        compiler_params=pltpu.CompilerParams(dimension_semantics=("parallel",), vmem_limit_bytes=VMEM_LIMIT),
        name="mix_and_route",
    )(x2d, gates, *o_groups, *lse_groups, o_del, *consts)


def _rank_body(ti_ref, rank_ref, cnt_ref, carry):
    i = pl.program_id(0)

    @pl.when(i == 0)
    def _():
        carry[...] = jnp.zeros_like(carry)

    ti = ti_ref[...]
    tm = ti.shape[0]
    lane = lax.broadcasted_iota(jnp.int32, ti.shape, 1)
    sel = [lane == ti[:, kk:kk + 1] for kk in range(TOP_K)]
    hit = sel[0]
    for kk in range(1, TOP_K):
        hit = hit | sel[kk]
    cnt = hit.astype(F32)
    ri = lax.broadcasted_iota(jnp.int32, (tm, tm), 0)
    ci = lax.broadcasted_iota(jnp.int32, (tm, tm), 1)
    incl = jnp.dot((ri >= ci).astype(BF16), cnt.astype(BF16), preferred_element_type=F32)
    before = incl - cnt + carry[...]
    rank = jnp.zeros(ti.shape, jnp.int32)
    for kk in range(TOP_K):
        r = jnp.sum(jnp.where(sel[kk], before, 0.0), axis=-1, keepdims=True)
        rank = jnp.where(lane == kk, r.astype(jnp.int32), rank)
    rank_ref[...] = rank
    carry[...] = carry[...] + incl[tm - 1:tm, :]
    cnt_ref[...] = carry[...]


def _route_ranks(top_i, tm):
    n = top_i.shape[0]
    return pl.pallas_call(
        _rank_body,
        grid=(n // tm,),
        in_specs=[pl.BlockSpec((tm, LANES), lambda i: (i, 0))],
        out_specs=[pl.BlockSpec((tm, LANES), lambda i: (i, 0)), pl.BlockSpec((1, LANES), lambda i: (0, 0))],
        out_shape=[jax.ShapeDtypeStruct((n, LANES), jnp.int32), jax.ShapeDtypeStruct((1, LANES), F32)],
        scratch_shapes=[pltpu.VMEM((1, LANES), F32)],
        compiler_params=pltpu.CompilerParams(dimension_semantics=("arbitrary",), vmem_limit_bytes=VMEM_LIMIT),
        name="route_ranks",
    )(top_i)


def _dispatch_body(dest_ref, h_ref, xs_in, xs_out, sem):
    del xs_in
    tm = h_ref.shape[0]

    def issue(t, carry):
        for kk in range(TOP_K):
            d = dest_ref[t * TOP_K + kk]
            pltpu.make_async_copy(h_ref.at[pl.ds(t, 1)], xs_out.at[pl.ds(d, 1)], sem).start()
        return carry

    lax.fori_loop(0, tm, issue, 0)
    for kk in range(TOP_K):
        pltpu.make_async_copy(h_ref, xs_out.at[pl.ds(0, tm)], sem).wait()


def _dispatch(h, dest_flat, n_slot, tm):
    n = h.shape[0]
    xs0 = jnp.zeros((n_slot, D_MODEL), F32)
    return pl.pallas_call(
        _dispatch_body,
        grid=(n // tm,),
        in_specs=[pl.BlockSpec((tm * TOP_K,), lambda i: (i,), memory_space=pltpu.SMEM),
                  pl.BlockSpec((tm, D_MODEL), lambda i: (i, 0)),
                  pl.BlockSpec(memory_space=pl.ANY)],
        out_specs=pl.BlockSpec(memory_space=pl.ANY),
        out_shape=jax.ShapeDtypeStruct((n_slot, D_MODEL), F32),
        scratch_shapes=[pltpu.SemaphoreType.DMA(())],
        input_output_aliases={2: 0},
        compiler_params=pltpu.CompilerParams(dimension_semantics=("arbitrary",), vmem_limit_bytes=VMEM_LIMIT,
                                             has_side_effects=True),
        name="moe_dispatch",
    )(dest_flat, h, xs0)


def _expert_body(be_ref, nu_ref, x_ref, wgu_ref, bgu_ref, wd_ref, bd_ref, o_ref, wgu_bf, wd_bf):
    i = pl.program_id(0)
    e = be_ref[i]
    prev = be_ref[jnp.maximum(i - 1, 0)]

    @pl.when((i == 0) | (e != prev))
    def _():
        wgu_bf[...] = wgu_ref[0].astype(BF16)
        wd_bf[...] = wd_ref[0].astype(BF16)

    @pl.when(i < nu_ref[0])
    def _():
        gu = jnp.dot(x_ref[...].astype(BF16), wgu_bf[...], preferred_element_type=F32) + bgu_ref[0]
        gt = jnp.minimum(gu[:, :D_FF], SWIGLU_LIMIT)
        up = jnp.clip(gu[:, D_FF:], -SWIGLU_LIMIT, SWIGLU_LIMIT)
        act = (up + 1.0) * gt * _sigmoid(SWIGLU_ALPHA * gt)
        o_ref[...] = jnp.dot(act.astype(BF16), wd_bf[...], preferred_element_type=F32) + bd_ref[0]

    @pl.when(i >= nu_ref[0])
    def _():
        o_ref[...] = jnp.zeros_like(o_ref)


def _experts(xs, blk_exp, n_used, w_gu, b_gu, w_down, b_down):
    n_slot = xs.shape[0]
    n_blk = n_slot // MOE_BM
    rows = lambda i, be, nu: (jnp.minimum(i, nu[0] - 1), 0)
    grid_spec = pltpu.PrefetchScalarGridSpec(
        num_scalar_prefetch=2,
        grid=(n_blk,),
        in_specs=[pl.BlockSpec((MOE_BM, D_MODEL), rows),
                  pl.BlockSpec((1, D_MODEL, 2 * D_FF), lambda i, be, nu: (be[i], 0, 0)),
                  pl.BlockSpec((1, 1, 2 * D_FF), lambda i, be, nu: (be[i], 0, 0)),
                  pl.BlockSpec((1, D_FF, D_MODEL), lambda i, be, nu: (be[i], 0, 0)),
                  pl.BlockSpec((1, 1, D_MODEL), lambda i, be, nu: (be[i], 0, 0))],
        out_specs=pl.BlockSpec((MOE_BM, D_MODEL), lambda i, be, nu: (i, 0)),
        scratch_shapes=[pltpu.VMEM((D_MODEL, 2 * D_FF), BF16), pltpu.VMEM((D_FF, D_MODEL), BF16)],
    )
    return pl.pallas_call(
        _expert_body,
        grid_spec=grid_spec,
        out_shape=jax.ShapeDtypeStruct((n_slot, D_MODEL), F32),
        compiler_params=pltpu.CompilerParams(dimension_semantics=("arbitrary",), vmem_limit_bytes=VMEM_LIMIT),
        name="moe_experts",
    )(blk_exp, n_used, xs, w_gu, b_gu.reshape(N_EXP, 1, 2 * D_FF), w_down, b_down.reshape(N_EXP, 1, D_MODEL))


def _combine_body(dest_ref, gate_ref, h_ref, g_ref, b_ref, ys_ref, y_ref, buf, sem):
    tm = h_ref.shape[0]

    def issue(t, carry):
        for kk in range(TOP_K):
            d = dest_ref[t * TOP_K + kk]
            pltpu.make_async_copy(ys_ref.at[pl.ds(d, 1)], buf.at[kk, pl.ds(t, 1)], sem).start()
        return carry

    lax.fori_loop(0, tm, issue, 0)
    for kk in range(TOP_K):
        pltpu.make_async_copy(ys_ref.at[pl.ds(0, tm)], buf.at[kk], sem).wait()
    gate = gate_ref[...]
    moe = gate[:, 0:1] * buf[0]
    for kk in range(1, TOP_K):
        moe = moe + gate[:, kk:kk + 1] * buf[kk]
    y_ref[...] = _layer_norm(DN_ALPHA * h_ref[...] + moe, g_ref[...], b_ref[...])


def _combine(ys, dest_flat, gate, h, ln_g, ln_b, tm):
    n = h.shape[0]
    return pl.pallas_call(
        _combine_body,
        grid=(n // tm,),
        in_specs=[pl.BlockSpec((tm * TOP_K,), lambda i: (i,), memory_space=pltpu.SMEM),
                  pl.BlockSpec((tm, LANES), lambda i: (i, 0)),
                  pl.BlockSpec((tm, D_MODEL), lambda i: (i, 0)),
                  pl.BlockSpec((1, D_MODEL), lambda i: (0, 0)),
                  pl.BlockSpec((1, D_MODEL), lambda i: (0, 0)),
                  pl.BlockSpec(memory_space=pl.ANY)],
        out_specs=pl.BlockSpec((tm, D_MODEL), lambda i: (i, 0)),
        out_shape=jax.ShapeDtypeStruct((n, D_MODEL), F32),
        scratch_shapes=[pltpu.VMEM((TOP_K, tm, D_MODEL), F32), pltpu.SemaphoreType.DMA(())],
        compiler_params=pltpu.CompilerParams(dimension_semantics=("arbitrary",), vmem_limit_bytes=VMEM_LIMIT),
        name="moe_combine",
    )(dest_flat, gate, h, ln_g, ln_b, ys)


def _moe(h, top_i, top_g, w_gu, b_gu, w_down, b_down, ln_g, ln_b, tm):
    n = h.shape[0]
    rank, counts = _route_ranks(top_i, tm)
    counts = counts[0, :N_EXP].astype(jnp.int32)
    padded = (counts + MOE_BM - 1) // MOE_BM * MOE_BM
    pad_end = jnp.cumsum(padded)
    pad_start = pad_end - padded
    n_blk = -(-(n * TOP_K + N_EXP * (MOE_BM - 1)) // MOE_BM)
    dest = (pad_start[top_i[:, :TOP_K]] + rank[:, :TOP_K]).reshape(n * TOP_K)
    blk_row0 = jnp.arange(n_blk, dtype=jnp.int32) * MOE_BM
    blk_exp = jnp.minimum(jnp.sum((pad_end[None, :] <= blk_row0[:, None]).astype(jnp.int32), axis=1), N_EXP - 1)
    n_used = (pad_end[-1:] // MOE_BM).astype(jnp.int32)
    xs = _dispatch(h, dest, n_blk * MOE_BM, tm)
    ys = _experts(xs, blk_exp, n_used, w_gu, b_gu, w_down, b_down)
    return _combine(ys, dest, top_g, h, ln_g, ln_b, tm)


def _pad_cols(a, width, fill=0.0):
    return jnp.pad(a, ((0, 0), (0, width - a.shape[1])), constant_values=fill)


def _decoder_layer(x, caches, conv_buf, s0, w_in, b_gate, conv_w, a_log, dt_bias, delta_norm_w,
                   w_branch_attn, w_branch_delta, w_out, ln1_g, ln1_b, router_w, router_b,
                   w_gu, b_gu, w_down, b_down, ln2_g, ln2_b):
    b, t, _ = x.shape
    n = b * t
    tm = min(256, n)
    x2d = x.reshape(n, D_MODEL)

    c_att, c_dz, c_small = 3 * D_ATT, 3 * D_ATT + D_CONV, 3 * D_ATT + D_CONV + D_DV
    c_gate = c_small + 2 * H_D
    ws = (w_in[:, :c_att], w_in[:, c_att:c_dz], w_in[:, c_dz:c_small],
          _pad_cols(w_in[:, c_small:c_gate], LANES), w_in[:, c_gate:])
    att, dqkv, z, small, gates = _in_projection(x2d, [w.astype(BF16) for w in ws], tm)
    att3 = att.reshape(b, t, 3 * D_ATT)
    dqkv3 = dqkv.reshape(b, t, D_CONV)

    kv_new = []
    if caches is None:
        o_groups, lse_groups = [], []
        for gi, dil in enumerate(ATT_DILS):
            o_g, lse_g = _attn_prompt(att3, gi, dil)
            o_groups.append(o_g)
            lse_groups.append(lse_g)
            win = min(dil * STEPS, t)
            kg = att3[:, t - win:, D_ATT + gi * GW:D_ATT + (gi + 1) * GW].reshape(b, win, H_G, HD)
            vg = att3[:, t - win:, 2 * D_ATT + gi * GW:2 * D_ATT + (gi + 1) * GW].reshape(b, win, H_G, HD)
            kv_new.append(jnp.stack([kg, vg], axis=2))
        conv_buf = jnp.zeros((b, CONV_W - 1, D_CONV), F32)
        s0 = jnp.zeros((b, H_D, DK, DV), F32)
    else:
        o_all, lse_all = _attn_sample(att3, caches)
        o_groups = [o_all[:, gi * GW:(gi + 1) * GW] for gi in range(len(ATT_DILS))]
        lse_groups = [lse_all[:, gi * LANES:(gi + 1) * LANES] for gi in range(len(ATT_DILS))]
        for gi in range(len(ATT_DILS)):
            kg = att3[:, :, D_ATT + gi * GW:D_ATT + (gi + 1) * GW].reshape(b, t, H_G, HD)
            vg = att3[:, :, 2 * D_ATT + gi * GW:2 * D_ATT + (gi + 1) * GW].reshape(b, t, H_G, HD)
            kv_new.append(jnp.stack([kg, vg], axis=2))

    o_del, s_new = _delta_net(dqkv3, small.reshape(b, t, LANES), z.reshape(b, t, D_DV), conv_buf, s0,
                              conv_w, a_log, dt_bias, delta_norm_w)
    conv_new = jnp.concatenate([conv_buf, dqkv3], axis=1)[:, -(CONV_W - 1):] if t < CONV_W - 1 \
        else dqkv3[:, t - (CONV_W - 1):]

    rw = _pad_cols(router_w, LANES)
    rb = _pad_cols(router_b.reshape(1, N_EXP), LANES, fill=NEG)
    h, top_i, top_g = _mix_and_route(
        x2d, gates, o_groups, lse_groups, o_del,
        w_branch_attn.astype(BF16), w_branch_delta.astype(BF16), w_out.astype(BF16),
        b_gate.reshape(1, 2 * D_MODEL), ln1_g.reshape(1, D_MODEL), ln1_b.reshape(1, D_MODEL), rw, rb, tm)
    y = _moe(h, top_i, top_g, w_gu, b_gu, w_down, b_down,
             ln2_g.reshape(1, D_MODEL), ln2_b.reshape(1, D_MODEL), tm)
    return y.reshape(b, t, D_MODEL), kv_new, conv_new, s_new


def kernel(x_prompt, x_sample, cache_kv_w128, cache_kv_w512, cache_kv_w2048, state_conv, state_delta,
           w_in, b_gate, conv_w, a_log, dt_bias, delta_norm_w, w_branch_attn, w_branch_delta, w_out,
           ln1_g, ln1_b, router_w, router_b, w_gu, b_gu, w_down, b_down, ln2_g, ln2_b):
    depth = w_in.shape[0]
    assert depth == 1
    l = 0
    lw = (w_in[l], b_gate[l], conv_w[l], a_log[l], dt_bias[l], delta_norm_w[l], w_branch_attn[l],
          w_branch_delta[l], w_out[l], ln1_g[l], ln1_b[l], router_w[l], router_b[l], w_gu[l], b_gu[l],
          w_down[l], b_down[l], ln2_g[l], ln2_b[l])
    yp, kv_p, cv_p, s_p = _decoder_layer(x_prompt, None, None, None, *lw)
    ys, kv_s, cv_s, s_s = _decoder_layer(x_sample, (cache_kv_w128[l], cache_kv_w512[l], cache_kv_w2048[l]),
                                         state_conv[l], state_delta[l], *lw)
    stk = lambda a: a[None]
    return (yp, ys, stk(kv_p[0]), stk(kv_p[1]), stk(kv_p[2]), stk(cv_p), stk(s_p),
            stk(kv_s[0]), stk(kv_s[1]), stk(kv_s[2]), stk(cv_s), stk(s_s))
```

```python
import functools
import math

import numpy as np
import jax
import jax.numpy as jnp
from jax import lax
from jax.experimental import pallas as pl
from jax.experimental.pallas import tpu as pltpu

F32 = jnp.float32
BF16 = jnp.bfloat16
HIGHEST = lax.Precision.HIGHEST

D_MODEL = 1024
ATT_DILS = (1, 4, 16)
STEPS = 128
H_G = 4
HD = 64
GW = H_G * HD
D_ATT = len(ATT_DILS) * GW
H_D = 4
DK = 128
DV = 128
D_DK = H_D * DK
D_DV = H_D * DV
D_CONV = 2 * D_DK + D_DV
CONV_W = 4
CHUNK = 64
N_EXP = 32
TOP_K = 4
D_FF = D_MODEL
SWIGLU_LIMIT = 7.0
SWIGLU_ALPHA = 1.702
DN_ALPHA = 2.0 ** 0.25
LN_EPS = 1e-5
NORM_EPS = 1e-6
NEG = -1e30

LANES = 128
SUBLANES = 8
VMEM_LIMIT = 56 * 1024 * 1024
MOE_BM = 256


def _sigmoid(x):
    return 1.0 / (1.0 + jnp.exp(-x))


def _layer_norm(v, g, b):
    mu = jnp.mean(v, axis=-1, keepdims=True)
    d = v - mu
    var = jnp.mean(d * d, axis=-1, keepdims=True)
    return d * lax.rsqrt(var + LN_EPS) * g + b


def _alibi_slopes():
    return 2.0 ** (-8.0 * np.arange(1, H_G + 1, dtype=np.float64) / H_G)


def _inproj_body(x_ref, *refs, dils):
    nw = len(dils)
    scr = refs[2 * nw] if len(refs) > 2 * nw else None
    x = x_ref[...].astype(BF16)
    for w_ref, o_ref, d in zip(refs[:nw], refs[nw:2 * nw], dils):
        y = jnp.dot(x, w_ref[...], preferred_element_type=F32)
        if d == 1:
            o_ref[...] = y
        else:
            for c in range(y.shape[1] // LANES):
                scr[c] = y[:, c * LANES:(c + 1) * LANES]
            for r in range(d):
                for c in range(y.shape[1] // LANES):
                    o_ref[0, r, :, c * LANES:(c + 1) * LANES] = scr[c, pl.ds(r, y.shape[0] // d, stride=d), :]


def _in_projection(x2d, ws, dils, t, tm):
    n = x2d.shape[0]
    tpb = t // tm
    out_specs, out_shape = [], []
    for w, d in zip(ws, dils):
        c = w.shape[1]
        if d == 1:
            out_specs.append(pl.BlockSpec((tm, c), lambda i: (i, 0)))
            out_shape.append(jax.ShapeDtypeStruct((n, c), F32))
        else:
            assert tm % (d * SUBLANES) == 0 and t % tm == 0
            out_specs.append(pl.BlockSpec((1, d, tm // d, c), lambda i: (i // tpb, 0, i % tpb, 0)))
            out_shape.append(jax.ShapeDtypeStruct((n // t, d, t // d, c), F32))
    wide = max([w.shape[1] for w, d in zip(ws, dils) if d > 1], default=0)
    return pl.pallas_call(
        functools.partial(_inproj_body, dils=tuple(dils)),
        grid=(n // tm,),
        in_specs=[pl.BlockSpec((tm, D_MODEL), lambda i: (i, 0))]
        + [pl.BlockSpec((D_MODEL, w.shape[1]), lambda i: (0, 0), pipeline_mode=pl.Buffered(1)) for w in ws],
        out_specs=out_specs,
        out_shape=out_shape,
        scratch_shapes=[pltpu.VMEM((wide // LANES, tm, LANES), F32)] if wide else [],
        compiler_params=pltpu.CompilerParams(dimension_semantics=("parallel",), vmem_limit_bytes=VMEM_LIMIT),
        name="in_projection",
    )(x2d, *ws)


def _prompt_bias_table(dil):
    qi = np.arange(STEPS)[:, None]
    kj = np.arange(2 * STEPS)[None, :]
    steps = qi + STEPS - kj
    valid = (steps >= 0) & (steps <= STEPS)
    slopes = _alibi_slopes()
    bias = -slopes[:, None, None] * (dil * steps)[None].astype(np.float64)
    later = np.where(valid[None], bias, NEG)
    first = np.where((valid & (kj >= STEPS))[None], bias, NEG)
    return jnp.asarray(np.stack([first, later]), F32)


def _attn_prompt_body(q_ref, kp_ref, kc_ref, vp_ref, vc_ref, bias_ref, o_ref, lse_ref):
    later = jnp.minimum(pl.program_id(2), 1)
    q = q_ref[0, 0] * (HD ** -0.5)
    kk = jnp.concatenate([kp_ref[0, 0], kc_ref[0, 0]], axis=0).astype(BF16)
    vv = jnp.concatenate([vp_ref[0, 0], vc_ref[0, 0]], axis=0).astype(BF16)
    lse_ref[0, 0] = jnp.zeros((STEPS, LANES), F32)
    for h in range(H_G):
        cs = slice(h * HD, (h + 1) * HD)
        s = lax.dot_general(q[:, cs].astype(BF16), kk[:, cs], (((1,), (1,)), ((), ())),
                            preferred_element_type=F32)
        s = s + bias_ref[later, h]
        mx = jnp.max(s, axis=-1, keepdims=True)
        p = jnp.exp(s - mx)
        den = jnp.sum(p, axis=-1, keepdims=True)
        o = jnp.dot(p.astype(BF16), vv[:, cs], preferred_element_type=F32)
        o_ref[0, 0, :, cs] = o / den
        lse_ref[0, 0, :, h:h + 1] = mx + jnp.log(den)


def _attn_prompt(a, dil):
    b, d, L, c = a.shape
    assert d == dil and c == 3 * GW and L % STEPS == 0
    nb = L // STEPS

    def spec(col, prev):
        if prev:
            return pl.BlockSpec((1, 1, STEPS, GW), lambda bi, r, n: (bi, r, jnp.maximum(n - 1, 0), col))
        return pl.BlockSpec((1, 1, STEPS, GW), lambda bi, r, n: (bi, r, n, col))

    return pl.pallas_call(
        _attn_prompt_body,
        grid=(b, dil, nb),
        in_specs=[spec(0, False), spec(1, True), spec(1, False), spec(2, True), spec(2, False),
                  pl.BlockSpec((2, H_G, STEPS, 2 * STEPS), lambda bi, r, n: (0, 0, 0, 0))],
        out_specs=[pl.BlockSpec((1, 1, STEPS, GW), lambda bi, r, n: (bi, r, n, 0)),
                   pl.BlockSpec((1, 1, STEPS, LANES), lambda bi, r, n: (bi, r, n, 0))],
        out_shape=[jax.ShapeDtypeStruct((b, dil, L, GW), F32),
                   jax.ShapeDtypeStruct((b, dil, L, LANES), F32)],
        compiler_params=pltpu.CompilerParams(dimension_semantics=("parallel", "parallel", "arbitrary"),
                                             vmem_limit_bytes=VMEM_LIMIT),
        name=f"attn_prompt_d{dil}",
    )(a, a, a, a, a, _prompt_bias_table(dil))


def _sample_bias_tables(s_new):
    slopes = _alibi_slopes()[:, None, None]
    s = np.arange(SUBLANES)[:, None]
    live = s < s_new
    cached = []
    for dil in ATT_DILS:
        lw = dil * STEPS
        dist = lw + s - np.arange(lw)[None, :]
        ok = live & (dist % dil == 0) & (dist <= lw)
        cached.append(jnp.asarray(np.where(ok[None], -slopes * dist[None], NEG), F32))
    s2 = np.arange(SUBLANES)[None, :]
    dist = s - s2
    new = []
    for dil in ATT_DILS:
        ok = live & (s2 < s_new) & (dist >= 0) & (dist % dil == 0)
        new.append(np.where(ok[None], -slopes * dist[None], NEG))
    return cached, jnp.asarray(np.stack(new), F32)


def _attn_sample_body(new_ref, c0_ref, c1_ref, c2_ref, bw0_ref, bw1_ref, bw2_ref, bn_ref, o_ref, lse_ref, *, s_new):
    nt = (((1,), (1,)), ((), ()))
    lse_ref[0] = jnp.zeros(lse_ref.shape[1:], F32)
    for gi, (cache, bw_ref) in enumerate(zip((c0_ref, c1_ref, c2_ref), (bw0_ref, bw1_ref, bw2_ref))):
        for h in range(H_G):
            cs = slice(gi * GW + h * HD, gi * GW + (h + 1) * HD)
            c0 = gi * 3 * GW + h * HD
            q = new_ref[0, :, c0:c0 + HD] * (HD ** -0.5)
            k_new = new_ref[0, :, GW + c0:GW + c0 + HD]
            v_new = new_ref[0, :, 2 * GW + c0:2 * GW + c0 + HD]
            kt = cache[0, 0, h].astype(BF16)
            vt = cache[0, 1, h].astype(BF16)
            sw = jnp.dot(q.astype(BF16), kt, preferred_element_type=F32) + bw_ref[h]
            sn = lax.dot_general(q, k_new, nt, preferred_element_type=F32, precision=HIGHEST) + bn_ref[gi, h]
            mx = jnp.maximum(jnp.max(sw, axis=-1, keepdims=True), jnp.max(sn, axis=-1, keepdims=True))
            pw = jnp.exp(sw - mx)
            pn = jnp.exp(sn - mx)
            den = jnp.sum(pw, axis=-1, keepdims=True) + jnp.sum(pn, axis=-1, keepdims=True)
            o = (lax.dot_general(pw.astype(BF16), vt, nt, preferred_element_type=F32)
                 + jnp.dot(pn, v_new, preferred_element_type=F32, precision=HIGHEST)) / den
            o_ref[0, :, cs] = o[0:s_new]
            lse_ref[0, :, gi * LANES + h:gi * LANES + h + 1] = (mx + jnp.log(den))[0:s_new]


def _attn_sample(att, caches):
    b, s_new, c = att.shape
    assert s_new <= min(SUBLANES, ATT_DILS[1])
    new = jnp.pad(att, ((0, 0), (0, SUBLANES - s_new), (0, 0)))
    bufs, specs = [], []
    for dil, cache in zip(ATT_DILS, caches):
        lw = cache.shape[1]
        assert lw == dil * STEPS
        bufs.append(jnp.transpose(cache, (0, 2, 3, 4, 1)))
        specs.append(pl.BlockSpec((1, 2, H_G, HD, lw), lambda bi: (bi, 0, 0, 0, 0)))
    bias_cached, bias_new = _sample_bias_tables(s_new)
    ng = len(ATT_DILS)
    o, lse = pl.pallas_call(
        functools.partial(_attn_sample_body, s_new=s_new),
        grid=(b,),
        in_specs=[pl.BlockSpec((1, SUBLANES, c), lambda bi: (bi, 0, 0))] + specs
        + [pl.BlockSpec(bc.shape, lambda bi: (0, 0, 0)) for bc in bias_cached]
        + [pl.BlockSpec(bias_new.shape, lambda bi: (0, 0, 0, 0))],
        out_specs=[pl.BlockSpec((1, s_new, D_ATT), lambda bi: (bi, 0, 0)),
                   pl.BlockSpec((1, s_new, ng * LANES), lambda bi: (bi, 0, 0))],
        out_shape=[jax.ShapeDtypeStruct((b, s_new, D_ATT), F32),
                   jax.ShapeDtypeStruct((b, s_new, ng * LANES), F32)],
        compiler_params=pltpu.CompilerParams(dimension_semantics=("parallel",), vmem_limit_bytes=VMEM_LIMIT),
        name="attn_sample",
    )(new, *bufs, *bias_cached, bias_new)
    return o.reshape(b * s_new, D_ATT), lse.reshape(b * s_new, ng * LANES)


HIST = SUBLANES


def _bdot(a, b):
    return jnp.dot(a.astype(BF16), b.astype(BF16), preferred_element_type=F32)


def _delta_prep_body(u_ref, prev_ref, small_ref, cbuf_ref, cw_ref, alog_ref, dtb_ref,
                     uo_ref, wq_ref, ak_ref, eg_ref, ext, *, t_valid, cp):
    c = pl.program_id(1)
    rows = cp * CHUNK

    @pl.when(c == 0)
    def _():
        ext[0:HIST, :] = cbuf_ref[0]

    @pl.when(c > 0)
    def _():
        ext[0:HIST, :] = prev_ref[0]

    ext[HIST:HIST + rows, :] = u_ref[0]
    conv = ext[HIST - 3:HIST - 3 + rows, :] * cw_ref[0:1, :]
    for i in range(1, CONV_W):
        conv = conv + ext[HIST - 3 + i:HIST - 3 + i + rows, :] * cw_ref[i:i + 1, :]
    act = conv * _sigmoid(conv)

    sm = small_ref[0]
    bmat = _sigmoid(sm)
    xg = sm + dtb_ref[...]
    gmat = -jnp.exp(alog_ref[...]) * (jnp.maximum(xg, 0.0) + jnp.log1p(jnp.exp(-jnp.abs(xg))))
    if t_valid < rows:
        live = lax.broadcasted_iota(jnp.int32, (rows, 1), 0) < t_valid
        act = jnp.where(live, act, 0.0)
        bmat = jnp.where(live, bmat, 0.0)
        gmat = jnp.where(live, gmat, 0.0)

    ri = lax.broadcasted_iota(jnp.int32, (CHUNK, CHUNK), 0)
    ci = lax.broadcasted_iota(jnp.int32, (CHUNK, CHUNK), 1)
    tril = ri >= ci
    strict = ri > ci

    pairs = [(j, h) for j in range(cp) for h in range(H_D)]
    rsl = lambda j: slice(j * CHUNK, (j + 1) * CHUNK)
    gcs, gcts = [], []
    for j in range(cp):
        gc_j = jnp.dot(tril.astype(F32), gmat[rsl(j)], preferred_element_type=F32, precision=HIGHEST)
        gcs.append(gc_j)
        gcts.append(gc_j.T)
    q = jnp.stack([act[rsl(j), h * DK:(h + 1) * DK] for j, h in pairs])
    k = jnp.stack([act[rsl(j), D_DK + h * DK:D_DK + (h + 1) * DK] for j, h in pairs])
    v = jnp.stack([act[rsl(j), 2 * D_DK + h * DV:2 * D_DK + (h + 1) * DV] for j, h in pairs])
    beta = jnp.stack([bmat[rsl(j), h:h + 1] for j, h in pairs])
    gcol = jnp.stack([gcs[j][:, H_D + h:H_D + h + 1] for j, h in pairs])
    grow = jnp.stack([gcts[j][H_D + h:H_D + h + 1, :] for j, h in pairs])
    glast = jnp.stack([gcs[j][CHUNK - 1:CHUNK, H_D + h:H_D + h + 1] for j, h in pairs])

    qn = q * lax.rsqrt(jnp.sum(q * q, axis=-1, keepdims=True) + NORM_EPS) * (DK ** -0.5)
    kn = k * lax.rsqrt(jnp.sum(k * k, axis=-1, keepdims=True) + NORM_EPS)
    decay = jnp.where(tril, jnp.exp(jnp.where(tril, gcol - grow, 0.0)), 0.0)
    eg = jnp.exp(gcol)
    kb = kn * beta
    kn_b = kn.astype(BF16)
    a_kk = jnp.einsum('bik,bjk->bij', kb.astype(BF16), kn_b, preferred_element_type=F32)
    a_qk = jnp.einsum('bik,bjk->bij', qn.astype(BF16), kn_b, preferred_element_type=F32)
    pw = jnp.where(strict, -(a_kk * decay), 0.0)
    tm1 = pw
    for _ in range(int(math.log2(CHUNK)) - 1):
        pw_b = pw.astype(BF16)
        pw = jnp.einsum('bij,bjk->bik', pw_b, pw_b, preferred_element_type=F32)
        tm1 = tm1 + pw + jnp.einsum('bij,bjk->bik', tm1.astype(BF16), pw.astype(BF16), preferred_element_type=F32)
    vb = v * beta
    kbg = kb * eg
    tm1_b = tm1.astype(BF16)
    u = vb + jnp.einsum('bij,bjk->bik', tm1_b, vb.astype(BF16), preferred_element_type=F32)
    w = (kbg + jnp.einsum('bij,bjk->bik', tm1_b, kbg.astype(BF16), preferred_element_type=F32)).astype(BF16)
    qd = (qn * eg).astype(BF16)
    aqd = (a_qk * decay).astype(BF16)
    kd = kn * jnp.exp(glast - gcol)
    egl = jnp.exp(glast)
    for i, (j, h) in enumerate(pairs):
        uo_ref[0, rsl(j), h * DV:(h + 1) * DV] = u[i]
        wq_ref[0, j, h, 0:CHUNK, :] = w[i]
        wq_ref[0, j, h, CHUNK:2 * CHUNK, :] = qd[i]
        ak_ref[0, j, h, 0:CHUNK, :] = aqd[i]
        ak_ref[0, j, h, CHUNK:CHUNK + DK, :] = kd[i].T.astype(BF16)
        eg_ref[0, j, h:h + 1, :] = jnp.broadcast_to(egl[i], (1, LANES))
    for j in range(cp):
        eg_ref[0, j, H_D:SUBLANES, :] = jnp.zeros((SUBLANES - H_D, LANES), F32)


def _delta_rec_body(u_ref, wq_ref, ak_ref, eg_ref, z_ref, s0_ref, nw_ref, o_ref, sout_ref, S, *, bb, nc):
    c = pl.program_id(1)

    @pl.when(c == 0)
    def _():
        S[...] = s0_ref[...]

    for b in range(bb):
        for h in range(H_D):
            cs = slice(h * DV, (h + 1) * DV)
            s_h = S[b, h]
            x = jnp.dot(wq_ref[b, 0, h], s_h.astype(BF16), preferred_element_type=F32)
            vn = u_ref[b, :, cs] - x[0:CHUNK]
            y = jnp.dot(ak_ref[b, 0, h], vn.astype(BF16), preferred_element_type=F32)
            o = x[CHUNK:2 * CHUNK] + y[0:CHUNK]
            S[b, h] = s_h * eg_ref[b, 0, h:h + 1, :] + y[CHUNK:CHUNK + DK]
            od = o * lax.rsqrt(jnp.mean(o * o, axis=-1, keepdims=True) + NORM_EPS) * nw_ref[...]
            zh = z_ref[b, :, cs]
            o_ref[b, :, cs] = od * (zh * _sigmoid(zh))

    @pl.when(c == nc - 1)
    def _():
        sout_ref[...] = S[...]


def _lane_row(vals, offset):
    row = jnp.zeros((1, LANES), F32)
    return row.at[0, offset:offset + vals.shape[0]].set(vals.astype(F32))


def _delta_net(dqkv, small, z, conv_buf, s0, conv_w, a_log, dt_bias, delta_norm_w):
    b, t, _ = dqkv.shape
    tp = -(-t // CHUNK) * CHUNK
    nc = tp // CHUNK
    assert tp == t or nc == 1
    if tp != t:
        pad = ((0, 0), (0, tp - t), (0, 0))
        dqkv, small, z = jnp.pad(dqkv, pad), jnp.pad(small, pad), jnp.pad(z, pad)
    cbuf = jnp.pad(conv_buf, ((0, 0), (HIST - (CONV_W - 1), 0), (0, 0)))
    cw = jnp.pad(conv_w, ((0, SUBLANES - CONV_W), (0, 0)))
    cp = 2 if nc % 2 == 0 else 1
    rows = cp * CHUNK
    full2 = lambda shape: pl.BlockSpec(shape, lambda bi, ci: (0,) * len(shape))
    u, wq, ak, eg = pl.pallas_call(
        functools.partial(_delta_prep_body, t_valid=t if nc == 1 else rows, cp=cp),
        grid=(b, nc // cp),
        in_specs=[pl.BlockSpec((1, rows, D_CONV), lambda bi, ci: (bi, ci, 0)),
                  pl.BlockSpec((1, HIST, D_CONV), lambda bi, ci: (bi, jnp.maximum(ci * (rows // HIST) - 1, 0), 0)),
                  pl.BlockSpec((1, rows, LANES), lambda bi, ci: (bi, ci, 0)),
                  pl.BlockSpec((1, HIST, D_CONV), lambda bi, ci: (bi, 0, 0)),
                  full2((SUBLANES, D_CONV)), full2((1, LANES)), full2((1, LANES))],
        out_specs=[pl.BlockSpec((1, rows, D_DV), lambda bi, ci: (bi, ci, 0)),
                   pl.BlockSpec((1, cp, H_D, 2 * CHUNK, DK), lambda bi, ci: (bi, ci, 0, 0, 0)),
                   pl.BlockSpec((1, cp, H_D, CHUNK + DK, CHUNK), lambda bi, ci: (bi, ci, 0, 0, 0)),
                   pl.BlockSpec((1, cp, SUBLANES, LANES), lambda bi, ci: (bi, ci, 0, 0))],
        out_shape=[jax.ShapeDtypeStruct((b, tp, D_DV), F32),
                   jax.ShapeDtypeStruct((b, nc, H_D, 2 * CHUNK, DK), BF16),
                   jax.ShapeDtypeStruct((b, nc, H_D, CHUNK + DK, CHUNK), BF16),
                   jax.ShapeDtypeStruct((b, nc, SUBLANES, LANES), F32)],
        scratch_shapes=[pltpu.VMEM((HIST + rows, D_CONV), F32)],
        compiler_params=pltpu.CompilerParams(dimension_semantics=("parallel", "parallel"),
                                             vmem_limit_bytes=VMEM_LIMIT),
        name="delta_prep",
    )(dqkv, dqkv, small, cbuf, cw, _lane_row(a_log, H_D), _lane_row(dt_bias, H_D))

    bb = 4 if b % 4 == 0 else 1
    o, s_new = pl.pallas_call(
        functools.partial(_delta_rec_body, bb=bb, nc=nc),
        grid=(b // bb, nc),
        in_specs=[pl.BlockSpec((bb, CHUNK, D_DV), lambda bi, ci: (bi, ci, 0)),
                  pl.BlockSpec((bb, 1, H_D, 2 * CHUNK, DK), lambda bi, ci: (bi, ci, 0, 0, 0)),
                  pl.BlockSpec((bb, 1, H_D, CHUNK + DK, CHUNK), lambda bi, ci: (bi, ci, 0, 0, 0)),
                  pl.BlockSpec((bb, 1, SUBLANES, LANES), lambda bi, ci: (bi, ci, 0, 0)),
                  pl.BlockSpec((bb, CHUNK, D_DV), lambda bi, ci: (bi, ci, 0)),
                  pl.BlockSpec((bb, H_D, DK, DV), lambda bi, ci: (bi, 0, 0, 0)),
                  full2((1, DV))],
        out_specs=[pl.BlockSpec((bb, CHUNK, D_DV), lambda bi, ci: (bi, ci, 0)),
                   pl.BlockSpec((bb, H_D, DK, DV), lambda bi, ci: (bi, 0, 0, 0))],
        out_shape=[jax.ShapeDtypeStruct((b, tp, D_DV), F32), jax.ShapeDtypeStruct((b, H_D, DK, DV), F32)],
        scratch_shapes=[pltpu.VMEM((bb, H_D, DK, DV), F32)],
        compiler_params=pltpu.CompilerParams(dimension_semantics=("parallel", "arbitrary"),
                                             vmem_limit_bytes=VMEM_LIMIT),
        name="delta_recurrence",
    )(u, wq, ak, eg, z, s0, delta_norm_w.reshape(1, DV).astype(F32))
    return o[:, :t].reshape(b * t, D_DV), s_new


def _lane_expand(cols, width):
    tm = cols[0].shape[0]
    seg = lax.broadcasted_iota(jnp.int32, (tm, len(cols) * width), 1) // width
    out = jnp.broadcast_to(cols[-1], seg.shape)
    for i in range(len(cols) - 2, -1, -1):
        out = jnp.where(seg == i, cols[i], out)
    return out


def _token_order(ref, scr):
    d, nc = ref.shape[1], ref.shape[3] // LANES
    if d == 1:
        return ref[0, 0]
    for r in range(d):
        for c in range(nc):
            scr[c, pl.ds(r, ref.shape[2], stride=d), :] = ref[0, r, :, c * LANES:(c + 1) * LANES]
    return jnp.concatenate([scr[c] for c in range(nc)], axis=-1)


def _mix_body(x_ref, gate_ref, o0_ref, o1_ref, o2_ref, l0_ref, l1_ref, l2_ref, od_ref,
              wa_ref, wd_ref, wo_ref, bg_ref, g1_ref, b1_ref, rw_ref, rb_ref,
              h_ref, ti_ref, tg_ref, *scratch):
    o_scr, l_scr = scratch[:3], scratch[3:]
    lses = [_token_order(l, s) for l, s in zip((l0_ref, l1_ref, l2_ref), l_scr)]
    mx = jnp.maximum(jnp.maximum(lses[0], lses[1]), lses[2])
    es = [jnp.exp(l - mx) for l in lses]
    inv = 1.0 / (es[0] + es[1] + es[2])
    o_att = None
    for e_g, o_g, s in zip(es, (o0_ref, o1_ref, o2_ref), o_scr):
        wgt = e_g * inv
        term = _lane_expand([wgt[:, h:h + 1] for h in range(H_G)], HD) * _token_order(o_g, s)
        o_att = term if o_att is None else o_att + term
    ga = _sigmoid(gate_ref[:, :D_MODEL] + bg_ref[:, :D_MODEL])
    gd = _sigmoid(gate_ref[:, D_MODEL:] + bg_ref[:, D_MODEL:])
    merged = (ga * jnp.dot(o_att.astype(BF16), wa_ref[...], preferred_element_type=F32)
              + gd * jnp.dot(od_ref[...].astype(BF16), wd_ref[...], preferred_element_type=F32))
    mix = jnp.dot(merged.astype(BF16), wo_ref[...], preferred_element_type=F32)
    h = _layer_norm(DN_ALPHA * x_ref[...] + mix, g1_ref[...], b1_ref[...])
    h_ref[...] = h

    logits = jnp.dot(h, rw_ref[...], preferred_element_type=F32, precision=HIGHEST) + rb_ref[...]
    lane = lax.broadcasted_iota(jnp.int32, logits.shape, 1)
    lane_f = lane.astype(F32)
    ti = jnp.zeros(logits.shape, F32)
    tv = jnp.zeros(logits.shape, F32)
    top = None
    for kk in range(TOP_K):
        m = jnp.max(logits, axis=-1, keepdims=True)
        idx = jnp.min(jnp.where(logits == m, lane_f, float(LANES)), axis=-1, keepdims=True)
        if top is None:
            top = m
        ti = jnp.where(lane == kk, idx, ti)
        tv = jnp.where(lane == kk, jnp.exp(m - top), tv)
        logits = jnp.where(lane_f == idx, -jnp.inf, logits)
    ti_ref[...] = ti.astype(jnp.int32)
    tg_ref[...] = tv / jnp.sum(tv, axis=-1, keepdims=True)


def _mix_and_route(x2d, gates, o_groups, lse_groups, o_del, wa, wd, wo, b_gate, ln_g, ln_b, rw, rb, tm):
    n = x2d.shape[0]
    tpb = o_groups[0].shape[1] * o_groups[0].shape[2] // tm
    row = lambda c: pl.BlockSpec((tm, c), lambda i: (i, 0))
    full = lambda a: pl.BlockSpec(a.shape, lambda i: (0, 0))

    def res(a):
        d = a.shape[1]
        return pl.BlockSpec((1, d, tm // d, a.shape[3]), lambda i: (i // tpb, 0, i % tpb, 0))

    consts = (wa, wd, wo, b_gate, ln_g, ln_b, rw, rb)
    return pl.pallas_call(
        _mix_body,
        grid=(n // tm,),
        in_specs=[row(D_MODEL), row(2 * D_MODEL)] + [res(a) for a in o_groups] + [res(a) for a in lse_groups]
        + [row(D_DV)] + [full(a) for a in consts],
        out_specs=[row(D_MODEL), row(LANES), row(LANES)],
        out_shape=[jax.ShapeDtypeStruct((n, D_MODEL), F32), jax.ShapeDtypeStruct((n, LANES), jnp.int32),
                   jax.ShapeDtypeStruct((n, LANES), F32)],
        scratch_shapes=[pltpu.VMEM((GW // LANES, tm, LANES), F32)] * len(o_groups)
        + [pltpu.VMEM((1, tm, LANES), F32)] * len(lse_groups),
        compiler_params=pltpu.CompilerParams(dimension_semantics=("parallel",), vmem_limit_bytes=VMEM_LIMIT),
        name="mix_and_route",
    )(x2d, gates, *o_groups, *lse_groups, o_del, *consts)


def _rank_body(ti_ref, rank_ref, cnt_ref, carry):
    i = pl.program_id(0)

    @pl.when(i == 0)
    def _():
        carry[...] = jnp.zeros_like(carry)

    ti = ti_ref[...]
    tm = ti.shape[0]
    lane = lax.broadcasted_iota(jnp.int32, ti.shape, 1)
    sel = [lane == ti[:, kk:kk + 1] for kk in range(TOP_K)]
    hit = sel[0]
    for kk in range(1, TOP_K):
        hit = hit | sel[kk]
    cnt = hit.astype(F32)
    ri = lax.broadcasted_iota(jnp.int32, (tm, tm), 0)
    ci = lax.broadcasted_iota(jnp.int32, (tm, tm), 1)
    incl = jnp.dot((ri >= ci).astype(BF16), cnt.astype(BF16), preferred_element_type=F32)
    before = incl - cnt + carry[...]
    rank = jnp.zeros(ti.shape, jnp.int32)
    for kk in range(TOP_K):
        r = jnp.sum(jnp.where(sel[kk], before, 0.0), axis=-1, keepdims=True)
        rank = jnp.where(lane == kk, r.astype(jnp.int32), rank)
    rank_ref[...] = rank
    carry[...] = carry[...] + incl[tm - 1:tm, :]
    cnt_ref[...] = carry[...]


def _route_ranks(top_i, tm):
    n = top_i.shape[0]
    return pl.pallas_call(
        _rank_body,
        grid=(n // tm,),
        in_specs=[pl.BlockSpec((tm, LANES), lambda i: (i, 0))],
        out_specs=[pl.BlockSpec((tm, LANES), lambda i: (i, 0)), pl.BlockSpec((1, LANES), lambda i: (0, 0))],
        out_shape=[jax.ShapeDtypeStruct((n, LANES), jnp.int32), jax.ShapeDtypeStruct((1, LANES), F32)],
        scratch_shapes=[pltpu.VMEM((1, LANES), F32)],
        compiler_params=pltpu.CompilerParams(dimension_semantics=("arbitrary",), vmem_limit_bytes=VMEM_LIMIT),
        name="route_ranks",
    )(top_i)


def _dispatch_body(pend_ref, padded_ref, dest_ref, h_ref, xs_out, zero, sem, zsem, *, max_tail):
    tm = h_ref.shape[0]

    @pl.when(pl.program_id(0) == 0)
    def _():
        zero[...] = jnp.zeros_like(zero)

        def last_block(e):
            start = pl.multiple_of(pend_ref[e] - MOE_BM, MOE_BM)
            return pltpu.make_async_copy(zero, xs_out.at[pl.ds(start, MOE_BM)], zsem)

        def tail_block(j):
            start = pl.multiple_of(pend_ref[N_EXP - 1] + j * MOE_BM, MOE_BM)
            return pltpu.make_async_copy(zero, xs_out.at[pl.ds(start, MOE_BM)], zsem)

        n_slot = xs_out.shape[0]
        for e in range(N_EXP):
            @pl.when(padded_ref[e] > 0)
            def _():
                last_block(e).start()
        for j in range(max_tail):
            @pl.when(pend_ref[N_EXP - 1] + j * MOE_BM < n_slot)
            def _():
                tail_block(j).start()
        for e in range(N_EXP):
            @pl.when(padded_ref[e] > 0)
            def _():
                last_block(e).wait()
        for j in range(max_tail):
            @pl.when(pend_ref[N_EXP - 1] + j * MOE_BM < n_slot)
            def _():
                tail_block(j).wait()

    def issue(t, carry):
        for kk in range(TOP_K):
            d = dest_ref[t * TOP_K + kk]
            pltpu.make_async_copy(h_ref.at[pl.ds(t, 1)], xs_out.at[pl.ds(d, 1)], sem).start()
        return carry

    lax.fori_loop(0, tm, issue, 0)
    for kk in range(TOP_K):
        pltpu.make_async_copy(h_ref, xs_out.at[pl.ds(0, tm)], sem).wait()


def _dispatch(h, dest_flat, pad_end, padded, n_slot, tm):
    n = h.shape[0]
    grid_spec = pltpu.PrefetchScalarGridSpec(
        num_scalar_prefetch=2,
        grid=(n // tm,),
        in_specs=[pl.BlockSpec((tm * TOP_K,), lambda i, pe, pd: (i,), memory_space=pltpu.SMEM),
                  pl.BlockSpec((tm, D_MODEL), lambda i, pe, pd: (i, 0))],
        out_specs=pl.BlockSpec(memory_space=pl.ANY),
        scratch_shapes=[pltpu.VMEM((MOE_BM, D_MODEL), F32), pltpu.SemaphoreType.DMA(()), pltpu.SemaphoreType.DMA(())],
    )
    max_tail = n_slot // MOE_BM - (n * TOP_K) // MOE_BM
    return pl.pallas_call(
        functools.partial(_dispatch_body, max_tail=max_tail),
        grid_spec=grid_spec,
        out_shape=jax.ShapeDtypeStruct((n_slot, D_MODEL), F32),
        compiler_params=pltpu.CompilerParams(dimension_semantics=("arbitrary",), vmem_limit_bytes=VMEM_LIMIT,
                                             has_side_effects=True),
        name="moe_dispatch",
    )(pad_end, padded, dest_flat, h)


def _expert_body(be_ref, nu_ref, x_ref, wgu_ref, bgu_ref, wd_ref, bd_ref, o_ref, wgu_bf, wd_bf):
    i = pl.program_id(0)
    e = be_ref[i]
    prev = be_ref[jnp.maximum(i - 1, 0)]

    @pl.when((i == 0) | (e != prev))
    def _():
        wgu_bf[...] = wgu_ref[0].astype(BF16)
        wd_bf[...] = wd_ref[0].astype(BF16)

    @pl.when(i < nu_ref[0])
    def _():
        gu = jnp.dot(x_ref[...].astype(BF16), wgu_bf[...], preferred_element_type=F32) + bgu_ref[0]
        gt = jnp.minimum(gu[:, :D_FF], SWIGLU_LIMIT)
        up = jnp.clip(gu[:, D_FF:], -SWIGLU_LIMIT, SWIGLU_LIMIT)
        act = (up + 1.0) * gt * _sigmoid(SWIGLU_ALPHA * gt)
        o_ref[...] = jnp.dot(act.astype(BF16), wd_bf[...], preferred_element_type=F32) + bd_ref[0]

    @pl.when(i >= nu_ref[0])
    def _():
        o_ref[...] = jnp.zeros_like(o_ref)


def _experts(xs, blk_exp, n_used, w_gu, b_gu, w_down, b_down):
    n_slot = xs.shape[0]
    n_blk = n_slot // MOE_BM
    rows = lambda i, be, nu: (jnp.minimum(i, nu[0] - 1), 0)
    grid_spec = pltpu.PrefetchScalarGridSpec(
        num_scalar_prefetch=2,
        grid=(n_blk,),
        in_specs=[pl.BlockSpec((MOE_BM, D_MODEL), rows),
                  pl.BlockSpec((1, D_MODEL, 2 * D_FF), lambda i, be, nu: (be[i], 0, 0)),
                  pl.BlockSpec((1, 1, 2 * D_FF), lambda i, be, nu: (be[i], 0, 0)),
                  pl.BlockSpec((1, D_FF, D_MODEL), lambda i, be, nu: (be[i], 0, 0)),
                  pl.BlockSpec((1, 1, D_MODEL), lambda i, be, nu: (be[i], 0, 0))],
        out_specs=pl.BlockSpec((MOE_BM, D_MODEL), lambda i, be, nu: (i, 0)),
        scratch_shapes=[pltpu.VMEM((D_MODEL, 2 * D_FF), BF16), pltpu.VMEM((D_FF, D_MODEL), BF16)],
    )
    return pl.pallas_call(
        _expert_body,
        grid_spec=grid_spec,
        out_shape=jax.ShapeDtypeStruct((n_slot, D_MODEL), F32),
        compiler_params=pltpu.CompilerParams(dimension_semantics=("arbitrary",), vmem_limit_bytes=VMEM_LIMIT),
        name="moe_experts",
    )(blk_exp, n_used, xs, w_gu, b_gu.reshape(N_EXP, 1, 2 * D_FF), w_down, b_down.reshape(N_EXP, 1, D_MODEL))


def _combine_body(dest_ref, gate_ref, h_ref, g_ref, b_ref, ys_ref, y_ref, buf, sem):
    tm = h_ref.shape[0]

    def issue(t, carry):
        for kk in range(TOP_K):
            d = dest_ref[t * TOP_K + kk]
            pltpu.make_async_copy(ys_ref.at[pl.ds(d, 1)], buf.at[kk, pl.ds(t, 1)], sem).start()
        return carry

    lax.fori_loop(0, tm, issue, 0)
    for kk in range(TOP_K):
        pltpu.make_async_copy(ys_ref.at[pl.ds(0, tm)], buf.at[kk], sem).wait()
    gate = gate_ref[...]
    moe = gate[:, 0:1] * buf[0]
    for kk in range(1, TOP_K):
        moe = moe + gate[:, kk:kk + 1] * buf[kk]
    y_ref[...] = _layer_norm(DN_ALPHA * h_ref[...] + moe, g_ref[...], b_ref[...])


def _combine(ys, dest_flat, gate, h, ln_g, ln_b, tm):
    n = h.shape[0]
    return pl.pallas_call(
        _combine_body,
        grid=(n // tm,),
        in_specs=[pl.BlockSpec((tm * TOP_K,), lambda i: (i,), memory_space=pltpu.SMEM),
                  pl.BlockSpec((tm, LANES), lambda i: (i, 0)),
                  pl.BlockSpec((tm, D_MODEL), lambda i: (i, 0)),
                  pl.BlockSpec((1, D_MODEL), lambda i: (0, 0)),
                  pl.BlockSpec((1, D_MODEL), lambda i: (0, 0)),
                  pl.BlockSpec(memory_space=pl.ANY)],
        out_specs=pl.BlockSpec((tm, D_MODEL), lambda i: (i, 0)),
        out_shape=jax.ShapeDtypeStruct((n, D_MODEL), F32),
        scratch_shapes=[pltpu.VMEM((TOP_K, tm, D_MODEL), F32), pltpu.SemaphoreType.DMA(())],
        compiler_params=pltpu.CompilerParams(dimension_semantics=("arbitrary",), vmem_limit_bytes=VMEM_LIMIT),
        name="moe_combine",
    )(dest_flat, gate, h, ln_g, ln_b, ys)


def _moe(h, top_i, top_g, w_gu, b_gu, w_down, b_down, ln_g, ln_b, tm):
    n = h.shape[0]
    rank, counts = _route_ranks(top_i, tm)
    counts = counts[0, :N_EXP].astype(jnp.int32)
    padded = (counts + MOE_BM - 1) // MOE_BM * MOE_BM
    pad_end = jnp.cumsum(padded)
    pad_start = pad_end - padded
    n_blk = -(-(n * TOP_K + N_EXP * (MOE_BM - 1)) // MOE_BM)
    dest = (pad_start[top_i[:, :TOP_K]] + rank[:, :TOP_K]).reshape(n * TOP_K)
    blk_row0 = jnp.arange(n_blk, dtype=jnp.int32) * MOE_BM
    blk_exp = jnp.minimum(jnp.sum((pad_end[None, :] <= blk_row0[:, None]).astype(jnp.int32), axis=1), N_EXP - 1)
    n_used = (pad_end[-1:] // MOE_BM).astype(jnp.int32)
    xs = _dispatch(h, dest, pad_end.astype(jnp.int32), padded, n_blk * MOE_BM, tm)
    ys = _experts(xs, blk_exp, n_used, w_gu, b_gu, w_down, b_down)
    return _combine(ys, dest, top_g, h, ln_g, ln_b, tm)


def _pad_cols(a, width, fill=0.0):
    return jnp.pad(a, ((0, 0), (0, width - a.shape[1])), constant_values=fill)


def _decoder_layer(x, caches, conv_buf, s0, w_in, b_gate, conv_w, a_log, dt_bias, delta_norm_w,
                   w_branch_attn, w_branch_delta, w_out, ln1_g, ln1_b, router_w, router_b,
                   w_gu, b_gu, w_down, b_down, ln2_g, ln2_b):
    b, t, _ = x.shape
    n = b * t
    tm = min(256, n)
    x2d = x.reshape(n, D_MODEL)

    c_att, c_dz, c_small = 3 * D_ATT, 3 * D_ATT + D_CONV, 3 * D_ATT + D_CONV + D_DV
    c_gate = c_small + 2 * H_D
    ng = len(ATT_DILS)
    w_groups = [jnp.concatenate([w_in[:, part * D_ATT + gi * GW:part * D_ATT + (gi + 1) * GW] for part in range(3)],
                                axis=1) for gi in range(ng)]
    ws = w_groups + [w_in[:, c_att:c_dz], w_in[:, c_dz:c_small],
                     _pad_cols(w_in[:, c_small:c_gate], LANES), w_in[:, c_gate:]]
    dils = (ATT_DILS if caches is None else (1,) * ng) + (1, 1, 1, 1)
    outs = _in_projection(x2d, [w.astype(BF16) for w in ws], dils, t, tm)
    att_groups, (dqkv, z, small, gates) = outs[:ng], outs[ng:]
    dqkv3 = dqkv.reshape(b, t, D_CONV)

    kv_new = []
    if caches is None:
        o_groups, lse_groups = [], []
        for dil, a in zip(ATT_DILS, att_groups):
            a = a.reshape(b, dil, t // dil, 3 * GW)
            o_g, lse_g = _attn_prompt(a, dil)
            o_groups.append(o_g)
            lse_groups.append(lse_g)
            win = min(dil * STEPS, t)
            last = a[:, :, (t - win) // dil:, GW:]
            last = jnp.transpose(last, (0, 2, 1, 3)).reshape(b, win, 2, H_G, HD)
            kv_new.append(last)
        conv_buf = jnp.zeros((b, CONV_W - 1, D_CONV), F32)
        s0 = jnp.zeros((b, H_D, DK, DV), F32)
    else:
        att3 = jnp.concatenate(att_groups, axis=1).reshape(b, t, 3 * D_ATT)
        o_all, lse_all = _attn_sample(att3, caches)
        o_groups = [o_all[:, gi * GW:(gi + 1) * GW].reshape(1, 1, n, GW) for gi in range(ng)]
        lse_groups = [lse_all[:, gi * LANES:(gi + 1) * LANES].reshape(1, 1, n, LANES) for gi in range(ng)]
        for a in att_groups:
            kv_new.append(a[:, GW:].reshape(b, t, 2, H_G, HD))

    o_del, s_new = _delta_net(dqkv3, small.reshape(b, t, LANES), z.reshape(b, t, D_DV), conv_buf, s0,
                              conv_w, a_log, dt_bias, delta_norm_w)
    conv_new = jnp.concatenate([conv_buf, dqkv3], axis=1)[:, -(CONV_W - 1):] if t < CONV_W - 1 \
        else dqkv3[:, t - (CONV_W - 1):]

    rw = _pad_cols(router_w, LANES)
    rb = _pad_cols(router_b.reshape(1, N_EXP), LANES, fill=NEG)
    h, top_i, top_g = _mix_and_route(
        x2d, gates, o_groups, lse_groups, o_del,
        w_branch_attn.astype(BF16), w_branch_delta.astype(BF16), w_out.astype(BF16),
        b_gate.reshape(1, 2 * D_MODEL), ln1_g.reshape(1, D_MODEL), ln1_b.reshape(1, D_MODEL), rw, rb, tm)
    y = _moe(h, top_i, top_g, w_gu, b_gu, w_down, b_down,
             ln2_g.reshape(1, D_MODEL), ln2_b.reshape(1, D_MODEL), tm)
    return y.reshape(b, t, D_MODEL), kv_new, conv_new, s_new


def kernel(x_prompt, x_sample, cache_kv_w128, cache_kv_w512, cache_kv_w2048, state_conv, state_delta,
           w_in, b_gate, conv_w, a_log, dt_bias, delta_norm_w, w_branch_attn, w_branch_delta, w_out,
           ln1_g, ln1_b, router_w, router_b, w_gu, b_gu, w_down, b_down, ln2_g, ln2_b):
    depth = w_in.shape[0]
    assert depth == 1
    l = 0
    lw = (w_in[l], b_gate[l], conv_w[l], a_log[l], dt_bias[l], delta_norm_w[l], w_branch_attn[l],
          w_branch_delta[l], w_out[l], ln1_g[l], ln1_b[l], router_w[l], router_b[l], w_gu[l], b_gu[l],
          w_down[l], b_down[l], ln2_g[l], ln2_b[l])
    yp, kv_p, cv_p, s_p = _decoder_layer(x_prompt, None, None, None, *lw)
    ys, kv_s, cv_s, s_s = _decoder_layer(x_sample, (cache_kv_w128[l], cache_kv_w512[l], cache_kv_w2048[l]),
                                         state_conv[l], state_delta[l], *lw)
    stk = lambda a: a[None]
    return (yp, ys, stk(kv_p[0]), stk(kv_p[1]), stk(kv_p[2]), stk(cv_p), stk(s_p),
            stk(kv_s[0]), stk(kv_s[1]), stk(kv_s[2]), stk(cv_s), stk(s_s))
```

```python
import functools
import math

import numpy as np
import jax
import jax.numpy as jnp
from jax import lax
from jax.experimental import pallas as pl
from jax.experimental.pallas import tpu as pltpu

F32 = jnp.float32
BF16 = jnp.bfloat16
HIGHEST = lax.Precision.HIGHEST

D_MODEL = 1024
ATT_DILS = (1, 4, 16)
STEPS = 128
H_G = 4
HD = 64
GW = H_G * HD
D_ATT = len(ATT_DILS) * GW
H_D = 4
DK = 128
DV = 128
D_DK = H_D * DK
D_DV = H_D * DV
D_CONV = 2 * D_DK + D_DV
CONV_W = 4
CHUNK = 64
N_EXP = 32
TOP_K = 4
D_FF = D_MODEL
SWIGLU_LIMIT = 7.0
SWIGLU_ALPHA = 1.702
DN_ALPHA = 2.0 ** 0.25
LN_EPS = 1e-5
NORM_EPS = 1e-6
NEG = -1e30

LANES = 128
SUBLANES = 8
VMEM_LIMIT = 56 * 1024 * 1024
MOE_BM = 512
TOK_ROWS = D_MODEL // LANES
assert TOK_ROWS == SUBLANES


def _sigmoid(x):
    return 1.0 / (1.0 + jnp.exp(-x))


def _layer_norm(v, g, b):
    mu = jnp.mean(v, axis=-1, keepdims=True)
    d = v - mu
    var = jnp.mean(d * d, axis=-1, keepdims=True)
    return d * lax.rsqrt(var + LN_EPS) * g + b


def _alibi_slopes():
    return 2.0 ** (-8.0 * np.arange(1, H_G + 1, dtype=np.float64) / H_G)


def _inproj_body(x_ref, *refs, dils):
    nw = len(dils)
    scr = refs[2 * nw] if len(refs) > 2 * nw else None
    x = x_ref[...].astype(BF16)
    for w_ref, o_ref, d in zip(refs[:nw], refs[nw:2 * nw], dils):
        y = jnp.dot(x, w_ref[...], preferred_element_type=F32)
        if d == 1:
            o_ref[...] = y
        else:
            for c in range(y.shape[1] // LANES):
                scr[c] = y[:, c * LANES:(c + 1) * LANES]
            for r in range(d):
                for c in range(y.shape[1] // LANES):
                    o_ref[0, r, :, c * LANES:(c + 1) * LANES] = scr[c, pl.ds(r, y.shape[0] // d, stride=d), :]


def _in_projection(x2d, ws, dils, t, tm):
    n = x2d.shape[0]
    tpb = t // tm
    out_specs, out_shape = [], []
    for w, d in zip(ws, dils):
        c = w.shape[1]
        if d == 1:
            out_specs.append(pl.BlockSpec((tm, c), lambda i: (i, 0)))
            out_shape.append(jax.ShapeDtypeStruct((n, c), F32))
        else:
            assert tm % (d * SUBLANES) == 0 and t % tm == 0
            out_specs.append(pl.BlockSpec((1, d, tm // d, c), lambda i: (i // tpb, 0, i % tpb, 0)))
            out_shape.append(jax.ShapeDtypeStruct((n // t, d, t // d, c), F32))
    wide = max([w.shape[1] for w, d in zip(ws, dils) if d > 1], default=0)
    return pl.pallas_call(
        functools.partial(_inproj_body, dils=tuple(dils)),
        grid=(n // tm,),
        in_specs=[pl.BlockSpec((tm, D_MODEL), lambda i: (i, 0))]
        + [pl.BlockSpec((D_MODEL, w.shape[1]), lambda i: (0, 0), pipeline_mode=pl.Buffered(1)) for w in ws],
        out_specs=out_specs,
        out_shape=out_shape,
        scratch_shapes=[pltpu.VMEM((wide // LANES, tm, LANES), F32)] if wide else [],
        compiler_params=pltpu.CompilerParams(dimension_semantics=("parallel",), vmem_limit_bytes=VMEM_LIMIT),
        name="in_projection",
    )(x2d, *ws)


def _prompt_bias_table(dil):
    qi = np.arange(STEPS)[:, None]
    kj = np.arange(2 * STEPS)[None, :]
    steps = qi + STEPS - kj
    valid = (steps >= 0) & (steps <= STEPS)
    slopes = _alibi_slopes()
    bias = -slopes[:, None, None] * (dil * steps)[None].astype(np.float64)
    later = np.where(valid[None], bias, NEG)
    first = np.where((valid & (kj >= STEPS))[None], bias, NEG)
    return jnp.asarray(np.stack([first, later]), F32)


def _attn_prompt_body(q_ref, kp_ref, kc_ref, vp_ref, vc_ref, bias_ref, o_ref, lse_ref):
    later = jnp.minimum(pl.program_id(2), 1)
    q = q_ref[0, 0] * (HD ** -0.5)
    kk = jnp.concatenate([kp_ref[0, 0], kc_ref[0, 0]], axis=0).astype(BF16)
    vv = jnp.concatenate([vp_ref[0, 0], vc_ref[0, 0]], axis=0).astype(BF16)
    lse_ref[0, 0] = jnp.zeros((STEPS, LANES), F32)
    for h in range(H_G):
        cs = slice(h * HD, (h + 1) * HD)
        s = lax.dot_general(q[:, cs].astype(BF16), kk[:, cs], (((1,), (1,)), ((), ())),
                            preferred_element_type=F32)
        s = s + bias_ref[later, h]
        mx = jnp.max(s, axis=-1, keepdims=True)
        p = jnp.exp(s - mx)
        den = jnp.sum(p, axis=-1, keepdims=True)
        o = jnp.dot(p.astype(BF16), vv[:, cs], preferred_element_type=F32)
        o_ref[0, 0, :, cs] = o / den
        lse_ref[0, 0, :, h:h + 1] = mx + jnp.log(den)


def _attn_prompt(a, dil):
    b, d, L, c = a.shape
    assert d == dil and c == 3 * GW and L % STEPS == 0
    nb = L // STEPS

    def spec(col, prev):
        if prev:
            return pl.BlockSpec((1, 1, STEPS, GW), lambda bi, r, n: (bi, r, jnp.maximum(n - 1, 0), col))
        return pl.BlockSpec((1, 1, STEPS, GW), lambda bi, r, n: (bi, r, n, col))

    return pl.pallas_call(
        _attn_prompt_body,
        grid=(b, dil, nb),
        in_specs=[spec(0, False), spec(1, True), spec(1, False), spec(2, True), spec(2, False),
                  pl.BlockSpec((2, H_G, STEPS, 2 * STEPS), lambda bi, r, n: (0, 0, 0, 0))],
        out_specs=[pl.BlockSpec((1, 1, STEPS, GW), lambda bi, r, n: (bi, r, n, 0)),
                   pl.BlockSpec((1, 1, STEPS, LANES), lambda bi, r, n: (bi, r, n, 0))],
        out_shape=[jax.ShapeDtypeStruct((b, dil, L, GW), F32),
                   jax.ShapeDtypeStruct((b, dil, L, LANES), F32)],
        compiler_params=pltpu.CompilerParams(dimension_semantics=("parallel", "parallel", "arbitrary"),
                                             vmem_limit_bytes=VMEM_LIMIT),
        name=f"attn_prompt_d{dil}",
    )(a, a, a, a, a, _prompt_bias_table(dil))


def _sample_bias_tables(s_new):
    slopes = _alibi_slopes()[:, None, None]
    s = np.arange(SUBLANES)[:, None]
    live = s < s_new
    cached = []
    for dil in ATT_DILS:
        lw = dil * STEPS
        dist = lw + s - np.arange(lw)[None, :]
        ok = live & (dist % dil == 0) & (dist <= lw)
        cached.append(jnp.asarray(np.where(ok[None], -slopes * dist[None], NEG), F32))
    s2 = np.arange(SUBLANES)[None, :]
    dist = s - s2
    new = []
    for dil in ATT_DILS:
        ok = live & (s2 < s_new) & (dist >= 0) & (dist % dil == 0)
        new.append(np.where(ok[None], -slopes * dist[None], NEG))
    return cached, jnp.asarray(np.stack(new), F32)


def _attn_sample_body(new_ref, c0_ref, c1_ref, c2_ref, bw0_ref, bw1_ref, bw2_ref, bn_ref, o_ref, lse_ref, *, s_new):
    nt = (((1,), (1,)), ((), ()))
    lse_ref[0] = jnp.zeros(lse_ref.shape[1:], F32)
    for gi, (cache, bw_ref) in enumerate(zip((c0_ref, c1_ref, c2_ref), (bw0_ref, bw1_ref, bw2_ref))):
        for h in range(H_G):
            cs = slice(gi * GW + h * HD, gi * GW + (h + 1) * HD)
            c0 = gi * 3 * GW + h * HD
            q = new_ref[0, :, c0:c0 + HD] * (HD ** -0.5)
            k_new = new_ref[0, :, GW + c0:GW + c0 + HD]
            v_new = new_ref[0, :, 2 * GW + c0:2 * GW + c0 + HD]
            kt = cache[0, 0, h].astype(BF16)
            vt = cache[0, 1, h].astype(BF16)
            sw = jnp.dot(q.astype(BF16), kt, preferred_element_type=F32) + bw_ref[h]
            sn = lax.dot_general(q, k_new, nt, preferred_element_type=F32, precision=HIGHEST) + bn_ref[gi, h]
            mx = jnp.maximum(jnp.max(sw, axis=-1, keepdims=True), jnp.max(sn, axis=-1, keepdims=True))
            pw = jnp.exp(sw - mx)
            pn = jnp.exp(sn - mx)
            den = jnp.sum(pw, axis=-1, keepdims=True) + jnp.sum(pn, axis=-1, keepdims=True)
            o = (lax.dot_general(pw.astype(BF16), vt, nt, preferred_element_type=F32)
                 + jnp.dot(pn, v_new, preferred_element_type=F32, precision=HIGHEST)) / den
            o_ref[0, :, cs] = o[0:s_new]
            lse_ref[0, :, gi * LANES + h:gi * LANES + h + 1] = (mx + jnp.log(den))[0:s_new]


def _attn_sample(att, caches):
    b, s_new, c = att.shape
    assert s_new <= min(SUBLANES, ATT_DILS[1])
    new = jnp.pad(att, ((0, 0), (0, SUBLANES - s_new), (0, 0)))
    bufs, specs = [], []
    for dil, cache in zip(ATT_DILS, caches):
        lw = cache.shape[1]
        assert lw == dil * STEPS
        bufs.append(jnp.transpose(cache, (0, 2, 3, 4, 1)))
        specs.append(pl.BlockSpec((1, 2, H_G, HD, lw), lambda bi: (bi, 0, 0, 0, 0)))
    bias_cached, bias_new = _sample_bias_tables(s_new)
    ng = len(ATT_DILS)
    o, lse = pl.pallas_call(
        functools.partial(_attn_sample_body, s_new=s_new),
        grid=(b,),
        in_specs=[pl.BlockSpec((1, SUBLANES, c), lambda bi: (bi, 0, 0))] + specs
        + [pl.BlockSpec(bc.shape, lambda bi: (0, 0, 0)) for bc in bias_cached]
        + [pl.BlockSpec(bias_new.shape, lambda bi: (0, 0, 0, 0))],
        out_specs=[pl.BlockSpec((1, s_new, D_ATT), lambda bi: (bi, 0, 0)),
                   pl.BlockSpec((1, s_new, ng * LANES), lambda bi: (bi, 0, 0))],
        out_shape=[jax.ShapeDtypeStruct((b, s_new, D_ATT), F32),
                   jax.ShapeDtypeStruct((b, s_new, ng * LANES), F32)],
        compiler_params=pltpu.CompilerParams(dimension_semantics=("parallel",), vmem_limit_bytes=VMEM_LIMIT),
        name="attn_sample",
    )(new, *bufs, *bias_cached, bias_new)
    return o.reshape(b * s_new, D_ATT), lse.reshape(b * s_new, ng * LANES)


HIST = SUBLANES


def _bdot(a, b):
    return jnp.dot(a.astype(BF16), b.astype(BF16), preferred_element_type=F32)


def _delta_prep_body(u_ref, prev_ref, small_ref, cbuf_ref, cw_ref, alog_ref, dtb_ref,
                     uo_ref, wq_ref, ak_ref, eg_ref, ext, *, t_valid, cp):
    c = pl.program_id(1)
    rows = cp * CHUNK

    @pl.when(c == 0)
    def _():
        ext[0:HIST, :] = cbuf_ref[0]

    @pl.when(c > 0)
    def _():
        ext[0:HIST, :] = prev_ref[0]

    ext[HIST:HIST + rows, :] = u_ref[0]
    conv = ext[HIST - 3:HIST - 3 + rows, :] * cw_ref[0:1, :]
    for i in range(1, CONV_W):
        conv = conv + ext[HIST - 3 + i:HIST - 3 + i + rows, :] * cw_ref[i:i + 1, :]
    act = conv * _sigmoid(conv)

    sm = small_ref[0]
    bmat = _sigmoid(sm)
    xg = sm + dtb_ref[...]
    gmat = -jnp.exp(alog_ref[...]) * (jnp.maximum(xg, 0.0) + jnp.log1p(jnp.exp(-jnp.abs(xg))))
    if t_valid < rows:
        live = lax.broadcasted_iota(jnp.int32, (rows, 1), 0) < t_valid
        act = jnp.where(live, act, 0.0)
        bmat = jnp.where(live, bmat, 0.0)
        gmat = jnp.where(live, gmat, 0.0)

    ri = lax.broadcasted_iota(jnp.int32, (CHUNK, CHUNK), 0)
    ci = lax.broadcasted_iota(jnp.int32, (CHUNK, CHUNK), 1)
    tril = ri >= ci
    strict = ri > ci

    pairs = [(j, h) for j in range(cp) for h in range(H_D)]
    rsl = lambda j: slice(j * CHUNK, (j + 1) * CHUNK)
    gcs, gcts = [], []
    for j in range(cp):
        gc_j = jnp.dot(tril.astype(F32), gmat[rsl(j)], preferred_element_type=F32, precision=HIGHEST)
        gcs.append(gc_j)
        gcts.append(gc_j.T)
    q = jnp.stack([act[rsl(j), h * DK:(h + 1) * DK] for j, h in pairs])
    k = jnp.stack([act[rsl(j), D_DK + h * DK:D_DK + (h + 1) * DK] for j, h in pairs])
    v = jnp.stack([act[rsl(j), 2 * D_DK + h * DV:2 * D_DK + (h + 1) * DV] for j, h in pairs])
    beta = jnp.stack([bmat[rsl(j), h:h + 1] for j, h in pairs])
    gcol = jnp.stack([gcs[j][:, H_D + h:H_D + h + 1] for j, h in pairs])
    grow = jnp.stack([gcts[j][H_D + h:H_D + h + 1, :] for j, h in pairs])
    glast = jnp.stack([gcs[j][CHUNK - 1:CHUNK, H_D + h:H_D + h + 1] for j, h in pairs])

    qn = q * lax.rsqrt(jnp.sum(q * q, axis=-1, keepdims=True) + NORM_EPS) * (DK ** -0.5)
    kn = k * lax.rsqrt(jnp.sum(k * k, axis=-1, keepdims=True) + NORM_EPS)
    decay = jnp.where(tril, jnp.exp(jnp.where(tril, gcol - grow, 0.0)), 0.0)
    eg = jnp.exp(gcol)
    kb = kn * beta
    kn_b = kn.astype(BF16)
    a_kk = jnp.einsum('bik,bjk->bij', kb.astype(BF16), kn_b, preferred_element_type=F32)
    a_qk = jnp.einsum('bik,bjk->bij', qn.astype(BF16), kn_b, preferred_element_type=F32)
    pw = jnp.where(strict, -(a_kk * decay), 0.0)
    tm1 = pw
    for _ in range(int(math.log2(CHUNK)) - 1):
        pw_b = pw.astype(BF16)
        pw = jnp.einsum('bij,bjk->bik', pw_b, pw_b, preferred_element_type=F32)
        tm1 = tm1 + pw + jnp.einsum('bij,bjk->bik', tm1.astype(BF16), pw.astype(BF16), preferred_element_type=F32)
    vb = v * beta
    kbg = kb * eg
    tm1_b = tm1.astype(BF16)
    u = vb + jnp.einsum('bij,bjk->bik', tm1_b, vb.astype(BF16), preferred_element_type=F32)
    w = (kbg + jnp.einsum('bij,bjk->bik', tm1_b, kbg.astype(BF16), preferred_element_type=F32)).astype(BF16)
    qd = (qn * eg).astype(BF16)
    aqd = (a_qk * decay).astype(BF16)
    kd = kn * jnp.exp(glast - gcol)
    egl = jnp.exp(glast)
    for i, (j, h) in enumerate(pairs):
        uo_ref[0, rsl(j), h * DV:(h + 1) * DV] = u[i]
        wq_ref[0, j, h, 0:CHUNK, :] = w[i]
        wq_ref[0, j, h, CHUNK:2 * CHUNK, :] = qd[i]
        ak_ref[0, j, h, 0:CHUNK, :] = aqd[i]
        ak_ref[0, j, h, CHUNK:CHUNK + DK, :] = kd[i].T.astype(BF16)
        eg_ref[0, j, h:h + 1, :] = jnp.broadcast_to(egl[i], (1, LANES))
    for j in range(cp):
        eg_ref[0, j, H_D:SUBLANES, :] = jnp.zeros((SUBLANES - H_D, LANES), F32)


def _delta_rec_body(u_ref, wq_ref, ak_ref, eg_ref, z_ref, s0_ref, nw_ref, o_ref, sout_ref, S, *, bb, nc):
    c = pl.program_id(1)

    @pl.when(c == 0)
    def _():
        S[...] = s0_ref[...]

    for b in range(bb):
        for h in range(H_D):
            cs = slice(h * DV, (h + 1) * DV)
            s_h = S[b, h]
            x = jnp.dot(wq_ref[b, 0, h], s_h.astype(BF16), preferred_element_type=F32)
            vn = u_ref[b, :, cs] - x[0:CHUNK]
            y = jnp.dot(ak_ref[b, 0, h], vn.astype(BF16), preferred_element_type=F32)
            o = x[CHUNK:2 * CHUNK] + y[0:CHUNK]
            S[b, h] = s_h * eg_ref[b, 0, h:h + 1, :] + y[CHUNK:CHUNK + DK]
            od = o * lax.rsqrt(jnp.mean(o * o, axis=-1, keepdims=True) + NORM_EPS) * nw_ref[...]
            zh = z_ref[b, :, cs]
            o_ref[b, :, cs] = od * (zh * _sigmoid(zh))

    @pl.when(c == nc - 1)
    def _():
        sout_ref[...] = S[...]


def _lane_row(vals, offset):
    row = jnp.zeros((1, LANES), F32)
    return row.at[0, offset:offset + vals.shape[0]].set(vals.astype(F32))


def _delta_net(dqkv, small, z, conv_buf, s0, conv_w, a_log, dt_bias, delta_norm_w):
    b, t, _ = dqkv.shape
    tp = -(-t // CHUNK) * CHUNK
    nc = tp // CHUNK
    assert tp == t or nc == 1
    if tp != t:
        pad = ((0, 0), (0, tp - t), (0, 0))
        dqkv, small, z = jnp.pad(dqkv, pad), jnp.pad(small, pad), jnp.pad(z, pad)
    cbuf = jnp.pad(conv_buf, ((0, 0), (HIST - (CONV_W - 1), 0), (0, 0)))
    cw = jnp.pad(conv_w, ((0, SUBLANES - CONV_W), (0, 0)))
    cp = 2 if nc % 2 == 0 else 1
    rows = cp * CHUNK
    full2 = lambda shape: pl.BlockSpec(shape, lambda bi, ci: (0,) * len(shape))
    u, wq, ak, eg = pl.pallas_call(
        functools.partial(_delta_prep_body, t_valid=t if nc == 1 else rows, cp=cp),
        grid=(b, nc // cp),
        in_specs=[pl.BlockSpec((1, rows, D_CONV), lambda bi, ci: (bi, ci, 0)),
                  pl.BlockSpec((1, HIST, D_CONV), lambda bi, ci: (bi, jnp.maximum(ci * (rows // HIST) - 1, 0), 0)),
                  pl.BlockSpec((1, rows, LANES), lambda bi, ci: (bi, ci, 0)),
                  pl.BlockSpec((1, HIST, D_CONV), lambda bi, ci: (bi, 0, 0)),
                  full2((SUBLANES, D_CONV)), full2((1, LANES)), full2((1, LANES))],
        out_specs=[pl.BlockSpec((1, rows, D_DV), lambda bi, ci: (bi, ci, 0)),
                   pl.BlockSpec((1, cp, H_D, 2 * CHUNK, DK), lambda bi, ci: (bi, ci, 0, 0, 0)),
                   pl.BlockSpec((1, cp, H_D, CHUNK + DK, CHUNK), lambda bi, ci: (bi, ci, 0, 0, 0)),
                   pl.BlockSpec((1, cp, SUBLANES, LANES), lambda bi, ci: (bi, ci, 0, 0))],
        out_shape=[jax.ShapeDtypeStruct((b, tp, D_DV), F32),
                   jax.ShapeDtypeStruct((b, nc, H_D, 2 * CHUNK, DK), BF16),
                   jax.ShapeDtypeStruct((b, nc, H_D, CHUNK + DK, CHUNK), BF16),
                   jax.ShapeDtypeStruct((b, nc, SUBLANES, LANES), F32)],
        scratch_shapes=[pltpu.VMEM((HIST + rows, D_CONV), F32)],
        compiler_params=pltpu.CompilerParams(dimension_semantics=("parallel", "parallel"),
                                             vmem_limit_bytes=VMEM_LIMIT),
        name="delta_prep",
    )(dqkv, dqkv, small, cbuf, cw, _lane_row(a_log, H_D), _lane_row(dt_bias, H_D))

    bb = 4 if b % 4 == 0 else 1
    o, s_new = pl.pallas_call(
        functools.partial(_delta_rec_body, bb=bb, nc=nc),
        grid=(b // bb, nc),
        in_specs=[pl.BlockSpec((bb, CHUNK, D_DV), lambda bi, ci: (bi, ci, 0)),
                  pl.BlockSpec((bb, 1, H_D, 2 * CHUNK, DK), lambda bi, ci: (bi, ci, 0, 0, 0)),
                  pl.BlockSpec((bb, 1, H_D, CHUNK + DK, CHUNK), lambda bi, ci: (bi, ci, 0, 0, 0)),
                  pl.BlockSpec((bb, 1, SUBLANES, LANES), lambda bi, ci: (bi, ci, 0, 0)),
                  pl.BlockSpec((bb, CHUNK, D_DV), lambda bi, ci: (bi, ci, 0)),
                  pl.BlockSpec((bb, H_D, DK, DV), lambda bi, ci: (bi, 0, 0, 0)),
                  full2((1, DV))],
        out_specs=[pl.BlockSpec((bb, CHUNK, D_DV), lambda bi, ci: (bi, ci, 0)),
                   pl.BlockSpec((bb, H_D, DK, DV), lambda bi, ci: (bi, 0, 0, 0))],
        out_shape=[jax.ShapeDtypeStruct((b, tp, D_DV), F32), jax.ShapeDtypeStruct((b, H_D, DK, DV), F32)],
        scratch_shapes=[pltpu.VMEM((bb, H_D, DK, DV), F32)],
        compiler_params=pltpu.CompilerParams(dimension_semantics=("parallel", "arbitrary"),
                                             vmem_limit_bytes=VMEM_LIMIT),
        name="delta_recurrence",
    )(u, wq, ak, eg, z, s0, delta_norm_w.reshape(1, DV).astype(F32))
    return o[:, :t].reshape(b * t, D_DV), s_new


def _lane_expand(cols, width):
    tm = cols[0].shape[0]
    seg = lax.broadcasted_iota(jnp.int32, (tm, len(cols) * width), 1) // width
    out = jnp.broadcast_to(cols[-1], seg.shape)
    for i in range(len(cols) - 2, -1, -1):
        out = jnp.where(seg == i, cols[i], out)
    return out


def _token_order(ref, scr):
    d, nc = ref.shape[1], ref.shape[3] // LANES
    if d == 1:
        return ref[0, 0]
    for r in range(d):
        for c in range(nc):
            scr[c, pl.ds(r, ref.shape[2], stride=d), :] = ref[0, r, :, c * LANES:(c + 1) * LANES]
    return jnp.concatenate([scr[c] for c in range(nc)], axis=-1)


def _mix_body(x_ref, gate_ref, o0_ref, o1_ref, o2_ref, l0_ref, l1_ref, l2_ref, od_ref,
              wa_ref, wd_ref, wo_ref, bg_ref, g1_ref, b1_ref, rw_ref, rb_ref,
              h_ref, ti_ref, tg_ref, *scratch):
    o_scr, l_scr = scratch[:3], scratch[3:]
    lses = [_token_order(l, s) for l, s in zip((l0_ref, l1_ref, l2_ref), l_scr)]
    mx = jnp.maximum(jnp.maximum(lses[0], lses[1]), lses[2])
    es = [jnp.exp(l - mx) for l in lses]
    inv = 1.0 / (es[0] + es[1] + es[2])
    o_att = None
    for e_g, o_g, s in zip(es, (o0_ref, o1_ref, o2_ref), o_scr):
        wgt = e_g * inv
        term = _lane_expand([wgt[:, h:h + 1] for h in range(H_G)], HD) * _token_order(o_g, s)
        o_att = term if o_att is None else o_att + term
    ga = _sigmoid(gate_ref[:, :D_MODEL] + bg_ref[:, :D_MODEL])
    gd = _sigmoid(gate_ref[:, D_MODEL:] + bg_ref[:, D_MODEL:])
    merged = (ga * jnp.dot(o_att.astype(BF16), wa_ref[...], preferred_element_type=F32)
              + gd * jnp.dot(od_ref[...].astype(BF16), wd_ref[...], preferred_element_type=F32))
    mix = jnp.dot(merged.astype(BF16), wo_ref[...], preferred_element_type=F32)
    h = _layer_norm(DN_ALPHA * x_ref[...] + mix, g1_ref[...], b1_ref[...])
    _rows_to_tiles(h_ref, h)

    logits = jnp.dot(h, rw_ref[...], preferred_element_type=F32, precision=HIGHEST) + rb_ref[...]
    lane = lax.broadcasted_iota(jnp.int32, logits.shape, 1)
    lane_f = lane.astype(F32)
    ti = jnp.zeros(logits.shape, F32)
    tv = jnp.zeros(logits.shape, F32)
    top = None
    for kk in range(TOP_K):
        m = jnp.max(logits, axis=-1, keepdims=True)
        idx = jnp.min(jnp.where(logits == m, lane_f, float(LANES)), axis=-1, keepdims=True)
        if top is None:
            top = m
        ti = jnp.where(lane == kk, idx, ti)
        tv = jnp.where(lane == kk, jnp.exp(m - top), tv)
        logits = jnp.where(lane_f == idx, -jnp.inf, logits)
    ti_ref[...] = ti.astype(jnp.int32)
    tg_ref[...] = tv / jnp.sum(tv, axis=-1, keepdims=True)


def _mix_and_route(x2d, gates, o_groups, lse_groups, o_del, wa, wd, wo, b_gate, ln_g, ln_b, rw, rb, tm):
    n = x2d.shape[0]
    tpb = o_groups[0].shape[1] * o_groups[0].shape[2] // tm
    row = lambda c: pl.BlockSpec((tm, c), lambda i: (i, 0))
    full = lambda a: pl.BlockSpec(a.shape, lambda i: (0, 0))

    def res(a):
        d = a.shape[1]
        return pl.BlockSpec((1, d, tm // d, a.shape[3]), lambda i: (i // tpb, 0, i % tpb, 0))

    consts = (wa, wd, wo, b_gate, ln_g, ln_b, rw, rb)
    return pl.pallas_call(
        _mix_body,
        grid=(n // tm,),
        in_specs=[row(D_MODEL), row(2 * D_MODEL)] + [res(a) for a in o_groups] + [res(a) for a in lse_groups]
        + [row(D_DV)] + [full(a) for a in consts],
        out_specs=[pl.BlockSpec((tm * TOK_ROWS, LANES), lambda i: (i, 0)), row(LANES), row(LANES)],
        out_shape=[jax.ShapeDtypeStruct((n * TOK_ROWS, LANES), F32), jax.ShapeDtypeStruct((n, LANES), jnp.int32),
                   jax.ShapeDtypeStruct((n, LANES), F32)],
        scratch_shapes=[pltpu.VMEM((GW // LANES, tm, LANES), F32)] * len(o_groups)
        + [pltpu.VMEM((1, tm, LANES), F32)] * len(lse_groups),
        compiler_params=pltpu.CompilerParams(dimension_semantics=("parallel",), vmem_limit_bytes=VMEM_LIMIT),
        name="mix_and_route",
    )(x2d, gates, *o_groups, *lse_groups, o_del, *consts)


def _rank_body(ti_ref, rank_ref, cnt_ref, carry):
    i = pl.program_id(0)

    @pl.when(i == 0)
    def _():
        carry[...] = jnp.zeros_like(carry)

    ti = ti_ref[...]
    tm = ti.shape[0]
    lane = lax.broadcasted_iota(jnp.int32, ti.shape, 1)
    sel = [lane == ti[:, kk:kk + 1] for kk in range(TOP_K)]
    hit = sel[0]
    for kk in range(1, TOP_K):
        hit = hit | sel[kk]
    cnt = hit.astype(F32)
    ri = lax.broadcasted_iota(jnp.int32, (tm, tm), 0)
    ci = lax.broadcasted_iota(jnp.int32, (tm, tm), 1)
    incl = jnp.dot((ri >= ci).astype(BF16), cnt.astype(BF16), preferred_element_type=F32)
    before = incl - cnt + carry[...]
    rank = jnp.zeros(ti.shape, jnp.int32)
    for kk in range(TOP_K):
        r = jnp.sum(jnp.where(sel[kk], before, 0.0), axis=-1, keepdims=True)
        rank = jnp.where(lane == kk, r.astype(jnp.int32), rank)
    rank_ref[...] = rank
    carry[...] = carry[...] + incl[tm - 1:tm, :]
    cnt_ref[...] = carry[...]


def _route_ranks(top_i, tm):
    n = top_i.shape[0]
    return pl.pallas_call(
        _rank_body,
        grid=(n // tm,),
        in_specs=[pl.BlockSpec((tm, LANES), lambda i: (i, 0))],
        out_specs=[pl.BlockSpec((tm, LANES), lambda i: (i, 0)), pl.BlockSpec((1, LANES), lambda i: (0, 0))],
        out_shape=[jax.ShapeDtypeStruct((n, LANES), jnp.int32), jax.ShapeDtypeStruct((1, LANES), F32)],
        scratch_shapes=[pltpu.VMEM((1, LANES), F32)],
        compiler_params=pltpu.CompilerParams(dimension_semantics=("arbitrary",), vmem_limit_bytes=VMEM_LIMIT),
        name="route_ranks",
    )(top_i)


def _rows_to_tiles(ref, val):
    m = val.shape[0]
    for s in range(TOK_ROWS):
        ref[pl.ds(s, m, stride=TOK_ROWS), :] = val[:, s * LANES:(s + 1) * LANES]


def _tiles_to_rows(ref, m):
    return jnp.concatenate([ref[pl.ds(s, m, stride=TOK_ROWS), :] for s in range(TOK_ROWS)], axis=-1)


def _dispatch_body(pend_ref, padded_ref, dest_ref, h_ref, xs_out, zero, sem, zsem, *, max_tail):
    tm = h_ref.shape[0] // TOK_ROWS
    blk = MOE_BM * TOK_ROWS

    @pl.when(pl.program_id(0) == 0)
    def _():
        zero[...] = jnp.zeros_like(zero)

        def last_block(e):
            start = pl.multiple_of((pend_ref[e] - MOE_BM) * TOK_ROWS, blk)
            return pltpu.make_async_copy(zero, xs_out.at[pl.ds(start, blk)], zsem)

        def tail_block(j):
            start = pl.multiple_of((pend_ref[N_EXP - 1] + j * MOE_BM) * TOK_ROWS, blk)
            return pltpu.make_async_copy(zero, xs_out.at[pl.ds(start, blk)], zsem)

        n_slot = xs_out.shape[0] // TOK_ROWS
        for e in range(N_EXP):
            @pl.when(padded_ref[e] > 0)
            def _():
                last_block(e).start()
        for j in range(max_tail):
            @pl.when(pend_ref[N_EXP - 1] + j * MOE_BM < n_slot)
            def _():
                tail_block(j).start()
        for e in range(N_EXP):
            @pl.when(padded_ref[e] > 0)
            def _():
                last_block(e).wait()
        for j in range(max_tail):
            @pl.when(pend_ref[N_EXP - 1] + j * MOE_BM < n_slot)
            def _():
                tail_block(j).wait()

    def issue(t, carry):
        src = h_ref.at[pl.ds(pl.multiple_of(t * TOK_ROWS, TOK_ROWS), TOK_ROWS)]
        for kk in range(TOP_K):
            d = pl.multiple_of(dest_ref[t * TOP_K + kk] * TOK_ROWS, TOK_ROWS)
            pltpu.make_async_copy(src, xs_out.at[pl.ds(d, TOK_ROWS)], sem).start()
        return carry

    lax.fori_loop(0, tm, issue, 0, unroll=8)
    for kk in range(TOP_K):
        pltpu.make_async_copy(h_ref, xs_out.at[pl.ds(0, tm * TOK_ROWS)], sem).wait()


def _dispatch(h_tiles, dest_flat, pad_end, padded, n_slot, tm):
    n = h_tiles.shape[0] // TOK_ROWS
    grid_spec = pltpu.PrefetchScalarGridSpec(
        num_scalar_prefetch=2,
        grid=(n // tm,),
        in_specs=[pl.BlockSpec((tm * TOP_K,), lambda i, pe, pd: (i,), memory_space=pltpu.SMEM),
                  pl.BlockSpec((tm * TOK_ROWS, LANES), lambda i, pe, pd: (i, 0))],
        out_specs=pl.BlockSpec(memory_space=pl.ANY),
        scratch_shapes=[pltpu.VMEM((MOE_BM * TOK_ROWS, LANES), F32), pltpu.SemaphoreType.DMA(()),
                        pltpu.SemaphoreType.DMA(())],
    )
    max_tail = n_slot // MOE_BM - (n * TOP_K) // MOE_BM
    return pl.pallas_call(
        functools.partial(_dispatch_body, max_tail=max_tail),
        grid_spec=grid_spec,
        out_shape=jax.ShapeDtypeStruct((n_slot * TOK_ROWS, LANES), F32),
        compiler_params=pltpu.CompilerParams(dimension_semantics=("arbitrary",), vmem_limit_bytes=VMEM_LIMIT,
                                             has_side_effects=True, disable_bounds_checks=True),
        name="moe_dispatch",
    )(pad_end, padded, dest_flat, h_tiles)


def _expert_body(be_ref, nu_ref, x_ref, wgu_ref, bgu_ref, wd_ref, bd_ref, o_ref, wgu_bf, wd_bf):
    i = pl.program_id(0)
    e = be_ref[i]
    prev = be_ref[jnp.maximum(i - 1, 0)]

    @pl.when((i == 0) | (e != prev))
    def _():
        wgu_bf[...] = wgu_ref[0].astype(BF16)
        wd_bf[...] = wd_ref[0].astype(BF16)

    @pl.when(i < nu_ref[0])
    def _():
        x = _tiles_to_rows(x_ref, MOE_BM)
        gu = jnp.dot(x.astype(BF16), wgu_bf[...], preferred_element_type=F32) + bgu_ref[0]
        gt = jnp.minimum(gu[:, :D_FF], SWIGLU_LIMIT)
        up = jnp.clip(gu[:, D_FF:], -SWIGLU_LIMIT, SWIGLU_LIMIT)
        act = (up + 1.0) * gt * _sigmoid(SWIGLU_ALPHA * gt)
        _rows_to_tiles(o_ref, jnp.dot(act.astype(BF16), wd_bf[...], preferred_element_type=F32) + bd_ref[0])

    @pl.when(i >= nu_ref[0])
    def _():
        o_ref[...] = jnp.zeros_like(o_ref)


def _experts(xs, blk_exp, n_used, w_gu, b_gu, w_down, b_down):
    n_slot = xs.shape[0] // TOK_ROWS
    n_blk = n_slot // MOE_BM
    rows = lambda i, be, nu: (jnp.minimum(i, nu[0] - 1), 0)
    grid_spec = pltpu.PrefetchScalarGridSpec(
        num_scalar_prefetch=2,
        grid=(n_blk,),
        in_specs=[pl.BlockSpec((MOE_BM * TOK_ROWS, LANES), rows),
                  pl.BlockSpec((1, D_MODEL, 2 * D_FF), lambda i, be, nu: (be[i], 0, 0)),
                  pl.BlockSpec((1, 1, 2 * D_FF), lambda i, be, nu: (be[i], 0, 0)),
                  pl.BlockSpec((1, D_FF, D_MODEL), lambda i, be, nu: (be[i], 0, 0)),
                  pl.BlockSpec((1, 1, D_MODEL), lambda i, be, nu: (be[i], 0, 0))],
        out_specs=pl.BlockSpec((MOE_BM * TOK_ROWS, LANES), lambda i, be, nu: (i, 0)),
        scratch_shapes=[pltpu.VMEM((D_MODEL, 2 * D_FF), BF16), pltpu.VMEM((D_FF, D_MODEL), BF16)],
    )
    return pl.pallas_call(
        _expert_body,
        grid_spec=grid_spec,
        out_shape=jax.ShapeDtypeStruct((n_slot * TOK_ROWS, LANES), F32),
        compiler_params=pltpu.CompilerParams(dimension_semantics=("arbitrary",), vmem_limit_bytes=VMEM_LIMIT),
        name="moe_experts",
    )(blk_exp, n_used, xs, w_gu, b_gu.reshape(N_EXP, 1, 2 * D_FF), w_down, b_down.reshape(N_EXP, 1, D_MODEL))


def _combine_body(dest_ref, gate_ref, h_ref, g_ref, b_ref, ys_ref, y_ref, buf, sem):
    tm = h_ref.shape[0] // TOK_ROWS

    def issue(t, carry):
        row = pl.multiple_of(t * TOK_ROWS, TOK_ROWS)
        for kk in range(TOP_K):
            d = pl.multiple_of(dest_ref[t * TOP_K + kk] * TOK_ROWS, TOK_ROWS)
            pltpu.make_async_copy(ys_ref.at[pl.ds(d, TOK_ROWS)], buf.at[kk, pl.ds(row, TOK_ROWS)], sem).start()
        return carry

    lax.fori_loop(0, tm, issue, 0, unroll=8)
    for kk in range(TOP_K):
        pltpu.make_async_copy(ys_ref.at[pl.ds(0, tm * TOK_ROWS)], buf.at[kk], sem).wait()
    gate = gate_ref[...]
    moe = gate[:, 0:1] * _tiles_to_rows(buf.at[0], tm)
    for kk in range(1, TOP_K):
        moe = moe + gate[:, kk:kk + 1] * _tiles_to_rows(buf.at[kk], tm)
    y_ref[...] = _layer_norm(DN_ALPHA * _tiles_to_rows(h_ref, tm) + moe, g_ref[...], b_ref[...])


def _combine(ys, dest_flat, gate, h_tiles, ln_g, ln_b, tm):
    n = h_tiles.shape[0] // TOK_ROWS
    return pl.pallas_call(
        _combine_body,
        grid=(n // tm,),
        in_specs=[pl.BlockSpec((tm * TOP_K,), lambda i: (i,), memory_space=pltpu.SMEM),
                  pl.BlockSpec((tm, LANES), lambda i: (i, 0)),
                  pl.BlockSpec((tm * TOK_ROWS, LANES), lambda i: (i, 0)),
                  pl.BlockSpec((1, D_MODEL), lambda i: (0, 0)),
                  pl.BlockSpec((1, D_MODEL), lambda i: (0, 0)),
                  pl.BlockSpec(memory_space=pl.ANY)],
        out_specs=pl.BlockSpec((tm, D_MODEL), lambda i: (i, 0)),
        out_shape=jax.ShapeDtypeStruct((n, D_MODEL), F32),
        scratch_shapes=[pltpu.VMEM((TOP_K, tm * TOK_ROWS, LANES), F32), pltpu.SemaphoreType.DMA(())],
        compiler_params=pltpu.CompilerParams(dimension_semantics=("arbitrary",), vmem_limit_bytes=VMEM_LIMIT,
                                             disable_bounds_checks=True),
        name="moe_combine",
    )(dest_flat, gate, h_tiles, ln_g, ln_b, ys)


def _moe(h, top_i, top_g, w_gu, b_gu, w_down, b_down, ln_g, ln_b, tm):
    n = top_i.shape[0]
    rank, counts = _route_ranks(top_i, tm)
    counts = counts[0, :N_EXP].astype(jnp.int32)
    padded = (counts + MOE_BM - 1) // MOE_BM * MOE_BM
    pad_end = jnp.cumsum(padded)
    pad_start = pad_end - padded
    n_blk = -(-(n * TOP_K + N_EXP * (MOE_BM - 1)) // MOE_BM)
    dest = (pad_start[top_i[:, :TOP_K]] + rank[:, :TOP_K]).reshape(n * TOP_K)
    blk_row0 = jnp.arange(n_blk, dtype=jnp.int32) * MOE_BM
    blk_exp = jnp.minimum(jnp.sum((pad_end[None, :] <= blk_row0[:, None]).astype(jnp.int32), axis=1), N_EXP - 1)
    n_used = (pad_end[-1:] // MOE_BM).astype(jnp.int32)
    xs = _dispatch(h, dest, pad_end.astype(jnp.int32), padded, n_blk * MOE_BM, tm)
    ys = _experts(xs, blk_exp, n_used, w_gu, b_gu, w_down, b_down)
    return _combine(ys, dest, top_g, h, ln_g, ln_b, tm)


def _pad_cols(a, width, fill=0.0):
    return jnp.pad(a, ((0, 0), (0, width - a.shape[1])), constant_values=fill)


def _decoder_layer(x, caches, conv_buf, s0, w_in, b_gate, conv_w, a_log, dt_bias, delta_norm_w,
                   w_branch_attn, w_branch_delta, w_out, ln1_g, ln1_b, router_w, router_b,
                   w_gu, b_gu, w_down, b_down, ln2_g, ln2_b):
    b, t, _ = x.shape
    n = b * t
    tm = min(256, n)
    x2d = x.reshape(n, D_MODEL)

    c_att, c_dz, c_small = 3 * D_ATT, 3 * D_ATT + D_CONV, 3 * D_ATT + D_CONV + D_DV
    c_gate = c_small + 2 * H_D
    ng = len(ATT_DILS)
    w_groups = [jnp.concatenate([w_in[:, part * D_ATT + gi * GW:part * D_ATT + (gi + 1) * GW] for part in range(3)],
                                axis=1) for gi in range(ng)]
    ws = w_groups + [w_in[:, c_att:c_dz], w_in[:, c_dz:c_small],
                     _pad_cols(w_in[:, c_small:c_gate], LANES), w_in[:, c_gate:]]
    dils = (ATT_DILS if caches is None else (1,) * ng) + (1, 1, 1, 1)
    outs = _in_projection(x2d, [w.astype(BF16) for w in ws], dils, t, tm)
    att_groups, (dqkv, z, small, gates) = outs[:ng], outs[ng:]
    dqkv3 = dqkv.reshape(b, t, D_CONV)

    kv_new = []
    if caches is None:
        o_groups, lse_groups = [], []
        for dil, a in zip(ATT_DILS, att_groups):
            a = a.reshape(b, dil, t // dil, 3 * GW)
            o_g, lse_g = _attn_prompt(a, dil)
            o_groups.append(o_g)
            lse_groups.append(lse_g)
            win = min(dil * STEPS, t)
            last = a[:, :, (t - win) // dil:, GW:]
            last = jnp.transpose(last, (0, 2, 1, 3)).reshape(b, win, 2, H_G, HD)
            kv_new.append(last)
        conv_buf = jnp.zeros((b, CONV_W - 1, D_CONV), F32)
        s0 = jnp.zeros((b, H_D, DK, DV), F32)
    else:
        att3 = jnp.concatenate(att_groups, axis=1).reshape(b, t, 3 * D_ATT)
        o_all, lse_all = _attn_sample(att3, caches)
        o_groups = [o_all[:, gi * GW:(gi + 1) * GW].reshape(1, 1, n, GW) for gi in range(ng)]
        lse_groups = [lse_all[:, gi * LANES:(gi + 1) * LANES].reshape(1, 1, n, LANES) for gi in range(ng)]
        for a in att_groups:
            kv_new.append(a[:, GW:].reshape(b, t, 2, H_G, HD))

    o_del, s_new = _delta_net(dqkv3, small.reshape(b, t, LANES), z.reshape(b, t, D_DV), conv_buf, s0,
                              conv_w, a_log, dt_bias, delta_norm_w)
    conv_new = jnp.concatenate([conv_buf, dqkv3], axis=1)[:, -(CONV_W - 1):] if t < CONV_W - 1 \
        else dqkv3[:, t - (CONV_W - 1):]

    rw = _pad_cols(router_w, LANES)
    rb = _pad_cols(router_b.reshape(1, N_EXP), LANES, fill=NEG)
    h, top_i, top_g = _mix_and_route(
        x2d, gates, o_groups, lse_groups, o_del,
        w_branch_attn.astype(BF16), w_branch_delta.astype(BF16), w_out.astype(BF16),
        b_gate.reshape(1, 2 * D_MODEL), ln1_g.reshape(1, D_MODEL), ln1_b.reshape(1, D_MODEL), rw, rb, tm)
    y = _moe(h, top_i, top_g, w_gu, b_gu, w_down, b_down,
             ln2_g.reshape(1, D_MODEL), ln2_b.reshape(1, D_MODEL), tm)
    return y.reshape(b, t, D_MODEL), kv_new, conv_new, s_new


def kernel(x_prompt, x_sample, cache_kv_w128, cache_kv_w512, cache_kv_w2048, state_conv, state_delta,
           w_in, b_gate, conv_w, a_log, dt_bias, delta_norm_w, w_branch_attn, w_branch_delta, w_out,
           ln1_g, ln1_b, router_w, router_b, w_gu, b_gu, w_down, b_down, ln2_g, ln2_b):
    depth = w_in.shape[0]
    assert depth == 1
    l = 0
    lw = (w_in[l], b_gate[l], conv_w[l], a_log[l], dt_bias[l], delta_norm_w[l], w_branch_attn[l],
          w_branch_delta[l], w_out[l], ln1_g[l], ln1_b[l], router_w[l], router_b[l], w_gu[l], b_gu[l],
          w_down[l], b_down[l], ln2_g[l], ln2_b[l])
    yp, kv_p, cv_p, s_p = _decoder_layer(x_prompt, None, None, None, *lw)
    ys, kv_s, cv_s, s_s = _decoder_layer(x_sample, (cache_kv_w128[l], cache_kv_w512[l], cache_kv_w2048[l]),
                                         state_conv[l], state_delta[l], *lw)
    stk = lambda a: a[None]
    return (yp, ys, stk(kv_p[0]), stk(kv_p[1]), stk(kv_p[2]), stk(cv_p), stk(s_p),
            stk(kv_s[0]), stk(kv_s[1]), stk(kv_s[2]), stk(cv_s), stk(s_s))
```

```python
import functools
import math

import numpy as np
import jax
import jax.numpy as jnp
from jax import lax
from jax.experimental import pallas as pl
from jax.experimental.pallas import tpu as pltpu

F32 = jnp.float32
BF16 = jnp.bfloat16
HIGHEST = lax.Precision.HIGHEST

D_MODEL = 1024
ATT_DILS = (1, 4, 16)
STEPS = 128
H_G = 4
HD = 64
GW = H_G * HD
D_ATT = len(ATT_DILS) * GW
H_D = 4
DK = 128
DV = 128
D_DK = H_D * DK
D_DV = H_D * DV
D_CONV = 2 * D_DK + D_DV
CONV_W = 4
CHUNK = 64
N_EXP = 32
TOP_K = 4
D_FF = D_MODEL
SWIGLU_LIMIT = 7.0
SWIGLU_ALPHA = 1.702
DN_ALPHA = 2.0 ** 0.25
LN_EPS = 1e-5
NORM_EPS = 1e-6
NEG = -1e30

LANES = 128
SUBLANES = 8
VMEM_LIMIT = 56 * 1024 * 1024
MOE_BM = 512
TOK_ROWS = D_MODEL // LANES
assert TOK_ROWS == SUBLANES


def _sigmoid(x):
    return 1.0 / (1.0 + jnp.exp(-x))


def _layer_norm(v, g, b):
    mu = jnp.mean(v, axis=-1, keepdims=True)
    d = v - mu
    var = jnp.mean(d * d, axis=-1, keepdims=True)
    return d * lax.rsqrt(var + LN_EPS) * g + b


def _alibi_slopes():
    return 2.0 ** (-8.0 * np.arange(1, H_G + 1, dtype=np.float64) / H_G)


def _inproj_body(x_ref, *refs, dils):
    nw = len(dils)
    scr = refs[2 * nw] if len(refs) > 2 * nw else None
    x = x_ref[...].astype(BF16)
    for w_ref, o_ref, d in zip(refs[:nw], refs[nw:2 * nw], dils):
        y = jnp.dot(x, w_ref[...], preferred_element_type=F32)
        if d == 1:
            o_ref[...] = y
        else:
            for c in range(y.shape[1] // LANES):
                scr[c] = y[:, c * LANES:(c + 1) * LANES]
            for r in range(d):
                for c in range(y.shape[1] // LANES):
                    o_ref[0, r, :, c * LANES:(c + 1) * LANES] = scr[c, pl.ds(r, y.shape[0] // d, stride=d), :]


def _in_projection(x2d, ws, dils, t, tm):
    n = x2d.shape[0]
    tpb = t // tm
    out_specs, out_shape = [], []
    for w, d in zip(ws, dils):
        c = w.shape[1]
        if d == 1:
            out_specs.append(pl.BlockSpec((tm, c), lambda i: (i, 0)))
            out_shape.append(jax.ShapeDtypeStruct((n, c), F32))
        else:
            assert tm % (d * SUBLANES) == 0 and t % tm == 0
            out_specs.append(pl.BlockSpec((1, d, tm // d, c), lambda i: (i // tpb, 0, i % tpb, 0)))
            out_shape.append(jax.ShapeDtypeStruct((n // t, d, t // d, c), F32))
    wide = max([w.shape[1] for w, d in zip(ws, dils) if d > 1], default=0)
    return pl.pallas_call(
        functools.partial(_inproj_body, dils=tuple(dils)),
        grid=(n // tm,),
        in_specs=[pl.BlockSpec((tm, D_MODEL), lambda i: (i, 0))]
        + [pl.BlockSpec((D_MODEL, w.shape[1]), lambda i: (0, 0), pipeline_mode=pl.Buffered(1)) for w in ws],
        out_specs=out_specs,
        out_shape=out_shape,
        scratch_shapes=[pltpu.VMEM((wide // LANES, tm, LANES), F32)] if wide else [],
        compiler_params=pltpu.CompilerParams(dimension_semantics=("parallel",), vmem_limit_bytes=VMEM_LIMIT),
        name="in_projection",
    )(x2d, *ws)


def _prompt_bias_table(dil):
    qi = np.arange(STEPS)[:, None]
    kj = np.arange(2 * STEPS)[None, :]
    steps = qi + STEPS - kj
    valid = (steps >= 0) & (steps <= STEPS)
    slopes = _alibi_slopes()
    bias = -slopes[:, None, None] * (dil * steps)[None].astype(np.float64)
    later = np.where(valid[None], bias, NEG)
    first = np.where((valid & (kj >= STEPS))[None], bias, NEG)
    return jnp.asarray(np.stack([first, later]).reshape(2, H_G * STEPS, 2 * STEPS), F32)


ATT_QB = 2


def _attn_prompt_body(q_ref, kp_ref, kc_ref, vp_ref, vc_ref, bias_ref, o_ref, lse_ref):
    kk = jnp.concatenate([kp_ref[0, 0], kc_ref[0, 0]], axis=0).astype(BF16)
    vv = jnp.concatenate([vp_ref[0, 0], vc_ref[0, 0]], axis=0).astype(BF16)
    head = lax.broadcasted_iota(jnp.int32, (STEPS, GW), 1) // HD
    lse_ref[0, 0] = jnp.zeros(lse_ref.shape[2:], F32)
    for j in range(ATT_QB):
        qrows = slice(j * STEPS, (j + 1) * STEPS)
        krows = slice(j * STEPS, (j + 2) * STEPS)
        later = jnp.minimum(pl.program_id(2), 1) if j == 0 else 1
        q = q_ref[0, 0, qrows, :] * (HD ** -0.5)
        qs = jnp.concatenate([jnp.where(head == h, q, 0.0) for h in range(H_G)], axis=0).astype(BF16)
        s = lax.dot_general(qs, kk[krows], (((1,), (1,)), ((), ())), preferred_element_type=F32)
        s = s + bias_ref[later]
        mx = jnp.max(s, axis=-1, keepdims=True)
        p = jnp.exp(s - mx)
        den = jnp.sum(p, axis=-1, keepdims=True)
        pv = jnp.dot(p.astype(BF16), vv[krows], preferred_element_type=F32) / den
        lse = mx + jnp.log(den)
        o = jnp.zeros((STEPS, GW), F32)
        for h in range(H_G):
            rows = slice(h * STEPS, (h + 1) * STEPS)
            o = jnp.where(head == h, pv[rows], o)
            lse_ref[0, 0, qrows, h:h + 1] = lse[rows]
        o_ref[0, 0, qrows, :] = o


def _attn_prompt(a, dil):
    b, d, L, c = a.shape
    rows = ATT_QB * STEPS
    assert d == dil and c == 3 * GW and L % rows == 0
    nb = L // rows

    def spec(col, prev):
        if prev:
            return pl.BlockSpec((1, 1, STEPS, GW), lambda bi, r, n: (bi, r, jnp.maximum(ATT_QB * n - 1, 0), col))
        return pl.BlockSpec((1, 1, rows, GW), lambda bi, r, n: (bi, r, n, col))

    return pl.pallas_call(
        _attn_prompt_body,
        grid=(b, dil, nb),
        in_specs=[spec(0, False), spec(1, True), spec(1, False), spec(2, True), spec(2, False),
                  pl.BlockSpec((2, H_G * STEPS, 2 * STEPS), lambda bi, r, n: (0, 0, 0))],
        out_specs=[pl.BlockSpec((1, 1, rows, GW), lambda bi, r, n: (bi, r, n, 0)),
                   pl.BlockSpec((1, 1, rows, LANES), lambda bi, r, n: (bi, r, n, 0))],
        out_shape=[jax.ShapeDtypeStruct((b, dil, L, GW), F32),
                   jax.ShapeDtypeStruct((b, dil, L, LANES), F32)],
        compiler_params=pltpu.CompilerParams(dimension_semantics=("parallel", "parallel", "arbitrary"),
                                             vmem_limit_bytes=VMEM_LIMIT),
        name=f"attn_prompt_d{dil}",
    )(a, a, a, a, a, _prompt_bias_table(dil))


def _sample_bias_tables(s_new):
    slopes = _alibi_slopes()[:, None, None]
    s = np.arange(SUBLANES)[:, None]
    live = s < s_new
    cached = []
    for dil in ATT_DILS:
        lw = dil * STEPS
        dist = lw + s - np.arange(lw)[None, :]
        ok = live & (dist % dil == 0) & (dist <= lw)
        cached.append(jnp.asarray(np.where(ok[None], -slopes * dist[None], NEG), F32))
    s2 = np.arange(SUBLANES)[None, :]
    dist = s - s2
    new = []
    for dil in ATT_DILS:
        ok = live & (s2 < s_new) & (dist >= 0) & (dist % dil == 0)
        new.append(np.where(ok[None], -slopes * dist[None], NEG))
    return cached, jnp.asarray(np.stack(new), F32)


def _attn_sample_body(new_ref, c0_ref, c1_ref, c2_ref, bw0_ref, bw1_ref, bw2_ref, bn_ref, o_ref, lse_ref, *, s_new):
    nt = (((1,), (1,)), ((), ()))
    lse_ref[0] = jnp.zeros(lse_ref.shape[1:], F32)
    for gi, (cache, bw_ref) in enumerate(zip((c0_ref, c1_ref, c2_ref), (bw0_ref, bw1_ref, bw2_ref))):
        for h in range(H_G):
            cs = slice(gi * GW + h * HD, gi * GW + (h + 1) * HD)
            c0 = gi * 3 * GW + h * HD
            q = new_ref[0, :, c0:c0 + HD] * (HD ** -0.5)
            k_new = new_ref[0, :, GW + c0:GW + c0 + HD]
            v_new = new_ref[0, :, 2 * GW + c0:2 * GW + c0 + HD]
            kt = cache[0, 0, h].astype(BF16)
            vt = cache[0, 1, h].astype(BF16)
            sw = jnp.dot(q.astype(BF16), kt, preferred_element_type=F32) + bw_ref[h]
            sn = lax.dot_general(q, k_new, nt, preferred_element_type=F32, precision=HIGHEST) + bn_ref[gi, h]
            mx = jnp.maximum(jnp.max(sw, axis=-1, keepdims=True), jnp.max(sn, axis=-1, keepdims=True))
            pw = jnp.exp(sw - mx)
            pn = jnp.exp(sn - mx)
            den = jnp.sum(pw, axis=-1, keepdims=True) + jnp.sum(pn, axis=-1, keepdims=True)
            o = (lax.dot_general(pw.astype(BF16), vt, nt, preferred_element_type=F32)
                 + jnp.dot(pn, v_new, preferred_element_type=F32, precision=HIGHEST)) / den
            o_ref[0, :, cs] = o[0:s_new]
            lse_ref[0, :, gi * LANES + h:gi * LANES + h + 1] = (mx + jnp.log(den))[0:s_new]


def _attn_sample(att, caches):
    b, s_new, c = att.shape
    assert s_new <= min(SUBLANES, ATT_DILS[1])
    new = jnp.pad(att, ((0, 0), (0, SUBLANES - s_new), (0, 0)))
    bufs, specs = [], []
    for dil, cache in zip(ATT_DILS, caches):
        lw = cache.shape[1]
        assert lw == dil * STEPS
        bufs.append(jnp.transpose(cache, (0, 2, 3, 4, 1)))
        specs.append(pl.BlockSpec((1, 2, H_G, HD, lw), lambda bi: (bi, 0, 0, 0, 0)))
    bias_cached, bias_new = _sample_bias_tables(s_new)
    ng = len(ATT_DILS)
    o, lse = pl.pallas_call(
        functools.partial(_attn_sample_body, s_new=s_new),
        grid=(b,),
        in_specs=[pl.BlockSpec((1, SUBLANES, c), lambda bi: (bi, 0, 0))] + specs
        + [pl.BlockSpec(bc.shape, lambda bi: (0, 0, 0)) for bc in bias_cached]
        + [pl.BlockSpec(bias_new.shape, lambda bi: (0, 0, 0, 0))],
        out_specs=[pl.BlockSpec((1, s_new, D_ATT), lambda bi: (bi, 0, 0)),
                   pl.BlockSpec((1, s_new, ng * LANES), lambda bi: (bi, 0, 0))],
        out_shape=[jax.ShapeDtypeStruct((b, s_new, D_ATT), F32),
                   jax.ShapeDtypeStruct((b, s_new, ng * LANES), F32)],
        compiler_params=pltpu.CompilerParams(dimension_semantics=("parallel",), vmem_limit_bytes=VMEM_LIMIT),
        name="attn_sample",
    )(new, *bufs, *bias_cached, bias_new)
    return o.reshape(b * s_new, D_ATT), lse.reshape(b * s_new, ng * LANES)


HIST = SUBLANES


def _bdot(a, b):
    return jnp.dot(a.astype(BF16), b.astype(BF16), preferred_element_type=F32)


def _delta_prep_body(u_ref, prev_ref, small_ref, cbuf_ref, cw_ref, alog_ref, dtb_ref,
                     uo_ref, wq_ref, ak_ref, eg_ref, ext, *, t_valid, cp):
    c = pl.program_id(1)
    rows = cp * CHUNK

    @pl.when(c == 0)
    def _():
        ext[0:HIST, :] = cbuf_ref[0]

    @pl.when(c > 0)
    def _():
        ext[0:HIST, :] = prev_ref[0]

    ext[HIST:HIST + rows, :] = u_ref[0]
    conv = ext[HIST - 3:HIST - 3 + rows, :] * cw_ref[0:1, :]
    for i in range(1, CONV_W):
        conv = conv + ext[HIST - 3 + i:HIST - 3 + i + rows, :] * cw_ref[i:i + 1, :]
    act = conv * _sigmoid(conv)

    sm = small_ref[0]
    bmat = _sigmoid(sm)
    xg = sm + dtb_ref[...]
    gmat = -jnp.exp(alog_ref[...]) * (jnp.maximum(xg, 0.0) + jnp.log1p(jnp.exp(-jnp.abs(xg))))
    if t_valid < rows:
        live = lax.broadcasted_iota(jnp.int32, (rows, 1), 0) < t_valid
        act = jnp.where(live, act, 0.0)
        bmat = jnp.where(live, bmat, 0.0)
        gmat = jnp.where(live, gmat, 0.0)

    ri = lax.broadcasted_iota(jnp.int32, (CHUNK, CHUNK), 0)
    ci = lax.broadcasted_iota(jnp.int32, (CHUNK, CHUNK), 1)
    tril = ri >= ci
    strict = ri > ci

    pairs = [(j, h) for j in range(cp) for h in range(H_D)]
    rsl = lambda j: slice(j * CHUNK, (j + 1) * CHUNK)
    gcs, gcts = [], []
    for j in range(cp):
        gc_j = jnp.dot(tril.astype(F32), gmat[rsl(j)], preferred_element_type=F32, precision=HIGHEST)
        gcs.append(gc_j)
        gcts.append(gc_j.T)
    q = jnp.stack([act[rsl(j), h * DK:(h + 1) * DK] for j, h in pairs])
    k = jnp.stack([act[rsl(j), D_DK + h * DK:D_DK + (h + 1) * DK] for j, h in pairs])
    v = jnp.stack([act[rsl(j), 2 * D_DK + h * DV:2 * D_DK + (h + 1) * DV] for j, h in pairs])
    beta = jnp.stack([bmat[rsl(j), h:h + 1] for j, h in pairs])
    gcol = jnp.stack([gcs[j][:, H_D + h:H_D + h + 1] for j, h in pairs])
    grow = jnp.stack([gcts[j][H_D + h:H_D + h + 1, :] for j, h in pairs])
    glast = jnp.stack([gcs[j][CHUNK - 1:CHUNK, H_D + h:H_D + h + 1] for j, h in pairs])

    qn = q * lax.rsqrt(jnp.sum(q * q, axis=-1, keepdims=True) + NORM_EPS) * (DK ** -0.5)
    kn = k * lax.rsqrt(jnp.sum(k * k, axis=-1, keepdims=True) + NORM_EPS)
    decay = jnp.where(tril, jnp.exp(jnp.where(tril, gcol - grow, 0.0)), 0.0)
    eg = jnp.exp(gcol)
    kb = kn * beta
    kn_b = kn.astype(BF16)
    a_kk = jnp.einsum('bik,bjk->bij', kb.astype(BF16), kn_b, preferred_element_type=F32)
    a_qk = jnp.einsum('bik,bjk->bij', qn.astype(BF16), kn_b, preferred_element_type=F32)
    pw = jnp.where(strict, -(a_kk * decay), 0.0)
    tm1 = pw
    for _ in range(int(math.log2(CHUNK)) - 1):
        pw_b = pw.astype(BF16)
        pw = jnp.einsum('bij,bjk->bik', pw_b, pw_b, preferred_element_type=F32)
        tm1 = tm1 + pw + jnp.einsum('bij,bjk->bik', tm1.astype(BF16), pw.astype(BF16), preferred_element_type=F32)
    vb = v * beta
    kbg = kb * eg
    tm1_b = tm1.astype(BF16)
    u = vb + jnp.einsum('bij,bjk->bik', tm1_b, vb.astype(BF16), preferred_element_type=F32)
    w = (kbg + jnp.einsum('bij,bjk->bik', tm1_b, kbg.astype(BF16), preferred_element_type=F32)).astype(BF16)
    qd = (qn * eg).astype(BF16)
    aqd = (a_qk * decay).astype(BF16)
    kd = kn * jnp.exp(glast - gcol)
    egl = jnp.exp(glast)
    for i, (j, h) in enumerate(pairs):
        uo_ref[0, rsl(j), h * DV:(h + 1) * DV] = u[i]
        wq_ref[0, j, h, 0:CHUNK, :] = w[i]
        wq_ref[0, j, h, CHUNK:2 * CHUNK, :] = qd[i]
        ak_ref[0, j, h, 0:CHUNK, :] = aqd[i]
        ak_ref[0, j, h, CHUNK:CHUNK + DK, :] = kd[i].T.astype(BF16)
        eg_ref[0, j, h:h + 1, :] = jnp.broadcast_to(egl[i], (1, LANES))
    for j in range(cp):
        eg_ref[0, j, H_D:SUBLANES, :] = jnp.zeros((SUBLANES - H_D, LANES), F32)


def _delta_rec_body(u_ref, wq_ref, ak_ref, eg_ref, z_ref, s0_ref, nw_ref, o_ref, sout_ref, S, *, bb, nc):
    c = pl.program_id(1)

    @pl.when(c == 0)
    def _():
        S[...] = s0_ref[...]

    for b in range(bb):
        for h in range(H_D):
            cs = slice(h * DV, (h + 1) * DV)
            s_h = S[b, h]
            x = jnp.dot(wq_ref[b, 0, h], s_h.astype(BF16), preferred_element_type=F32)
            vn = u_ref[b, :, cs] - x[0:CHUNK]
            y = jnp.dot(ak_ref[b, 0, h], vn.astype(BF16), preferred_element_type=F32)
            o = x[CHUNK:2 * CHUNK] + y[0:CHUNK]
            S[b, h] = s_h * eg_ref[b, 0, h:h + 1, :] + y[CHUNK:CHUNK + DK]
            od = o * lax.rsqrt(jnp.mean(o * o, axis=-1, keepdims=True) + NORM_EPS) * nw_ref[...]
            zh = z_ref[b, :, cs]
            o_ref[b, :, cs] = od * (zh * _sigmoid(zh))

    @pl.when(c == nc - 1)
    def _():
        sout_ref[...] = S[...]


def _lane_row(vals, offset):
    row = jnp.zeros((1, LANES), F32)
    return row.at[0, offset:offset + vals.shape[0]].set(vals.astype(F32))


def _delta_net(dqkv, small, z, conv_buf, s0, conv_w, a_log, dt_bias, delta_norm_w):
    b, t, _ = dqkv.shape
    tp = -(-t // CHUNK) * CHUNK
    nc = tp // CHUNK
    assert tp == t or nc == 1
    if tp != t:
        pad = ((0, 0), (0, tp - t), (0, 0))
        dqkv, small, z = jnp.pad(dqkv, pad), jnp.pad(small, pad), jnp.pad(z, pad)
    cbuf = jnp.pad(conv_buf, ((0, 0), (HIST - (CONV_W - 1), 0), (0, 0)))
    cw = jnp.pad(conv_w, ((0, SUBLANES - CONV_W), (0, 0)))
    cp = next(c for c in (4, 2, 1) if nc % c == 0)
    rows = cp * CHUNK
    full2 = lambda shape: pl.BlockSpec(shape, lambda bi, ci: (0,) * len(shape))
    u, wq, ak, eg = pl.pallas_call(
        functools.partial(_delta_prep_body, t_valid=t if nc == 1 else rows, cp=cp),
        grid=(b, nc // cp),
        in_specs=[pl.BlockSpec((1, rows, D_CONV), lambda bi, ci: (bi, ci, 0)),
                  pl.BlockSpec((1, HIST, D_CONV), lambda bi, ci: (bi, jnp.maximum(ci * (rows // HIST) - 1, 0), 0)),
                  pl.BlockSpec((1, rows, LANES), lambda bi, ci: (bi, ci, 0)),
                  pl.BlockSpec((1, HIST, D_CONV), lambda bi, ci: (bi, 0, 0)),
                  full2((SUBLANES, D_CONV)), full2((1, LANES)), full2((1, LANES))],
        out_specs=[pl.BlockSpec((1, rows, D_DV), lambda bi, ci: (bi, ci, 0)),
                   pl.BlockSpec((1, cp, H_D, 2 * CHUNK, DK), lambda bi, ci: (bi, ci, 0, 0, 0)),
                   pl.BlockSpec((1, cp, H_D, CHUNK + DK, CHUNK), lambda bi, ci: (bi, ci, 0, 0, 0)),
                   pl.BlockSpec((1, cp, SUBLANES, LANES), lambda bi, ci: (bi, ci, 0, 0))],
        out_shape=[jax.ShapeDtypeStruct((b, tp, D_DV), F32),
                   jax.ShapeDtypeStruct((b, nc, H_D, 2 * CHUNK, DK), BF16),
                   jax.ShapeDtypeStruct((b, nc, H_D, CHUNK + DK, CHUNK), BF16),
                   jax.ShapeDtypeStruct((b, nc, SUBLANES, LANES), F32)],
        scratch_shapes=[pltpu.VMEM((HIST + rows, D_CONV), F32)],
        compiler_params=pltpu.CompilerParams(dimension_semantics=("parallel", "parallel"),
                                             vmem_limit_bytes=VMEM_LIMIT),
        name="delta_prep",
    )(dqkv, dqkv, small, cbuf, cw, _lane_row(a_log, H_D), _lane_row(dt_bias, H_D))

    bb = 4 if b % 4 == 0 else 1
    o, s_new = pl.pallas_call(
        functools.partial(_delta_rec_body, bb=bb, nc=nc),
        grid=(b // bb, nc),
        in_specs=[pl.BlockSpec((bb, CHUNK, D_DV), lambda bi, ci: (bi, ci, 0)),
                  pl.BlockSpec((bb, 1, H_D, 2 * CHUNK, DK), lambda bi, ci: (bi, ci, 0, 0, 0)),
                  pl.BlockSpec((bb, 1, H_D, CHUNK + DK, CHUNK), lambda bi, ci: (bi, ci, 0, 0, 0)),
                  pl.BlockSpec((bb, 1, SUBLANES, LANES), lambda bi, ci: (bi, ci, 0, 0)),
                  pl.BlockSpec((bb, CHUNK, D_DV), lambda bi, ci: (bi, ci, 0)),
                  pl.BlockSpec((bb, H_D, DK, DV), lambda bi, ci: (bi, 0, 0, 0)),
                  full2((1, DV))],
        out_specs=[pl.BlockSpec((bb, CHUNK, D_DV), lambda bi, ci: (bi, ci, 0)),
                   pl.BlockSpec((bb, H_D, DK, DV), lambda bi, ci: (bi, 0, 0, 0))],
        out_shape=[jax.ShapeDtypeStruct((b, tp, D_DV), F32), jax.ShapeDtypeStruct((b, H_D, DK, DV), F32)],
        scratch_shapes=[pltpu.VMEM((bb, H_D, DK, DV), F32)],
        compiler_params=pltpu.CompilerParams(dimension_semantics=("parallel", "arbitrary"),
                                             vmem_limit_bytes=VMEM_LIMIT),
        name="delta_recurrence",
    )(u, wq, ak, eg, z, s0, delta_norm_w.reshape(1, DV).astype(F32))
    return o[:, :t].reshape(b * t, D_DV), s_new


def _lane_expand(cols, width):
    tm = cols[0].shape[0]
    seg = lax.broadcasted_iota(jnp.int32, (tm, len(cols) * width), 1) // width
    out = jnp.broadcast_to(cols[-1], seg.shape)
    for i in range(len(cols) - 2, -1, -1):
        out = jnp.where(seg == i, cols[i], out)
    return out


def _token_order(ref, scr):
    d, nc = ref.shape[1], ref.shape[3] // LANES
    if d == 1:
        return ref[0, 0]
    for r in range(d):
        for c in range(nc):
            scr[c, pl.ds(r, ref.shape[2], stride=d), :] = ref[0, r, :, c * LANES:(c + 1) * LANES]
    return jnp.concatenate([scr[c] for c in range(nc)], axis=-1)


def _mix_body(x_ref, gate_ref, o0_ref, o1_ref, o2_ref, l0_ref, l1_ref, l2_ref, od_ref,
              wa_ref, wd_ref, wo_ref, bg_ref, g1_ref, b1_ref, rw_ref, rb_ref,
              h_ref, ti_ref, tg_ref, *scratch):
    o_scr, l_scr = scratch[:3], scratch[3:]
    lses = [_token_order(l, s) for l, s in zip((l0_ref, l1_ref, l2_ref), l_scr)]
    mx = jnp.maximum(jnp.maximum(lses[0], lses[1]), lses[2])
    es = [jnp.exp(l - mx) for l in lses]
    inv = 1.0 / (es[0] + es[1] + es[2])
    o_att = None
    for e_g, o_g, s in zip(es, (o0_ref, o1_ref, o2_ref), o_scr):
        wgt = e_g * inv
        term = _lane_expand([wgt[:, h:h + 1] for h in range(H_G)], HD) * _token_order(o_g, s)
        o_att = term if o_att is None else o_att + term
    ga = _sigmoid(gate_ref[:, :D_MODEL] + bg_ref[:, :D_MODEL])
    gd = _sigmoid(gate_ref[:, D_MODEL:] + bg_ref[:, D_MODEL:])
    merged = (ga * jnp.dot(o_att.astype(BF16), wa_ref[...], preferred_element_type=F32)
              + gd * jnp.dot(od_ref[...].astype(BF16), wd_ref[...], preferred_element_type=F32))
    mix = jnp.dot(merged.astype(BF16), wo_ref[...], preferred_element_type=F32)
    h = _layer_norm(DN_ALPHA * x_ref[...] + mix, g1_ref[...], b1_ref[...])
    _rows_to_tiles(h_ref, h)

    h_hi = h.astype(BF16)
    h_lo = (h - h_hi.astype(F32)).astype(BF16)
    part = jnp.dot(h_hi, rw_ref[...], preferred_element_type=F32)
    logits = (part[:, :LANES] + part[:, LANES:]
              + jnp.dot(h_lo, rw_ref[:, :LANES], preferred_element_type=F32) + rb_ref[...])
    lane = lax.broadcasted_iota(jnp.int32, logits.shape, 1)
    lane_f = lane.astype(F32)
    ti = jnp.zeros(logits.shape, F32)
    tv = jnp.zeros(logits.shape, F32)
    top = None
    for kk in range(TOP_K):
        m = jnp.max(logits, axis=-1, keepdims=True)
        idx = jnp.min(jnp.where(logits == m, lane_f, float(LANES)), axis=-1, keepdims=True)
        if top is None:
            top = m
        ti = jnp.where(lane == kk, idx, ti)
        tv = jnp.where(lane == kk, jnp.exp(m - top), tv)
        logits = jnp.where(lane_f == idx, -jnp.inf, logits)
    ti_ref[...] = ti.astype(jnp.int32)
    tg_ref[...] = tv / jnp.sum(tv, axis=-1, keepdims=True)


def _mix_and_route(x2d, gates, o_groups, lse_groups, o_del, wa, wd, wo, b_gate, ln_g, ln_b, rw, rb, tm):
    n = x2d.shape[0]
    tpb = o_groups[0].shape[1] * o_groups[0].shape[2] // tm
    row = lambda c: pl.BlockSpec((tm, c), lambda i: (i, 0))
    full = lambda a: pl.BlockSpec(a.shape, lambda i: (0, 0))

    def res(a):
        d = a.shape[1]
        return pl.BlockSpec((1, d, tm // d, a.shape[3]), lambda i: (i // tpb, 0, i % tpb, 0))

    consts = (wa, wd, wo, b_gate, ln_g, ln_b, rw, rb)
    return pl.pallas_call(
        _mix_body,
        grid=(n // tm,),
        in_specs=[row(D_MODEL), row(2 * D_MODEL)] + [res(a) for a in o_groups] + [res(a) for a in lse_groups]
        + [row(D_DV)] + [full(a) for a in consts],
        out_specs=[pl.BlockSpec((tm * TOK_ROWS, LANES), lambda i: (i, 0)), row(LANES), row(LANES)],
        out_shape=[jax.ShapeDtypeStruct((n * TOK_ROWS, LANES), F32), jax.ShapeDtypeStruct((n, LANES), jnp.int32),
                   jax.ShapeDtypeStruct((n, LANES), F32)],
        scratch_shapes=[pltpu.VMEM((GW // LANES, tm, LANES), F32)] * len(o_groups)
        + [pltpu.VMEM((1, tm, LANES), F32)] * len(lse_groups),
        compiler_params=pltpu.CompilerParams(dimension_semantics=("parallel",), vmem_limit_bytes=VMEM_LIMIT),
        name="mix_and_route",
    )(x2d, gates, *o_groups, *lse_groups, o_del, *consts)


def _rank_body(ti_ref, rank_ref, cnt_ref, carry):
    i = pl.program_id(0)

    @pl.when(i == 0)
    def _():
        carry[...] = jnp.zeros_like(carry)

    ti = ti_ref[...]
    tm = ti.shape[0]
    lane = lax.broadcasted_iota(jnp.int32, ti.shape, 1)
    sel = [lane == ti[:, kk:kk + 1] for kk in range(TOP_K)]
    hit = sel[0]
    for kk in range(1, TOP_K):
        hit = hit | sel[kk]
    cnt = hit.astype(F32)
    ri = lax.broadcasted_iota(jnp.int32, (tm, tm), 0)
    ci = lax.broadcasted_iota(jnp.int32, (tm, tm), 1)
    incl = jnp.dot((ri >= ci).astype(BF16), cnt.astype(BF16), preferred_element_type=F32)
    before = incl - cnt + carry[...]
    rank = jnp.zeros(ti.shape, jnp.int32)
    for kk in range(TOP_K):
        r = jnp.sum(jnp.where(sel[kk], before, 0.0), axis=-1, keepdims=True)
        rank = jnp.where(lane == kk, r.astype(jnp.int32), rank)
    rank_ref[...] = rank
    carry[...] = carry[...] + incl[tm - 1:tm, :]
    cnt_ref[...] = carry[...]


def _route_ranks(top_i, tm):
    n = top_i.shape[0]
    return pl.pallas_call(
        _rank_body,
        grid=(n // tm,),
        in_specs=[pl.BlockSpec((tm, LANES), lambda i: (i, 0))],
        out_specs=[pl.BlockSpec((tm, LANES), lambda i: (i, 0)), pl.BlockSpec((1, LANES), lambda i: (0, 0))],
        out_shape=[jax.ShapeDtypeStruct((n, LANES), jnp.int32), jax.ShapeDtypeStruct((1, LANES), F32)],
        scratch_shapes=[pltpu.VMEM((1, LANES), F32)],
        compiler_params=pltpu.CompilerParams(dimension_semantics=("arbitrary",), vmem_limit_bytes=VMEM_LIMIT),
        name="route_ranks",
    )(top_i)


def _rows_to_tiles(ref, val):
    m = val.shape[0]
    for s in range(TOK_ROWS):
        ref[pl.ds(s, m, stride=TOK_ROWS), :] = val[:, s * LANES:(s + 1) * LANES]


def _tiles_to_rows(ref, m):
    return jnp.concatenate([ref[pl.ds(s, m, stride=TOK_ROWS), :] for s in range(TOK_ROWS)], axis=-1)


def _dispatch_body(pend_ref, padded_ref, dest_ref, h_ref, xs_out, zero, sem, zsem, *, max_tail):
    tm = h_ref.shape[0] // TOK_ROWS
    blk = MOE_BM * TOK_ROWS

    @pl.when(pl.program_id(0) == 0)
    def _():
        zero[...] = jnp.zeros_like(zero)

        def last_block(e):
            start = pl.multiple_of((pend_ref[e] - MOE_BM) * TOK_ROWS, blk)
            return pltpu.make_async_copy(zero, xs_out.at[pl.ds(start, blk)], zsem)

        def tail_block(j):
            start = pl.multiple_of((pend_ref[N_EXP - 1] + j * MOE_BM) * TOK_ROWS, blk)
            return pltpu.make_async_copy(zero, xs_out.at[pl.ds(start, blk)], zsem)

        n_slot = xs_out.shape[0] // TOK_ROWS
        for e in range(N_EXP):
            @pl.when(padded_ref[e] > 0)
            def _():
                last_block(e).start()
        for j in range(max_tail):
            @pl.when(pend_ref[N_EXP - 1] + j * MOE_BM < n_slot)
            def _():
                tail_block(j).start()
        for e in range(N_EXP):
            @pl.when(padded_ref[e] > 0)
            def _():
                last_block(e).wait()
        for j in range(max_tail):
            @pl.when(pend_ref[N_EXP - 1] + j * MOE_BM < n_slot)
            def _():
                tail_block(j).wait()

    def issue(t, carry):
        src = h_ref.at[pl.ds(pl.multiple_of(t * TOK_ROWS, TOK_ROWS), TOK_ROWS)]
        for kk in range(TOP_K):
            d = pl.multiple_of(dest_ref[t * TOP_K + kk] * TOK_ROWS, TOK_ROWS)
            pltpu.make_async_copy(src, xs_out.at[pl.ds(d, TOK_ROWS)], sem).start(priority=kk % 2)
        return carry

    lax.fori_loop(0, tm, issue, 0, unroll=8)
    for kk in range(TOP_K):
        pltpu.make_async_copy(h_ref, xs_out.at[pl.ds(0, tm * TOK_ROWS)], sem).wait()


def _dispatch(h_tiles, dest_flat, pad_end, padded, n_slot, tm):
    n = h_tiles.shape[0] // TOK_ROWS
    grid_spec = pltpu.PrefetchScalarGridSpec(
        num_scalar_prefetch=2,
        grid=(n // tm,),
        in_specs=[pl.BlockSpec((tm * TOP_K,), lambda i, pe, pd: (i,), memory_space=pltpu.SMEM),
                  pl.BlockSpec((tm * TOK_ROWS, LANES), lambda i, pe, pd: (i, 0))],
        out_specs=pl.BlockSpec(memory_space=pl.ANY),
        scratch_shapes=[pltpu.VMEM((MOE_BM * TOK_ROWS, LANES), F32), pltpu.SemaphoreType.DMA(()),
                        pltpu.SemaphoreType.DMA(())],
    )
    max_tail = n_slot // MOE_BM - (n * TOP_K) // MOE_BM
    return pl.pallas_call(
        functools.partial(_dispatch_body, max_tail=max_tail),
        grid_spec=grid_spec,
        out_shape=jax.ShapeDtypeStruct((n_slot * TOK_ROWS, LANES), F32),
        compiler_params=pltpu.CompilerParams(dimension_semantics=("arbitrary",), vmem_limit_bytes=VMEM_LIMIT,
                                             has_side_effects=True, disable_bounds_checks=True),
        name="moe_dispatch",
    )(pad_end, padded, dest_flat, h_tiles)


def _expert_body(be_ref, nu_ref, x_ref, wgu_ref, bgu_ref, wd_ref, bd_ref, o_ref, wgu_bf, wd_bf):
    i = pl.program_id(0)
    e = be_ref[i]
    prev = be_ref[jnp.maximum(i - 1, 0)]

    @pl.when((i == 0) | (e != prev))
    def _():
        wgu_bf[...] = wgu_ref[0].astype(BF16)
        wd_bf[...] = wd_ref[0].astype(BF16)

    @pl.when(i < nu_ref[0])
    def _():
        x = _tiles_to_rows(x_ref, MOE_BM)
        gu = jnp.dot(x.astype(BF16), wgu_bf[...], preferred_element_type=F32) + bgu_ref[0]
        gt = jnp.minimum(gu[:, :D_FF], SWIGLU_LIMIT)
        up = jnp.clip(gu[:, D_FF:], -SWIGLU_LIMIT, SWIGLU_LIMIT)
        act = (up + 1.0) * gt * _sigmoid(SWIGLU_ALPHA * gt)
        _rows_to_tiles(o_ref, jnp.dot(act.astype(BF16), wd_bf[...], preferred_element_type=F32) + bd_ref[0])

    @pl.when(i >= nu_ref[0])
    def _():
        o_ref[...] = jnp.zeros_like(o_ref)


def _experts(xs, blk_exp, n_used, w_gu, b_gu, w_down, b_down):
    n_slot = xs.shape[0] // TOK_ROWS
    n_blk = n_slot // MOE_BM
    rows = lambda i, be, nu: (jnp.minimum(i, nu[0] - 1), 0)
    grid_spec = pltpu.PrefetchScalarGridSpec(
        num_scalar_prefetch=2,
        grid=(n_blk,),
        in_specs=[pl.BlockSpec((MOE_BM * TOK_ROWS, LANES), rows),
                  pl.BlockSpec((1, D_MODEL, 2 * D_FF), lambda i, be, nu: (be[i], 0, 0)),
                  pl.BlockSpec((1, 1, 2 * D_FF), lambda i, be, nu: (be[i], 0, 0)),
                  pl.BlockSpec((1, D_FF, D_MODEL), lambda i, be, nu: (be[i], 0, 0)),
                  pl.BlockSpec((1, 1, D_MODEL), lambda i, be, nu: (be[i], 0, 0))],
        out_specs=pl.BlockSpec((MOE_BM * TOK_ROWS, LANES), lambda i, be, nu: (i, 0)),
        scratch_shapes=[pltpu.VMEM((D_MODEL, 2 * D_FF), BF16), pltpu.VMEM((D_FF, D_MODEL), BF16)],
    )
    return pl.pallas_call(
        _expert_body,
        grid_spec=grid_spec,
        out_shape=jax.ShapeDtypeStruct((n_slot * TOK_ROWS, LANES), F32),
        compiler_params=pltpu.CompilerParams(dimension_semantics=("arbitrary",), vmem_limit_bytes=VMEM_LIMIT),
        name="moe_experts",
    )(blk_exp, n_used, xs, w_gu, b_gu.reshape(N_EXP, 1, 2 * D_FF), w_down, b_down.reshape(N_EXP, 1, D_MODEL))


def _combine_body(dest_ref, gate_ref, h_ref, g_ref, b_ref, ys_ref, y_ref, buf, sem):
    tm = h_ref.shape[0] // TOK_ROWS

    def issue(t, carry):
        row = pl.multiple_of(t * TOK_ROWS, TOK_ROWS)
        for kk in range(TOP_K):
            d = pl.multiple_of(dest_ref[t * TOP_K + kk] * TOK_ROWS, TOK_ROWS)
            pltpu.make_async_copy(ys_ref.at[pl.ds(d, TOK_ROWS)], buf.at[kk, pl.ds(row, TOK_ROWS)],
                                  sem).start(priority=kk % 2)
        return carry

    lax.fori_loop(0, tm, issue, 0, unroll=8)
    for kk in range(TOP_K):
        pltpu.make_async_copy(ys_ref.at[pl.ds(0, tm * TOK_ROWS)], buf.at[kk], sem).wait()
    gate = gate_ref[...]
    moe = gate[:, 0:1] * _tiles_to_rows(buf.at[0], tm)
    for kk in range(1, TOP_K):
        moe = moe + gate[:, kk:kk + 1] * _tiles_to_rows(buf.at[kk], tm)
    y_ref[...] = _layer_norm(DN_ALPHA * _tiles_to_rows(h_ref, tm) + moe, g_ref[...], b_ref[...])


def _combine(ys, dest_flat, gate, h_tiles, ln_g, ln_b, tm):
    n = h_tiles.shape[0] // TOK_ROWS
    return pl.pallas_call(
        _combine_body,
        grid=(n // tm,),
        in_specs=[pl.BlockSpec((tm * TOP_K,), lambda i: (i,), memory_space=pltpu.SMEM),
                  pl.BlockSpec((tm, LANES), lambda i: (i, 0)),
                  pl.BlockSpec((tm * TOK_ROWS, LANES), lambda i: (i, 0)),
                  pl.BlockSpec((1, D_MODEL), lambda i: (0, 0)),
                  pl.BlockSpec((1, D_MODEL), lambda i: (0, 0)),
                  pl.BlockSpec(memory_space=pl.ANY)],
        out_specs=pl.BlockSpec((tm, D_MODEL), lambda i: (i, 0)),
        out_shape=jax.ShapeDtypeStruct((n, D_MODEL), F32),
        scratch_shapes=[pltpu.VMEM((TOP_K, tm * TOK_ROWS, LANES), F32), pltpu.SemaphoreType.DMA(())],
        compiler_params=pltpu.CompilerParams(dimension_semantics=("arbitrary",), vmem_limit_bytes=VMEM_LIMIT,
                                             disable_bounds_checks=True),
        name="moe_combine",
    )(dest_flat, gate, h_tiles, ln_g, ln_b, ys)


def _moe(h, top_i, top_g, w_gu, b_gu, w_down, b_down, ln_g, ln_b, tm):
    n = top_i.shape[0]
    rank, counts = _route_ranks(top_i, tm)
    counts = counts[0, :N_EXP].astype(jnp.int32)
    padded = (counts + MOE_BM - 1) // MOE_BM * MOE_BM
    pad_end = jnp.cumsum(padded)
    pad_start = pad_end - padded
    n_blk = -(-(n * TOP_K + N_EXP * (MOE_BM - 1)) // MOE_BM)
    dest = (pad_start[top_i[:, :TOP_K]] + rank[:, :TOP_K]).reshape(n * TOP_K)
    blk_row0 = jnp.arange(n_blk, dtype=jnp.int32) * MOE_BM
    blk_exp = jnp.minimum(jnp.sum((pad_end[None, :] <= blk_row0[:, None]).astype(jnp.int32), axis=1), N_EXP - 1)
    n_used = (pad_end[-1:] // MOE_BM).astype(jnp.int32)
    xs = _dispatch(h, dest, pad_end.astype(jnp.int32), padded, n_blk * MOE_BM, tm)
    ys = _experts(xs, blk_exp, n_used, w_gu, b_gu, w_down, b_down)
    return _combine(ys, dest, top_g, h, ln_g, ln_b, tm)


def _pad_cols(a, width, fill=0.0):
    return jnp.pad(a, ((0, 0), (0, width - a.shape[1])), constant_values=fill)


def _decoder_layer(x, caches, conv_buf, s0, w_in, b_gate, conv_w, a_log, dt_bias, delta_norm_w,
                   w_branch_attn, w_branch_delta, w_out, ln1_g, ln1_b, router_w, router_b,
                   w_gu, b_gu, w_down, b_down, ln2_g, ln2_b):
    b, t, _ = x.shape
    n = b * t
    tm = min(256, n)
    x2d = x.reshape(n, D_MODEL)

    c_att, c_dz, c_small = 3 * D_ATT, 3 * D_ATT + D_CONV, 3 * D_ATT + D_CONV + D_DV
    c_gate = c_small + 2 * H_D
    ng = len(ATT_DILS)
    w_groups = [jnp.concatenate([w_in[:, part * D_ATT + gi * GW:part * D_ATT + (gi + 1) * GW] for part in range(3)],
                                axis=1) for gi in range(ng)]
    ws = w_groups + [w_in[:, c_att:c_dz], w_in[:, c_dz:c_small],
                     _pad_cols(w_in[:, c_small:c_gate], LANES), w_in[:, c_gate:]]
    dils = (ATT_DILS if caches is None else (1,) * ng) + (1, 1, 1, 1)
    outs = _in_projection(x2d, [w.astype(BF16) for w in ws], dils, t, tm)
    att_groups, (dqkv, z, small, gates) = outs[:ng], outs[ng:]
    dqkv3 = dqkv.reshape(b, t, D_CONV)

    kv_new = []
    if caches is None:
        o_groups, lse_groups = [], []
        for dil, a in zip(ATT_DILS, att_groups):
            a = a.reshape(b, dil, t // dil, 3 * GW)
            o_g, lse_g = _attn_prompt(a, dil)
            o_groups.append(o_g)
            lse_groups.append(lse_g)
            win = min(dil * STEPS, t)
            last = a[:, :, (t - win) // dil:, GW:]
            last = jnp.transpose(last, (0, 2, 1, 3)).reshape(b, win, 2, H_G, HD)
            kv_new.append(last)
        conv_buf = jnp.zeros((b, CONV_W - 1, D_CONV), F32)
        s0 = jnp.zeros((b, H_D, DK, DV), F32)
    else:
        att3 = jnp.concatenate(att_groups, axis=1).reshape(b, t, 3 * D_ATT)
        o_all, lse_all = _attn_sample(att3, caches)
        o_groups = [o_all[:, gi * GW:(gi + 1) * GW].reshape(1, 1, n, GW) for gi in range(ng)]
        lse_groups = [lse_all[:, gi * LANES:(gi + 1) * LANES].reshape(1, 1, n, LANES) for gi in range(ng)]
        for a in att_groups:
            kv_new.append(a[:, GW:].reshape(b, t, 2, H_G, HD))

    o_del, s_new = _delta_net(dqkv3, small.reshape(b, t, LANES), z.reshape(b, t, D_DV), conv_buf, s0,
                              conv_w, a_log, dt_bias, delta_norm_w)
    conv_new = jnp.concatenate([conv_buf, dqkv3], axis=1)[:, -(CONV_W - 1):] if t < CONV_W - 1 \
        else dqkv3[:, t - (CONV_W - 1):]

    rw = _pad_cols(router_w, LANES)
    rw_hi = rw.astype(BF16)
    rw = jnp.concatenate([rw_hi, (rw - rw_hi.astype(F32)).astype(BF16)], axis=1)
    rb = _pad_cols(router_b.reshape(1, N_EXP), LANES, fill=NEG)
    h, top_i, top_g = _mix_and_route(
        x2d, gates, o_groups, lse_groups, o_del,
        w_branch_attn.astype(BF16), w_branch_delta.astype(BF16), w_out.astype(BF16),
        b_gate.reshape(1, 2 * D_MODEL), ln1_g.reshape(1, D_MODEL), ln1_b.reshape(1, D_MODEL), rw, rb, tm)
    y = _moe(h, top_i, top_g, w_gu, b_gu, w_down, b_down,
             ln2_g.reshape(1, D_MODEL), ln2_b.reshape(1, D_MODEL), tm)
    return y.reshape(b, t, D_MODEL), kv_new, conv_new, s_new


def kernel(x_prompt, x_sample, cache_kv_w128, cache_kv_w512, cache_kv_w2048, state_conv, state_delta,
           w_in, b_gate, conv_w, a_log, dt_bias, delta_norm_w, w_branch_attn, w_branch_delta, w_out,
           ln1_g, ln1_b, router_w, router_b, w_gu, b_gu, w_down, b_down, ln2_g, ln2_b):
    depth = w_in.shape[0]
    assert depth == 1
    l = 0
    lw = (w_in[l], b_gate[l], conv_w[l], a_log[l], dt_bias[l], delta_norm_w[l], w_branch_attn[l],
          w_branch_delta[l], w_out[l], ln1_g[l], ln1_b[l], router_w[l], router_b[l], w_gu[l], b_gu[l],
          w_down[l], b_down[l], ln2_g[l], ln2_b[l])
    yp, kv_p, cv_p, s_p = _decoder_layer(x_prompt, None, None, None, *lw)
    ys, kv_s, cv_s, s_s = _decoder_layer(x_sample, (cache_kv_w128[l], cache_kv_w512[l], cache_kv_w2048[l]),
                                         state_conv[l], state_delta[l], *lw)
    stk = lambda a: a[None]
    return (yp, ys, stk(kv_p[0]), stk(kv_p[1]), stk(kv_p[2]), stk(cv_p), stk(s_p),
            stk(kv_s[0]), stk(kv_s[1]), stk(kv_s[2]), stk(cv_s), stk(s_s))
```

```python
import functools
import math

import numpy as np
import jax
import jax.numpy as jnp
from jax import lax
from jax.experimental import pallas as pl
from jax.experimental.pallas import tpu as pltpu

F32 = jnp.float32
BF16 = jnp.bfloat16
HIGHEST = lax.Precision.HIGHEST

D_MODEL = 1024
ATT_DILS = (1, 4, 16)
STEPS = 128
H_G = 4
HD = 64
GW = H_G * HD
D_ATT = len(ATT_DILS) * GW
H_D = 4
DK = 128
DV = 128
D_DK = H_D * DK
D_DV = H_D * DV
D_CONV = 2 * D_DK + D_DV
CONV_W = 4
CHUNK = 64
N_EXP = 32
TOP_K = 4
D_FF = D_MODEL
SWIGLU_LIMIT = 7.0
SWIGLU_ALPHA = 1.702
DN_ALPHA = 2.0 ** 0.25
LN_EPS = 1e-5
NORM_EPS = 1e-6
NEG = -1e30

LANES = 128
SUBLANES = 8
VMEM_LIMIT = 56 * 1024 * 1024
MOE_BM = 512
IN_TM = 512
TOK_ROWS = D_MODEL // LANES
assert TOK_ROWS == SUBLANES


def _sigmoid(x):
    return 1.0 / (1.0 + jnp.exp(-x))


def _layer_norm(v, g, b):
    mu = jnp.mean(v, axis=-1, keepdims=True)
    d = v - mu
    var = jnp.mean(d * d, axis=-1, keepdims=True)
    return d * lax.rsqrt(var + LN_EPS) * g + b


def _alibi_slopes():
    return 2.0 ** (-8.0 * np.arange(1, H_G + 1, dtype=np.float64) / H_G)


def _inproj_body(x_ref, *refs, dils):
    nw = len(dils)
    scr = refs[2 * nw] if len(refs) > 2 * nw else None
    x = x_ref[...].astype(BF16)
    for w_ref, o_ref, d in zip(refs[:nw], refs[nw:2 * nw], dils):
        y = jnp.dot(x, w_ref[...], preferred_element_type=F32)
        if d == 1:
            o_ref[...] = y
        else:
            for c in range(y.shape[1] // LANES):
                scr[c] = y[:, c * LANES:(c + 1) * LANES]
            for r in range(d):
                for c in range(y.shape[1] // LANES):
                    o_ref[0, r, :, c * LANES:(c + 1) * LANES] = scr[c, pl.ds(r, y.shape[0] // d, stride=d), :]


def _in_projection(x2d, ws, dils, t, tm):
    n = x2d.shape[0]
    tpb = t // tm
    out_specs, out_shape = [], []
    for w, d in zip(ws, dils):
        c = w.shape[1]
        if d == 1:
            out_specs.append(pl.BlockSpec((tm, c), lambda i: (i, 0)))
            out_shape.append(jax.ShapeDtypeStruct((n, c), F32))
        else:
            assert tm % (d * SUBLANES) == 0 and t % tm == 0
            out_specs.append(pl.BlockSpec((1, d, tm // d, c), lambda i: (i // tpb, 0, i % tpb, 0)))
            out_shape.append(jax.ShapeDtypeStruct((n // t, d, t // d, c), F32))
    wide = max([w.shape[1] for w, d in zip(ws, dils) if d > 1], default=0)
    return pl.pallas_call(
        functools.partial(_inproj_body, dils=tuple(dils)),
        grid=(n // tm,),
        in_specs=[pl.BlockSpec((tm, D_MODEL), lambda i: (i, 0))]
        + [pl.BlockSpec((D_MODEL, w.shape[1]), lambda i: (0, 0), pipeline_mode=pl.Buffered(1)) for w in ws],
        out_specs=out_specs,
        out_shape=out_shape,
        scratch_shapes=[pltpu.VMEM((wide // LANES, tm, LANES), F32)] if wide else [],
        compiler_params=pltpu.CompilerParams(dimension_semantics=("parallel",), vmem_limit_bytes=VMEM_LIMIT),
        name="in_projection",
    )(x2d, *ws)


def _prompt_bias_table(dil):
    qi = np.arange(STEPS)[:, None]
    kj = np.arange(2 * STEPS)[None, :]
    steps = qi + STEPS - kj
    valid = (steps >= 0) & (steps <= STEPS)
    slopes = _alibi_slopes()
    bias = -slopes[:, None, None] * (dil * steps)[None].astype(np.float64)
    later = np.where(valid[None], bias, NEG)
    first = np.where((valid & (kj >= STEPS))[None], bias, NEG)
    return jnp.asarray(np.stack([first, later]).reshape(2, H_G * STEPS, 2 * STEPS), F32)


ATT_QB = 2


def _attn_prompt_body(q_ref, kp_ref, kc_ref, vp_ref, vc_ref, bias_ref, o_ref, lse_ref):
    kk = jnp.concatenate([kp_ref[0, 0], kc_ref[0, 0]], axis=0).astype(BF16)
    vv = jnp.concatenate([vp_ref[0, 0], vc_ref[0, 0]], axis=0).astype(BF16)
    head = lax.broadcasted_iota(jnp.int32, (STEPS, GW), 1) // HD
    lse_ref[0, 0] = jnp.zeros(lse_ref.shape[2:], F32)
    for j in range(ATT_QB):
        qrows = slice(j * STEPS, (j + 1) * STEPS)
        krows = slice(j * STEPS, (j + 2) * STEPS)
        later = jnp.minimum(pl.program_id(2), 1) if j == 0 else 1
        q = q_ref[0, 0, qrows, :] * (HD ** -0.5)
        qs = jnp.concatenate([jnp.where(head == h, q, 0.0) for h in range(H_G)], axis=0).astype(BF16)
        s = lax.dot_general(qs, kk[krows], (((1,), (1,)), ((), ())), preferred_element_type=F32)
        s = s + bias_ref[later]
        mx = jnp.max(s, axis=-1, keepdims=True)
        p = jnp.exp(s - mx)
        den = jnp.sum(p, axis=-1, keepdims=True)
        pv = jnp.dot(p.astype(BF16), vv[krows], preferred_element_type=F32) / den
        lse = mx + jnp.log(den)
        o = jnp.zeros((STEPS, GW), F32)
        for h in range(H_G):
            rows = slice(h * STEPS, (h + 1) * STEPS)
            o = jnp.where(head == h, pv[rows], o)
            lse_ref[0, 0, qrows, h:h + 1] = lse[rows]
        o_ref[0, 0, qrows, :] = o


def _attn_prompt(a, dil):
    b, d, L, c = a.shape
    rows = ATT_QB * STEPS
    assert d == dil and c == 3 * GW and L % rows == 0
    nb = L // rows

    def spec(col, prev):
        if prev:
            return pl.BlockSpec((1, 1, STEPS, GW), lambda bi, r, n: (bi, r, jnp.maximum(ATT_QB * n - 1, 0), col))
        return pl.BlockSpec((1, 1, rows, GW), lambda bi, r, n: (bi, r, n, col))

    return pl.pallas_call(
        _attn_prompt_body,
        grid=(b, dil, nb),
        in_specs=[spec(0, False), spec(1, True), spec(1, False), spec(2, True), spec(2, False),
                  pl.BlockSpec((2, H_G * STEPS, 2 * STEPS), lambda bi, r, n: (0, 0, 0))],
        out_specs=[pl.BlockSpec((1, 1, rows, GW), lambda bi, r, n: (bi, r, n, 0)),
                   pl.BlockSpec((1, 1, rows, LANES), lambda bi, r, n: (bi, r, n, 0))],
        out_shape=[jax.ShapeDtypeStruct((b, dil, L, GW), F32),
                   jax.ShapeDtypeStruct((b, dil, L, LANES), F32)],
        compiler_params=pltpu.CompilerParams(dimension_semantics=("parallel", "parallel", "arbitrary"),
                                             vmem_limit_bytes=VMEM_LIMIT),
        name=f"attn_prompt_d{dil}",
    )(a, a, a, a, a, _prompt_bias_table(dil))


def _sample_bias_tables(s_new):
    slopes = _alibi_slopes()[:, None, None]
    s = np.arange(SUBLANES)[:, None]
    live = s < s_new
    cached = []
    for dil in ATT_DILS:
        lw = dil * STEPS
        dist = lw + s - np.arange(lw)[None, :]
        ok = live & (dist % dil == 0) & (dist <= lw)
        cached.append(jnp.asarray(np.where(ok[None], -slopes * dist[None], NEG), F32))
    s2 = np.arange(SUBLANES)[None, :]
    dist = s - s2
    new = []
    for dil in ATT_DILS:
        ok = live & (s2 < s_new) & (dist >= 0) & (dist % dil == 0)
        new.append(np.where(ok[None], -slopes * dist[None], NEG))
    return cached, jnp.asarray(np.stack(new), F32)


def _attn_sample_body(new_ref, c0_ref, c1_ref, c2_ref, bw0_ref, bw1_ref, bw2_ref, bn_ref, o_ref, lse_ref, *, s_new):
    nt = (((1,), (1,)), ((), ()))
    lse_ref[0] = jnp.zeros(lse_ref.shape[1:], F32)
    for gi, (cache, bw_ref) in enumerate(zip((c0_ref, c1_ref, c2_ref), (bw0_ref, bw1_ref, bw2_ref))):
        for h in range(H_G):
            cs = slice(gi * GW + h * HD, gi * GW + (h + 1) * HD)
            c0 = gi * 3 * GW + h * HD
            q = new_ref[0, :, c0:c0 + HD] * (HD ** -0.5)
            k_new = new_ref[0, :, GW + c0:GW + c0 + HD]
            v_new = new_ref[0, :, 2 * GW + c0:2 * GW + c0 + HD]
            kt = cache[0, 0, h].astype(BF16)
            vt = cache[0, 1, h].astype(BF16)
            sw = jnp.dot(q.astype(BF16), kt, preferred_element_type=F32) + bw_ref[h]
            sn = lax.dot_general(q, k_new, nt, preferred_element_type=F32, precision=HIGHEST) + bn_ref[gi, h]
            mx = jnp.maximum(jnp.max(sw, axis=-1, keepdims=True), jnp.max(sn, axis=-1, keepdims=True))
            pw = jnp.exp(sw - mx)
            pn = jnp.exp(sn - mx)
            den = jnp.sum(pw, axis=-1, keepdims=True) + jnp.sum(pn, axis=-1, keepdims=True)
            o = (lax.dot_general(pw.astype(BF16), vt, nt, preferred_element_type=F32)
                 + jnp.dot(pn, v_new, preferred_element_type=F32, precision=HIGHEST)) / den
            o_ref[0, :, cs] = o[0:s_new]
            lse_ref[0, :, gi * LANES + h:gi * LANES + h + 1] = (mx + jnp.log(den))[0:s_new]


def _attn_sample(att, caches):
    b, s_new, c = att.shape
    assert s_new <= min(SUBLANES, ATT_DILS[1])
    new = jnp.pad(att, ((0, 0), (0, SUBLANES - s_new), (0, 0)))
    bufs, specs = [], []
    for dil, cache in zip(ATT_DILS, caches):
        lw = cache.shape[1]
        assert lw == dil * STEPS
        bufs.append(jnp.transpose(cache, (0, 2, 3, 4, 1)))
        specs.append(pl.BlockSpec((1, 2, H_G, HD, lw), lambda bi: (bi, 0, 0, 0, 0)))
    bias_cached, bias_new = _sample_bias_tables(s_new)
    ng = len(ATT_DILS)
    o, lse = pl.pallas_call(
        functools.partial(_attn_sample_body, s_new=s_new),
        grid=(b,),
        in_specs=[pl.BlockSpec((1, SUBLANES, c), lambda bi: (bi, 0, 0))] + specs
        + [pl.BlockSpec(bc.shape, lambda bi: (0, 0, 0)) for bc in bias_cached]
        + [pl.BlockSpec(bias_new.shape, lambda bi: (0, 0, 0, 0))],
        out_specs=[pl.BlockSpec((1, s_new, D_ATT), lambda bi: (bi, 0, 0)),
                   pl.BlockSpec((1, s_new, ng * LANES), lambda bi: (bi, 0, 0))],
        out_shape=[jax.ShapeDtypeStruct((b, s_new, D_ATT), F32),
                   jax.ShapeDtypeStruct((b, s_new, ng * LANES), F32)],
        compiler_params=pltpu.CompilerParams(dimension_semantics=("parallel",), vmem_limit_bytes=VMEM_LIMIT),
        name="attn_sample",
    )(new, *bufs, *bias_cached, bias_new)
    return o.reshape(b * s_new, D_ATT), lse.reshape(b * s_new, ng * LANES)


HIST = SUBLANES


def _bdot(a, b):
    return jnp.dot(a.astype(BF16), b.astype(BF16), preferred_element_type=F32)


def _delta_prep_body(u_ref, prev_ref, small_ref, cbuf_ref, cw_ref, alog_ref, dtb_ref,
                     uo_ref, wq_ref, ak_ref, eg_ref, ext, *, t_valid, cp):
    c = pl.program_id(1)
    rows = cp * CHUNK

    @pl.when(c == 0)
    def _():
        ext[0:HIST, :] = cbuf_ref[0]

    @pl.when(c > 0)
    def _():
        ext[0:HIST, :] = prev_ref[0]

    ext[HIST:HIST + rows, :] = u_ref[0]
    conv = ext[HIST - 3:HIST - 3 + rows, :] * cw_ref[0:1, :]
    for i in range(1, CONV_W):
        conv = conv + ext[HIST - 3 + i:HIST - 3 + i + rows, :] * cw_ref[i:i + 1, :]
    act = conv * _sigmoid(conv)

    sm = small_ref[0]
    bmat = _sigmoid(sm)
    xg = sm + dtb_ref[...]
    gmat = -jnp.exp(alog_ref[...]) * (jnp.maximum(xg, 0.0) + jnp.log1p(jnp.exp(-jnp.abs(xg))))
    if t_valid < rows:
        live = lax.broadcasted_iota(jnp.int32, (rows, 1), 0) < t_valid
        act = jnp.where(live, act, 0.0)
        bmat = jnp.where(live, bmat, 0.0)
        gmat = jnp.where(live, gmat, 0.0)

    ri = lax.broadcasted_iota(jnp.int32, (CHUNK, CHUNK), 0)
    ci = lax.broadcasted_iota(jnp.int32, (CHUNK, CHUNK), 1)
    tril = ri >= ci
    strict = ri > ci

    pairs = [(j, h) for j in range(cp) for h in range(H_D)]
    rsl = lambda j: slice(j * CHUNK, (j + 1) * CHUNK)
    gcs, gcts = [], []
    for j in range(cp):
        gc_j = jnp.dot(tril.astype(F32), gmat[rsl(j)], preferred_element_type=F32, precision=HIGHEST)
        gcs.append(gc_j)
        gcts.append(gc_j.T)
    q = jnp.stack([act[rsl(j), h * DK:(h + 1) * DK] for j, h in pairs])
    k = jnp.stack([act[rsl(j), D_DK + h * DK:D_DK + (h + 1) * DK] for j, h in pairs])
    v = jnp.stack([act[rsl(j), 2 * D_DK + h * DV:2 * D_DK + (h + 1) * DV] for j, h in pairs])
    beta = jnp.stack([bmat[rsl(j), h:h + 1] for j, h in pairs])
    gcol = jnp.stack([gcs[j][:, H_D + h:H_D + h + 1] for j, h in pairs])
    grow = jnp.stack([gcts[j][H_D + h:H_D + h + 1, :] for j, h in pairs])
    glast = jnp.stack([gcs[j][CHUNK - 1:CHUNK, H_D + h:H_D + h + 1] for j, h in pairs])

    qn = q * lax.rsqrt(jnp.sum(q * q, axis=-1, keepdims=True) + NORM_EPS) * (DK ** -0.5)
    kn = k * lax.rsqrt(jnp.sum(k * k, axis=-1, keepdims=True) + NORM_EPS)
    decay = jnp.where(tril, jnp.exp(jnp.where(tril, gcol - grow, 0.0)), 0.0)
    eg = jnp.exp(gcol)
    kb = kn * beta
    kn_b = kn.astype(BF16)
    a_kk = jnp.einsum('bik,bjk->bij', kb.astype(BF16), kn_b, preferred_element_type=F32)
    a_qk = jnp.einsum('bik,bjk->bij', qn.astype(BF16), kn_b, preferred_element_type=F32)
    pw = jnp.where(strict, -(a_kk * decay), 0.0)
    tm1 = pw
    for _ in range(int(math.log2(CHUNK)) - 1):
        pw_b = pw.astype(BF16)
        pw = jnp.einsum('bij,bjk->bik', pw_b, pw_b, preferred_element_type=F32)
        tm1 = tm1 + pw + jnp.einsum('bij,bjk->bik', tm1.astype(BF16), pw.astype(BF16), preferred_element_type=F32)
    vb = v * beta
    kbg = kb * eg
    tm1_b = tm1.astype(BF16)
    u = vb + jnp.einsum('bij,bjk->bik', tm1_b, vb.astype(BF16), preferred_element_type=F32)
    w = (kbg + jnp.einsum('bij,bjk->bik', tm1_b, kbg.astype(BF16), preferred_element_type=F32)).astype(BF16)
    qd = (qn * eg).astype(BF16)
    aqd = (a_qk * decay).astype(BF16)
    kd = kn * jnp.exp(glast - gcol)
    egl = jnp.exp(glast)
    for i, (j, h) in enumerate(pairs):
        uo_ref[0, rsl(j), h * DV:(h + 1) * DV] = u[i]
        wq_ref[0, j, h, 0:CHUNK, :] = w[i]
        wq_ref[0, j, h, CHUNK:2 * CHUNK, :] = qd[i]
        ak_ref[0, j, h, 0:CHUNK, :] = aqd[i]
        ak_ref[0, j, h, CHUNK:CHUNK + DK, :] = kd[i].T.astype(BF16)
        eg_ref[0, j, h:h + 1, :] = jnp.broadcast_to(egl[i], (1, LANES))
    for j in range(cp):
        eg_ref[0, j, H_D:SUBLANES, :] = jnp.zeros((SUBLANES - H_D, LANES), F32)


def _delta_rec_body(u_ref, wq_ref, ak_ref, eg_ref, z_ref, s0_ref, nw_ref, o_ref, sout_ref, S, *, bb, nc):
    c = pl.program_id(1)

    @pl.when(c == 0)
    def _():
        S[...] = s0_ref[...]

    for b in range(bb):
        for h in range(H_D):
            cs = slice(h * DV, (h + 1) * DV)
            s_h = S[b, h]
            x = jnp.dot(wq_ref[b, 0, h], s_h.astype(BF16), preferred_element_type=F32)
            vn = u_ref[b, :, cs] - x[0:CHUNK]
            y = jnp.dot(ak_ref[b, 0, h], vn.astype(BF16), preferred_element_type=F32)
            o = x[CHUNK:2 * CHUNK] + y[0:CHUNK]
            S[b, h] = s_h * eg_ref[b, 0, h:h + 1, :] + y[CHUNK:CHUNK + DK]
            od = o * lax.rsqrt(jnp.mean(o * o, axis=-1, keepdims=True) + NORM_EPS) * nw_ref[...]
            zh = z_ref[b, :, cs]
            o_ref[b, :, cs] = od * (zh * _sigmoid(zh))

    @pl.when(c == nc - 1)
    def _():
        sout_ref[...] = S[...]


def _lane_row(vals, offset):
    row = jnp.zeros((1, LANES), F32)
    return row.at[0, offset:offset + vals.shape[0]].set(vals.astype(F32))


def _delta_net(dqkv, small, z, conv_buf, s0, conv_w, a_log, dt_bias, delta_norm_w):
    b, t, _ = dqkv.shape
    tp = -(-t // CHUNK) * CHUNK
    nc = tp // CHUNK
    assert tp == t or nc == 1
    if tp != t:
        pad = ((0, 0), (0, tp - t), (0, 0))
        dqkv, small, z = jnp.pad(dqkv, pad), jnp.pad(small, pad), jnp.pad(z, pad)
    cbuf = jnp.pad(conv_buf, ((0, 0), (HIST - (CONV_W - 1), 0), (0, 0)))
    cw = jnp.pad(conv_w, ((0, SUBLANES - CONV_W), (0, 0)))
    cp = next(c for c in (4, 2, 1) if nc % c == 0)
    rows = cp * CHUNK
    full2 = lambda shape: pl.BlockSpec(shape, lambda bi, ci: (0,) * len(shape))
    u, wq, ak, eg = pl.pallas_call(
        functools.partial(_delta_prep_body, t_valid=t if nc == 1 else rows, cp=cp),
        grid=(b, nc // cp),
        in_specs=[pl.BlockSpec((1, rows, D_CONV), lambda bi, ci: (bi, ci, 0)),
                  pl.BlockSpec((1, HIST, D_CONV), lambda bi, ci: (bi, jnp.maximum(ci * (rows // HIST) - 1, 0), 0)),
                  pl.BlockSpec((1, rows, LANES), lambda bi, ci: (bi, ci, 0)),
                  pl.BlockSpec((1, HIST, D_CONV), lambda bi, ci: (bi, 0, 0)),
                  full2((SUBLANES, D_CONV)), full2((1, LANES)), full2((1, LANES))],
        out_specs=[pl.BlockSpec((1, rows, D_DV), lambda bi, ci: (bi, ci, 0)),
                   pl.BlockSpec((1, cp, H_D, 2 * CHUNK, DK), lambda bi, ci: (bi, ci, 0, 0, 0)),
                   pl.BlockSpec((1, cp, H_D, CHUNK + DK, CHUNK), lambda bi, ci: (bi, ci, 0, 0, 0)),
                   pl.BlockSpec((1, cp, SUBLANES, LANES), lambda bi, ci: (bi, ci, 0, 0))],
        out_shape=[jax.ShapeDtypeStruct((b, tp, D_DV), F32),
                   jax.ShapeDtypeStruct((b, nc, H_D, 2 * CHUNK, DK), BF16),
                   jax.ShapeDtypeStruct((b, nc, H_D, CHUNK + DK, CHUNK), BF16),
                   jax.ShapeDtypeStruct((b, nc, SUBLANES, LANES), F32)],
        scratch_shapes=[pltpu.VMEM((HIST + rows, D_CONV), F32)],
        compiler_params=pltpu.CompilerParams(dimension_semantics=("parallel", "parallel"),
                                             vmem_limit_bytes=VMEM_LIMIT),
        name="delta_prep",
    )(dqkv, dqkv, small, cbuf, cw, _lane_row(a_log, H_D), _lane_row(dt_bias, H_D))

    bb = next(c for c in (8, 4, 2, 1) if b % c == 0)
    o, s_new = pl.pallas_call(
        functools.partial(_delta_rec_body, bb=bb, nc=nc),
        grid=(b // bb, nc),
        in_specs=[pl.BlockSpec((bb, CHUNK, D_DV), lambda bi, ci: (bi, ci, 0)),
                  pl.BlockSpec((bb, 1, H_D, 2 * CHUNK, DK), lambda bi, ci: (bi, ci, 0, 0, 0)),
                  pl.BlockSpec((bb, 1, H_D, CHUNK + DK, CHUNK), lambda bi, ci: (bi, ci, 0, 0, 0)),
                  pl.BlockSpec((bb, 1, SUBLANES, LANES), lambda bi, ci: (bi, ci, 0, 0)),
                  pl.BlockSpec((bb, CHUNK, D_DV), lambda bi, ci: (bi, ci, 0)),
                  pl.BlockSpec((bb, H_D, DK, DV), lambda bi, ci: (bi, 0, 0, 0)),
                  full2((1, DV))],
        out_specs=[pl.BlockSpec((bb, CHUNK, D_DV), lambda bi, ci: (bi, ci, 0)),
                   pl.BlockSpec((bb, H_D, DK, DV), lambda bi, ci: (bi, 0, 0, 0))],
        out_shape=[jax.ShapeDtypeStruct((b, tp, D_DV), F32), jax.ShapeDtypeStruct((b, H_D, DK, DV), F32)],
        scratch_shapes=[pltpu.VMEM((bb, H_D, DK, DV), F32)],
        compiler_params=pltpu.CompilerParams(dimension_semantics=("parallel", "arbitrary"),
                                             vmem_limit_bytes=VMEM_LIMIT),
        name="delta_recurrence",
    )(u, wq, ak, eg, z, s0, delta_norm_w.reshape(1, DV).astype(F32))
    return o[:, :t].reshape(b * t, D_DV), s_new


def _lane_expand(cols, width):
    tm = cols[0].shape[0]
    seg = lax.broadcasted_iota(jnp.int32, (tm, len(cols) * width), 1) // width
    out = jnp.broadcast_to(cols[-1], seg.shape)
    for i in range(len(cols) - 2, -1, -1):
        out = jnp.where(seg == i, cols[i], out)
    return out


def _token_order(ref, scr):
    d, nc = ref.shape[1], ref.shape[3] // LANES
    if d == 1:
        return ref[0, 0]
    for r in range(d):
        for c in range(nc):
            scr[c, pl.ds(r, ref.shape[2], stride=d), :] = ref[0, r, :, c * LANES:(c + 1) * LANES]
    return jnp.concatenate([scr[c] for c in range(nc)], axis=-1)


def _mix_body(x_ref, gate_ref, o0_ref, o1_ref, o2_ref, l0_ref, l1_ref, l2_ref, od_ref,
              wa_ref, wd_ref, wo_ref, bg_ref, g1_ref, b1_ref, rw_ref, rb_ref,
              h_ref, ti_ref, tg_ref, *scratch):
    o_scr, l_scr = scratch[:3], scratch[3:]
    lses = [_token_order(l, s) for l, s in zip((l0_ref, l1_ref, l2_ref), l_scr)]
    mx = jnp.maximum(jnp.maximum(lses[0], lses[1]), lses[2])
    es = [jnp.exp(l - mx) for l in lses]
    inv = 1.0 / (es[0] + es[1] + es[2])
    o_att = None
    for e_g, o_g, s in zip(es, (o0_ref, o1_ref, o2_ref), o_scr):
        wgt = e_g * inv
        term = _lane_expand([wgt[:, h:h + 1] for h in range(H_G)], HD) * _token_order(o_g, s)
        o_att = term if o_att is None else o_att + term
    ga = _sigmoid(gate_ref[:, :D_MODEL] + bg_ref[:, :D_MODEL])
    gd = _sigmoid(gate_ref[:, D_MODEL:] + bg_ref[:, D_MODEL:])
    merged = (ga * jnp.dot(o_att.astype(BF16), wa_ref[...], preferred_element_type=F32)
              + gd * jnp.dot(od_ref[...].astype(BF16), wd_ref[...], preferred_element_type=F32))
    mix = jnp.dot(merged.astype(BF16), wo_ref[...], preferred_element_type=F32)
    h = _layer_norm(DN_ALPHA * x_ref[...] + mix, g1_ref[...], b1_ref[...])
    _rows_to_tiles(h_ref, h)

    h_hi = h.astype(BF16)
    h_lo = (h - h_hi.astype(F32)).astype(BF16)
    part = jnp.dot(h_hi, rw_ref[...], preferred_element_type=F32)
    logits = (part[:, :LANES] + part[:, LANES:]
              + jnp.dot(h_lo, rw_ref[:, :LANES], preferred_element_type=F32) + rb_ref[...])
    lane = lax.broadcasted_iota(jnp.int32, logits.shape, 1)
    lane_f = lane.astype(F32)
    ti = jnp.zeros(logits.shape, F32)
    tv = jnp.zeros(logits.shape, F32)
    top = None
    for kk in range(TOP_K):
        m = jnp.max(logits, axis=-1, keepdims=True)
        idx = jnp.min(jnp.where(logits == m, lane_f, float(LANES)), axis=-1, keepdims=True)
        if top is None:
            top = m
        ti = jnp.where(lane == kk, idx, ti)
        tv = jnp.where(lane == kk, jnp.exp(m - top), tv)
        logits = jnp.where(lane_f == idx, -jnp.inf, logits)
    ti_ref[...] = ti.astype(jnp.int32)
    tg_ref[...] = tv / jnp.sum(tv, axis=-1, keepdims=True)


def _mix_and_route(x2d, gates, o_groups, lse_groups, o_del, wa, wd, wo, b_gate, ln_g, ln_b, rw, rb, tm):
    n = x2d.shape[0]
    tpb = o_groups[0].shape[1] * o_groups[0].shape[2] // tm
    row = lambda c: pl.BlockSpec((tm, c), lambda i: (i, 0))
    full = lambda a: pl.BlockSpec(a.shape, lambda i: (0, 0))

    def res(a):
        d = a.shape[1]
        return pl.BlockSpec((1, d, tm // d, a.shape[3]), lambda i: (i // tpb, 0, i % tpb, 0))

    consts = (wa, wd, wo, b_gate, ln_g, ln_b, rw, rb)
    return pl.pallas_call(
        _mix_body,
        grid=(n // tm,),
        in_specs=[row(D_MODEL), row(2 * D_MODEL)] + [res(a) for a in o_groups] + [res(a) for a in lse_groups]
        + [row(D_DV)] + [full(a) for a in consts],
        out_specs=[pl.BlockSpec((tm * TOK_ROWS, LANES), lambda i: (i, 0)), row(LANES), row(LANES)],
        out_shape=[jax.ShapeDtypeStruct((n * TOK_ROWS, LANES), F32), jax.ShapeDtypeStruct((n, LANES), jnp.int32),
                   jax.ShapeDtypeStruct((n, LANES), F32)],
        scratch_shapes=[pltpu.VMEM((GW // LANES, tm, LANES), F32)] * len(o_groups)
        + [pltpu.VMEM((1, tm, LANES), F32)] * len(lse_groups),
        compiler_params=pltpu.CompilerParams(dimension_semantics=("parallel",), vmem_limit_bytes=VMEM_LIMIT),
        name="mix_and_route",
    )(x2d, gates, *o_groups, *lse_groups, o_del, *consts)


def _rank_body(ti_ref, rank_ref, cnt_ref, carry):
    i = pl.program_id(0)

    @pl.when(i == 0)
    def _():
        carry[...] = jnp.zeros_like(carry)

    ti = ti_ref[...]
    tm = ti.shape[0]
    lane = lax.broadcasted_iota(jnp.int32, ti.shape, 1)
    sel = [lane == ti[:, kk:kk + 1] for kk in range(TOP_K)]
    hit = sel[0]
    for kk in range(1, TOP_K):
        hit = hit | sel[kk]
    cnt = hit.astype(F32)
    ri = lax.broadcasted_iota(jnp.int32, (tm, tm), 0)
    ci = lax.broadcasted_iota(jnp.int32, (tm, tm), 1)
    incl = jnp.dot((ri >= ci).astype(BF16), cnt.astype(BF16), preferred_element_type=F32)
    before = incl - cnt + carry[...]
    rank = jnp.zeros(ti.shape, jnp.int32)
    for kk in range(TOP_K):
        r = jnp.sum(jnp.where(sel[kk], before, 0.0), axis=-1, keepdims=True)
        rank = jnp.where(lane == kk, r.astype(jnp.int32), rank)
    rank_ref[...] = rank
    carry[...] = carry[...] + incl[tm - 1:tm, :]
    cnt_ref[...] = carry[...]


def _route_ranks(top_i, tm):
    n = top_i.shape[0]
    return pl.pallas_call(
        _rank_body,
        grid=(n // tm,),
        in_specs=[pl.BlockSpec((tm, LANES), lambda i: (i, 0))],
        out_specs=[pl.BlockSpec((tm, LANES), lambda i: (i, 0)), pl.BlockSpec((1, LANES), lambda i: (0, 0))],
        out_shape=[jax.ShapeDtypeStruct((n, LANES), jnp.int32), jax.ShapeDtypeStruct((1, LANES), F32)],
        scratch_shapes=[pltpu.VMEM((1, LANES), F32)],
        compiler_params=pltpu.CompilerParams(dimension_semantics=("arbitrary",), vmem_limit_bytes=VMEM_LIMIT),
        name="route_ranks",
    )(top_i)


def _rows_to_tiles(ref, val):
    m = val.shape[0]
    for s in range(TOK_ROWS):
        ref[pl.ds(s, m, stride=TOK_ROWS), :] = val[:, s * LANES:(s + 1) * LANES]


def _tiles_to_rows(ref, m):
    return jnp.concatenate([ref[pl.ds(s, m, stride=TOK_ROWS), :] for s in range(TOK_ROWS)], axis=-1)


def _dispatch_body(pend_ref, padded_ref, dest_ref, h_ref, xs_out, zero, sem, zsem, *, max_tail):
    tm = h_ref.shape[0] // TOK_ROWS
    blk = MOE_BM * TOK_ROWS

    @pl.when(pl.program_id(0) == 0)
    def _():
        zero[...] = jnp.zeros_like(zero)

        def last_block(e):
            start = pl.multiple_of((pend_ref[e] - MOE_BM) * TOK_ROWS, blk)
            return pltpu.make_async_copy(zero, xs_out.at[pl.ds(start, blk)], zsem)

        def tail_block(j):
            start = pl.multiple_of((pend_ref[N_EXP - 1] + j * MOE_BM) * TOK_ROWS, blk)
            return pltpu.make_async_copy(zero, xs_out.at[pl.ds(start, blk)], zsem)

        n_slot = xs_out.shape[0] // TOK_ROWS
        for e in range(N_EXP):
            @pl.when(padded_ref[e] > 0)
            def _():
                last_block(e).start()
        for j in range(max_tail):
            @pl.when(pend_ref[N_EXP - 1] + j * MOE_BM < n_slot)
            def _():
                tail_block(j).start()
        for e in range(N_EXP):
            @pl.when(padded_ref[e] > 0)
            def _():
                last_block(e).wait()
        for j in range(max_tail):
            @pl.when(pend_ref[N_EXP - 1] + j * MOE_BM < n_slot)
            def _():
                tail_block(j).wait()

    def issue(t, carry):
        src = h_ref.at[pl.ds(pl.multiple_of(t * TOK_ROWS, TOK_ROWS), TOK_ROWS)]
        for kk in range(TOP_K):
            d = pl.multiple_of(dest_ref[t * TOP_K + kk] * TOK_ROWS, TOK_ROWS)
            pltpu.make_async_copy(src, xs_out.at[pl.ds(d, TOK_ROWS)], sem).start(priority=kk % 2)
        return carry

    lax.fori_loop(0, tm, issue, 0, unroll=8)
    for kk in range(TOP_K):
        pltpu.make_async_copy(h_ref, xs_out.at[pl.ds(0, tm * TOK_ROWS)], sem).wait()


def _dispatch(h_tiles, dest_flat, pad_end, padded, n_slot, tm):
    n = h_tiles.shape[0] // TOK_ROWS
    grid_spec = pltpu.PrefetchScalarGridSpec(
        num_scalar_prefetch=2,
        grid=(n // tm,),
        in_specs=[pl.BlockSpec((tm * TOP_K,), lambda i, pe, pd: (i,), memory_space=pltpu.SMEM),
                  pl.BlockSpec((tm * TOK_ROWS, LANES), lambda i, pe, pd: (i, 0))],
        out_specs=pl.BlockSpec(memory_space=pl.ANY),
        scratch_shapes=[pltpu.VMEM((MOE_BM * TOK_ROWS, LANES), F32), pltpu.SemaphoreType.DMA(()),
                        pltpu.SemaphoreType.DMA(())],
    )
    max_tail = n_slot // MOE_BM - (n * TOP_K) // MOE_BM
    return pl.pallas_call(
        functools.partial(_dispatch_body, max_tail=max_tail),
        grid_spec=grid_spec,
        out_shape=jax.ShapeDtypeStruct((n_slot * TOK_ROWS, LANES), F32),
        compiler_params=pltpu.CompilerParams(dimension_semantics=("arbitrary",), vmem_limit_bytes=VMEM_LIMIT,
                                             has_side_effects=True, disable_bounds_checks=True),
        name="moe_dispatch",
    )(pad_end, padded, dest_flat, h_tiles)


def _expert_body(be_ref, nu_ref, x_ref, wgu_ref, bgu_ref, wd_ref, bd_ref, o_ref, wgu_bf, wd_bf):
    i = pl.program_id(0)
    e = be_ref[i]
    prev = be_ref[jnp.maximum(i - 1, 0)]

    @pl.when((i == 0) | (e != prev))
    def _():
        wgu_bf[...] = wgu_ref[0].astype(BF16)
        wd_bf[...] = wd_ref[0].astype(BF16)

    @pl.when(i < nu_ref[0])
    def _():
        x = _tiles_to_rows(x_ref, MOE_BM)
        gu = jnp.dot(x.astype(BF16), wgu_bf[...], preferred_element_type=F32) + bgu_ref[0]
        gt = jnp.minimum(gu[:, :D_FF], SWIGLU_LIMIT)
        up = jnp.clip(gu[:, D_FF:], -SWIGLU_LIMIT, SWIGLU_LIMIT)
        act = (up + 1.0) * gt * _sigmoid(SWIGLU_ALPHA * gt)
        _rows_to_tiles(o_ref, jnp.dot(act.astype(BF16), wd_bf[...], preferred_element_type=F32) + bd_ref[0])

    @pl.when(i >= nu_ref[0])
    def _():
        o_ref[...] = jnp.zeros_like(o_ref)


def _experts(xs, blk_exp, n_used, w_gu, b_gu, w_down, b_down):
    n_slot = xs.shape[0] // TOK_ROWS
    n_blk = n_slot // MOE_BM
    rows = lambda i, be, nu: (jnp.minimum(i, nu[0] - 1), 0)
    grid_spec = pltpu.PrefetchScalarGridSpec(
        num_scalar_prefetch=2,
        grid=(n_blk,),
        in_specs=[pl.BlockSpec((MOE_BM * TOK_ROWS, LANES), rows),
                  pl.BlockSpec((1, D_MODEL, 2 * D_FF), lambda i, be, nu: (be[i], 0, 0)),
                  pl.BlockSpec((1, 1, 2 * D_FF), lambda i, be, nu: (be[i], 0, 0)),
                  pl.BlockSpec((1, D_FF, D_MODEL), lambda i, be, nu: (be[i], 0, 0)),
                  pl.BlockSpec((1, 1, D_MODEL), lambda i, be, nu: (be[i], 0, 0))],
        out_specs=pl.BlockSpec((MOE_BM * TOK_ROWS, LANES), lambda i, be, nu: (i, 0)),
        scratch_shapes=[pltpu.VMEM((D_MODEL, 2 * D_FF), BF16), pltpu.VMEM((D_FF, D_MODEL), BF16)],
    )
    return pl.pallas_call(
        _expert_body,
        grid_spec=grid_spec,
        out_shape=jax.ShapeDtypeStruct((n_slot * TOK_ROWS, LANES), F32),
        compiler_params=pltpu.CompilerParams(dimension_semantics=("arbitrary",), vmem_limit_bytes=VMEM_LIMIT),
        name="moe_experts",
    )(blk_exp, n_used, xs, w_gu, b_gu.reshape(N_EXP, 1, 2 * D_FF), w_down, b_down.reshape(N_EXP, 1, D_MODEL))


def _combine_body(dest_ref, dest_next_ref, gate_ref, h_ref, g_ref, b_ref, ys_ref, y_ref, buf, sem):
    tm = h_ref.shape[0] // TOK_ROWS
    i = pl.program_id(0)
    slot = i % 2

    def start_tile(idx_ref, s):
        def issue(t, carry):
            row = pl.multiple_of(t * TOK_ROWS, TOK_ROWS)
            for kk in range(TOP_K):
                d = pl.multiple_of(idx_ref[t * TOP_K + kk] * TOK_ROWS, TOK_ROWS)
                pltpu.make_async_copy(ys_ref.at[pl.ds(d, TOK_ROWS)], buf.at[s, kk, pl.ds(row, TOK_ROWS)],
                                      sem.at[s]).start(priority=kk % 2)
            return carry

        lax.fori_loop(0, tm, issue, 0, unroll=8)

    @pl.when(i == 0)
    def _():
        start_tile(dest_ref, 0)

    @pl.when(i + 1 < pl.num_programs(0))
    def _():
        start_tile(dest_next_ref, 1 - slot)

    for kk in range(TOP_K):
        pltpu.make_async_copy(ys_ref.at[pl.ds(0, tm * TOK_ROWS)], buf.at[slot, kk], sem.at[slot]).wait()
    gate = gate_ref[...]
    moe = gate[:, 0:1] * _tiles_to_rows(buf.at[slot, 0], tm)
    for kk in range(1, TOP_K):
        moe = moe + gate[:, kk:kk + 1] * _tiles_to_rows(buf.at[slot, kk], tm)
    y_ref[...] = _layer_norm(DN_ALPHA * _tiles_to_rows(h_ref, tm) + moe, g_ref[...], b_ref[...])


def _combine(ys, dest_flat, gate, h_tiles, ln_g, ln_b, tm):
    n = h_tiles.shape[0] // TOK_ROWS
    return pl.pallas_call(
        _combine_body,
        grid=(n // tm,),
        in_specs=[pl.BlockSpec((tm * TOP_K,), lambda i: (i,), memory_space=pltpu.SMEM),
                  pl.BlockSpec((tm * TOP_K,), lambda i: (jnp.minimum(i + 1, n // tm - 1),), memory_space=pltpu.SMEM),
                  pl.BlockSpec((tm, LANES), lambda i: (i, 0)),
                  pl.BlockSpec((tm * TOK_ROWS, LANES), lambda i: (i, 0)),
                  pl.BlockSpec((1, D_MODEL), lambda i: (0, 0)),
                  pl.BlockSpec((1, D_MODEL), lambda i: (0, 0)),
                  pl.BlockSpec(memory_space=pl.ANY)],
        out_specs=pl.BlockSpec((tm, D_MODEL), lambda i: (i, 0)),
        out_shape=jax.ShapeDtypeStruct((n, D_MODEL), F32),
        scratch_shapes=[pltpu.VMEM((2, TOP_K, tm * TOK_ROWS, LANES), F32), pltpu.SemaphoreType.DMA((2,))],
        compiler_params=pltpu.CompilerParams(dimension_semantics=("arbitrary",), vmem_limit_bytes=VMEM_LIMIT,
                                             disable_bounds_checks=True),
        name="moe_combine",
    )(dest_flat, dest_flat, gate, h_tiles, ln_g, ln_b, ys)


def _moe(h, top_i, top_g, w_gu, b_gu, w_down, b_down, ln_g, ln_b, tm):
    n = top_i.shape[0]
    rank, counts = _route_ranks(top_i, tm)
    counts = counts[0, :N_EXP].astype(jnp.int32)
    padded = (counts + MOE_BM - 1) // MOE_BM * MOE_BM
    pad_end = jnp.cumsum(padded)
    pad_start = pad_end - padded
    n_blk = -(-(n * TOP_K + N_EXP * (MOE_BM - 1)) // MOE_BM)
    dest = (pad_start[top_i[:, :TOP_K]] + rank[:, :TOP_K]).reshape(n * TOP_K)
    blk_row0 = jnp.arange(n_blk, dtype=jnp.int32) * MOE_BM
    blk_exp = jnp.minimum(jnp.sum((pad_end[None, :] <= blk_row0[:, None]).astype(jnp.int32), axis=1), N_EXP - 1)
    n_used = (pad_end[-1:] // MOE_BM).astype(jnp.int32)
    xs = _dispatch(h, dest, pad_end.astype(jnp.int32), padded, n_blk * MOE_BM, tm)
    ys = _experts(xs, blk_exp, n_used, w_gu, b_gu, w_down, b_down)
    return _combine(ys, dest, top_g, h, ln_g, ln_b, tm)


def _pad_cols(a, width, fill=0.0):
    return jnp.pad(a, ((0, 0), (0, width - a.shape[1])), constant_values=fill)


def _decoder_layer(x, caches, conv_buf, s0, w_in, b_gate, conv_w, a_log, dt_bias, delta_norm_w,
                   w_branch_attn, w_branch_delta, w_out, ln1_g, ln1_b, router_w, router_b,
                   w_gu, b_gu, w_down, b_down, ln2_g, ln2_b):
    b, t, _ = x.shape
    n = b * t
    tm = min(256, n)
    x2d = x.reshape(n, D_MODEL)

    c_att, c_dz, c_small = 3 * D_ATT, 3 * D_ATT + D_CONV, 3 * D_ATT + D_CONV + D_DV
    c_gate = c_small + 2 * H_D
    ng = len(ATT_DILS)
    w_groups = [jnp.concatenate([w_in[:, part * D_ATT + gi * GW:part * D_ATT + (gi + 1) * GW] for part in range(3)],
                                axis=1) for gi in range(ng)]
    ws = w_groups + [w_in[:, c_att:c_dz], w_in[:, c_dz:c_small],
                     _pad_cols(w_in[:, c_small:c_gate], LANES), w_in[:, c_gate:]]
    dils = (ATT_DILS if caches is None else (1,) * ng) + (1, 1, 1, 1)
    outs = _in_projection(x2d, [w.astype(BF16) for w in ws], dils, t, min(IN_TM, n))
    att_groups, (dqkv, z, small, gates) = outs[:ng], outs[ng:]
    dqkv3 = dqkv.reshape(b, t, D_CONV)

    kv_new = []
    if caches is None:
        o_groups, lse_groups = [], []
        for dil, a in zip(ATT_DILS, att_groups):
            a = a.reshape(b, dil, t // dil, 3 * GW)
            o_g, lse_g = _attn_prompt(a, dil)
            o_groups.append(o_g)
            lse_groups.append(lse_g)
            win = min(dil * STEPS, t)
            last = a[:, :, (t - win) // dil:, GW:]
            last = jnp.transpose(last, (0, 2, 1, 3)).reshape(b, win, 2, H_G, HD)
            kv_new.append(last)
        conv_buf = jnp.zeros((b, CONV_W - 1, D_CONV), F32)
        s0 = jnp.zeros((b, H_D, DK, DV), F32)
    else:
        att3 = jnp.concatenate(att_groups, axis=1).reshape(b, t, 3 * D_ATT)
        o_all, lse_all = _attn_sample(att3, caches)
        o_groups = [o_all[:, gi * GW:(gi + 1) * GW].reshape(1, 1, n, GW) for gi in range(ng)]
        lse_groups = [lse_all[:, gi * LANES:(gi + 1) * LANES].reshape(1, 1, n, LANES) for gi in range(ng)]
        for a in att_groups:
            kv_new.append(a[:, GW:].reshape(b, t, 2, H_G, HD))

    o_del, s_new = _delta_net(dqkv3, small.reshape(b, t, LANES), z.reshape(b, t, D_DV), conv_buf, s0,
                              conv_w, a_log, dt_bias, delta_norm_w)
    conv_new = jnp.concatenate([conv_buf, dqkv3], axis=1)[:, -(CONV_W - 1):] if t < CONV_W - 1 \
        else dqkv3[:, t - (CONV_W - 1):]

    rw = _pad_cols(router_w, LANES)
    rw_hi = rw.astype(BF16)
    rw = jnp.concatenate([rw_hi, (rw - rw_hi.astype(F32)).astype(BF16)], axis=1)
    rb = _pad_cols(router_b.reshape(1, N_EXP), LANES, fill=NEG)
    h, top_i, top_g = _mix_and_route(
        x2d, gates, o_groups, lse_groups, o_del,
        w_branch_attn.astype(BF16), w_branch_delta.astype(BF16), w_out.astype(BF16),
        b_gate.reshape(1, 2 * D_MODEL), ln1_g.reshape(1, D_MODEL), ln1_b.reshape(1, D_MODEL), rw, rb, tm)
    y = _moe(h, top_i, top_g, w_gu, b_gu, w_down, b_down,
             ln2_g.reshape(1, D_MODEL), ln2_b.reshape(1, D_MODEL), tm)
    return y.reshape(b, t, D_MODEL), kv_new, conv_new, s_new


def kernel(x_prompt, x_sample, cache_kv_w128, cache_kv_w512, cache_kv_w2048, state_conv, state_delta,
           w_in, b_gate, conv_w, a_log, dt_bias, delta_norm_w, w_branch_attn, w_branch_delta, w_out,
           ln1_g, ln1_b, router_w, router_b, w_gu, b_gu, w_down, b_down, ln2_g, ln2_b):
    depth = w_in.shape[0]
    assert depth == 1
    l = 0
    lw = (w_in[l], b_gate[l], conv_w[l], a_log[l], dt_bias[l], delta_norm_w[l], w_branch_attn[l],
          w_branch_delta[l], w_out[l], ln1_g[l], ln1_b[l], router_w[l], router_b[l], w_gu[l], b_gu[l],
          w_down[l], b_down[l], ln2_g[l], ln2_b[l])
    yp, kv_p, cv_p, s_p = _decoder_layer(x_prompt, None, None, None, *lw)
    ys, kv_s, cv_s, s_s = _decoder_layer(x_sample, (cache_kv_w128[l], cache_kv_w512[l], cache_kv_w2048[l]),
                                         state_conv[l], state_delta[l], *lw)
    stk = lambda a: a[None]
    return (yp, ys, stk(kv_p[0]), stk(kv_p[1]), stk(kv_p[2]), stk(cv_p), stk(s_p),
            stk(kv_s[0]), stk(kv_s[1]), stk(kv_s[2]), stk(cv_s), stk(s_s))
```

```python
import functools
import math

import numpy as np
import jax
import jax.numpy as jnp
from jax import lax
from jax.experimental import pallas as pl
from jax.experimental.pallas import tpu as pltpu

F32 = jnp.float32
BF16 = jnp.bfloat16
HIGHEST = lax.Precision.HIGHEST

D_MODEL = 1024
ATT_DILS = (1, 4, 16)
STEPS = 128
H_G = 4
HD = 64
GW = H_G * HD
D_ATT = len(ATT_DILS) * GW
H_D = 4
DK = 128
DV = 128
D_DK = H_D * DK
D_DV = H_D * DV
D_CONV = 2 * D_DK + D_DV
CONV_W = 4
CHUNK = 64
N_EXP = 32
TOP_K = 4
D_FF = D_MODEL
SWIGLU_LIMIT = 7.0
SWIGLU_ALPHA = 1.702
DN_ALPHA = 2.0 ** 0.25
LN_EPS = 1e-5
NORM_EPS = 1e-6
NEG = -1e30

LANES = 128
SUBLANES = 8
VMEM_LIMIT = 56 * 1024 * 1024
MOE_BM = 512
IN_TM = 512
TOK_ROWS = D_MODEL // LANES
assert TOK_ROWS == SUBLANES


def _sigmoid(x):
    return 1.0 / (1.0 + jnp.exp(-x))


def _layer_norm(v, g, b):
    mu = jnp.mean(v, axis=-1, keepdims=True)
    d = v - mu
    var = jnp.mean(d * d, axis=-1, keepdims=True)
    return d * lax.rsqrt(var + LN_EPS) * g + b


def _alibi_slopes():
    return 2.0 ** (-8.0 * np.arange(1, H_G + 1, dtype=np.float64) / H_G)


def _inproj_body(x_ref, *refs, dils):
    nw = len(dils)
    scr = refs[2 * nw] if len(refs) > 2 * nw else None
    x = x_ref[...].astype(BF16)
    for w_ref, o_ref, d in zip(refs[:nw], refs[nw:2 * nw], dils):
        y = jnp.dot(x, w_ref[...], preferred_element_type=F32)
        if d == 1:
            o_ref[...] = y
        else:
            for c in range(y.shape[1] // LANES):
                scr[c] = y[:, c * LANES:(c + 1) * LANES]
            for r in range(d):
                for c in range(y.shape[1] // LANES):
                    o_ref[0, r, :, c * LANES:(c + 1) * LANES] = scr[c, pl.ds(r, y.shape[0] // d, stride=d), :]


def _in_projection(x2d, ws, dils, t, tm):
    n = x2d.shape[0]
    tpb = t // tm
    out_specs, out_shape = [], []
    for w, d in zip(ws, dils):
        c = w.shape[1]
        if d == 1:
            out_specs.append(pl.BlockSpec((tm, c), lambda i: (i, 0)))
            out_shape.append(jax.ShapeDtypeStruct((n, c), F32))
        else:
            assert tm % (d * SUBLANES) == 0 and t % tm == 0
            out_specs.append(pl.BlockSpec((1, d, tm // d, c), lambda i: (i // tpb, 0, i % tpb, 0)))
            out_shape.append(jax.ShapeDtypeStruct((n // t, d, t // d, c), F32))
    wide = max([w.shape[1] for w, d in zip(ws, dils) if d > 1], default=0)
    return pl.pallas_call(
        functools.partial(_inproj_body, dils=tuple(dils)),
        grid=(n // tm,),
        in_specs=[pl.BlockSpec((tm, D_MODEL), lambda i: (i, 0))]
        + [pl.BlockSpec((D_MODEL, w.shape[1]), lambda i: (0, 0), pipeline_mode=pl.Buffered(1)) for w in ws],
        out_specs=out_specs,
        out_shape=out_shape,
        scratch_shapes=[pltpu.VMEM((wide // LANES, tm, LANES), F32)] if wide else [],
        compiler_params=pltpu.CompilerParams(dimension_semantics=("parallel",), vmem_limit_bytes=VMEM_LIMIT),
        name="in_projection",
    )(x2d, *ws)


def _prompt_bias_table(dil):
    qi = np.arange(STEPS)[:, None]
    kj = np.arange(2 * STEPS)[None, :]
    steps = qi + STEPS - kj
    valid = (steps >= 0) & (steps <= STEPS)
    slopes = _alibi_slopes()
    bias = -slopes[:, None, None] * (dil * steps)[None].astype(np.float64)
    later = np.where(valid[None], bias, NEG)
    first = np.where((valid & (kj >= STEPS))[None], bias, NEG)
    return jnp.asarray(np.stack([first, later]).reshape(2, H_G * STEPS, 2 * STEPS), F32)


ATT_QB = 2


def _attn_prompt_body(q_ref, kp_ref, kc_ref, vp_ref, vc_ref, bias_ref, o_ref, lse_ref):
    kk = jnp.concatenate([kp_ref[0, 0], kc_ref[0, 0]], axis=0).astype(BF16)
    vv = jnp.concatenate([vp_ref[0, 0], vc_ref[0, 0]], axis=0).astype(BF16)
    head = lax.broadcasted_iota(jnp.int32, (STEPS, GW), 1) // HD
    lse_ref[0, 0] = jnp.zeros(lse_ref.shape[2:], F32)
    for j in range(ATT_QB):
        qrows = slice(j * STEPS, (j + 1) * STEPS)
        krows = slice(j * STEPS, (j + 2) * STEPS)
        later = jnp.minimum(pl.program_id(2), 1) if j == 0 else 1
        q = q_ref[0, 0, qrows, :] * (HD ** -0.5)
        qs = jnp.concatenate([jnp.where(head == h, q, 0.0) for h in range(H_G)], axis=0).astype(BF16)
        s = lax.dot_general(qs, kk[krows], (((1,), (1,)), ((), ())), preferred_element_type=F32)
        s = s + bias_ref[later]
        mx = jnp.max(s, axis=-1, keepdims=True)
        p = jnp.exp(s - mx)
        den = jnp.sum(p, axis=-1, keepdims=True)
        pv = jnp.dot(p.astype(BF16), vv[krows], preferred_element_type=F32) / den
        lse = mx + jnp.log(den)
        o = jnp.zeros((STEPS, GW), F32)
        for h in range(H_G):
            rows = slice(h * STEPS, (h + 1) * STEPS)
            o = jnp.where(head == h, pv[rows], o)
            lse_ref[0, 0, qrows, h:h + 1] = lse[rows]
        o_ref[0, 0, qrows, :] = o


def _attn_prompt(a, dil):
    b, d, L, c = a.shape
    rows = ATT_QB * STEPS
    assert d == dil and c == 3 * GW and L % rows == 0
    nb = L // rows

    def spec(col, prev):
        if prev:
            return pl.BlockSpec((1, 1, STEPS, GW), lambda bi, r, n: (bi, r, jnp.maximum(ATT_QB * n - 1, 0), col))
        return pl.BlockSpec((1, 1, rows, GW), lambda bi, r, n: (bi, r, n, col))

    return pl.pallas_call(
        _attn_prompt_body,
        grid=(b, dil, nb),
        in_specs=[spec(0, False), spec(1, True), spec(1, False), spec(2, True), spec(2, False),
                  pl.BlockSpec((2, H_G * STEPS, 2 * STEPS), lambda bi, r, n: (0, 0, 0))],
        out_specs=[pl.BlockSpec((1, 1, rows, GW), lambda bi, r, n: (bi, r, n, 0)),
                   pl.BlockSpec((1, 1, rows, LANES), lambda bi, r, n: (bi, r, n, 0))],
        out_shape=[jax.ShapeDtypeStruct((b, dil, L, GW), F32),
                   jax.ShapeDtypeStruct((b, dil, L, LANES), F32)],
        compiler_params=pltpu.CompilerParams(dimension_semantics=("parallel", "parallel", "arbitrary"),
                                             vmem_limit_bytes=VMEM_LIMIT),
        name=f"attn_prompt_d{dil}",
    )(a, a, a, a, a, _prompt_bias_table(dil))


def _sample_bias_tables(s_new):
    slopes = _alibi_slopes()[:, None, None]
    s = np.arange(SUBLANES)[:, None]
    live = s < s_new
    cached = []
    for dil in ATT_DILS:
        lw = dil * STEPS
        dist = lw + s - np.arange(lw)[None, :]
        ok = live & (dist % dil == 0) & (dist <= lw)
        cached.append(jnp.asarray(np.where(ok[None], -slopes * dist[None], NEG), F32))
    s2 = np.arange(SUBLANES)[None, :]
    dist = s - s2
    new = []
    for dil in ATT_DILS:
        ok = live & (s2 < s_new) & (dist >= 0) & (dist % dil == 0)
        new.append(np.where(ok[None], -slopes * dist[None], NEG))
    return cached, jnp.asarray(np.stack(new), F32)


def _attn_sample_body(new_ref, c0_ref, c1_ref, c2_ref, bw0_ref, bw1_ref, bw2_ref, bn_ref, o_ref, lse_ref, *, s_new):
    nt = (((1,), (1,)), ((), ()))
    lse_ref[0] = jnp.zeros(lse_ref.shape[1:], F32)
    for gi, (cache, bw_ref) in enumerate(zip((c0_ref, c1_ref, c2_ref), (bw0_ref, bw1_ref, bw2_ref))):
        for h in range(H_G):
            cs = slice(gi * GW + h * HD, gi * GW + (h + 1) * HD)
            c0 = gi * 3 * GW + h * HD
            q = new_ref[0, :, c0:c0 + HD] * (HD ** -0.5)
            k_new = new_ref[0, :, GW + c0:GW + c0 + HD]
            v_new = new_ref[0, :, 2 * GW + c0:2 * GW + c0 + HD]
            kt = cache[0, 0, h].astype(BF16)
            vt = cache[0, 1, h].astype(BF16)
            sw = jnp.dot(q.astype(BF16), kt, preferred_element_type=F32) + bw_ref[h]
            sn = lax.dot_general(q, k_new, nt, preferred_element_type=F32, precision=HIGHEST) + bn_ref[gi, h]
            mx = jnp.maximum(jnp.max(sw, axis=-1, keepdims=True), jnp.max(sn, axis=-1, keepdims=True))
            pw = jnp.exp(sw - mx)
            pn = jnp.exp(sn - mx)
            den = jnp.sum(pw, axis=-1, keepdims=True) + jnp.sum(pn, axis=-1, keepdims=True)
            o = (lax.dot_general(pw.astype(BF16), vt, nt, preferred_element_type=F32)
                 + jnp.dot(pn, v_new, preferred_element_type=F32, precision=HIGHEST)) / den
            o_ref[0, :, cs] = o[0:s_new]
            lse_ref[0, :, gi * LANES + h:gi * LANES + h + 1] = (mx + jnp.log(den))[0:s_new]


def _attn_sample(att, caches):
    b, s_new, c = att.shape
    assert s_new <= min(SUBLANES, ATT_DILS[1])
    new = jnp.pad(att, ((0, 0), (0, SUBLANES - s_new), (0, 0)))
    bufs, specs = [], []
    for dil, cache in zip(ATT_DILS, caches):
        lw = cache.shape[1]
        assert lw == dil * STEPS
        bufs.append(jnp.transpose(cache, (0, 2, 3, 4, 1)))
        specs.append(pl.BlockSpec((1, 2, H_G, HD, lw), lambda bi: (bi, 0, 0, 0, 0)))
    bias_cached, bias_new = _sample_bias_tables(s_new)
    ng = len(ATT_DILS)
    o, lse = pl.pallas_call(
        functools.partial(_attn_sample_body, s_new=s_new),
        grid=(b,),
        in_specs=[pl.BlockSpec((1, SUBLANES, c), lambda bi: (bi, 0, 0))] + specs
        + [pl.BlockSpec(bc.shape, lambda bi: (0, 0, 0)) for bc in bias_cached]
        + [pl.BlockSpec(bias_new.shape, lambda bi: (0, 0, 0, 0))],
        out_specs=[pl.BlockSpec((1, s_new, D_ATT), lambda bi: (bi, 0, 0)),
                   pl.BlockSpec((1, s_new, ng * LANES), lambda bi: (bi, 0, 0))],
        out_shape=[jax.ShapeDtypeStruct((b, s_new, D_ATT), F32),
                   jax.ShapeDtypeStruct((b, s_new, ng * LANES), F32)],
        compiler_params=pltpu.CompilerParams(dimension_semantics=("parallel",), vmem_limit_bytes=VMEM_LIMIT),
        name="attn_sample",
    )(new, *bufs, *bias_cached, bias_new)
    return o.reshape(b * s_new, D_ATT), lse.reshape(b * s_new, ng * LANES)


HIST = SUBLANES


def _bdot(a, b):
    return jnp.dot(a.astype(BF16), b.astype(BF16), preferred_element_type=F32)


def _delta_prep_body(u_ref, prev_ref, small_ref, cbuf_ref, cw_ref, alog_ref, dtb_ref,
                     uo_ref, wq_ref, ak_ref, eg_ref, ext, *, t_valid, cp):
    c = pl.program_id(1)
    rows = cp * CHUNK

    @pl.when(c == 0)
    def _():
        ext[0:HIST, :] = cbuf_ref[0]

    @pl.when(c > 0)
    def _():
        ext[0:HIST, :] = prev_ref[0]

    ext[HIST:HIST + rows, :] = u_ref[0]
    conv = ext[HIST - 3:HIST - 3 + rows, :] * cw_ref[0:1, :]
    for i in range(1, CONV_W):
        conv = conv + ext[HIST - 3 + i:HIST - 3 + i + rows, :] * cw_ref[i:i + 1, :]
    act = conv * _sigmoid(conv)

    sm = small_ref[0]
    bmat = _sigmoid(sm)
    xg = sm + dtb_ref[...]
    gmat = -jnp.exp(alog_ref[...]) * (jnp.maximum(xg, 0.0) + jnp.log1p(jnp.exp(-jnp.abs(xg))))
    if t_valid < rows:
        live = lax.broadcasted_iota(jnp.int32, (rows, 1), 0) < t_valid
        act = jnp.where(live, act, 0.0)
        bmat = jnp.where(live, bmat, 0.0)
        gmat = jnp.where(live, gmat, 0.0)

    ri = lax.broadcasted_iota(jnp.int32, (CHUNK, CHUNK), 0)
    ci = lax.broadcasted_iota(jnp.int32, (CHUNK, CHUNK), 1)
    tril = ri >= ci
    strict = ri > ci

    pairs = [(j, h) for j in range(cp) for h in range(H_D)]
    rsl = lambda j: slice(j * CHUNK, (j + 1) * CHUNK)
    gcs, gcts = [], []
    for j in range(cp):
        gc_j = jnp.dot(tril.astype(F32), gmat[rsl(j)], preferred_element_type=F32, precision=HIGHEST)
        gcs.append(gc_j)
        gcts.append(gc_j.T)
    q = jnp.stack([act[rsl(j), h * DK:(h + 1) * DK] for j, h in pairs])
    k = jnp.stack([act[rsl(j), D_DK + h * DK:D_DK + (h + 1) * DK] for j, h in pairs])
    v = jnp.stack([act[rsl(j), 2 * D_DK + h * DV:2 * D_DK + (h + 1) * DV] for j, h in pairs])
    beta = jnp.stack([bmat[rsl(j), h:h + 1] for j, h in pairs])
    gcol = jnp.stack([gcs[j][:, H_D + h:H_D + h + 1] for j, h in pairs])
    grow = jnp.stack([gcts[j][H_D + h:H_D + h + 1, :] for j, h in pairs])
    glast = jnp.stack([gcs[j][CHUNK - 1:CHUNK, H_D + h:H_D + h + 1] for j, h in pairs])

    qn = q * lax.rsqrt(jnp.sum(q * q, axis=-1, keepdims=True) + NORM_EPS) * (DK ** -0.5)
    kn = k * lax.rsqrt(jnp.sum(k * k, axis=-1, keepdims=True) + NORM_EPS)
    decay = jnp.where(tril, jnp.exp(jnp.where(tril, gcol - grow, 0.0)), 0.0)
    eg = jnp.exp(gcol)
    kb = kn * beta
    kn_b = kn.astype(BF16)
    a_kk = jnp.einsum('bik,bjk->bij', kb.astype(BF16), kn_b, preferred_element_type=F32)
    a_qk = jnp.einsum('bik,bjk->bij', qn.astype(BF16), kn_b, preferred_element_type=F32)
    pw = jnp.where(strict, -(a_kk * decay), 0.0)
    tm1 = pw
    for _ in range(int(math.log2(CHUNK)) - 1):
        pw_b = pw.astype(BF16)
        pw = jnp.einsum('bij,bjk->bik', pw_b, pw_b, preferred_element_type=F32)
        tm1 = tm1 + pw + jnp.einsum('bij,bjk->bik', tm1.astype(BF16), pw.astype(BF16), preferred_element_type=F32)
    vb = v * beta
    kbg = kb * eg
    tm1_b = tm1.astype(BF16)
    u = vb + jnp.einsum('bij,bjk->bik', tm1_b, vb.astype(BF16), preferred_element_type=F32)
    w = (kbg + jnp.einsum('bij,bjk->bik', tm1_b, kbg.astype(BF16), preferred_element_type=F32)).astype(BF16)
    qd = (qn * eg).astype(BF16)
    aqd = (a_qk * decay).astype(BF16)
    kd = kn * jnp.exp(glast - gcol)
    egl = jnp.exp(glast)
    for i, (j, h) in enumerate(pairs):
        uo_ref[0, rsl(j), h * DV:(h + 1) * DV] = u[i]
        wq_ref[0, j, h, 0:CHUNK, :] = w[i]
        wq_ref[0, j, h, CHUNK:2 * CHUNK, :] = qd[i]
        ak_ref[0, j, h, 0:CHUNK, :] = aqd[i]
        ak_ref[0, j, h, CHUNK:CHUNK + DK, :] = kd[i].T.astype(BF16)
        eg_ref[0, j, h:h + 1, :] = jnp.broadcast_to(egl[i], (1, LANES))
    for j in range(cp):
        eg_ref[0, j, H_D:SUBLANES, :] = jnp.zeros((SUBLANES - H_D, LANES), F32)


def _delta_rec_body(u_ref, wq_ref, ak_ref, eg_ref, z_ref, s0_ref, nw_ref, o_ref, sout_ref, S, *, bb, nc):
    c = pl.program_id(1)

    @pl.when(c == 0)
    def _():
        S[...] = s0_ref[...]

    for b in range(bb):
        for h in range(H_D):
            cs = slice(h * DV, (h + 1) * DV)
            s_h = S[b, h]
            x = jnp.dot(wq_ref[b, 0, h], s_h.astype(BF16), preferred_element_type=F32)
            vn = u_ref[b, :, cs] - x[0:CHUNK]
            y = jnp.dot(ak_ref[b, 0, h], vn.astype(BF16), preferred_element_type=F32)
            o = x[CHUNK:2 * CHUNK] + y[0:CHUNK]
            S[b, h] = s_h * eg_ref[b, 0, h:h + 1, :] + y[CHUNK:CHUNK + DK]
            od = o * lax.rsqrt(jnp.mean(o * o, axis=-1, keepdims=True) + NORM_EPS) * nw_ref[...]
            zh = z_ref[b, :, cs]
            o_ref[b, :, cs] = od * (zh * _sigmoid(zh))

    @pl.when(c == nc - 1)
    def _():
        sout_ref[...] = S[...]


def _lane_row(vals, offset):
    row = jnp.zeros((1, LANES), F32)
    return row.at[0, offset:offset + vals.shape[0]].set(vals.astype(F32))


def _delta_net(dqkv, small, z, conv_buf, s0, conv_w, a_log, dt_bias, delta_norm_w):
    b, t, _ = dqkv.shape
    tp = -(-t // CHUNK) * CHUNK
    nc = tp // CHUNK
    assert tp == t or nc == 1
    if tp != t:
        pad = ((0, 0), (0, tp - t), (0, 0))
        dqkv, small, z = jnp.pad(dqkv, pad), jnp.pad(small, pad), jnp.pad(z, pad)
    cbuf = jnp.pad(conv_buf, ((0, 0), (HIST - (CONV_W - 1), 0), (0, 0)))
    cw = jnp.pad(conv_w, ((0, SUBLANES - CONV_W), (0, 0)))
    cp = next(c for c in (4, 2, 1) if nc % c == 0)
    rows = cp * CHUNK
    full2 = lambda shape: pl.BlockSpec(shape, lambda bi, ci: (0,) * len(shape))
    u, wq, ak, eg = pl.pallas_call(
        functools.partial(_delta_prep_body, t_valid=t if nc == 1 else rows, cp=cp),
        grid=(b, nc // cp),
        in_specs=[pl.BlockSpec((1, rows, D_CONV), lambda bi, ci: (bi, ci, 0)),
                  pl.BlockSpec((1, HIST, D_CONV), lambda bi, ci: (bi, jnp.maximum(ci * (rows // HIST) - 1, 0), 0)),
                  pl.BlockSpec((1, rows, LANES), lambda bi, ci: (bi, ci, 0)),
                  pl.BlockSpec((1, HIST, D_CONV), lambda bi, ci: (bi, 0, 0)),
                  full2((SUBLANES, D_CONV)), full2((1, LANES)), full2((1, LANES))],
        out_specs=[pl.BlockSpec((1, rows, D_DV), lambda bi, ci: (bi, ci, 0)),
                   pl.BlockSpec((1, cp, H_D, 2 * CHUNK, DK), lambda bi, ci: (bi, ci, 0, 0, 0)),
                   pl.BlockSpec((1, cp, H_D, CHUNK + DK, CHUNK), lambda bi, ci: (bi, ci, 0, 0, 0)),
                   pl.BlockSpec((1, cp, SUBLANES, LANES), lambda bi, ci: (bi, ci, 0, 0))],
        out_shape=[jax.ShapeDtypeStruct((b, tp, D_DV), F32),
                   jax.ShapeDtypeStruct((b, nc, H_D, 2 * CHUNK, DK), BF16),
                   jax.ShapeDtypeStruct((b, nc, H_D, CHUNK + DK, CHUNK), BF16),
                   jax.ShapeDtypeStruct((b, nc, SUBLANES, LANES), F32)],
        scratch_shapes=[pltpu.VMEM((HIST + rows, D_CONV), F32)],
        compiler_params=pltpu.CompilerParams(dimension_semantics=("parallel", "parallel"),
                                             vmem_limit_bytes=VMEM_LIMIT),
        name="delta_prep",
    )(dqkv, dqkv, small, cbuf, cw, _lane_row(a_log, H_D), _lane_row(dt_bias, H_D))

    bb = next(c for c in (8, 4, 2, 1) if b % c == 0)
    o, s_new = pl.pallas_call(
        functools.partial(_delta_rec_body, bb=bb, nc=nc),
        grid=(b // bb, nc),
        in_specs=[pl.BlockSpec((bb, CHUNK, D_DV), lambda bi, ci: (bi, ci, 0)),
                  pl.BlockSpec((bb, 1, H_D, 2 * CHUNK, DK), lambda bi, ci: (bi, ci, 0, 0, 0)),
                  pl.BlockSpec((bb, 1, H_D, CHUNK + DK, CHUNK), lambda bi, ci: (bi, ci, 0, 0, 0)),
                  pl.BlockSpec((bb, 1, SUBLANES, LANES), lambda bi, ci: (bi, ci, 0, 0)),
                  pl.BlockSpec((bb, CHUNK, D_DV), lambda bi, ci: (bi, ci, 0)),
                  pl.BlockSpec((bb, H_D, DK, DV), lambda bi, ci: (bi, 0, 0, 0)),
                  full2((1, DV))],
        out_specs=[pl.BlockSpec((bb, CHUNK, D_DV), lambda bi, ci: (bi, ci, 0)),
                   pl.BlockSpec((bb, H_D, DK, DV), lambda bi, ci: (bi, 0, 0, 0))],
        out_shape=[jax.ShapeDtypeStruct((b, tp, D_DV), F32), jax.ShapeDtypeStruct((b, H_D, DK, DV), F32)],
        scratch_shapes=[pltpu.VMEM((bb, H_D, DK, DV), F32)],
        compiler_params=pltpu.CompilerParams(dimension_semantics=("parallel", "arbitrary"),
                                             vmem_limit_bytes=VMEM_LIMIT),
        name="delta_recurrence",
    )(u, wq, ak, eg, z, s0, delta_norm_w.reshape(1, DV).astype(F32))
    return o[:, :t].reshape(b * t, D_DV), s_new


def _lane_expand(cols, width):
    tm = cols[0].shape[0]
    seg = lax.broadcasted_iota(jnp.int32, (tm, len(cols) * width), 1) // width
    out = jnp.broadcast_to(cols[-1], seg.shape)
    for i in range(len(cols) - 2, -1, -1):
        out = jnp.where(seg == i, cols[i], out)
    return out


def _token_order(ref, scr):
    d, nc = ref.shape[1], ref.shape[3] // LANES
    if d == 1:
        return ref[0, 0]
    for r in range(d):
        for c in range(nc):
            scr[c, pl.ds(r, ref.shape[2], stride=d), :] = ref[0, r, :, c * LANES:(c + 1) * LANES]
    return jnp.concatenate([scr[c] for c in range(nc)], axis=-1)


def _mix_body(x_ref, gate_ref, o0_ref, o1_ref, o2_ref, l0_ref, l1_ref, l2_ref, od_ref,
              wa_ref, wd_ref, wo_ref, bg_ref, g1_ref, b1_ref, rw_ref, rb_ref,
              h_ref, ti_ref, tg_ref, *scratch):
    o_scr, l_scr = scratch[:3], scratch[3:]
    lses = [_token_order(l, s) for l, s in zip((l0_ref, l1_ref, l2_ref), l_scr)]
    mx = jnp.maximum(jnp.maximum(lses[0], lses[1]), lses[2])
    es = [jnp.exp(l - mx) for l in lses]
    inv = 1.0 / (es[0] + es[1] + es[2])
    o_att = None
    for e_g, o_g, s in zip(es, (o0_ref, o1_ref, o2_ref), o_scr):
        wgt = e_g * inv
        term = _lane_expand([wgt[:, h:h + 1] for h in range(H_G)], HD) * _token_order(o_g, s)
        o_att = term if o_att is None else o_att + term
    ga = _sigmoid(gate_ref[:, :D_MODEL] + bg_ref[:, :D_MODEL])
    gd = _sigmoid(gate_ref[:, D_MODEL:] + bg_ref[:, D_MODEL:])
    merged = (ga * jnp.dot(o_att.astype(BF16), wa_ref[...], preferred_element_type=F32)
              + gd * jnp.dot(od_ref[...].astype(BF16), wd_ref[...], preferred_element_type=F32))
    mix = jnp.dot(merged.astype(BF16), wo_ref[...], preferred_element_type=F32)
    h = _layer_norm(DN_ALPHA * x_ref[...] + mix, g1_ref[...], b1_ref[...])
    _rows_to_tiles(h_ref, h)

    h_hi = h.astype(BF16)
    h_lo = (h - h_hi.astype(F32)).astype(BF16)
    part = jnp.dot(h_hi, rw_ref[...], preferred_element_type=F32)
    logits = (part[:, :LANES] + part[:, LANES:]
              + jnp.dot(h_lo, rw_ref[:, :LANES], preferred_element_type=F32) + rb_ref[...])
    lane = lax.broadcasted_iota(jnp.int32, logits.shape, 1)
    lane_f = lane.astype(F32)
    ti = jnp.zeros(logits.shape, F32)
    tv = jnp.zeros(logits.shape, F32)
    top = None
    for kk in range(TOP_K):
        m = jnp.max(logits, axis=-1, keepdims=True)
        idx = jnp.min(jnp.where(logits == m, lane_f, float(LANES)), axis=-1, keepdims=True)
        if top is None:
            top = m
        ti = jnp.where(lane == kk, idx, ti)
        tv = jnp.where(lane == kk, jnp.exp(m - top), tv)
        logits = jnp.where(lane_f == idx, -jnp.inf, logits)
    ti_ref[...] = ti.astype(jnp.int32)
    tg_ref[...] = tv / jnp.sum(tv, axis=-1, keepdims=True)


def _mix_and_route(x2d, gates, o_groups, lse_groups, o_del, wa, wd, wo, b_gate, ln_g, ln_b, rw, rb, tm):
    n = x2d.shape[0]
    tpb = o_groups[0].shape[1] * o_groups[0].shape[2] // tm
    row = lambda c: pl.BlockSpec((tm, c), lambda i: (i, 0))
    full = lambda a: pl.BlockSpec(a.shape, lambda i: (0, 0))

    def res(a):
        d = a.shape[1]
        return pl.BlockSpec((1, d, tm // d, a.shape[3]), lambda i: (i // tpb, 0, i % tpb, 0))

    consts = (wa, wd, wo, b_gate, ln_g, ln_b, rw, rb)
    return pl.pallas_call(
        _mix_body,
        grid=(n // tm,),
        in_specs=[row(D_MODEL), row(2 * D_MODEL)] + [res(a) for a in o_groups] + [res(a) for a in lse_groups]
        + [row(D_DV)] + [full(a) for a in consts],
        out_specs=[pl.BlockSpec((tm * TOK_ROWS, LANES), lambda i: (i, 0)), row(LANES), row(LANES)],
        out_shape=[jax.ShapeDtypeStruct((n * TOK_ROWS, LANES), F32), jax.ShapeDtypeStruct((n, LANES), jnp.int32),
                   jax.ShapeDtypeStruct((n, LANES), F32)],
        scratch_shapes=[pltpu.VMEM((GW // LANES, tm, LANES), F32)] * len(o_groups)
        + [pltpu.VMEM((1, tm, LANES), F32)] * len(lse_groups),
        compiler_params=pltpu.CompilerParams(dimension_semantics=("parallel",), vmem_limit_bytes=VMEM_LIMIT),
        name="mix_and_route",
    )(x2d, gates, *o_groups, *lse_groups, o_del, *consts)


def _rank_body(ti_ref, rank_ref, cnt_ref, carry):
    i = pl.program_id(0)

    @pl.when(i == 0)
    def _():
        carry[...] = jnp.zeros_like(carry)

    ti = ti_ref[...]
    tm = ti.shape[0]
    lane = lax.broadcasted_iota(jnp.int32, ti.shape, 1)
    sel = [lane == ti[:, kk:kk + 1] for kk in range(TOP_K)]
    hit = sel[0]
    for kk in range(1, TOP_K):
        hit = hit | sel[kk]
    cnt = hit.astype(F32)
    ri = lax.broadcasted_iota(jnp.int32, (tm, tm), 0)
    ci = lax.broadcasted_iota(jnp.int32, (tm, tm), 1)
    incl = jnp.dot((ri >= ci).astype(BF16), cnt.astype(BF16), preferred_element_type=F32)
    before = incl - cnt + carry[...]
    rank = jnp.zeros(ti.shape, jnp.int32)
    for kk in range(TOP_K):
        r = jnp.sum(jnp.where(sel[kk], before, 0.0), axis=-1, keepdims=True)
        rank = jnp.where(lane == kk, r.astype(jnp.int32), rank)
    rank_ref[...] = rank
    carry[...] = carry[...] + incl[tm - 1:tm, :]
    cnt_ref[...] = carry[...]


def _route_ranks(top_i, tm):
    n = top_i.shape[0]
    return pl.pallas_call(
        _rank_body,
        grid=(n // tm,),
        in_specs=[pl.BlockSpec((tm, LANES), lambda i: (i, 0))],
        out_specs=[pl.BlockSpec((tm, LANES), lambda i: (i, 0)), pl.BlockSpec((1, LANES), lambda i: (0, 0))],
        out_shape=[jax.ShapeDtypeStruct((n, LANES), jnp.int32), jax.ShapeDtypeStruct((1, LANES), F32)],
        scratch_shapes=[pltpu.VMEM((1, LANES), F32)],
        compiler_params=pltpu.CompilerParams(dimension_semantics=("arbitrary",), vmem_limit_bytes=VMEM_LIMIT),
        name="route_ranks",
    )(top_i)


def _rows_to_tiles(ref, val):
    m = val.shape[0]
    for s in range(TOK_ROWS):
        ref[pl.ds(s, m, stride=TOK_ROWS), :] = val[:, s * LANES:(s + 1) * LANES]


def _tiles_to_rows(ref, m):
    return jnp.concatenate([ref[pl.ds(s, m, stride=TOK_ROWS), :] for s in range(TOK_ROWS)], axis=-1)


def _dispatch_body(pend_ref, padded_ref, dest_ref, h_ref, *refs, max_tail):
    tm = h_ref.shape[0] // TOK_ROWS
    blk = MOE_BM * TOK_ROWS
    first_group = len(refs) == 4
    if first_group:
        xs_out, zero, sem, zsem = refs
    else:
        _, xs_out, sem = refs

    def zero_fill():
        zero[...] = jnp.zeros_like(zero)

        def last_block(e):
            start = pl.multiple_of((pend_ref[e] - MOE_BM) * TOK_ROWS, blk)
            return pltpu.make_async_copy(zero, xs_out.at[pl.ds(start, blk)], zsem)

        def tail_block(j):
            start = pl.multiple_of((pend_ref[N_EXP - 1] + j * MOE_BM) * TOK_ROWS, blk)
            return pltpu.make_async_copy(zero, xs_out.at[pl.ds(start, blk)], zsem)

        n_slot = xs_out.shape[0] // TOK_ROWS
        for e in range(N_EXP):
            @pl.when(padded_ref[e] > 0)
            def _():
                last_block(e).start()
        for j in range(max_tail):
            @pl.when(pend_ref[N_EXP - 1] + j * MOE_BM < n_slot)
            def _():
                tail_block(j).start()
        for e in range(N_EXP):
            @pl.when(padded_ref[e] > 0)
            def _():
                last_block(e).wait()
        for j in range(max_tail):
            @pl.when(pend_ref[N_EXP - 1] + j * MOE_BM < n_slot)
            def _():
                tail_block(j).wait()

    if first_group:
        pl.when(pl.program_id(0) == 0)(zero_fill)

    def issue(t, carry):
        src = h_ref.at[pl.ds(pl.multiple_of(t * TOK_ROWS, TOK_ROWS), TOK_ROWS)]
        for kk in range(TOP_K):
            d = pl.multiple_of(dest_ref[t * TOP_K + kk] * TOK_ROWS, TOK_ROWS)
            pltpu.make_async_copy(src, xs_out.at[pl.ds(d, TOK_ROWS)], sem).start(priority=kk % 2)
        return carry

    lax.fori_loop(0, tm, issue, 0, unroll=8)
    for kk in range(TOP_K):
        pltpu.make_async_copy(h_ref, xs_out.at[pl.ds(0, tm * TOK_ROWS)], sem).wait()


def _dispatch(h_tiles, dest_flat, pad_end, padded, n_slot, tm, xs_prev=None):
    n = h_tiles.shape[0] // TOK_ROWS
    first_group = xs_prev is None
    in_specs = [pl.BlockSpec((tm * TOP_K,), lambda i, pe, pd: (i,), memory_space=pltpu.SMEM),
                pl.BlockSpec((tm * TOK_ROWS, LANES), lambda i, pe, pd: (i, 0))]
    scratch = [pltpu.SemaphoreType.DMA(())]
    if first_group:
        scratch = [pltpu.VMEM((MOE_BM * TOK_ROWS, LANES), F32)] + scratch + [pltpu.SemaphoreType.DMA(())]
    else:
        in_specs.append(pl.BlockSpec(memory_space=pl.ANY))
    grid_spec = pltpu.PrefetchScalarGridSpec(
        num_scalar_prefetch=2, grid=(n // tm,), in_specs=in_specs,
        out_specs=pl.BlockSpec(memory_space=pl.ANY), scratch_shapes=scratch)
    max_tail = n_slot // MOE_BM - (n * TOP_K) // MOE_BM
    return pl.pallas_call(
        functools.partial(_dispatch_body, max_tail=max_tail),
        grid_spec=grid_spec,
        out_shape=jax.ShapeDtypeStruct((n_slot * TOK_ROWS, LANES), F32),
        input_output_aliases={} if first_group else {4: 0},
        compiler_params=pltpu.CompilerParams(dimension_semantics=("arbitrary",), vmem_limit_bytes=VMEM_LIMIT,
                                             has_side_effects=True, disable_bounds_checks=True),
        name="moe_dispatch",
    )(pad_end, padded, dest_flat, h_tiles, *(() if first_group else (xs_prev,)))


def _expert_body(be_ref, nu_ref, x_ref, wgu_ref, bgu_ref, wd_ref, bd_ref, o_ref, wgu_bf, wd_bf):
    i = pl.program_id(0)
    e = be_ref[i]
    prev = be_ref[jnp.maximum(i - 1, 0)]

    @pl.when((i == 0) | (e != prev))
    def _():
        wgu_bf[...] = wgu_ref[0].astype(BF16)
        wd_bf[...] = wd_ref[0].astype(BF16)

    @pl.when(i < nu_ref[0])
    def _():
        x = _tiles_to_rows(x_ref, MOE_BM)
        gu = jnp.dot(x.astype(BF16), wgu_bf[...], preferred_element_type=F32) + bgu_ref[0]
        gt = jnp.minimum(gu[:, :D_FF], SWIGLU_LIMIT)
        up = jnp.clip(gu[:, D_FF:], -SWIGLU_LIMIT, SWIGLU_LIMIT)
        act = (up + 1.0) * gt * _sigmoid(SWIGLU_ALPHA * gt)
        _rows_to_tiles(o_ref, jnp.dot(act.astype(BF16), wd_bf[...], preferred_element_type=F32) + bd_ref[0])

    @pl.when(i >= nu_ref[0])
    def _():
        o_ref[...] = jnp.zeros_like(o_ref)


def _experts(xs, blk_exp, n_used, w_gu, b_gu, w_down, b_down):
    n_slot = xs.shape[0] // TOK_ROWS
    n_blk = n_slot // MOE_BM
    rows = lambda i, be, nu: (jnp.minimum(i, nu[0] - 1), 0)
    grid_spec = pltpu.PrefetchScalarGridSpec(
        num_scalar_prefetch=2,
        grid=(n_blk,),
        in_specs=[pl.BlockSpec((MOE_BM * TOK_ROWS, LANES), rows),
                  pl.BlockSpec((1, D_MODEL, 2 * D_FF), lambda i, be, nu: (be[i], 0, 0)),
                  pl.BlockSpec((1, 1, 2 * D_FF), lambda i, be, nu: (be[i], 0, 0)),
                  pl.BlockSpec((1, D_FF, D_MODEL), lambda i, be, nu: (be[i], 0, 0)),
                  pl.BlockSpec((1, 1, D_MODEL), lambda i, be, nu: (be[i], 0, 0))],
        out_specs=pl.BlockSpec((MOE_BM * TOK_ROWS, LANES), lambda i, be, nu: (i, 0)),
        scratch_shapes=[pltpu.VMEM((D_MODEL, 2 * D_FF), BF16), pltpu.VMEM((D_FF, D_MODEL), BF16)],
    )
    return pl.pallas_call(
        _expert_body,
        grid_spec=grid_spec,
        out_shape=jax.ShapeDtypeStruct((n_slot * TOK_ROWS, LANES), F32),
        compiler_params=pltpu.CompilerParams(dimension_semantics=("arbitrary",), vmem_limit_bytes=VMEM_LIMIT),
        name="moe_experts",
    )(blk_exp, n_used, xs, w_gu, b_gu.reshape(N_EXP, 1, 2 * D_FF), w_down, b_down.reshape(N_EXP, 1, D_MODEL))


def _combine_body(dest_ref, dest_next_ref, gate_ref, h_ref, g_ref, b_ref, ys_ref, y_ref, buf, sem):
    tm = h_ref.shape[0] // TOK_ROWS
    i = pl.program_id(0)
    slot = i % 2

    def start_tile(idx_ref, s):
        def issue(t, carry):
            row = pl.multiple_of(t * TOK_ROWS, TOK_ROWS)
            for kk in range(TOP_K):
                d = pl.multiple_of(idx_ref[t * TOP_K + kk] * TOK_ROWS, TOK_ROWS)
                pltpu.make_async_copy(ys_ref.at[pl.ds(d, TOK_ROWS)], buf.at[s, kk, pl.ds(row, TOK_ROWS)],
                                      sem.at[s]).start(priority=kk % 2)
            return carry

        lax.fori_loop(0, tm, issue, 0, unroll=8)

    @pl.when(i == 0)
    def _():
        start_tile(dest_ref, 0)

    @pl.when(i + 1 < pl.num_programs(0))
    def _():
        start_tile(dest_next_ref, 1 - slot)

    for kk in range(TOP_K):
        pltpu.make_async_copy(ys_ref.at[pl.ds(0, tm * TOK_ROWS)], buf.at[slot, kk], sem.at[slot]).wait()
    gate = gate_ref[...]
    moe = gate[:, 0:1] * _tiles_to_rows(buf.at[slot, 0], tm)
    for kk in range(1, TOP_K):
        moe = moe + gate[:, kk:kk + 1] * _tiles_to_rows(buf.at[slot, kk], tm)
    y_ref[...] = _layer_norm(DN_ALPHA * _tiles_to_rows(h_ref, tm) + moe, g_ref[...], b_ref[...])


def _combine(ys, dest_flat, gate, h_tiles, ln_g, ln_b, tm):
    n = h_tiles.shape[0] // TOK_ROWS
    return pl.pallas_call(
        _combine_body,
        grid=(n // tm,),
        in_specs=[pl.BlockSpec((tm * TOP_K,), lambda i: (i,), memory_space=pltpu.SMEM),
                  pl.BlockSpec((tm * TOP_K,), lambda i: (jnp.minimum(i + 1, n // tm - 1),), memory_space=pltpu.SMEM),
                  pl.BlockSpec((tm, LANES), lambda i: (i, 0)),
                  pl.BlockSpec((tm * TOK_ROWS, LANES), lambda i: (i, 0)),
                  pl.BlockSpec((1, D_MODEL), lambda i: (0, 0)),
                  pl.BlockSpec((1, D_MODEL), lambda i: (0, 0)),
                  pl.BlockSpec(memory_space=pl.ANY)],
        out_specs=pl.BlockSpec((tm, D_MODEL), lambda i: (i, 0)),
        out_shape=jax.ShapeDtypeStruct((n, D_MODEL), F32),
        scratch_shapes=[pltpu.VMEM((2, TOP_K, tm * TOK_ROWS, LANES), F32), pltpu.SemaphoreType.DMA((2,))],
        compiler_params=pltpu.CompilerParams(dimension_semantics=("arbitrary",), vmem_limit_bytes=VMEM_LIMIT,
                                             disable_bounds_checks=True),
        name="moe_combine",
    )(dest_flat, dest_flat, gate, h_tiles, ln_g, ln_b, ys)


def _moe(groups, w_gu, b_gu, w_down, b_down, ln_g, ln_b):
    ranks, counts = [], []
    for _, top_i, _, tm in groups:
        rank, cnt = _route_ranks(top_i, tm)
        ranks.append(rank)
        counts.append(cnt[0, :N_EXP].astype(jnp.int32))
    total = sum(counts)
    padded = (total + MOE_BM - 1) // MOE_BM * MOE_BM
    pad_end = jnp.cumsum(padded).astype(jnp.int32)
    n_assign = sum(g[1].shape[0] for g in groups) * TOP_K
    n_blk = -(-(n_assign + N_EXP * (MOE_BM - 1)) // MOE_BM)
    blk_row0 = jnp.arange(n_blk, dtype=jnp.int32) * MOE_BM
    blk_exp = jnp.minimum(jnp.sum((pad_end[None, :] <= blk_row0[:, None]).astype(jnp.int32), axis=1), N_EXP - 1)
    n_used = pad_end[-1:] // MOE_BM
    base = pad_end - padded
    xs, dests = None, []
    for (h, top_i, _, tm), rank, cnt in zip(groups, ranks, counts):
        dest = (base[top_i[:, :TOP_K]] + rank[:, :TOP_K]).reshape(top_i.shape[0] * TOP_K)
        dests.append(dest)
        xs = _dispatch(h, dest, pad_end, padded, n_blk * MOE_BM, tm, xs)
        base = base + cnt
    ys = _experts(xs, blk_exp, n_used, w_gu, b_gu, w_down, b_down)
    return [_combine(ys, dest, top_g, h, ln_g, ln_b, tm) for (h, _, top_g, tm), dest in zip(groups, dests)]


def _pad_cols(a, width, fill=0.0):
    return jnp.pad(a, ((0, 0), (0, width - a.shape[1])), constant_values=fill)


def _token_mixers_and_route(x, caches, conv_buf, s0, w_in, b_gate, conv_w, a_log, dt_bias, delta_norm_w,
                            w_branch_attn, w_branch_delta, w_out, ln1_g, ln1_b, router_w, router_b):
    b, t, _ = x.shape
    n = b * t
    tm = min(256, n)
    x2d = x.reshape(n, D_MODEL)

    c_att, c_dz, c_small = 3 * D_ATT, 3 * D_ATT + D_CONV, 3 * D_ATT + D_CONV + D_DV
    c_gate = c_small + 2 * H_D
    ng = len(ATT_DILS)
    w_groups = [jnp.concatenate([w_in[:, part * D_ATT + gi * GW:part * D_ATT + (gi + 1) * GW] for part in range(3)],
                                axis=1) for gi in range(ng)]
    ws = w_groups + [w_in[:, c_att:c_dz], w_in[:, c_dz:c_small],
                     _pad_cols(w_in[:, c_small:c_gate], LANES), w_in[:, c_gate:]]
    dils = (ATT_DILS if caches is None else (1,) * ng) + (1, 1, 1, 1)
    outs = _in_projection(x2d, [w.astype(BF16) for w in ws], dils, t, min(IN_TM, n))
    att_groups, (dqkv, z, small, gates) = outs[:ng], outs[ng:]
    dqkv3 = dqkv.reshape(b, t, D_CONV)

    kv_new = []
    if caches is None:
        o_groups, lse_groups = [], []
        for dil, a in zip(ATT_DILS, att_groups):
            a = a.reshape(b, dil, t // dil, 3 * GW)
            o_g, lse_g = _attn_prompt(a, dil)
            o_groups.append(o_g)
            lse_groups.append(lse_g)
            win = min(dil * STEPS, t)
            last = a[:, :, (t - win) // dil:, GW:]
            last = jnp.transpose(last, (0, 2, 1, 3)).reshape(b, win, 2, H_G, HD)
            kv_new.append(last)
        conv_buf = jnp.zeros((b, CONV_W - 1, D_CONV), F32)
        s0 = jnp.zeros((b, H_D, DK, DV), F32)
    else:
        att3 = jnp.concatenate(att_groups, axis=1).reshape(b, t, 3 * D_ATT)
        o_all, lse_all = _attn_sample(att3, caches)
        o_groups = [o_all[:, gi * GW:(gi + 1) * GW].reshape(1, 1, n, GW) for gi in range(ng)]
        lse_groups = [lse_all[:, gi * LANES:(gi + 1) * LANES].reshape(1, 1, n, LANES) for gi in range(ng)]
        for a in att_groups:
            kv_new.append(a[:, GW:].reshape(b, t, 2, H_G, HD))

    o_del, s_new = _delta_net(dqkv3, small.reshape(b, t, LANES), z.reshape(b, t, D_DV), conv_buf, s0,
                              conv_w, a_log, dt_bias, delta_norm_w)
    conv_new = jnp.concatenate([conv_buf, dqkv3], axis=1)[:, -(CONV_W - 1):] if t < CONV_W - 1 \
        else dqkv3[:, t - (CONV_W - 1):]

    rw = _pad_cols(router_w, LANES)
    rw_hi = rw.astype(BF16)
    rw = jnp.concatenate([rw_hi, (rw - rw_hi.astype(F32)).astype(BF16)], axis=1)
    rb = _pad_cols(router_b.reshape(1, N_EXP), LANES, fill=NEG)
    h, top_i, top_g = _mix_and_route(
        x2d, gates, o_groups, lse_groups, o_del,
        w_branch_attn.astype(BF16), w_branch_delta.astype(BF16), w_out.astype(BF16),
        b_gate.reshape(1, 2 * D_MODEL), ln1_g.reshape(1, D_MODEL), ln1_b.reshape(1, D_MODEL), rw, rb, tm)
    return (h, top_i, top_g, tm), kv_new, conv_new, s_new


def kernel(x_prompt, x_sample, cache_kv_w128, cache_kv_w512, cache_kv_w2048, state_conv, state_delta,
           w_in, b_gate, conv_w, a_log, dt_bias, delta_norm_w, w_branch_attn, w_branch_delta, w_out,
           ln1_g, ln1_b, router_w, router_b, w_gu, b_gu, w_down, b_down, ln2_g, ln2_b):
    depth = w_in.shape[0]
    assert depth == 1
    l = 0
    lw = (w_in[l], b_gate[l], conv_w[l], a_log[l], dt_bias[l], delta_norm_w[l], w_branch_attn[l],
          w_branch_delta[l], w_out[l], ln1_g[l], ln1_b[l], router_w[l], router_b[l], w_gu[l], b_gu[l],
          w_down[l], b_down[l], ln2_g[l], ln2_b[l])
    mixers = lw[:13]
    routed_p, kv_p, cv_p, s_p = _token_mixers_and_route(x_prompt, None, None, None, *mixers)
    routed_s, kv_s, cv_s, s_s = _token_mixers_and_route(
        x_sample, (cache_kv_w128[l], cache_kv_w512[l], cache_kv_w2048[l]), state_conv[l], state_delta[l], *mixers)
    w_gu_l, b_gu_l, w_down_l, b_down_l, ln2_g_l, ln2_b_l = lw[13:]
    yp, ys = _moe([routed_p, routed_s], w_gu_l, b_gu_l, w_down_l, b_down_l,
                  ln2_g_l.reshape(1, D_MODEL), ln2_b_l.reshape(1, D_MODEL))
    yp, ys = yp.reshape(x_prompt.shape), ys.reshape(x_sample.shape)
    stk = lambda a: a[None]
    return (yp, ys, stk(kv_p[0]), stk(kv_p[1]), stk(kv_p[2]), stk(cv_p), stk(s_p),
            stk(kv_s[0]), stk(kv_s[1]), stk(kv_s[2]), stk(cv_s), stk(s_s))
```

```python
import functools
import math

import numpy as np
import jax
import jax.numpy as jnp
from jax import lax
from jax.experimental import pallas as pl
from jax.experimental.pallas import tpu as pltpu

F32 = jnp.float32
BF16 = jnp.bfloat16
HIGHEST = lax.Precision.HIGHEST

D_MODEL = 1024
ATT_DILS = (1, 4, 16)
STEPS = 128
H_G = 4
HD = 64
GW = H_G * HD
D_ATT = len(ATT_DILS) * GW
H_D = 4
DK = 128
DV = 128
D_DK = H_D * DK
D_DV = H_D * DV
D_CONV = 2 * D_DK + D_DV
CONV_W = 4
CHUNK = 64
N_EXP = 32
TOP_K = 4
D_FF = D_MODEL
SWIGLU_LIMIT = 7.0
SWIGLU_ALPHA = 1.702
DN_ALPHA = 2.0 ** 0.25
LN_EPS = 1e-5
NORM_EPS = 1e-6
NEG = -1e30

LANES = 128
SUBLANES = 8
VMEM_LIMIT = 56 * 1024 * 1024
MOE_BM = 512
IN_TM = 512
TOK_TM = 512
TOK_ROWS = D_MODEL // LANES
assert TOK_ROWS == SUBLANES


def _sigmoid(x):
    return 1.0 / (1.0 + jnp.exp(-x))


def _layer_norm(v, g, b):
    mu = jnp.mean(v, axis=-1, keepdims=True)
    d = v - mu
    var = jnp.mean(d * d, axis=-1, keepdims=True)
    return d * lax.rsqrt(var + LN_EPS) * g + b


def _alibi_slopes():
    return 2.0 ** (-8.0 * np.arange(1, H_G + 1, dtype=np.float64) / H_G)


def _inproj_body(x_ref, *refs, dils):
    nw = len(dils)
    scr = refs[2 * nw] if len(refs) > 2 * nw else None
    x = x_ref[...].astype(BF16)
    for w_ref, o_ref, d in zip(refs[:nw], refs[nw:2 * nw], dils):
        y = jnp.dot(x, w_ref[...], preferred_element_type=F32)
        if d == 1:
            o_ref[...] = y
        else:
            for c in range(y.shape[1] // LANES):
                scr[c] = y[:, c * LANES:(c + 1) * LANES]
            for r in range(d):
                for c in range(y.shape[1] // LANES):
                    o_ref[0, r, :, c * LANES:(c + 1) * LANES] = scr[c, pl.ds(r, y.shape[0] // d, stride=d), :]


def _in_projection(x2d, ws, dils, t, tm):
    n = x2d.shape[0]
    tpb = t // tm
    out_specs, out_shape = [], []
    for w, d in zip(ws, dils):
        c = w.shape[1]
        if d == 1:
            out_specs.append(pl.BlockSpec((tm, c), lambda i: (i, 0)))
            out_shape.append(jax.ShapeDtypeStruct((n, c), F32))
        else:
            assert tm % (d * SUBLANES) == 0 and t % tm == 0
            out_specs.append(pl.BlockSpec((1, d, tm // d, c), lambda i: (i // tpb, 0, i % tpb, 0)))
            out_shape.append(jax.ShapeDtypeStruct((n // t, d, t // d, c), F32))
    wide = max([w.shape[1] for w, d in zip(ws, dils) if d > 1], default=0)
    return pl.pallas_call(
        functools.partial(_inproj_body, dils=tuple(dils)),
        grid=(n // tm,),
        in_specs=[pl.BlockSpec((tm, D_MODEL), lambda i: (i, 0))]
        + [pl.BlockSpec((D_MODEL, w.shape[1]), lambda i: (0, 0), pipeline_mode=pl.Buffered(1)) for w in ws],
        out_specs=out_specs,
        out_shape=out_shape,
        scratch_shapes=[pltpu.VMEM((wide // LANES, tm, LANES), F32)] if wide else [],
        compiler_params=pltpu.CompilerParams(dimension_semantics=("parallel",), vmem_limit_bytes=VMEM_LIMIT),
        name="in_projection",
    )(x2d, *ws)


def _prompt_bias_table(dil):
    qi = np.arange(STEPS)[:, None]
    kj = np.arange(2 * STEPS)[None, :]
    steps = qi + STEPS - kj
    valid = (steps >= 0) & (steps <= STEPS)
    slopes = _alibi_slopes()
    bias = -slopes[:, None, None] * (dil * steps)[None].astype(np.float64)
    later = np.where(valid[None], bias, NEG)
    first = np.where((valid & (kj >= STEPS))[None], bias, NEG)
    return jnp.asarray(np.stack([first, later]).reshape(2, H_G * STEPS, 2 * STEPS), F32)


ATT_QB = 4


def _attn_prompt_body(q_ref, kp_ref, kc_ref, vp_ref, vc_ref, bias_ref, o_ref, lse_ref):
    kk = jnp.concatenate([kp_ref[0, 0], kc_ref[0, 0]], axis=0).astype(BF16)
    vv = jnp.concatenate([vp_ref[0, 0], vc_ref[0, 0]], axis=0).astype(BF16)
    head = lax.broadcasted_iota(jnp.int32, (STEPS, GW), 1) // HD
    lse_ref[0, 0] = jnp.zeros(lse_ref.shape[2:], F32)
    for j in range(o_ref.shape[2] // STEPS):
        qrows = slice(j * STEPS, (j + 1) * STEPS)
        krows = slice(j * STEPS, (j + 2) * STEPS)
        later = jnp.minimum(pl.program_id(2), 1) if j == 0 else 1
        q = q_ref[0, 0, qrows, :] * (HD ** -0.5)
        qs = jnp.concatenate([jnp.where(head == h, q, 0.0) for h in range(H_G)], axis=0).astype(BF16)
        s = lax.dot_general(qs, kk[krows], (((1,), (1,)), ((), ())), preferred_element_type=F32)
        s = s + bias_ref[later]
        mx = jnp.max(s, axis=-1, keepdims=True)
        p = jnp.exp(s - mx)
        den = jnp.sum(p, axis=-1, keepdims=True)
        pv = jnp.dot(p.astype(BF16), vv[krows], preferred_element_type=F32) / den
        lse = mx + jnp.log(den)
        o = jnp.zeros((STEPS, GW), F32)
        for h in range(H_G):
            rows = slice(h * STEPS, (h + 1) * STEPS)
            o = jnp.where(head == h, pv[rows], o)
            lse_ref[0, 0, qrows, h:h + 1] = lse[rows]
        o_ref[0, 0, qrows, :] = o


def _attn_prompt(a, dil):
    b, d, L, c = a.shape
    qb = next(k for k in range(ATT_QB, 0, -1) if L % (k * STEPS) == 0)
    rows = qb * STEPS
    assert d == dil and c == 3 * GW and L % STEPS == 0
    nb = L // rows

    def spec(col, prev):
        if prev:
            return pl.BlockSpec((1, 1, STEPS, GW), lambda bi, r, n: (bi, r, jnp.maximum(qb * n - 1, 0), col))
        return pl.BlockSpec((1, 1, rows, GW), lambda bi, r, n: (bi, r, n, col))

    return pl.pallas_call(
        _attn_prompt_body,
        grid=(b, dil, nb),
        in_specs=[spec(0, False), spec(1, True), spec(1, False), spec(2, True), spec(2, False),
                  pl.BlockSpec((2, H_G * STEPS, 2 * STEPS), lambda bi, r, n: (0, 0, 0))],
        out_specs=[pl.BlockSpec((1, 1, rows, GW), lambda bi, r, n: (bi, r, n, 0)),
                   pl.BlockSpec((1, 1, rows, LANES), lambda bi, r, n: (bi, r, n, 0))],
        out_shape=[jax.ShapeDtypeStruct((b, dil, L, GW), F32),
                   jax.ShapeDtypeStruct((b, dil, L, LANES), F32)],
        compiler_params=pltpu.CompilerParams(dimension_semantics=("parallel", "parallel", "arbitrary"),
                                             vmem_limit_bytes=VMEM_LIMIT),
        name=f"attn_prompt_d{dil}",
    )(a, a, a, a, a, _prompt_bias_table(dil))


def _sample_bias_tables(s_new):
    slopes = _alibi_slopes()[:, None, None]
    s = np.arange(SUBLANES)[:, None]
    live = s < s_new
    cached = []
    for dil in ATT_DILS:
        lw = dil * STEPS
        dist = lw + s - np.arange(lw)[None, :]
        ok = live & (dist % dil == 0) & (dist <= lw)
        cached.append(jnp.asarray(np.where(ok[None], -slopes * dist[None], NEG), F32))
    s2 = np.arange(SUBLANES)[None, :]
    dist = s - s2
    new = []
    for dil in ATT_DILS:
        ok = live & (s2 < s_new) & (dist >= 0) & (dist % dil == 0)
        new.append(np.where(ok[None], -slopes * dist[None], NEG))
    return cached, jnp.asarray(np.stack(new), F32)


def _attn_sample_body(new_ref, c0_ref, c1_ref, c2_ref, bw0_ref, bw1_ref, bw2_ref, bn_ref, o_ref, lse_ref, *, s_new):
    lse_ref[0] = jnp.zeros(lse_ref.shape[1:], F32)
    for gi, (cache, bw_ref) in enumerate(zip((c0_ref, c1_ref, c2_ref), (bw0_ref, bw1_ref, bw2_ref))):
        def heads(part):
            c0 = gi * 3 * GW + part * GW
            return jnp.stack([new_ref[0, :, c0 + h * HD:c0 + (h + 1) * HD] for h in range(H_G)])

        q, k_new, v_new = heads(0) * (HD ** -0.5), heads(1), heads(2)
        kt = cache[0, 0].astype(BF16)
        vt = cache[0, 1].astype(BF16)
        sw = jnp.einsum('hqe,hek->hqk', q.astype(BF16), kt, preferred_element_type=F32) + bw_ref[...]
        sn = jnp.einsum('hqe,hse->hqs', q, k_new, preferred_element_type=F32, precision=HIGHEST) + bn_ref[gi]
        mx = jnp.maximum(jnp.max(sw, axis=-1, keepdims=True), jnp.max(sn, axis=-1, keepdims=True))
        pw = jnp.exp(sw - mx)
        pn = jnp.exp(sn - mx)
        den = jnp.sum(pw, axis=-1, keepdims=True) + jnp.sum(pn, axis=-1, keepdims=True)
        o = (jnp.einsum('hqk,hek->hqe', pw.astype(BF16), vt, preferred_element_type=F32)
             + jnp.einsum('hqs,hse->hqe', pn, v_new, preferred_element_type=F32, precision=HIGHEST)) / den
        lse = mx + jnp.log(den)
        for h in range(H_G):
            o_ref[0, :, gi * GW + h * HD:gi * GW + (h + 1) * HD] = o[h, 0:s_new]
            lse_ref[0, :, gi * LANES + h:gi * LANES + h + 1] = lse[h, 0:s_new]


def _attn_sample(att, caches):
    b, s_new, c = att.shape
    assert s_new <= min(SUBLANES, ATT_DILS[1])
    new = jnp.pad(att, ((0, 0), (0, SUBLANES - s_new), (0, 0)))
    bufs, specs = [], []
    for dil, cache in zip(ATT_DILS, caches):
        lw = cache.shape[1]
        assert lw == dil * STEPS
        bufs.append(jnp.transpose(cache, (0, 2, 3, 4, 1)))
        specs.append(pl.BlockSpec((1, 2, H_G, HD, lw), lambda bi: (bi, 0, 0, 0, 0)))
    bias_cached, bias_new = _sample_bias_tables(s_new)
    ng = len(ATT_DILS)
    o, lse = pl.pallas_call(
        functools.partial(_attn_sample_body, s_new=s_new),
        grid=(b,),
        in_specs=[pl.BlockSpec((1, SUBLANES, c), lambda bi: (bi, 0, 0))] + specs
        + [pl.BlockSpec(bc.shape, lambda bi: (0, 0, 0)) for bc in bias_cached]
        + [pl.BlockSpec(bias_new.shape, lambda bi: (0, 0, 0, 0))],
        out_specs=[pl.BlockSpec((1, s_new, D_ATT), lambda bi: (bi, 0, 0)),
                   pl.BlockSpec((1, s_new, ng * LANES), lambda bi: (bi, 0, 0))],
        out_shape=[jax.ShapeDtypeStruct((b, s_new, D_ATT), F32),
                   jax.ShapeDtypeStruct((b, s_new, ng * LANES), F32)],
        compiler_params=pltpu.CompilerParams(dimension_semantics=("parallel",), vmem_limit_bytes=VMEM_LIMIT),
        name="attn_sample",
    )(new, *bufs, *bias_cached, bias_new)
    return o.reshape(b * s_new, D_ATT), lse.reshape(b * s_new, ng * LANES)


HIST = SUBLANES


def _bdot(a, b):
    return jnp.dot(a.astype(BF16), b.astype(BF16), preferred_element_type=F32)


def _delta_prep_body(u_ref, prev_ref, small_ref, cbuf_ref, cw_ref, alog_ref, dtb_ref,
                     uo_ref, wq_ref, ak_ref, eg_ref, ext, *, t_valid, cp):
    c = pl.program_id(1)
    rows = cp * CHUNK

    @pl.when(c == 0)
    def _():
        ext[0:HIST, :] = cbuf_ref[0]

    @pl.when(c > 0)
    def _():
        ext[0:HIST, :] = prev_ref[0]

    ext[HIST:HIST + rows, :] = u_ref[0]
    conv = ext[HIST - 3:HIST - 3 + rows, :] * cw_ref[0:1, :]
    for i in range(1, CONV_W):
        conv = conv + ext[HIST - 3 + i:HIST - 3 + i + rows, :] * cw_ref[i:i + 1, :]
    act = conv * _sigmoid(conv)

    sm = small_ref[0]
    bmat = _sigmoid(sm)
    xg = sm + dtb_ref[...]
    gmat = -jnp.exp(alog_ref[...]) * (jnp.maximum(xg, 0.0) + jnp.log1p(jnp.exp(-jnp.abs(xg))))
    if t_valid < rows:
        live = lax.broadcasted_iota(jnp.int32, (rows, 1), 0) < t_valid
        act = jnp.where(live, act, 0.0)
        bmat = jnp.where(live, bmat, 0.0)
        gmat = jnp.where(live, gmat, 0.0)

    ri = lax.broadcasted_iota(jnp.int32, (CHUNK, CHUNK), 0)
    ci = lax.broadcasted_iota(jnp.int32, (CHUNK, CHUNK), 1)
    tril = ri >= ci
    strict = ri > ci

    pairs = [(j, h) for j in range(cp) for h in range(H_D)]
    rsl = lambda j: slice(j * CHUNK, (j + 1) * CHUNK)
    gcs, gcts = [], []
    for j in range(cp):
        gc_j = jnp.dot(tril.astype(F32), gmat[rsl(j)], preferred_element_type=F32, precision=HIGHEST)
        gcs.append(gc_j)
        gcts.append(gc_j.T)
    q = jnp.stack([act[rsl(j), h * DK:(h + 1) * DK] for j, h in pairs])
    k = jnp.stack([act[rsl(j), D_DK + h * DK:D_DK + (h + 1) * DK] for j, h in pairs])
    v = jnp.stack([act[rsl(j), 2 * D_DK + h * DV:2 * D_DK + (h + 1) * DV] for j, h in pairs])
    beta = jnp.stack([bmat[rsl(j), h:h + 1] for j, h in pairs])
    gcol = jnp.stack([gcs[j][:, H_D + h:H_D + h + 1] for j, h in pairs])
    grow = jnp.stack([gcts[j][H_D + h:H_D + h + 1, :] for j, h in pairs])
    glast = jnp.stack([gcs[j][CHUNK - 1:CHUNK, H_D + h:H_D + h + 1] for j, h in pairs])

    qn = q * lax.rsqrt(jnp.sum(q * q, axis=-1, keepdims=True) + NORM_EPS) * (DK ** -0.5)
    kn = k * lax.rsqrt(jnp.sum(k * k, axis=-1, keepdims=True) + NORM_EPS)
    decay = jnp.where(tril, jnp.exp(jnp.where(tril, gcol - grow, 0.0)), 0.0)
    eg = jnp.exp(gcol)
    kb = kn * beta
    kn_b = kn.astype(BF16)
    a_kk = jnp.einsum('bik,bjk->bij', kb.astype(BF16), kn_b, preferred_element_type=F32)
    a_qk = jnp.einsum('bik,bjk->bij', qn.astype(BF16), kn_b, preferred_element_type=F32)
    pw = jnp.where(strict, -(a_kk * decay), 0.0)
    tm1 = pw
    for _ in range(int(math.log2(CHUNK)) - 1):
        pw_b = pw.astype(BF16)
        pw = jnp.einsum('bij,bjk->bik', pw_b, pw_b, preferred_element_type=F32)
        tm1 = tm1 + pw + jnp.einsum('bij,bjk->bik', tm1.astype(BF16), pw.astype(BF16), preferred_element_type=F32)
    vb = v * beta
    kbg = kb * eg
    tm1_b = tm1.astype(BF16)
    u = vb + jnp.einsum('bij,bjk->bik', tm1_b, vb.astype(BF16), preferred_element_type=F32)
    w = (kbg + jnp.einsum('bij,bjk->bik', tm1_b, kbg.astype(BF16), preferred_element_type=F32)).astype(BF16)
    qd = (qn * eg).astype(BF16)
    aqd = (a_qk * decay).astype(BF16)
    kd = kn * jnp.exp(glast - gcol)
    egl = jnp.exp(glast)
    for i, (j, h) in enumerate(pairs):
        uo_ref[0, rsl(j), h * DV:(h + 1) * DV] = u[i]
        wq_ref[0, j, h, 0:CHUNK, :] = w[i]
        wq_ref[0, j, h, CHUNK:2 * CHUNK, :] = qd[i]
        ak_ref[0, j, h, 0:CHUNK, :] = aqd[i]
        ak_ref[0, j, h, CHUNK:CHUNK + DK, :] = kd[i].T.astype(BF16)
        eg_ref[0, j, h:h + 1, :] = jnp.broadcast_to(egl[i], (1, LANES))
    for j in range(cp):
        eg_ref[0, j, H_D:SUBLANES, :] = jnp.zeros((SUBLANES - H_D, LANES), F32)


def _delta_rec_body(u_ref, wq_ref, ak_ref, eg_ref, z_ref, s0_ref, nw_ref, o_ref, sout_ref, S, *, bb, nc):
    c = pl.program_id(1)

    @pl.when(c == 0)
    def _():
        S[...] = s0_ref[...]

    for b in range(bb):
        for h in range(H_D):
            cs = slice(h * DV, (h + 1) * DV)
            s_h = S[b, h]
            x = jnp.dot(wq_ref[b, 0, h], s_h.astype(BF16), preferred_element_type=F32)
            vn = u_ref[b, :, cs] - x[0:CHUNK]
            y = jnp.dot(ak_ref[b, 0, h], vn.astype(BF16), preferred_element_type=F32)
            o = x[CHUNK:2 * CHUNK] + y[0:CHUNK]
            S[b, h] = s_h * eg_ref[b, 0, h:h + 1, :] + y[CHUNK:CHUNK + DK]
            od = o * lax.rsqrt(jnp.mean(o * o, axis=-1, keepdims=True) + NORM_EPS) * nw_ref[...]
            zh = z_ref[b, :, cs]
            o_ref[b, :, cs] = od * (zh * _sigmoid(zh))

    @pl.when(c == nc - 1)
    def _():
        sout_ref[...] = S[...]


def _lane_row(vals, offset):
    row = jnp.zeros((1, LANES), F32)
    return row.at[0, offset:offset + vals.shape[0]].set(vals.astype(F32))


def _delta_net(dqkv, small, z, conv_buf, s0, conv_w, a_log, dt_bias, delta_norm_w):
    b, t, _ = dqkv.shape
    tp = -(-t // CHUNK) * CHUNK
    nc = tp // CHUNK
    assert tp == t or nc == 1
    if tp != t:
        pad = ((0, 0), (0, tp - t), (0, 0))
        dqkv, small, z = jnp.pad(dqkv, pad), jnp.pad(small, pad), jnp.pad(z, pad)
    cbuf = jnp.pad(conv_buf, ((0, 0), (HIST - (CONV_W - 1), 0), (0, 0)))
    cw = jnp.pad(conv_w, ((0, SUBLANES - CONV_W), (0, 0)))
    cp = next(c for c in (4, 2, 1) if nc % c == 0)
    rows = cp * CHUNK
    full2 = lambda shape: pl.BlockSpec(shape, lambda bi, ci: (0,) * len(shape))
    u, wq, ak, eg = pl.pallas_call(
        functools.partial(_delta_prep_body, t_valid=t if nc == 1 else rows, cp=cp),
        grid=(b, nc // cp),
        in_specs=[pl.BlockSpec((1, rows, D_CONV), lambda bi, ci: (bi, ci, 0)),
                  pl.BlockSpec((1, HIST, D_CONV), lambda bi, ci: (bi, jnp.maximum(ci * (rows // HIST) - 1, 0), 0)),
                  pl.BlockSpec((1, rows, LANES), lambda bi, ci: (bi, ci, 0)),
                  pl.BlockSpec((1, HIST, D_CONV), lambda bi, ci: (bi, 0, 0)),
                  full2((SUBLANES, D_CONV)), full2((1, LANES)), full2((1, LANES))],
        out_specs=[pl.BlockSpec((1, rows, D_DV), lambda bi, ci: (bi, ci, 0)),
                   pl.BlockSpec((1, cp, H_D, 2 * CHUNK, DK), lambda bi, ci: (bi, ci, 0, 0, 0)),
                   pl.BlockSpec((1, cp, H_D, CHUNK + DK, CHUNK), lambda bi, ci: (bi, ci, 0, 0, 0)),
                   pl.BlockSpec((1, cp, SUBLANES, LANES), lambda bi, ci: (bi, ci, 0, 0))],
        out_shape=[jax.ShapeDtypeStruct((b, tp, D_DV), F32),
                   jax.ShapeDtypeStruct((b, nc, H_D, 2 * CHUNK, DK), BF16),
                   jax.ShapeDtypeStruct((b, nc, H_D, CHUNK + DK, CHUNK), BF16),
                   jax.ShapeDtypeStruct((b, nc, SUBLANES, LANES), F32)],
        scratch_shapes=[pltpu.VMEM((HIST + rows, D_CONV), F32)],
        compiler_params=pltpu.CompilerParams(dimension_semantics=("parallel", "parallel"),
                                             vmem_limit_bytes=VMEM_LIMIT),
        name="delta_prep",
    )(dqkv, dqkv, small, cbuf, cw, _lane_row(a_log, H_D), _lane_row(dt_bias, H_D))

    bb = next(c for c in (8, 4, 2, 1) if b % c == 0)
    o, s_new = pl.pallas_call(
        functools.partial(_delta_rec_body, bb=bb, nc=nc),
        grid=(b // bb, nc),
        in_specs=[pl.BlockSpec((bb, CHUNK, D_DV), lambda bi, ci: (bi, ci, 0)),
                  pl.BlockSpec((bb, 1, H_D, 2 * CHUNK, DK), lambda bi, ci: (bi, ci, 0, 0, 0)),
                  pl.BlockSpec((bb, 1, H_D, CHUNK + DK, CHUNK), lambda bi, ci: (bi, ci, 0, 0, 0)),
                  pl.BlockSpec((bb, 1, SUBLANES, LANES), lambda bi, ci: (bi, ci, 0, 0)),
                  pl.BlockSpec((bb, CHUNK, D_DV), lambda bi, ci: (bi, ci, 0)),
                  pl.BlockSpec((bb, H_D, DK, DV), lambda bi, ci: (bi, 0, 0, 0)),
                  full2((1, DV))],
        out_specs=[pl.BlockSpec((bb, CHUNK, D_DV), lambda bi, ci: (bi, ci, 0)),
                   pl.BlockSpec((bb, H_D, DK, DV), lambda bi, ci: (bi, 0, 0, 0))],
        out_shape=[jax.ShapeDtypeStruct((b, tp, D_DV), F32), jax.ShapeDtypeStruct((b, H_D, DK, DV), F32)],
        scratch_shapes=[pltpu.VMEM((bb, H_D, DK, DV), F32)],
        compiler_params=pltpu.CompilerParams(dimension_semantics=("parallel", "arbitrary"),
                                             vmem_limit_bytes=VMEM_LIMIT),
        name="delta_recurrence",
    )(u, wq, ak, eg, z, s0, delta_norm_w.reshape(1, DV).astype(F32))
    return o[:, :t].reshape(b * t, D_DV), s_new


def _lane_expand(cols, width):
    tm = cols[0].shape[0]
    seg = lax.broadcasted_iota(jnp.int32, (tm, len(cols) * width), 1) // width
    out = jnp.broadcast_to(cols[-1], seg.shape)
    for i in range(len(cols) - 2, -1, -1):
        out = jnp.where(seg == i, cols[i], out)
    return out


def _token_order(ref, scr):
    d, nc = ref.shape[1], ref.shape[3] // LANES
    if d == 1:
        return ref[0, 0]
    for r in range(d):
        for c in range(nc):
            scr[c, pl.ds(r, ref.shape[2], stride=d), :] = ref[0, r, :, c * LANES:(c + 1) * LANES]
    return jnp.concatenate([scr[c] for c in range(nc)], axis=-1)


def _mix_body(x_ref, gate_ref, o0_ref, o1_ref, o2_ref, l0_ref, l1_ref, l2_ref, od_ref,
              wa_ref, wd_ref, wo_ref, bg_ref, g1_ref, b1_ref, rw_ref, rb_ref,
              h_ref, ti_ref, tg_ref, cnt_ref, *scratch):
    o_scr, l_scr = scratch[:3], scratch[3:]
    lses = [_token_order(l, s) for l, s in zip((l0_ref, l1_ref, l2_ref), l_scr)]
    mx = jnp.maximum(jnp.maximum(lses[0], lses[1]), lses[2])
    es = [jnp.exp(l - mx) for l in lses]
    inv = 1.0 / (es[0] + es[1] + es[2])
    o_att = None
    for e_g, o_g, s in zip(es, (o0_ref, o1_ref, o2_ref), o_scr):
        wgt = e_g * inv
        term = _lane_expand([wgt[:, h:h + 1] for h in range(H_G)], HD) * _token_order(o_g, s)
        o_att = term if o_att is None else o_att + term
    ga = _sigmoid(gate_ref[:, :D_MODEL] + bg_ref[:, :D_MODEL])
    gd = _sigmoid(gate_ref[:, D_MODEL:] + bg_ref[:, D_MODEL:])
    merged = (ga * jnp.dot(o_att.astype(BF16), wa_ref[...], preferred_element_type=F32)
              + gd * jnp.dot(od_ref[...].astype(BF16), wd_ref[...], preferred_element_type=F32))
    mix = jnp.dot(merged.astype(BF16), wo_ref[...], preferred_element_type=F32)
    h = _layer_norm(DN_ALPHA * x_ref[...] + mix, g1_ref[...], b1_ref[...])
    _rows_to_tiles(h_ref, h)

    h_hi = h.astype(BF16)
    h_lo = (h - h_hi.astype(F32)).astype(BF16)
    part = jnp.dot(h_hi, rw_ref[...], preferred_element_type=F32)
    logits = (part[:, :LANES] + part[:, LANES:]
              + jnp.dot(h_lo, rw_ref[:, :LANES], preferred_element_type=F32) + rb_ref[...])
    lane = lax.broadcasted_iota(jnp.int32, logits.shape, 1)
    lane_f = lane.astype(F32)
    ti = jnp.zeros(logits.shape, F32)
    tv = jnp.zeros(logits.shape, F32)
    hits = jnp.zeros(logits.shape, F32)
    top = None
    for kk in range(TOP_K):
        m = jnp.max(logits, axis=-1, keepdims=True)
        idx = jnp.min(jnp.where(logits == m, lane_f, float(LANES)), axis=-1, keepdims=True)
        if top is None:
            top = m
        ti = jnp.where(lane == kk, idx, ti)
        tv = jnp.where(lane == kk, jnp.exp(m - top), tv)
        chosen = lane_f == idx
        hits = jnp.where(chosen, 1.0, hits)
        logits = jnp.where(chosen, -jnp.inf, logits)
    ti_ref[...] = ti.astype(jnp.int32)
    tg_ref[...] = tv / jnp.sum(tv, axis=-1, keepdims=True)

    @pl.when(pl.program_id(0) == 0)
    def _():
        cnt_ref[...] = jnp.zeros_like(cnt_ref)

    cnt_ref[...] += jnp.sum(hits, axis=0, keepdims=True)


def _mix_and_route(x2d, gates, o_groups, lse_groups, o_del, wa, wd, wo, b_gate, ln_g, ln_b, rw, rb, tm):
    n = x2d.shape[0]
    tpb = o_groups[0].shape[1] * o_groups[0].shape[2] // tm
    row = lambda c: pl.BlockSpec((tm, c), lambda i: (i, 0))
    full = lambda a: pl.BlockSpec(a.shape, lambda i: (0, 0))

    def res(a):
        d = a.shape[1]
        return pl.BlockSpec((1, d, tm // d, a.shape[3]), lambda i: (i // tpb, 0, i % tpb, 0))

    consts = (wa, wd, wo, b_gate, ln_g, ln_b, rw, rb)
    return pl.pallas_call(
        _mix_body,
        grid=(n // tm,),
        in_specs=[row(D_MODEL), row(2 * D_MODEL)] + [res(a) for a in o_groups] + [res(a) for a in lse_groups]
        + [row(D_DV)] + [full(a) for a in consts],
        out_specs=[pl.BlockSpec((tm * TOK_ROWS, LANES), lambda i: (i, 0)), row(LANES), row(LANES),
                   pl.BlockSpec((1, LANES), lambda i: (0, 0))],
        out_shape=[jax.ShapeDtypeStruct((n * TOK_ROWS, LANES), F32), jax.ShapeDtypeStruct((n, LANES), jnp.int32),
                   jax.ShapeDtypeStruct((n, LANES), F32), jax.ShapeDtypeStruct((1, LANES), F32)],
        scratch_shapes=[pltpu.VMEM((GW // LANES, tm, LANES), F32)] * len(o_groups)
        + [pltpu.VMEM((1, tm, LANES), F32)] * len(lse_groups),
        compiler_params=pltpu.CompilerParams(dimension_semantics=("arbitrary",), vmem_limit_bytes=VMEM_LIMIT),
        name="mix_and_route",
    )(x2d, gates, *o_groups, *lse_groups, o_del, *consts)


def _rank_body(ti_ref, base_ref, dest_ref, carry):
    i = pl.program_id(0)

    @pl.when(i == 0)
    def _():
        carry[...] = base_ref[...]

    ti = ti_ref[...]
    tm = ti.shape[0]
    lane = lax.broadcasted_iota(jnp.int32, ti.shape, 1)
    sel = [lane == ti[:, kk:kk + 1] for kk in range(TOP_K)]
    hit = sel[0]
    for kk in range(1, TOP_K):
        hit = hit | sel[kk]
    cnt = hit.astype(F32)
    ri = lax.broadcasted_iota(jnp.int32, (tm, tm), 0)
    ci = lax.broadcasted_iota(jnp.int32, (tm, tm), 1)
    incl = jnp.dot((ri >= ci).astype(BF16), cnt.astype(BF16), preferred_element_type=F32)
    before = incl - cnt + carry[...]
    slot = jnp.zeros(ti.shape, F32)
    for kk in range(TOP_K):
        r = jnp.sum(jnp.where(sel[kk], before, 0.0), axis=-1, keepdims=True)
        slot = jnp.where(lane == kk, r, slot)
    dest_ref[...] = slot.T[0:SUBLANES].astype(jnp.int32)
    carry[...] = carry[...] + incl[tm - 1:tm, :]


def _route_slots(top_i, base, tm):
    n = top_i.shape[0]
    return pl.pallas_call(
        _rank_body,
        grid=(n // tm,),
        in_specs=[pl.BlockSpec((tm, LANES), lambda i: (i, 0)), pl.BlockSpec((1, LANES), lambda i: (0, 0))],
        out_specs=pl.BlockSpec((SUBLANES, tm), lambda i: (i, 0)),
        out_shape=jax.ShapeDtypeStruct((n // tm * SUBLANES, tm), jnp.int32),
        scratch_shapes=[pltpu.VMEM((1, LANES), F32)],
        compiler_params=pltpu.CompilerParams(dimension_semantics=("arbitrary",), vmem_limit_bytes=VMEM_LIMIT),
        name="route_slots",
    )(top_i, base)


def _rows_to_tiles(ref, val):
    m = val.shape[0]
    for s in range(TOK_ROWS):
        ref[pl.ds(s, m, stride=TOK_ROWS), :] = val[:, s * LANES:(s + 1) * LANES]


def _tiles_to_rows(ref, m):
    return jnp.concatenate([ref[pl.ds(s, m, stride=TOK_ROWS), :] for s in range(TOK_ROWS)], axis=-1)


def _dispatch_body(pend_ref, padded_ref, dest_ref, h_ref, *refs, max_tail):
    tm = h_ref.shape[0] // TOK_ROWS
    blk = MOE_BM * TOK_ROWS
    first_group = len(refs) == 4
    if first_group:
        xs_out, zero, sem, zsem = refs
    else:
        _, xs_out, sem = refs

    def zero_fill():
        zero[...] = jnp.zeros_like(zero)

        def last_block(e):
            start = pl.multiple_of((pend_ref[e] - MOE_BM) * TOK_ROWS, blk)
            return pltpu.make_async_copy(zero, xs_out.at[pl.ds(start, blk)], zsem)

        def tail_block(j):
            start = pl.multiple_of((pend_ref[N_EXP - 1] + j * MOE_BM) * TOK_ROWS, blk)
            return pltpu.make_async_copy(zero, xs_out.at[pl.ds(start, blk)], zsem)

        n_slot = xs_out.shape[0] // TOK_ROWS
        for e in range(N_EXP):
            @pl.when(padded_ref[e] > 0)
            def _():
                last_block(e).start()
        for j in range(max_tail):
            @pl.when(pend_ref[N_EXP - 1] + j * MOE_BM < n_slot)
            def _():
                tail_block(j).start()
        for e in range(N_EXP):
            @pl.when(padded_ref[e] > 0)
            def _():
                last_block(e).wait()
        for j in range(max_tail):
            @pl.when(pend_ref[N_EXP - 1] + j * MOE_BM < n_slot)
            def _():
                tail_block(j).wait()

    if first_group:
        pl.when(pl.program_id(0) == 0)(zero_fill)

    def issue(t, carry):
        src = h_ref.at[pl.ds(pl.multiple_of(t * TOK_ROWS, TOK_ROWS), TOK_ROWS)]
        for kk in range(TOP_K):
            d = pl.multiple_of(dest_ref[kk * tm + t] * TOK_ROWS, TOK_ROWS)
            pltpu.make_async_copy(src, xs_out.at[pl.ds(d, TOK_ROWS)], sem).start(priority=kk % 2)
        return carry

    lax.fori_loop(0, tm, issue, 0, unroll=8)
    for kk in range(TOP_K):
        pltpu.make_async_copy(h_ref, xs_out.at[pl.ds(0, tm * TOK_ROWS)], sem).wait()


def _dispatch(h_tiles, dest_flat, pad_end, padded, n_slot, tm, xs_prev=None):
    n = h_tiles.shape[0] // TOK_ROWS
    first_group = xs_prev is None
    in_specs = [pl.BlockSpec((SUBLANES * tm,), lambda i, pe, pd: (i,), memory_space=pltpu.SMEM),
                pl.BlockSpec((tm * TOK_ROWS, LANES), lambda i, pe, pd: (i, 0))]
    scratch = [pltpu.SemaphoreType.DMA(())]
    if first_group:
        scratch = [pltpu.VMEM((MOE_BM * TOK_ROWS, LANES), F32)] + scratch + [pltpu.SemaphoreType.DMA(())]
    else:
        in_specs.append(pl.BlockSpec(memory_space=pl.ANY))
    grid_spec = pltpu.PrefetchScalarGridSpec(
        num_scalar_prefetch=2, grid=(n // tm,), in_specs=in_specs,
        out_specs=pl.BlockSpec(memory_space=pl.ANY), scratch_shapes=scratch)
    max_tail = n_slot // MOE_BM - (n * TOP_K) // MOE_BM
    return pl.pallas_call(
        functools.partial(_dispatch_body, max_tail=max_tail),
        grid_spec=grid_spec,
        out_shape=jax.ShapeDtypeStruct((n_slot * TOK_ROWS, LANES), F32),
        input_output_aliases={} if first_group else {4: 0},
        compiler_params=pltpu.CompilerParams(dimension_semantics=("arbitrary",), vmem_limit_bytes=VMEM_LIMIT,
                                             has_side_effects=True, disable_bounds_checks=True),
        name="moe_dispatch",
    )(pad_end, padded, dest_flat, h_tiles, *(() if first_group else (xs_prev,)))


def _expert_body(be_ref, nu_ref, x_ref, wgu_ref, bgu_ref, wd_ref, bd_ref, o_ref, wgu_bf, wd_bf):
    i = pl.program_id(0)
    e = be_ref[i]
    prev = be_ref[jnp.maximum(i - 1, 0)]

    @pl.when((i == 0) | (e != prev))
    def _():
        wgu_bf[...] = wgu_ref[0].astype(BF16)
        wd_bf[...] = wd_ref[0].astype(BF16)

    @pl.when(i < nu_ref[0])
    def _():
        x = _tiles_to_rows(x_ref, MOE_BM)
        gu = jnp.dot(x.astype(BF16), wgu_bf[...], preferred_element_type=F32) + bgu_ref[0]
        gt = jnp.minimum(gu[:, :D_FF], SWIGLU_LIMIT)
        up = jnp.clip(gu[:, D_FF:], -SWIGLU_LIMIT, SWIGLU_LIMIT)
        act = (up + 1.0) * gt * _sigmoid(SWIGLU_ALPHA * gt)
        _rows_to_tiles(o_ref, jnp.dot(act.astype(BF16), wd_bf[...], preferred_element_type=F32) + bd_ref[0])

    @pl.when(i >= nu_ref[0])
    def _():
        o_ref[...] = jnp.zeros_like(o_ref)


def _experts(xs, blk_exp, n_used, w_gu, b_gu, w_down, b_down):
    n_slot = xs.shape[0] // TOK_ROWS
    n_blk = n_slot // MOE_BM
    rows = lambda i, be, nu: (jnp.minimum(i, nu[0] - 1), 0)
    grid_spec = pltpu.PrefetchScalarGridSpec(
        num_scalar_prefetch=2,
        grid=(n_blk,),
        in_specs=[pl.BlockSpec((MOE_BM * TOK_ROWS, LANES), rows),
                  pl.BlockSpec((1, D_MODEL, 2 * D_FF), lambda i, be, nu: (be[i], 0, 0)),
                  pl.BlockSpec((1, 1, 2 * D_FF), lambda i, be, nu: (be[i], 0, 0)),
                  pl.BlockSpec((1, D_FF, D_MODEL), lambda i, be, nu: (be[i], 0, 0)),
                  pl.BlockSpec((1, 1, D_MODEL), lambda i, be, nu: (be[i], 0, 0))],
        out_specs=pl.BlockSpec((MOE_BM * TOK_ROWS, LANES), lambda i, be, nu: (i, 0)),
        scratch_shapes=[pltpu.VMEM((D_MODEL, 2 * D_FF), BF16), pltpu.VMEM((D_FF, D_MODEL), BF16)],
    )
    return pl.pallas_call(
        _expert_body,
        grid_spec=grid_spec,
        out_shape=jax.ShapeDtypeStruct((n_slot * TOK_ROWS, LANES), F32),
        compiler_params=pltpu.CompilerParams(dimension_semantics=("arbitrary",), vmem_limit_bytes=VMEM_LIMIT),
        name="moe_experts",
    )(blk_exp, n_used, xs, w_gu, b_gu.reshape(N_EXP, 1, 2 * D_FF), w_down, b_down.reshape(N_EXP, 1, D_MODEL))


def _combine_body(dest_ref, dest_next_ref, gate_ref, h_ref, g_ref, b_ref, ys_ref, y_ref, buf, sem):
    tm = h_ref.shape[0] // TOK_ROWS
    i = pl.program_id(0)
    slot = i % 2

    def start_tile(idx_ref, s):
        def issue(t, carry):
            row = pl.multiple_of(t * TOK_ROWS, TOK_ROWS)
            for kk in range(TOP_K):
                d = pl.multiple_of(idx_ref[kk * tm + t] * TOK_ROWS, TOK_ROWS)
                pltpu.make_async_copy(ys_ref.at[pl.ds(d, TOK_ROWS)], buf.at[s, kk, pl.ds(row, TOK_ROWS)],
                                      sem.at[s]).start(priority=kk % 2)
            return carry

        lax.fori_loop(0, tm, issue, 0, unroll=8)

    @pl.when(i == 0)
    def _():
        start_tile(dest_ref, 0)

    @pl.when(i + 1 < pl.num_programs(0))
    def _():
        start_tile(dest_next_ref, 1 - slot)

    for kk in range(TOP_K):
        pltpu.make_async_copy(ys_ref.at[pl.ds(0, tm * TOK_ROWS)], buf.at[slot, kk], sem.at[slot]).wait()
    gate = gate_ref[...]
    moe = gate[:, 0:1] * _tiles_to_rows(buf.at[slot, 0], tm)
    for kk in range(1, TOP_K):
        moe = moe + gate[:, kk:kk + 1] * _tiles_to_rows(buf.at[slot, kk], tm)
    y_ref[...] = _layer_norm(DN_ALPHA * _tiles_to_rows(h_ref, tm) + moe, g_ref[...], b_ref[...])


def _combine(ys, dest_flat, gate, h_tiles, ln_g, ln_b, tm):
    n = h_tiles.shape[0] // TOK_ROWS
    return pl.pallas_call(
        _combine_body,
        grid=(n // tm,),
        in_specs=[pl.BlockSpec((SUBLANES * tm,), lambda i: (i,), memory_space=pltpu.SMEM),
                  pl.BlockSpec((SUBLANES * tm,), lambda i: (jnp.minimum(i + 1, n // tm - 1),), memory_space=pltpu.SMEM),
                  pl.BlockSpec((tm, LANES), lambda i: (i, 0)),
                  pl.BlockSpec((tm * TOK_ROWS, LANES), lambda i: (i, 0)),
                  pl.BlockSpec((1, D_MODEL), lambda i: (0, 0)),
                  pl.BlockSpec((1, D_MODEL), lambda i: (0, 0)),
                  pl.BlockSpec(memory_space=pl.ANY)],
        out_specs=pl.BlockSpec((tm, D_MODEL), lambda i: (i, 0)),
        out_shape=jax.ShapeDtypeStruct((n, D_MODEL), F32),
        scratch_shapes=[pltpu.VMEM((2, TOP_K, tm * TOK_ROWS, LANES), F32), pltpu.SemaphoreType.DMA((2,))],
        compiler_params=pltpu.CompilerParams(dimension_semantics=("arbitrary",), vmem_limit_bytes=VMEM_LIMIT,
                                             disable_bounds_checks=True),
        name="moe_combine",
    )(dest_flat, dest_flat, gate, h_tiles, ln_g, ln_b, ys)


def _moe(groups, w_gu, b_gu, w_down, b_down, ln_g, ln_b):
    counts = [g[3][0, :N_EXP].astype(jnp.int32) for g in groups]
    total = sum(counts)
    padded = (total + MOE_BM - 1) // MOE_BM * MOE_BM
    pad_end = jnp.cumsum(padded).astype(jnp.int32)
    n_assign = sum(g[1].shape[0] for g in groups) * TOP_K
    n_blk = -(-(n_assign + N_EXP * (MOE_BM - 1)) // MOE_BM)
    blk_row0 = jnp.arange(n_blk, dtype=jnp.int32) * MOE_BM
    blk_exp = jnp.minimum(jnp.sum((pad_end[None, :] <= blk_row0[:, None]).astype(jnp.int32), axis=1), N_EXP - 1)
    n_used = pad_end[-1:] // MOE_BM
    base = pad_end - padded
    xs, dests = None, []
    for (h, top_i, _, _, tm), cnt in zip(groups, counts):
        base_row = jnp.zeros((1, LANES), F32).at[0, :N_EXP].set(base.astype(F32))
        dest = _route_slots(top_i, base_row, tm).reshape(-1)
        dests.append(dest)
        xs = _dispatch(h, dest, pad_end, padded, n_blk * MOE_BM, tm, xs)
        base = base + cnt
    ys = _experts(xs, blk_exp, n_used, w_gu, b_gu, w_down, b_down)
    return [_combine(ys, dest, top_g, h, ln_g, ln_b, tm) for (h, _, top_g, _, tm), dest in zip(groups, dests)]


def _pad_cols(a, width, fill=0.0):
    return jnp.pad(a, ((0, 0), (0, width - a.shape[1])), constant_values=fill)


def _token_mixers_and_route(x, caches, conv_buf, s0, w_in, b_gate, conv_w, a_log, dt_bias, delta_norm_w,
                            w_branch_attn, w_branch_delta, w_out, ln1_g, ln1_b, router_w, router_b):
    b, t, _ = x.shape
    n = b * t
    tm = min(TOK_TM, n)
    x2d = x.reshape(n, D_MODEL)

    c_att, c_dz, c_small = 3 * D_ATT, 3 * D_ATT + D_CONV, 3 * D_ATT + D_CONV + D_DV
    c_gate = c_small + 2 * H_D
    ng = len(ATT_DILS)
    w_groups = [jnp.concatenate([w_in[:, part * D_ATT + gi * GW:part * D_ATT + (gi + 1) * GW] for part in range(3)],
                                axis=1) for gi in range(ng)]
    ws = w_groups + [w_in[:, c_att:c_dz], w_in[:, c_dz:c_small],
                     _pad_cols(w_in[:, c_small:c_gate], LANES), w_in[:, c_gate:]]
    dils = (ATT_DILS if caches is None else (1,) * ng) + (1, 1, 1, 1)
    outs = _in_projection(x2d, [w.astype(BF16) for w in ws], dils, t, min(IN_TM, n))
    att_groups, (dqkv, z, small, gates) = outs[:ng], outs[ng:]
    dqkv3 = dqkv.reshape(b, t, D_CONV)

    kv_new = []
    if caches is None:
        o_groups, lse_groups = [], []
        for dil, a in zip(ATT_DILS, att_groups):
            a = a.reshape(b, dil, t // dil, 3 * GW)
            o_g, lse_g = _attn_prompt(a, dil)
            o_groups.append(o_g)
            lse_groups.append(lse_g)
            win = min(dil * STEPS, t)
            last = a[:, :, (t - win) // dil:, GW:]
            last = jnp.transpose(last, (0, 2, 1, 3)).reshape(b, win, 2, H_G, HD)
            kv_new.append(last)
        conv_buf = jnp.zeros((b, CONV_W - 1, D_CONV), F32)
        s0 = jnp.zeros((b, H_D, DK, DV), F32)
    else:
        att3 = jnp.concatenate(att_groups, axis=1).reshape(b, t, 3 * D_ATT)
        o_all, lse_all = _attn_sample(att3, caches)
        o_groups = [o_all[:, gi * GW:(gi + 1) * GW].reshape(1, 1, n, GW) for gi in range(ng)]
        lse_groups = [lse_all[:, gi * LANES:(gi + 1) * LANES].reshape(1, 1, n, LANES) for gi in range(ng)]
        for a in att_groups:
            kv_new.append(a[:, GW:].reshape(b, t, 2, H_G, HD))

    o_del, s_new = _delta_net(dqkv3, small.reshape(b, t, LANES), z.reshape(b, t, D_DV), conv_buf, s0,
                              conv_w, a_log, dt_bias, delta_norm_w)
    conv_new = jnp.concatenate([conv_buf, dqkv3], axis=1)[:, -(CONV_W - 1):] if t < CONV_W - 1 \
        else dqkv3[:, t - (CONV_W - 1):]

    rw = _pad_cols(router_w, LANES)
    rw_hi = rw.astype(BF16)
    rw = jnp.concatenate([rw_hi, (rw - rw_hi.astype(F32)).astype(BF16)], axis=1)
    rb = _pad_cols(router_b.reshape(1, N_EXP), LANES, fill=NEG)
    h, top_i, top_g, counts = _mix_and_route(
        x2d, gates, o_groups, lse_groups, o_del,
        w_branch_attn.astype(BF16), w_branch_delta.astype(BF16), w_out.astype(BF16),
        b_gate.reshape(1, 2 * D_MODEL), ln1_g.reshape(1, D_MODEL), ln1_b.reshape(1, D_MODEL), rw, rb, tm)
    return (h, top_i, top_g, counts, tm), kv_new, conv_new, s_new


def kernel(x_prompt, x_sample, cache_kv_w128, cache_kv_w512, cache_kv_w2048, state_conv, state_delta,
           w_in, b_gate, conv_w, a_log, dt_bias, delta_norm_w, w_branch_attn, w_branch_delta, w_out,
           ln1_g, ln1_b, router_w, router_b, w_gu, b_gu, w_down, b_down, ln2_g, ln2_b):
    depth = w_in.shape[0]
    assert depth == 1
    l = 0
    lw = (w_in[l], b_gate[l], conv_w[l], a_log[l], dt_bias[l], delta_norm_w[l], w_branch_attn[l],
          w_branch_delta[l], w_out[l], ln1_g[l], ln1_b[l], router_w[l], router_b[l], w_gu[l], b_gu[l],
          w_down[l], b_down[l], ln2_g[l], ln2_b[l])
    mixers = lw[:13]
    routed_p, kv_p, cv_p, s_p = _token_mixers_and_route(x_prompt, None, None, None, *mixers)
    routed_s, kv_s, cv_s, s_s = _token_mixers_and_route(
        x_sample, (cache_kv_w128[l], cache_kv_w512[l], cache_kv_w2048[l]), state_conv[l], state_delta[l], *mixers)
    w_gu_l, b_gu_l, w_down_l, b_down_l, ln2_g_l, ln2_b_l = lw[13:]
    yp, ys = _moe([routed_p, routed_s], w_gu_l, b_gu_l, w_down_l, b_down_l,
                  ln2_g_l.reshape(1, D_MODEL), ln2_b_l.reshape(1, D_MODEL))
    yp, ys = yp.reshape(x_prompt.shape), ys.reshape(x_sample.shape)
    stk = lambda a: a[None]
    return (yp, ys, stk(kv_p[0]), stk(kv_p[1]), stk(kv_p[2]), stk(cv_p), stk(s_p),
            stk(kv_s[0]), stk(kv_s[1]), stk(kv_s[2]), stk(cv_s), stk(s_s))
```

```python
import functools
import math

import numpy as np
import jax
import jax.numpy as jnp
from jax import lax
from jax.experimental import pallas as pl
from jax.experimental.pallas import tpu as pltpu

F32 = jnp.float32
BF16 = jnp.bfloat16
HIGHEST = lax.Precision.HIGHEST

D_MODEL = 1024
ATT_DILS = (1, 4, 16)
STEPS = 128
H_G = 4
HD = 64
GW = H_G * HD
D_ATT = len(ATT_DILS) * GW
H_D = 4
DK = 128
DV = 128
D_DK = H_D * DK
D_DV = H_D * DV
D_CONV = 2 * D_DK + D_DV
CONV_W = 4
CHUNK = 64
N_EXP = 32
TOP_K = 4
D_FF = D_MODEL
SWIGLU_LIMIT = 7.0
SWIGLU_ALPHA = 1.702
DN_ALPHA = 2.0 ** 0.25
LN_EPS = 1e-5
NORM_EPS = 1e-6
NEG = -1e30

LANES = 128
SUBLANES = 8
VMEM_LIMIT = 56 * 1024 * 1024
MOE_BM = 512
IN_TM = 512
TOK_TM = 512
TOK_ROWS = D_MODEL // LANES
assert TOK_ROWS == SUBLANES


def _sigmoid(x):
    return 1.0 / (1.0 + jnp.exp(-x))


def _layer_norm(v, g, b):
    mu = jnp.mean(v, axis=-1, keepdims=True)
    d = v - mu
    var = jnp.mean(d * d, axis=-1, keepdims=True)
    return d * lax.rsqrt(var + LN_EPS) * g + b


def _alibi_slopes():
    return 2.0 ** (-8.0 * np.arange(1, H_G + 1, dtype=np.float64) / H_G)


def _inproj_body(x_ref, *refs, dils):
    nw = len(dils)
    scr = refs[2 * nw] if len(refs) > 2 * nw else None
    x = x_ref[...].astype(BF16)
    for w_ref, o_ref, d in zip(refs[:nw], refs[nw:2 * nw], dils):
        y = jnp.dot(x, w_ref[...], preferred_element_type=F32)
        if d == 1:
            o_ref[...] = y
        else:
            for c in range(y.shape[1] // LANES):
                scr[c] = y[:, c * LANES:(c + 1) * LANES]
            for r in range(d):
                for c in range(y.shape[1] // LANES):
                    o_ref[0, r, :, c * LANES:(c + 1) * LANES] = scr[c, pl.ds(r, y.shape[0] // d, stride=d), :]


def _in_projection(x2d, ws, dils, t, tm):
    n = x2d.shape[0]
    tpb = t // tm
    out_specs, out_shape = [], []
    for w, d in zip(ws, dils):
        c = w.shape[1]
        if d == 1:
            out_specs.append(pl.BlockSpec((tm, c), lambda i: (i, 0)))
            out_shape.append(jax.ShapeDtypeStruct((n, c), F32))
        else:
            assert tm % (d * SUBLANES) == 0 and t % tm == 0
            out_specs.append(pl.BlockSpec((1, d, tm // d, c), lambda i: (i // tpb, 0, i % tpb, 0)))
            out_shape.append(jax.ShapeDtypeStruct((n // t, d, t // d, c), F32))
    wide = max([w.shape[1] for w, d in zip(ws, dils) if d > 1], default=0)
    return pl.pallas_call(
        functools.partial(_inproj_body, dils=tuple(dils)),
        grid=(n // tm,),
        in_specs=[pl.BlockSpec((tm, D_MODEL), lambda i: (i, 0))]
        + [pl.BlockSpec((D_MODEL, w.shape[1]), lambda i: (0, 0), pipeline_mode=pl.Buffered(1)) for w in ws],
        out_specs=out_specs,
        out_shape=out_shape,
        scratch_shapes=[pltpu.VMEM((wide // LANES, tm, LANES), F32)] if wide else [],
        compiler_params=pltpu.CompilerParams(dimension_semantics=("parallel",), vmem_limit_bytes=VMEM_LIMIT),
        name="in_projection",
    )(x2d, *ws)


def _prompt_bias_table(dil):
    qi = np.arange(STEPS)[:, None]
    kj = np.arange(2 * STEPS)[None, :]
    steps = qi + STEPS - kj
    valid = (steps >= 0) & (steps <= STEPS)
    slopes = _alibi_slopes()
    bias = -slopes[:, None, None] * (dil * steps)[None].astype(np.float64)
    later = np.where(valid[None], bias, NEG)
    first = np.where((valid & (kj >= STEPS))[None], bias, NEG)
    return jnp.asarray(np.stack([first, later]).reshape(2, H_G * STEPS, 2 * STEPS), F32)


ATT_QB = 4


def _attn_prompt_body(q_ref, kp_ref, kc_ref, vp_ref, vc_ref, bias_ref, o_ref, lse_ref):
    head = lax.broadcasted_iota(jnp.int32, (STEPS, GW), 1) // HD
    for r in range(q_ref.shape[1]):
        kk = jnp.concatenate([kp_ref[0, r], kc_ref[0, r]], axis=0).astype(BF16)
        vv = jnp.concatenate([vp_ref[0, r], vc_ref[0, r]], axis=0).astype(BF16)
        lse_ref[0, r] = jnp.zeros(lse_ref.shape[2:], F32)
        for j in range(o_ref.shape[2] // STEPS):
            qrows = slice(j * STEPS, (j + 1) * STEPS)
            krows = slice(j * STEPS, (j + 2) * STEPS)
            later = jnp.minimum(pl.program_id(2), 1) if j == 0 else 1
            q = q_ref[0, r, qrows, :] * (HD ** -0.5)
            qs = jnp.concatenate([jnp.where(head == h, q, 0.0) for h in range(H_G)], axis=0).astype(BF16)
            s = lax.dot_general(qs, kk[krows], (((1,), (1,)), ((), ())), preferred_element_type=F32)
            s = s + bias_ref[later]
            mx = jnp.max(s, axis=-1, keepdims=True)
            p = jnp.exp(s - mx)
            den = jnp.sum(p, axis=-1, keepdims=True)
            pv = jnp.dot(p.astype(BF16), vv[krows], preferred_element_type=F32) / den
            lse = mx + jnp.log(den)
            o = jnp.zeros((STEPS, GW), F32)
            for h in range(H_G):
                rows = slice(h * STEPS, (h + 1) * STEPS)
                o = jnp.where(head == h, pv[rows], o)
                lse_ref[0, r, qrows, h:h + 1] = lse[rows]
            o_ref[0, r, qrows, :] = o


def _attn_prompt(a, dil):
    b, d, L, c = a.shape
    assert d == dil and c == 3 * GW and L % STEPS == 0
    qb = next(k for k in range(ATT_QB, 0, -1) if L % (k * STEPS) == 0)
    rb = next(k for k in range(ATT_QB // qb, 0, -1) if dil % k == 0)
    rows = qb * STEPS
    nb = L // rows

    def spec(col, prev):
        if prev:
            return pl.BlockSpec((1, rb, STEPS, GW), lambda bi, r, n: (bi, r, jnp.maximum(qb * n - 1, 0), col))
        return pl.BlockSpec((1, rb, rows, GW), lambda bi, r, n: (bi, r, n, col))

    return pl.pallas_call(
        _attn_prompt_body,
        grid=(b, dil // rb, nb),
        in_specs=[spec(0, False), spec(1, True), spec(1, False), spec(2, True), spec(2, False),
                  pl.BlockSpec((2, H_G * STEPS, 2 * STEPS), lambda bi, r, n: (0, 0, 0))],
        out_specs=[pl.BlockSpec((1, rb, rows, GW), lambda bi, r, n: (bi, r, n, 0)),
                   pl.BlockSpec((1, rb, rows, LANES), lambda bi, r, n: (bi, r, n, 0))],
        out_shape=[jax.ShapeDtypeStruct((b, dil, L, GW), F32),
                   jax.ShapeDtypeStruct((b, dil, L, LANES), F32)],
        compiler_params=pltpu.CompilerParams(dimension_semantics=("parallel", "parallel", "arbitrary"),
                                             vmem_limit_bytes=VMEM_LIMIT),
        name=f"attn_prompt_d{dil}",
    )(a, a, a, a, a, _prompt_bias_table(dil))


def _sample_bias_tables(s_new):
    slopes = _alibi_slopes()[:, None, None]
    s = np.arange(SUBLANES)[:, None]
    live = s < s_new
    cached = []
    for dil in ATT_DILS:
        lw = dil * STEPS
        dist = lw + s - np.arange(lw)[None, :]
        ok = live & (dist % dil == 0) & (dist <= lw)
        cached.append(jnp.asarray(np.where(ok[None], -slopes * dist[None], NEG), F32))
    s2 = np.arange(SUBLANES)[None, :]
    dist = s - s2
    new = []
    for dil in ATT_DILS:
        ok = live & (s2 < s_new) & (dist >= 0) & (dist % dil == 0)
        new.append(np.where(ok[None], -slopes * dist[None], NEG))
    return cached, jnp.asarray(np.stack(new), F32)


def _attn_sample_body(new_ref, c0_ref, c1_ref, c2_ref, bw0_ref, bw1_ref, bw2_ref, bn_ref, o_ref, lse_ref, *, s_new):
    lse_ref[0] = jnp.zeros(lse_ref.shape[1:], F32)
    for gi, (cache, bw_ref) in enumerate(zip((c0_ref, c1_ref, c2_ref), (bw0_ref, bw1_ref, bw2_ref))):
        def heads(part):
            c0 = gi * 3 * GW + part * GW
            return jnp.stack([new_ref[0, :, c0 + h * HD:c0 + (h + 1) * HD] for h in range(H_G)])

        q, k_new, v_new = heads(0) * (HD ** -0.5), heads(1), heads(2)
        kt = cache[0, 0].astype(BF16)
        vt = cache[0, 1].astype(BF16)
        sw = jnp.einsum('hqe,hek->hqk', q.astype(BF16), kt, preferred_element_type=F32) + bw_ref[...]
        sn = jnp.einsum('hqe,hse->hqs', q, k_new, preferred_element_type=F32, precision=HIGHEST) + bn_ref[gi]
        mx = jnp.maximum(jnp.max(sw, axis=-1, keepdims=True), jnp.max(sn, axis=-1, keepdims=True))
        pw = jnp.exp(sw - mx)
        pn = jnp.exp(sn - mx)
        den = jnp.sum(pw, axis=-1, keepdims=True) + jnp.sum(pn, axis=-1, keepdims=True)
        o = (jnp.einsum('hqk,hek->hqe', pw.astype(BF16), vt, preferred_element_type=F32)
             + jnp.einsum('hqs,hse->hqe', pn, v_new, preferred_element_type=F32, precision=HIGHEST)) / den
        lse = mx + jnp.log(den)
        for h in range(H_G):
            o_ref[0, :, gi * GW + h * HD:gi * GW + (h + 1) * HD] = o[h, 0:s_new]
            lse_ref[0, :, gi * LANES + h:gi * LANES + h + 1] = lse[h, 0:s_new]


def _attn_sample(att, caches):
    b, s_new, c = att.shape
    assert s_new <= min(SUBLANES, ATT_DILS[1])
    new = jnp.pad(att, ((0, 0), (0, SUBLANES - s_new), (0, 0)))
    bufs, specs = [], []
    for dil, cache in zip(ATT_DILS, caches):
        lw = cache.shape[1]
        assert lw == dil * STEPS
        bufs.append(jnp.transpose(cache, (0, 2, 3, 4, 1)))
        specs.append(pl.BlockSpec((1, 2, H_G, HD, lw), lambda bi: (bi, 0, 0, 0, 0)))
    bias_cached, bias_new = _sample_bias_tables(s_new)
    ng = len(ATT_DILS)
    o, lse = pl.pallas_call(
        functools.partial(_attn_sample_body, s_new=s_new),
        grid=(b,),
        in_specs=[pl.BlockSpec((1, SUBLANES, c), lambda bi: (bi, 0, 0))] + specs
        + [pl.BlockSpec(bc.shape, lambda bi: (0, 0, 0)) for bc in bias_cached]
        + [pl.BlockSpec(bias_new.shape, lambda bi: (0, 0, 0, 0))],
        out_specs=[pl.BlockSpec((1, s_new, D_ATT), lambda bi: (bi, 0, 0)),
                   pl.BlockSpec((1, s_new, ng * LANES), lambda bi: (bi, 0, 0))],
        out_shape=[jax.ShapeDtypeStruct((b, s_new, D_ATT), F32),
                   jax.ShapeDtypeStruct((b, s_new, ng * LANES), F32)],
        compiler_params=pltpu.CompilerParams(dimension_semantics=("parallel",), vmem_limit_bytes=VMEM_LIMIT),
        name="attn_sample",
    )(new, *bufs, *bias_cached, bias_new)
    return o.reshape(b * s_new, D_ATT), lse.reshape(b * s_new, ng * LANES)


HIST = SUBLANES


def _bdot(a, b):
    return jnp.dot(a.astype(BF16), b.astype(BF16), preferred_element_type=F32)


def _delta_prep_body(u_ref, prev_ref, small_ref, cbuf_ref, cw_ref, alog_ref, dtb_ref,
                     uo_ref, wq_ref, ak_ref, eg_ref, ext, *, t_valid, cp):
    c = pl.program_id(1)
    rows = cp * CHUNK

    @pl.when(c == 0)
    def _():
        ext[0:HIST, :] = cbuf_ref[0]

    @pl.when(c > 0)
    def _():
        ext[0:HIST, :] = prev_ref[0]

    ext[HIST:HIST + rows, :] = u_ref[0]
    conv = ext[HIST - 3:HIST - 3 + rows, :] * cw_ref[0:1, :]
    for i in range(1, CONV_W):
        conv = conv + ext[HIST - 3 + i:HIST - 3 + i + rows, :] * cw_ref[i:i + 1, :]
    act = conv * _sigmoid(conv)

    sm = small_ref[0]
    bmat = _sigmoid(sm)
    xg = sm + dtb_ref[...]
    gmat = -jnp.exp(alog_ref[...]) * (jnp.maximum(xg, 0.0) + jnp.log1p(jnp.exp(-jnp.abs(xg))))
    if t_valid < rows:
        live = lax.broadcasted_iota(jnp.int32, (rows, 1), 0) < t_valid
        act = jnp.where(live, act, 0.0)
        bmat = jnp.where(live, bmat, 0.0)
        gmat = jnp.where(live, gmat, 0.0)

    ri = lax.broadcasted_iota(jnp.int32, (CHUNK, CHUNK), 0)
    ci = lax.broadcasted_iota(jnp.int32, (CHUNK, CHUNK), 1)
    tril = ri >= ci
    strict = ri > ci

    pairs = [(j, h) for j in range(cp) for h in range(H_D)]
    rsl = lambda j: slice(j * CHUNK, (j + 1) * CHUNK)
    gcs, gcts = [], []
    for j in range(cp):
        gc_j = jnp.dot(tril.astype(F32), gmat[rsl(j)], preferred_element_type=F32, precision=HIGHEST)
        gcs.append(gc_j)
        gcts.append(gc_j.T)
    q = jnp.stack([act[rsl(j), h * DK:(h + 1) * DK] for j, h in pairs])
    k = jnp.stack([act[rsl(j), D_DK + h * DK:D_DK + (h + 1) * DK] for j, h in pairs])
    v = jnp.stack([act[rsl(j), 2 * D_DK + h * DV:2 * D_DK + (h + 1) * DV] for j, h in pairs])
    beta = jnp.stack([bmat[rsl(j), h:h + 1] for j, h in pairs])
    gcol = jnp.stack([gcs[j][:, H_D + h:H_D + h + 1] for j, h in pairs])
    grow = jnp.stack([gcts[j][H_D + h:H_D + h + 1, :] for j, h in pairs])
    glast = jnp.stack([gcs[j][CHUNK - 1:CHUNK, H_D + h:H_D + h + 1] for j, h in pairs])

    qn = q * lax.rsqrt(jnp.sum(q * q, axis=-1, keepdims=True) + NORM_EPS) * (DK ** -0.5)
    kn = k * lax.rsqrt(jnp.sum(k * k, axis=-1, keepdims=True) + NORM_EPS)
    decay = jnp.where(tril, jnp.exp(jnp.where(tril, gcol - grow, 0.0)), 0.0)
    eg = jnp.exp(gcol)
    kb = kn * beta
    kn_b = kn.astype(BF16)
    a_kk = jnp.einsum('bik,bjk->bij', kb.astype(BF16), kn_b, preferred_element_type=F32)
    a_qk = jnp.einsum('bik,bjk->bij', qn.astype(BF16), kn_b, preferred_element_type=F32)
    pw = jnp.where(strict, -(a_kk * decay), 0.0)
    tm1 = pw
    for _ in range(int(math.log2(CHUNK)) - 1):
        pw_b = pw.astype(BF16)
        pw = jnp.einsum('bij,bjk->bik', pw_b, pw_b, preferred_element_type=F32)
        tm1 = tm1 + pw + jnp.einsum('bij,bjk->bik', tm1.astype(BF16), pw.astype(BF16), preferred_element_type=F32)
    vb = v * beta
    kbg = kb * eg
    tm1_b = tm1.astype(BF16)
    u = vb + jnp.einsum('bij,bjk->bik', tm1_b, vb.astype(BF16), preferred_element_type=F32)
    w = (kbg + jnp.einsum('bij,bjk->bik', tm1_b, kbg.astype(BF16), preferred_element_type=F32)).astype(BF16)
    qd = (qn * eg).astype(BF16)
    aqd = (a_qk * decay).astype(BF16)
    kd = kn * jnp.exp(glast - gcol)
    egl = jnp.exp(glast)
    for i, (j, h) in enumerate(pairs):
        uo_ref[0, rsl(j), h * DV:(h + 1) * DV] = u[i]
        wq_ref[0, j, h, 0:CHUNK, :] = w[i]
        wq_ref[0, j, h, CHUNK:2 * CHUNK, :] = qd[i]
        ak_ref[0, j, h, 0:CHUNK, :] = aqd[i]
        ak_ref[0, j, h, CHUNK:CHUNK + DK, :] = kd[i].T.astype(BF16)
        eg_ref[0, j, h:h + 1, :] = jnp.broadcast_to(egl[i], (1, LANES))
    for j in range(cp):
        eg_ref[0, j, H_D:SUBLANES, :] = jnp.zeros((SUBLANES - H_D, LANES), F32)


def _delta_rec_body(u_ref, wq_ref, ak_ref, eg_ref, z_ref, s0_ref, nw_ref, o_ref, sout_ref, S, *, bb, nc):
    c = pl.program_id(1)

    @pl.when(c == 0)
    def _():
        S[...] = s0_ref[...]

    for b in range(bb):
        for h in range(H_D):
            cs = slice(h * DV, (h + 1) * DV)
            s_h = S[b, h]
            x = jnp.dot(wq_ref[b, 0, h], s_h.astype(BF16), preferred_element_type=F32)
            vn = u_ref[b, :, cs] - x[0:CHUNK]
            y = jnp.dot(ak_ref[b, 0, h], vn.astype(BF16), preferred_element_type=F32)
            o = x[CHUNK:2 * CHUNK] + y[0:CHUNK]
            S[b, h] = s_h * eg_ref[b, 0, h:h + 1, :] + y[CHUNK:CHUNK + DK]
            od = o * lax.rsqrt(jnp.mean(o * o, axis=-1, keepdims=True) + NORM_EPS) * nw_ref[...]
            zh = z_ref[b, :, cs]
            o_ref[b, :, cs] = od * (zh * _sigmoid(zh))

    @pl.when(c == nc - 1)
    def _():
        sout_ref[...] = S[...]


def _lane_row(vals, offset):
    row = jnp.zeros((1, LANES), F32)
    return row.at[0, offset:offset + vals.shape[0]].set(vals.astype(F32))


def _delta_net(dqkv, small, z, conv_buf, s0, conv_w, a_log, dt_bias, delta_norm_w):
    b, t, _ = dqkv.shape
    tp = -(-t // CHUNK) * CHUNK
    nc = tp // CHUNK
    assert tp == t or nc == 1
    if tp != t:
        pad = ((0, 0), (0, tp - t), (0, 0))
        dqkv, small, z = jnp.pad(dqkv, pad), jnp.pad(small, pad), jnp.pad(z, pad)
    cbuf = jnp.pad(conv_buf, ((0, 0), (HIST - (CONV_W - 1), 0), (0, 0)))
    cw = jnp.pad(conv_w, ((0, SUBLANES - CONV_W), (0, 0)))
    cp = next(c for c in (4, 2, 1) if nc % c == 0)
    rows = cp * CHUNK
    full2 = lambda shape: pl.BlockSpec(shape, lambda bi, ci: (0,) * len(shape))
    u, wq, ak, eg = pl.pallas_call(
        functools.partial(_delta_prep_body, t_valid=t if nc == 1 else rows, cp=cp),
        grid=(b, nc // cp),
        in_specs=[pl.BlockSpec((1, rows, D_CONV), lambda bi, ci: (bi, ci, 0)),
                  pl.BlockSpec((1, HIST, D_CONV), lambda bi, ci: (bi, jnp.maximum(ci * (rows // HIST) - 1, 0), 0)),
                  pl.BlockSpec((1, rows, LANES), lambda bi, ci: (bi, ci, 0)),
                  pl.BlockSpec((1, HIST, D_CONV), lambda bi, ci: (bi, 0, 0)),
                  full2((SUBLANES, D_CONV)), full2((1, LANES)), full2((1, LANES))],
        out_specs=[pl.BlockSpec((1, rows, D_DV), lambda bi, ci: (bi, ci, 0)),
                   pl.BlockSpec((1, cp, H_D, 2 * CHUNK, DK), lambda bi, ci: (bi, ci, 0, 0, 0)),
                   pl.BlockSpec((1, cp, H_D, CHUNK + DK, CHUNK), lambda bi, ci: (bi, ci, 0, 0, 0)),
                   pl.BlockSpec((1, cp, SUBLANES, LANES), lambda bi, ci: (bi, ci, 0, 0))],
        out_shape=[jax.ShapeDtypeStruct((b, tp, D_DV), F32),
                   jax.ShapeDtypeStruct((b, nc, H_D, 2 * CHUNK, DK), BF16),
                   jax.ShapeDtypeStruct((b, nc, H_D, CHUNK + DK, CHUNK), BF16),
                   jax.ShapeDtypeStruct((b, nc, SUBLANES, LANES), F32)],
        scratch_shapes=[pltpu.VMEM((HIST + rows, D_CONV), F32)],
        compiler_params=pltpu.CompilerParams(dimension_semantics=("parallel", "parallel"),
                                             vmem_limit_bytes=VMEM_LIMIT),
        name="delta_prep",
    )(dqkv, dqkv, small, cbuf, cw, _lane_row(a_log, H_D), _lane_row(dt_bias, H_D))

    bb = next(c for c in (8, 4, 2, 1) if b % c == 0)
    o, s_new = pl.pallas_call(
        functools.partial(_delta_rec_body, bb=bb, nc=nc),
        grid=(b // bb, nc),
        in_specs=[pl.BlockSpec((bb, CHUNK, D_DV), lambda bi, ci: (bi, ci, 0)),
                  pl.BlockSpec((bb, 1, H_D, 2 * CHUNK, DK), lambda bi, ci: (bi, ci, 0, 0, 0)),
                  pl.BlockSpec((bb, 1, H_D, CHUNK + DK, CHUNK), lambda bi, ci: (bi, ci, 0, 0, 0)),
                  pl.BlockSpec((bb, 1, SUBLANES, LANES), lambda bi, ci: (bi, ci, 0, 0)),
                  pl.BlockSpec((bb, CHUNK, D_DV), lambda bi, ci: (bi, ci, 0)),
                  pl.BlockSpec((bb, H_D, DK, DV), lambda bi, ci: (bi, 0, 0, 0)),
                  full2((1, DV))],
        out_specs=[pl.BlockSpec((bb, CHUNK, D_DV), lambda bi, ci: (bi, ci, 0)),
                   pl.BlockSpec((bb, H_D, DK, DV), lambda bi, ci: (bi, 0, 0, 0))],
        out_shape=[jax.ShapeDtypeStruct((b, tp, D_DV), F32), jax.ShapeDtypeStruct((b, H_D, DK, DV), F32)],
        scratch_shapes=[pltpu.VMEM((bb, H_D, DK, DV), F32)],
        compiler_params=pltpu.CompilerParams(dimension_semantics=("parallel", "arbitrary"),
                                             vmem_limit_bytes=VMEM_LIMIT),
        name="delta_recurrence",
    )(u, wq, ak, eg, z, s0, delta_norm_w.reshape(1, DV).astype(F32))
    return o[:, :t].reshape(b * t, D_DV), s_new


def _lane_expand(cols, width):
    tm = cols[0].shape[0]
    seg = lax.broadcasted_iota(jnp.int32, (tm, len(cols) * width), 1) // width
    out = jnp.broadcast_to(cols[-1], seg.shape)
    for i in range(len(cols) - 2, -1, -1):
        out = jnp.where(seg == i, cols[i], out)
    return out


def _token_order(ref, scr):
    d, nc = ref.shape[1], ref.shape[3] // LANES
    if d == 1:
        return ref[0, 0]
    for r in range(d):
        for c in range(nc):
            scr[c, pl.ds(r, ref.shape[2], stride=d), :] = ref[0, r, :, c * LANES:(c + 1) * LANES]
    return jnp.concatenate([scr[c] for c in range(nc)], axis=-1)


def _mix_body(x_ref, gate_ref, o0_ref, o1_ref, o2_ref, l0_ref, l1_ref, l2_ref, od_ref,
              wa_ref, wd_ref, wo_ref, bg_ref, g1_ref, b1_ref, rw_ref, rb_ref,
              h_ref, ti_ref, tg_ref, cnt_ref, *scratch):
    o_scr, l_scr = scratch[:3], scratch[3:]
    lses = [_token_order(l, s) for l, s in zip((l0_ref, l1_ref, l2_ref), l_scr)]
    mx = jnp.maximum(jnp.maximum(lses[0], lses[1]), lses[2])
    es = [jnp.exp(l - mx) for l in lses]
    inv = 1.0 / (es[0] + es[1] + es[2])
    o_att = None
    for e_g, o_g, s in zip(es, (o0_ref, o1_ref, o2_ref), o_scr):
        wgt = e_g * inv
        term = _lane_expand([wgt[:, h:h + 1] for h in range(H_G)], HD) * _token_order(o_g, s)
        o_att = term if o_att is None else o_att + term
    ga = _sigmoid(gate_ref[:, :D_MODEL] + bg_ref[:, :D_MODEL])
    gd = _sigmoid(gate_ref[:, D_MODEL:] + bg_ref[:, D_MODEL:])
    merged = (ga * jnp.dot(o_att.astype(BF16), wa_ref[...], preferred_element_type=F32)
              + gd * jnp.dot(od_ref[...].astype(BF16), wd_ref[...], preferred_element_type=F32))
    mix = jnp.dot(merged.astype(BF16), wo_ref[...], preferred_element_type=F32)
    h = _layer_norm(DN_ALPHA * x_ref[...] + mix, g1_ref[...], b1_ref[...])
    _rows_to_tiles(h_ref, h)

    h_hi = h.astype(BF16)
    h_lo = (h - h_hi.astype(F32)).astype(BF16)
    part = jnp.dot(h_hi, rw_ref[...], preferred_element_type=F32)
    logits = (part[:, :LANES] + part[:, LANES:]
              + jnp.dot(h_lo, rw_ref[:, :LANES], preferred_element_type=F32) + rb_ref[...])
    lane = lax.broadcasted_iota(jnp.int32, logits.shape, 1)
    lane_f = lane.astype(F32)
    ti = jnp.zeros(logits.shape, F32)
    tv = jnp.zeros(logits.shape, F32)
    hits = jnp.zeros(logits.shape, F32)
    top = None
    for kk in range(TOP_K):
        m = jnp.max(logits, axis=-1, keepdims=True)
        idx = jnp.min(jnp.where(logits == m, lane_f, float(LANES)), axis=-1, keepdims=True)
        if top is None:
            top = m
        ti = jnp.where(lane == kk, idx, ti)
        tv = jnp.where(lane == kk, jnp.exp(m - top), tv)
        chosen = lane_f == idx
        hits = jnp.where(chosen, 1.0, hits)
        logits = jnp.where(chosen, -jnp.inf, logits)
    ti_ref[...] = ti.astype(jnp.int32)
    tg_ref[...] = tv / jnp.sum(tv, axis=-1, keepdims=True)

    @pl.when(pl.program_id(0) == 0)
    def _():
        cnt_ref[...] = jnp.zeros_like(cnt_ref)

    cnt_ref[...] += jnp.sum(hits, axis=0, keepdims=True)


def _mix_and_route(x2d, gates, o_groups, lse_groups, o_del, wa, wd, wo, b_gate, ln_g, ln_b, rw, rb, tm):
    n = x2d.shape[0]
    tpb = o_groups[0].shape[1] * o_groups[0].shape[2] // tm
    row = lambda c: pl.BlockSpec((tm, c), lambda i: (i, 0))
    full = lambda a: pl.BlockSpec(a.shape, lambda i: (0, 0))

    def res(a):
        d = a.shape[1]
        return pl.BlockSpec((1, d, tm // d, a.shape[3]), lambda i: (i // tpb, 0, i % tpb, 0))

    consts = (wa, wd, wo, b_gate, ln_g, ln_b, rw, rb)
    return pl.pallas_call(
        _mix_body,
        grid=(n // tm,),
        in_specs=[row(D_MODEL), row(2 * D_MODEL)] + [res(a) for a in o_groups] + [res(a) for a in lse_groups]
        + [row(D_DV)] + [full(a) for a in consts],
        out_specs=[pl.BlockSpec((tm * TOK_ROWS, LANES), lambda i: (i, 0)), row(LANES), row(LANES),
                   pl.BlockSpec((1, LANES), lambda i: (0, 0))],
        out_shape=[jax.ShapeDtypeStruct((n * TOK_ROWS, LANES), F32), jax.ShapeDtypeStruct((n, LANES), jnp.int32),
                   jax.ShapeDtypeStruct((n, LANES), F32), jax.ShapeDtypeStruct((1, LANES), F32)],
        scratch_shapes=[pltpu.VMEM((GW // LANES, tm, LANES), F32)] * len(o_groups)
        + [pltpu.VMEM((1, tm, LANES), F32)] * len(lse_groups),
        compiler_params=pltpu.CompilerParams(dimension_semantics=("arbitrary",), vmem_limit_bytes=VMEM_LIMIT),
        name="mix_and_route",
    )(x2d, gates, *o_groups, *lse_groups, o_del, *consts)


def _rank_body(ti_ref, base_ref, dest_ref, carry):
    i = pl.program_id(0)

    @pl.when(i == 0)
    def _():
        carry[...] = base_ref[...]

    ti = ti_ref[...]
    tm = ti.shape[0]
    lane = lax.broadcasted_iota(jnp.int32, ti.shape, 1)
    sel = [lane == ti[:, kk:kk + 1] for kk in range(TOP_K)]
    hit = sel[0]
    for kk in range(1, TOP_K):
        hit = hit | sel[kk]
    cnt = hit.astype(F32)
    ri = lax.broadcasted_iota(jnp.int32, (tm, tm), 0)
    ci = lax.broadcasted_iota(jnp.int32, (tm, tm), 1)
    incl = jnp.dot((ri >= ci).astype(BF16), cnt.astype(BF16), preferred_element_type=F32)
    before = incl - cnt + carry[...]
    slot = jnp.zeros(ti.shape, F32)
    for kk in range(TOP_K):
        r = jnp.sum(jnp.where(sel[kk], before, 0.0), axis=-1, keepdims=True)
        slot = jnp.where(lane == kk, r, slot)
    dest_ref[...] = slot.T[0:SUBLANES].astype(jnp.int32)
    carry[...] = carry[...] + incl[tm - 1:tm, :]


def _route_slots(top_i, base, tm):
    n = top_i.shape[0]
    return pl.pallas_call(
        _rank_body,
        grid=(n // tm,),
        in_specs=[pl.BlockSpec((tm, LANES), lambda i: (i, 0)), pl.BlockSpec((1, LANES), lambda i: (0, 0))],
        out_specs=pl.BlockSpec((SUBLANES, tm), lambda i: (i, 0)),
        out_shape=jax.ShapeDtypeStruct((n // tm * SUBLANES, tm), jnp.int32),
        scratch_shapes=[pltpu.VMEM((1, LANES), F32)],
        compiler_params=pltpu.CompilerParams(dimension_semantics=("arbitrary",), vmem_limit_bytes=VMEM_LIMIT),
        name="route_slots",
    )(top_i, base)


def _rows_to_tiles(ref, val):
    m = val.shape[0]
    for s in range(TOK_ROWS):
        ref[pl.ds(s, m, stride=TOK_ROWS), :] = val[:, s * LANES:(s + 1) * LANES]


def _tiles_to_rows(ref, m):
    return jnp.concatenate([ref[pl.ds(s, m, stride=TOK_ROWS), :] for s in range(TOK_ROWS)], axis=-1)


def _dispatch_body(pend_ref, padded_ref, dest_ref, h_ref, *refs, max_tail):
    tm = h_ref.shape[0] // TOK_ROWS
    blk = MOE_BM * TOK_ROWS
    first_group = len(refs) == 4
    if first_group:
        xs_out, zero, sem, zsem = refs
    else:
        _, xs_out, sem = refs

    def zero_fill():
        zero[...] = jnp.zeros_like(zero)

        def last_block(e):
            start = pl.multiple_of((pend_ref[e] - MOE_BM) * TOK_ROWS, blk)
            return pltpu.make_async_copy(zero, xs_out.at[pl.ds(start, blk)], zsem)

        def tail_block(j):
            start = pl.multiple_of((pend_ref[N_EXP - 1] + j * MOE_BM) * TOK_ROWS, blk)
            return pltpu.make_async_copy(zero, xs_out.at[pl.ds(start, blk)], zsem)

        n_slot = xs_out.shape[0] // TOK_ROWS
        for e in range(N_EXP):
            @pl.when(padded_ref[e] > 0)
            def _():
                last_block(e).start()
        for j in range(max_tail):
            @pl.when(pend_ref[N_EXP - 1] + j * MOE_BM < n_slot)
            def _():
                tail_block(j).start()
        for e in range(N_EXP):
            @pl.when(padded_ref[e] > 0)
            def _():
                last_block(e).wait()
        for j in range(max_tail):
            @pl.when(pend_ref[N_EXP - 1] + j * MOE_BM < n_slot)
            def _():
                tail_block(j).wait()

    if first_group:
        pl.when(pl.program_id(0) == 0)(zero_fill)

    def issue(t, carry):
        src = h_ref.at[pl.ds(pl.multiple_of(t * TOK_ROWS, TOK_ROWS), TOK_ROWS)]
        for kk in range(TOP_K):
            d = pl.multiple_of(dest_ref[kk * tm + t] * TOK_ROWS, TOK_ROWS)
            pltpu.make_async_copy(src, xs_out.at[pl.ds(d, TOK_ROWS)], sem).start(priority=kk % 2)
        return carry

    lax.fori_loop(0, tm, issue, 0, unroll=8)
    for kk in range(TOP_K):
        pltpu.make_async_copy(h_ref, xs_out.at[pl.ds(0, tm * TOK_ROWS)], sem).wait()


def _dispatch(h_tiles, dest_flat, pad_end, padded, n_slot, tm, xs_prev=None):
    n = h_tiles.shape[0] // TOK_ROWS
    first_group = xs_prev is None
    in_specs = [pl.BlockSpec((SUBLANES * tm,), lambda i, pe, pd: (i,), memory_space=pltpu.SMEM),
                pl.BlockSpec((tm * TOK_ROWS, LANES), lambda i, pe, pd: (i, 0))]
    scratch = [pltpu.SemaphoreType.DMA(())]
    if first_group:
        scratch = [pltpu.VMEM((MOE_BM * TOK_ROWS, LANES), F32)] + scratch + [pltpu.SemaphoreType.DMA(())]
    else:
        in_specs.append(pl.BlockSpec(memory_space=pl.ANY))
    grid_spec = pltpu.PrefetchScalarGridSpec(
        num_scalar_prefetch=2, grid=(n // tm,), in_specs=in_specs,
        out_specs=pl.BlockSpec(memory_space=pl.ANY), scratch_shapes=scratch)
    max_tail = n_slot // MOE_BM - (n * TOP_K) // MOE_BM
    return pl.pallas_call(
        functools.partial(_dispatch_body, max_tail=max_tail),
        grid_spec=grid_spec,
        out_shape=jax.ShapeDtypeStruct((n_slot * TOK_ROWS, LANES), F32),
        input_output_aliases={} if first_group else {4: 0},
        compiler_params=pltpu.CompilerParams(dimension_semantics=("arbitrary",), vmem_limit_bytes=VMEM_LIMIT,
                                             has_side_effects=True, disable_bounds_checks=True),
        name="moe_dispatch",
    )(pad_end, padded, dest_flat, h_tiles, *(() if first_group else (xs_prev,)))


def _expert_body(be_ref, nu_ref, nxt_ref, par_ref, x_ref, bgu_ref, bd_ref, wgu_hbm, wd_hbm, o_ref,
                 wgu_f32, wd_f32, wgu_bf, wd_bf, sem):
    i = pl.program_id(0)
    e = be_ref[i]
    prev = be_ref[jnp.maximum(i - 1, 0)]
    live = i < nu_ref[0]

    def fetch(expert, slot):
        return (pltpu.make_async_copy(wgu_hbm.at[expert], wgu_f32.at[slot], sem.at[0, slot]),
                pltpu.make_async_copy(wd_hbm.at[expert], wd_f32.at[slot], sem.at[1, slot]))

    @pl.when(live & ((i == 0) | (e != prev)))
    def _():
        slot = par_ref[i]

        @pl.when(i == 0)
        def _():
            for cp in fetch(e, slot):
                cp.start()

        for cp in fetch(e, slot):
            cp.wait()
        wgu_bf[...] = wgu_f32[slot].astype(BF16)
        wd_bf[...] = wd_f32[slot].astype(BF16)

        @pl.when(nxt_ref[i] >= 0)
        def _():
            for cp in fetch(nxt_ref[i], 1 - slot):
                cp.start()

    @pl.when(live)
    def _():
        x = _tiles_to_rows(x_ref, MOE_BM)
        gu = jnp.dot(x.astype(BF16), wgu_bf[...], preferred_element_type=F32) + bgu_ref[0]
        gt = jnp.minimum(gu[:, :D_FF], SWIGLU_LIMIT)
        up = jnp.clip(gu[:, D_FF:], -SWIGLU_LIMIT, SWIGLU_LIMIT)
        act = (up + 1.0) * gt * _sigmoid(SWIGLU_ALPHA * gt)
        _rows_to_tiles(o_ref, jnp.dot(act.astype(BF16), wd_bf[...], preferred_element_type=F32) + bd_ref[0])

    @pl.when(i >= nu_ref[0])
    def _():
        o_ref[...] = jnp.zeros_like(o_ref)


def _experts(xs, blk_exp, n_used, padded, w_gu, b_gu, w_down, b_down):
    n_slot = xs.shape[0] // TOK_ROWS
    n_blk = n_slot // MOE_BM
    ids = jnp.arange(N_EXP, dtype=jnp.int32)
    has = padded > 0
    later = jnp.where(has[None, :] & (ids[None, :] > ids[:, None]), ids[None, :], N_EXP)
    nxt_e = jnp.min(later, axis=1)
    nxt_e = jnp.where(nxt_e < N_EXP, nxt_e, -1).astype(jnp.int32)
    par_e = ((jnp.cumsum(has.astype(jnp.int32)) - 1) % 2).astype(jnp.int32)
    rows = lambda i, be, nu, nx, pa: (jnp.minimum(i, nu[0] - 1), 0)
    grid_spec = pltpu.PrefetchScalarGridSpec(
        num_scalar_prefetch=4,
        grid=(n_blk,),
        in_specs=[pl.BlockSpec((MOE_BM * TOK_ROWS, LANES), rows),
                  pl.BlockSpec((1, 1, 2 * D_FF), lambda i, be, nu, nx, pa: (be[i], 0, 0)),
                  pl.BlockSpec((1, 1, D_MODEL), lambda i, be, nu, nx, pa: (be[i], 0, 0)),
                  pl.BlockSpec(memory_space=pl.ANY), pl.BlockSpec(memory_space=pl.ANY)],
        out_specs=pl.BlockSpec((MOE_BM * TOK_ROWS, LANES), lambda i, be, nu, nx, pa: (i, 0)),
        scratch_shapes=[pltpu.VMEM((2, D_MODEL, 2 * D_FF), F32), pltpu.VMEM((2, D_FF, D_MODEL), F32),
                        pltpu.VMEM((D_MODEL, 2 * D_FF), BF16), pltpu.VMEM((D_FF, D_MODEL), BF16),
                        pltpu.SemaphoreType.DMA((2, 2))],
    )
    return pl.pallas_call(
        _expert_body,
        grid_spec=grid_spec,
        out_shape=jax.ShapeDtypeStruct((n_slot * TOK_ROWS, LANES), F32),
        compiler_params=pltpu.CompilerParams(dimension_semantics=("arbitrary",), vmem_limit_bytes=VMEM_LIMIT),
        name="moe_experts",
    )(blk_exp, n_used, nxt_e[blk_exp], par_e[blk_exp], xs, b_gu.reshape(N_EXP, 1, 2 * D_FF),
      b_down.reshape(N_EXP, 1, D_MODEL), w_gu, w_down)


def _combine_body(dest_ref, dest_next_ref, gate_ref, h_ref, g_ref, b_ref, ys_ref, y_ref, buf, sem):
    tm = h_ref.shape[0] // TOK_ROWS
    i = pl.program_id(0)
    slot = i % 2

    def start_tile(idx_ref, s):
        def issue(t, carry):
            row = pl.multiple_of(t * TOK_ROWS, TOK_ROWS)
            for kk in range(TOP_K):
                d = pl.multiple_of(idx_ref[kk * tm + t] * TOK_ROWS, TOK_ROWS)
                pltpu.make_async_copy(ys_ref.at[pl.ds(d, TOK_ROWS)], buf.at[s, kk, pl.ds(row, TOK_ROWS)],
                                      sem.at[s]).start(priority=kk % 2)
            return carry

        lax.fori_loop(0, tm, issue, 0, unroll=8)

    @pl.when(i == 0)
    def _():
        start_tile(dest_ref, 0)

    @pl.when(i + 1 < pl.num_programs(0))
    def _():
        start_tile(dest_next_ref, 1 - slot)

    for kk in range(TOP_K):
        pltpu.make_async_copy(ys_ref.at[pl.ds(0, tm * TOK_ROWS)], buf.at[slot, kk], sem.at[slot]).wait()
    gate = gate_ref[...]
    moe = gate[:, 0:1] * _tiles_to_rows(buf.at[slot, 0], tm)
    for kk in range(1, TOP_K):
        moe = moe + gate[:, kk:kk + 1] * _tiles_to_rows(buf.at[slot, kk], tm)
    y_ref[...] = _layer_norm(DN_ALPHA * _tiles_to_rows(h_ref, tm) + moe, g_ref[...], b_ref[...])


def _combine(ys, dest_flat, gate, h_tiles, ln_g, ln_b, tm):
    n = h_tiles.shape[0] // TOK_ROWS
    return pl.pallas_call(
        _combine_body,
        grid=(n // tm,),
        in_specs=[pl.BlockSpec((SUBLANES * tm,), lambda i: (i,), memory_space=pltpu.SMEM),
                  pl.BlockSpec((SUBLANES * tm,), lambda i: (jnp.minimum(i + 1, n // tm - 1),), memory_space=pltpu.SMEM),
                  pl.BlockSpec((tm, LANES), lambda i: (i, 0)),
                  pl.BlockSpec((tm * TOK_ROWS, LANES), lambda i: (i, 0)),
                  pl.BlockSpec((1, D_MODEL), lambda i: (0, 0)),
                  pl.BlockSpec((1, D_MODEL), lambda i: (0, 0)),
                  pl.BlockSpec(memory_space=pl.ANY)],
        out_specs=pl.BlockSpec((tm, D_MODEL), lambda i: (i, 0)),
        out_shape=jax.ShapeDtypeStruct((n, D_MODEL), F32),
        scratch_shapes=[pltpu.VMEM((2, TOP_K, tm * TOK_ROWS, LANES), F32), pltpu.SemaphoreType.DMA((2,))],
        compiler_params=pltpu.CompilerParams(dimension_semantics=("arbitrary",), vmem_limit_bytes=VMEM_LIMIT,
                                             disable_bounds_checks=True),
        name="moe_combine",
    )(dest_flat, dest_flat, gate, h_tiles, ln_g, ln_b, ys)


def _moe(groups, w_gu, b_gu, w_down, b_down, ln_g, ln_b):
    counts = [g[3][0, :N_EXP].astype(jnp.int32) for g in groups]
    total = sum(counts)
    padded = (total + MOE_BM - 1) // MOE_BM * MOE_BM
    pad_end = jnp.cumsum(padded).astype(jnp.int32)
    n_assign = sum(g[1].shape[0] for g in groups) * TOP_K
    n_blk = -(-(n_assign + N_EXP * (MOE_BM - 1)) // MOE_BM)
    blk_row0 = jnp.arange(n_blk, dtype=jnp.int32) * MOE_BM
    blk_exp = jnp.minimum(jnp.sum((pad_end[None, :] <= blk_row0[:, None]).astype(jnp.int32), axis=1), N_EXP - 1)
    n_used = pad_end[-1:] // MOE_BM
    base = pad_end - padded
    xs, dests = None, []
    for (h, top_i, _, _, tm), cnt in zip(groups, counts):
        base_row = jnp.zeros((1, LANES), F32).at[0, :N_EXP].set(base.astype(F32))
        dest = _route_slots(top_i, base_row, tm).reshape(-1)
        dests.append(dest)
        xs = _dispatch(h, dest, pad_end, padded, n_blk * MOE_BM, tm, xs)
        base = base + cnt
    ys = _experts(xs, blk_exp, n_used, padded, w_gu, b_gu, w_down, b_down)
    return [_combine(ys, dest, top_g, h, ln_g, ln_b, tm) for (h, _, top_g, _, tm), dest in zip(groups, dests)]


def _pad_cols(a, width, fill=0.0):
    return jnp.pad(a, ((0, 0), (0, width - a.shape[1])), constant_values=fill)


def _token_mixers_and_route(x, caches, conv_buf, s0, w_in, b_gate, conv_w, a_log, dt_bias, delta_norm_w,
                            w_branch_attn, w_branch_delta, w_out, ln1_g, ln1_b, router_w, router_b):
    b, t, _ = x.shape
    n = b * t
    tm = min(TOK_TM, n)
    x2d = x.reshape(n, D_MODEL)

    c_att, c_dz, c_small = 3 * D_ATT, 3 * D_ATT + D_CONV, 3 * D_ATT + D_CONV + D_DV
    c_gate = c_small + 2 * H_D
    ng = len(ATT_DILS)
    w_groups = [jnp.concatenate([w_in[:, part * D_ATT + gi * GW:part * D_ATT + (gi + 1) * GW] for part in range(3)],
                                axis=1) for gi in range(ng)]
    ws = w_groups + [w_in[:, c_att:c_dz], w_in[:, c_dz:c_small],
                     _pad_cols(w_in[:, c_small:c_gate], LANES), w_in[:, c_gate:]]
    dils = (ATT_DILS if caches is None else (1,) * ng) + (1, 1, 1, 1)
    outs = _in_projection(x2d, [w.astype(BF16) for w in ws], dils, t, min(IN_TM, n))
    att_groups, (dqkv, z, small, gates) = outs[:ng], outs[ng:]
    dqkv3 = dqkv.reshape(b, t, D_CONV)

    kv_new = []
    if caches is None:
        o_groups, lse_groups = [], []
        for dil, a in zip(ATT_DILS, att_groups):
            a = a.reshape(b, dil, t // dil, 3 * GW)
            o_g, lse_g = _attn_prompt(a, dil)
            o_groups.append(o_g)
            lse_groups.append(lse_g)
            win = min(dil * STEPS, t)
            last = a[:, :, (t - win) // dil:, GW:]
            last = jnp.transpose(last, (0, 2, 1, 3)).reshape(b, win, 2, H_G, HD)
            kv_new.append(last)
        conv_buf = jnp.zeros((b, CONV_W - 1, D_CONV), F32)
        s0 = jnp.zeros((b, H_D, DK, DV), F32)
    else:
        att3 = jnp.concatenate(att_groups, axis=1).reshape(b, t, 3 * D_ATT)
        o_all, lse_all = _attn_sample(att3, caches)
        o_groups = [o_all[:, gi * GW:(gi + 1) * GW].reshape(1, 1, n, GW) for gi in range(ng)]
        lse_groups = [lse_all[:, gi * LANES:(gi + 1) * LANES].reshape(1, 1, n, LANES) for gi in range(ng)]
        for a in att_groups:
            kv_new.append(a[:, GW:].reshape(b, t, 2, H_G, HD))

    o_del, s_new = _delta_net(dqkv3, small.reshape(b, t, LANES), z.reshape(b, t, D_DV), conv_buf, s0,
                              conv_w, a_log, dt_bias, delta_norm_w)
    conv_new = jnp.concatenate([conv_buf, dqkv3], axis=1)[:, -(CONV_W - 1):] if t < CONV_W - 1 \
        else dqkv3[:, t - (CONV_W - 1):]

    rw = _pad_cols(router_w, LANES)
    rw_hi = rw.astype(BF16)
    rw = jnp.concatenate([rw_hi, (rw - rw_hi.astype(F32)).astype(BF16)], axis=1)
    rb = _pad_cols(router_b.reshape(1, N_EXP), LANES, fill=NEG)
    h, top_i, top_g, counts = _mix_and_route(
        x2d, gates, o_groups, lse_groups, o_del,
        w_branch_attn.astype(BF16), w_branch_delta.astype(BF16), w_out.astype(BF16),
        b_gate.reshape(1, 2 * D_MODEL), ln1_g.reshape(1, D_MODEL), ln1_b.reshape(1, D_MODEL), rw, rb, tm)
    return (h, top_i, top_g, counts, tm), kv_new, conv_new, s_new


def kernel(x_prompt, x_sample, cache_kv_w128, cache_kv_w512, cache_kv_w2048, state_conv, state_delta,
           w_in, b_gate, conv_w, a_log, dt_bias, delta_norm_w, w_branch_attn, w_branch_delta, w_out,
           ln1_g, ln1_b, router_w, router_b, w_gu, b_gu, w_down, b_down, ln2_g, ln2_b):
    depth = w_in.shape[0]
    assert depth == 1
    l = 0
    lw = (w_in[l], b_gate[l], conv_w[l], a_log[l], dt_bias[l], delta_norm_w[l], w_branch_attn[l],
          w_branch_delta[l], w_out[l], ln1_g[l], ln1_b[l], router_w[l], router_b[l], w_gu[l], b_gu[l],
          w_down[l], b_down[l], ln2_g[l], ln2_b[l])
    mixers = lw[:13]
    routed_p, kv_p, cv_p, s_p = _token_mixers_and_route(x_prompt, None, None, None, *mixers)
    routed_s, kv_s, cv_s, s_s = _token_mixers_and_route(
        x_sample, (cache_kv_w128[l], cache_kv_w512[l], cache_kv_w2048[l]), state_conv[l], state_delta[l], *mixers)
    w_gu_l, b_gu_l, w_down_l, b_down_l, ln2_g_l, ln2_b_l = lw[13:]
    yp, ys = _moe([routed_p, routed_s], w_gu_l, b_gu_l, w_down_l, b_down_l,
                  ln2_g_l.reshape(1, D_MODEL), ln2_b_l.reshape(1, D_MODEL))
    yp, ys = yp.reshape(x_prompt.shape), ys.reshape(x_sample.shape)
    stk = lambda a: a[None]
    return (yp, ys, stk(kv_p[0]), stk(kv_p[1]), stk(kv_p[2]), stk(cv_p), stk(s_p),
            stk(kv_s[0]), stk(kv_s[1]), stk(kv_s[2]), stk(cv_s), stk(s_s))
```

```python
import functools
import math

import numpy as np
import jax
import jax.numpy as jnp
from jax import lax
from jax.experimental import pallas as pl
from jax.experimental.pallas import tpu as pltpu

F32 = jnp.float32
BF16 = jnp.bfloat16
HIGHEST = lax.Precision.HIGHEST

D_MODEL = 1024
ATT_DILS = (1, 4, 16)
STEPS = 128
H_G = 4
HD = 64
GW = H_G * HD
D_ATT = len(ATT_DILS) * GW
H_D = 4
DK = 128
DV = 128
D_DK = H_D * DK
D_DV = H_D * DV
D_CONV = 2 * D_DK + D_DV
CONV_W = 4
CHUNK = 64
N_EXP = 32
TOP_K = 4
D_FF = D_MODEL
SWIGLU_LIMIT = 7.0
SWIGLU_ALPHA = 1.702
DN_ALPHA = 2.0 ** 0.25
LN_EPS = 1e-5
NORM_EPS = 1e-6
NEG = -1e30

LANES = 128
SUBLANES = 8
VMEM_LIMIT = 56 * 1024 * 1024
MOE_BM = 512
IN_TM = 512
TOK_TM = 512
TOK_ROWS = D_MODEL // LANES
assert TOK_ROWS == SUBLANES


def _sigmoid(x):
    return 1.0 / (1.0 + jnp.exp(-x))


def _layer_norm(v, g, b):
    mu = jnp.mean(v, axis=-1, keepdims=True)
    d = v - mu
    var = jnp.mean(d * d, axis=-1, keepdims=True)
    return d * lax.rsqrt(var + LN_EPS) * g + b


def _alibi_slopes():
    return 2.0 ** (-8.0 * np.arange(1, H_G + 1, dtype=np.float64) / H_G)


def _inproj_body(x_ref, *refs, dils):
    nw = len(dils)
    scr = refs[2 * nw] if len(refs) > 2 * nw else None
    x = x_ref[...].astype(BF16)
    for w_ref, o_ref, d in zip(refs[:nw], refs[nw:2 * nw], dils):
        y = jnp.dot(x, w_ref[...], preferred_element_type=F32)
        if d == 1:
            o_ref[...] = y
        else:
            for c in range(y.shape[1] // LANES):
                scr[c] = y[:, c * LANES:(c + 1) * LANES]
            for r in range(d):
                for c in range(y.shape[1] // LANES):
                    o_ref[0, r, :, c * LANES:(c + 1) * LANES] = scr[c, pl.ds(r, y.shape[0] // d, stride=d), :]


def _in_projection(x2d, ws, dils, t, tm):
    n = x2d.shape[0]
    tpb = t // tm
    out_specs, out_shape = [], []
    for w, d in zip(ws, dils):
        c = w.shape[1]
        if d == 1:
            out_specs.append(pl.BlockSpec((tm, c), lambda i: (i, 0)))
            out_shape.append(jax.ShapeDtypeStruct((n, c), F32))
        else:
            assert tm % (d * SUBLANES) == 0 and t % tm == 0
            out_specs.append(pl.BlockSpec((1, d, tm // d, c), lambda i: (i // tpb, 0, i % tpb, 0)))
            out_shape.append(jax.ShapeDtypeStruct((n // t, d, t // d, c), F32))
    wide = max([w.shape[1] for w, d in zip(ws, dils) if d > 1], default=0)
    return pl.pallas_call(
        functools.partial(_inproj_body, dils=tuple(dils)),
        grid=(n // tm,),
        in_specs=[pl.BlockSpec((tm, D_MODEL), lambda i: (i, 0))]
        + [pl.BlockSpec((D_MODEL, w.shape[1]), lambda i: (0, 0), pipeline_mode=pl.Buffered(1)) for w in ws],
        out_specs=out_specs,
        out_shape=out_shape,
        scratch_shapes=[pltpu.VMEM((wide // LANES, tm, LANES), F32)] if wide else [],
        compiler_params=pltpu.CompilerParams(dimension_semantics=("parallel",), vmem_limit_bytes=VMEM_LIMIT),
        name="in_projection",
    )(x2d, *ws)


def _prompt_bias_table(dil):
    qi = np.arange(STEPS)[:, None]
    kj = np.arange(2 * STEPS)[None, :]
    steps = qi + STEPS - kj
    valid = (steps >= 0) & (steps <= STEPS)
    slopes = _alibi_slopes()
    bias = -slopes[:, None, None] * (dil * steps)[None].astype(np.float64)
    later = np.where(valid[None], bias, NEG)
    first = np.where((valid & (kj >= STEPS))[None], bias, NEG)
    return jnp.asarray(np.stack([first, later]).reshape(2, H_G * STEPS, 2 * STEPS), F32)


ATT_QB = 4


def _attn_prompt_body(q_ref, kp_ref, kc_ref, vp_ref, vc_ref, bias_ref, o_ref, lse_ref):
    head = lax.broadcasted_iota(jnp.int32, (STEPS, GW), 1) // HD
    for r in range(q_ref.shape[1]):
        kk = jnp.concatenate([kp_ref[0, r], kc_ref[0, r]], axis=0).astype(BF16)
        vv = jnp.concatenate([vp_ref[0, r], vc_ref[0, r]], axis=0).astype(BF16)
        lse_ref[0, r] = jnp.zeros(lse_ref.shape[2:], F32)
        for j in range(o_ref.shape[2] // STEPS):
            qrows = slice(j * STEPS, (j + 1) * STEPS)
            krows = slice(j * STEPS, (j + 2) * STEPS)
            later = jnp.minimum(pl.program_id(2), 1) if j == 0 else 1
            q = q_ref[0, r, qrows, :] * (HD ** -0.5)
            qs = jnp.concatenate([jnp.where(head == h, q, 0.0) for h in range(H_G)], axis=0).astype(BF16)
            s = lax.dot_general(qs, kk[krows], (((1,), (1,)), ((), ())), preferred_element_type=F32)
            s = s + bias_ref[later]
            mx = jnp.max(s, axis=-1, keepdims=True)
            p = jnp.exp(s - mx)
            den = jnp.sum(p, axis=-1, keepdims=True)
            pv = jnp.dot(p.astype(BF16), vv[krows], preferred_element_type=F32) / den
            lse = mx + jnp.log(den)
            o = jnp.zeros((STEPS, GW), F32)
            for h in range(H_G):
                rows = slice(h * STEPS, (h + 1) * STEPS)
                o = jnp.where(head == h, pv[rows], o)
                lse_ref[0, r, qrows, h:h + 1] = lse[rows]
            o_ref[0, r, qrows, :] = o


def _attn_prompt(a, dil):
    b, d, L, c = a.shape
    assert d == dil and c == 3 * GW and L % STEPS == 0
    qb = next(k for k in range(ATT_QB, 0, -1) if L % (k * STEPS) == 0)
    rb = next(k for k in range(ATT_QB // qb, 0, -1) if dil % k == 0)
    rows = qb * STEPS
    nb = L // rows

    def spec(col, prev):
        if prev:
            return pl.BlockSpec((1, rb, STEPS, GW), lambda bi, r, n: (bi, r, jnp.maximum(qb * n - 1, 0), col))
        return pl.BlockSpec((1, rb, rows, GW), lambda bi, r, n: (bi, r, n, col))

    return pl.pallas_call(
        _attn_prompt_body,
        grid=(b, dil // rb, nb),
        in_specs=[spec(0, False), spec(1, True), spec(1, False), spec(2, True), spec(2, False),
                  pl.BlockSpec((2, H_G * STEPS, 2 * STEPS), lambda bi, r, n: (0, 0, 0))],
        out_specs=[pl.BlockSpec((1, rb, rows, GW), lambda bi, r, n: (bi, r, n, 0)),
                   pl.BlockSpec((1, rb, rows, LANES), lambda bi, r, n: (bi, r, n, 0))],
        out_shape=[jax.ShapeDtypeStruct((b, dil, L, GW), F32),
                   jax.ShapeDtypeStruct((b, dil, L, LANES), F32)],
        compiler_params=pltpu.CompilerParams(dimension_semantics=("parallel", "parallel", "arbitrary"),
                                             vmem_limit_bytes=VMEM_LIMIT),
        name=f"attn_prompt_d{dil}",
    )(a, a, a, a, a, _prompt_bias_table(dil))


def _sample_bias_tables(s_new):
    slopes = _alibi_slopes()[:, None, None]
    s = np.arange(SUBLANES)[:, None]
    live = s < s_new
    cached = []
    for dil in ATT_DILS:
        lw = dil * STEPS
        dist = lw + s - np.arange(lw)[None, :]
        ok = live & (dist % dil == 0) & (dist <= lw)
        cached.append(jnp.asarray(np.where(ok[None], -slopes * dist[None], NEG), F32))
    s2 = np.arange(SUBLANES)[None, :]
    dist = s - s2
    new = []
    for dil in ATT_DILS:
        ok = live & (s2 < s_new) & (dist >= 0) & (dist % dil == 0)
        new.append(np.where(ok[None], -slopes * dist[None], NEG))
    return cached, jnp.asarray(np.stack(new), F32)


def _attn_sample_body(new_ref, c0_ref, c1_ref, c2_ref, bw0_ref, bw1_ref, bw2_ref, bn_ref, o_ref, lse_ref, *, s_new):
    lse_ref[0] = jnp.zeros(lse_ref.shape[1:], F32)
    for gi, (cache, bw_ref) in enumerate(zip((c0_ref, c1_ref, c2_ref), (bw0_ref, bw1_ref, bw2_ref))):
        def heads(part):
            c0 = gi * 3 * GW + part * GW
            return jnp.stack([new_ref[0, :, c0 + h * HD:c0 + (h + 1) * HD] for h in range(H_G)])

        q, k_new, v_new = heads(0) * (HD ** -0.5), heads(1), heads(2)
        kt = cache[0, 0].astype(BF16)
        vt = cache[0, 1].astype(BF16)
        sw = jnp.einsum('hqe,hek->hqk', q.astype(BF16), kt, preferred_element_type=F32) + bw_ref[...]
        sn = jnp.einsum('hqe,hse->hqs', q, k_new, preferred_element_type=F32, precision=HIGHEST) + bn_ref[gi]
        mx = jnp.maximum(jnp.max(sw, axis=-1, keepdims=True), jnp.max(sn, axis=-1, keepdims=True))
        pw = jnp.exp(sw - mx)
        pn = jnp.exp(sn - mx)
        den = jnp.sum(pw, axis=-1, keepdims=True) + jnp.sum(pn, axis=-1, keepdims=True)
        o = (jnp.einsum('hqk,hek->hqe', pw.astype(BF16), vt, preferred_element_type=F32)
             + jnp.einsum('hqs,hse->hqe', pn, v_new, preferred_element_type=F32, precision=HIGHEST)) / den
        lse = mx + jnp.log(den)
        for h in range(H_G):
            o_ref[0, :, gi * GW + h * HD:gi * GW + (h + 1) * HD] = o[h, 0:s_new]
            lse_ref[0, :, gi * LANES + h:gi * LANES + h + 1] = lse[h, 0:s_new]


def _attn_sample(att, caches):
    b, s_new, c = att.shape
    assert s_new <= min(SUBLANES, ATT_DILS[1])
    new = jnp.pad(att, ((0, 0), (0, SUBLANES - s_new), (0, 0)))
    bufs, specs = [], []
    for dil, cache in zip(ATT_DILS, caches):
        lw = cache.shape[1]
        assert lw == dil * STEPS
        bufs.append(jnp.transpose(cache, (0, 2, 3, 4, 1)))
        specs.append(pl.BlockSpec((1, 2, H_G, HD, lw), lambda bi: (bi, 0, 0, 0, 0)))
    bias_cached, bias_new = _sample_bias_tables(s_new)
    ng = len(ATT_DILS)
    o, lse = pl.pallas_call(
        functools.partial(_attn_sample_body, s_new=s_new),
        grid=(b,),
        in_specs=[pl.BlockSpec((1, SUBLANES, c), lambda bi: (bi, 0, 0))] + specs
        + [pl.BlockSpec(bc.shape, lambda bi: (0, 0, 0)) for bc in bias_cached]
        + [pl.BlockSpec(bias_new.shape, lambda bi: (0, 0, 0, 0))],
        out_specs=[pl.BlockSpec((1, s_new, D_ATT), lambda bi: (bi, 0, 0)),
                   pl.BlockSpec((1, s_new, ng * LANES), lambda bi: (bi, 0, 0))],
        out_shape=[jax.ShapeDtypeStruct((b, s_new, D_ATT), F32),
                   jax.ShapeDtypeStruct((b, s_new, ng * LANES), F32)],
        compiler_params=pltpu.CompilerParams(dimension_semantics=("parallel",), vmem_limit_bytes=VMEM_LIMIT),
        name="attn_sample",
    )(new, *bufs, *bias_cached, bias_new)
    return o.reshape(b * s_new, D_ATT), lse.reshape(b * s_new, ng * LANES)


HIST = SUBLANES


def _delta_prep_body(u_ref, prev_ref, small_ref, cbuf_ref, cw_ref, alog_ref, dtb_ref,
                     uo_ref, wq_ref, ak_ref, eg_ref, ext, *, t_valid, cp):
    c = pl.program_id(1)
    rows = cp * CHUNK

    @pl.when(c == 0)
    def _():
        ext[0:HIST, :] = cbuf_ref[0]

    @pl.when(c > 0)
    def _():
        ext[0:HIST, :] = prev_ref[0]

    ext[HIST:HIST + rows, :] = u_ref[0]
    conv = ext[HIST - 3:HIST - 3 + rows, :] * cw_ref[0:1, :]
    for i in range(1, CONV_W):
        conv = conv + ext[HIST - 3 + i:HIST - 3 + i + rows, :] * cw_ref[i:i + 1, :]
    act = conv * _sigmoid(conv)

    sm = small_ref[0]
    bmat = _sigmoid(sm)
    xg = sm + dtb_ref[...]
    gmat = -jnp.exp(alog_ref[...]) * (jnp.maximum(xg, 0.0) + jnp.log1p(jnp.exp(-jnp.abs(xg))))
    if t_valid < rows:
        live = lax.broadcasted_iota(jnp.int32, (rows, 1), 0) < t_valid
        act = jnp.where(live, act, 0.0)
        bmat = jnp.where(live, bmat, 0.0)
        gmat = jnp.where(live, gmat, 0.0)

    ri = lax.broadcasted_iota(jnp.int32, (CHUNK, CHUNK), 0)
    ci = lax.broadcasted_iota(jnp.int32, (CHUNK, CHUNK), 1)
    tril = ri >= ci
    strict = ri > ci

    pairs = [(j, h) for j in range(cp) for h in range(H_D)]
    rsl = lambda j: slice(j * CHUNK, (j + 1) * CHUNK)
    gcs, gcts = [], []
    for j in range(cp):
        gc_j = jnp.dot(tril.astype(F32), gmat[rsl(j)], preferred_element_type=F32, precision=HIGHEST)
        gcs.append(gc_j)
        gcts.append(gc_j.T)
    q = jnp.stack([act[rsl(j), h * DK:(h + 1) * DK] for j, h in pairs])
    k = jnp.stack([act[rsl(j), D_DK + h * DK:D_DK + (h + 1) * DK] for j, h in pairs])
    v = jnp.stack([act[rsl(j), 2 * D_DK + h * DV:2 * D_DK + (h + 1) * DV] for j, h in pairs])
    beta = jnp.stack([bmat[rsl(j), h:h + 1] for j, h in pairs])
    gcol = jnp.stack([gcs[j][:, H_D + h:H_D + h + 1] for j, h in pairs])
    grow = jnp.stack([gcts[j][H_D + h:H_D + h + 1, :] for j, h in pairs])
    glast = jnp.stack([gcs[j][CHUNK - 1:CHUNK, H_D + h:H_D + h + 1] for j, h in pairs])

    qn = q * lax.rsqrt(jnp.sum(q * q, axis=-1, keepdims=True) + NORM_EPS) * (DK ** -0.5)
    kn = k * lax.rsqrt(jnp.sum(k * k, axis=-1, keepdims=True) + NORM_EPS)
    decay = jnp.where(tril, jnp.exp(jnp.where(tril, gcol - grow, 0.0)), 0.0)
    eg = jnp.exp(gcol)
    kb = kn * beta
    kn_b = kn.astype(BF16)
    a_kk = jnp.einsum('bik,bjk->bij', kb.astype(BF16), kn_b, preferred_element_type=F32)
    a_qk = jnp.einsum('bik,bjk->bij', qn.astype(BF16), kn_b, preferred_element_type=F32)
    pw = jnp.where(strict, -(a_kk * decay), 0.0)
    tm1 = pw
    for _ in range(int(math.log2(CHUNK)) - 1):
        pw_b = pw.astype(BF16)
        pw = jnp.einsum('bij,bjk->bik', pw_b, pw_b, preferred_element_type=F32)
        tm1 = tm1 + pw + jnp.einsum('bij,bjk->bik', tm1.astype(BF16), pw.astype(BF16), preferred_element_type=F32)
    vb = v * beta
    kbg = kb * eg
    tm1_b = tm1.astype(BF16)
    u = vb + jnp.einsum('bij,bjk->bik', tm1_b, vb.astype(BF16), preferred_element_type=F32)
    w = (kbg + jnp.einsum('bij,bjk->bik', tm1_b, kbg.astype(BF16), preferred_element_type=F32)).astype(BF16)
    qd = (qn * eg).astype(BF16)
    aqd = (a_qk * decay).astype(BF16)
    kd = kn * jnp.exp(glast - gcol)
    egl = jnp.exp(glast)
    for i, (j, h) in enumerate(pairs):
        uo_ref[0, rsl(j), h * DV:(h + 1) * DV] = u[i]
        wq_ref[0, j, h, 0:CHUNK, :] = w[i]
        wq_ref[0, j, h, CHUNK:2 * CHUNK, :] = qd[i]
        ak_ref[0, j, h, 0:CHUNK, :] = aqd[i]
        ak_ref[0, j, h, CHUNK:CHUNK + DK, :] = kd[i].T.astype(BF16)
        eg_ref[0, j, h:h + 1, :] = jnp.broadcast_to(egl[i], (1, LANES))
    for j in range(cp):
        eg_ref[0, j, H_D:SUBLANES, :] = jnp.zeros((SUBLANES - H_D, LANES), F32)


def _delta_rec_body(u_ref, wq_ref, ak_ref, eg_ref, z_ref, s0_ref, nw_ref, o_ref, sout_ref, S, *, bb, nc):
    c = pl.program_id(1)

    @pl.when(c == 0)
    def _():
        S[...] = s0_ref[...]

    pairs = [(b, h) for b in range(bb) for h in range(H_D)]
    n = len(pairs)
    cols = lambda h: slice(h * DV, (h + 1) * DV)
    s_all = S[...].reshape(n, DK, DV)
    x = jnp.einsum('nij,njk->nik', wq_ref[:, 0].reshape(n, 2 * CHUNK, DK), s_all.astype(BF16),
                   preferred_element_type=F32)
    u = jnp.stack([u_ref[b, :, cols(h)] for b, h in pairs])
    vn = u - x[:, 0:CHUNK]
    y = jnp.einsum('nij,njk->nik', ak_ref[:, 0].reshape(n, CHUNK + DK, CHUNK), vn.astype(BF16),
                   preferred_element_type=F32)
    o = x[:, CHUNK:2 * CHUNK] + y[:, 0:CHUNK]
    eg = jnp.stack([eg_ref[b, 0, h:h + 1, :] for b, h in pairs])
    S[...] = (s_all * eg + y[:, CHUNK:CHUNK + DK]).reshape(bb, H_D, DK, DV)
    od = o * lax.rsqrt(jnp.mean(o * o, axis=-1, keepdims=True) + NORM_EPS) * nw_ref[...]
    z = jnp.stack([z_ref[b, :, cols(h)] for b, h in pairs])
    out = od * (z * _sigmoid(z))
    for i, (b, h) in enumerate(pairs):
        o_ref[b, :, cols(h)] = out[i]

    @pl.when(c == nc - 1)
    def _():
        sout_ref[...] = S[...]


def _lane_row(vals, offset):
    row = jnp.zeros((1, LANES), F32)
    return row.at[0, offset:offset + vals.shape[0]].set(vals.astype(F32))


def _delta_net(dqkv, small, z, conv_buf, s0, conv_w, a_log, dt_bias, delta_norm_w):
    b, t, _ = dqkv.shape
    tp = -(-t // CHUNK) * CHUNK
    nc = tp // CHUNK
    assert tp == t or nc == 1
    if tp != t:
        pad = ((0, 0), (0, tp - t), (0, 0))
        dqkv, small, z = jnp.pad(dqkv, pad), jnp.pad(small, pad), jnp.pad(z, pad)
    cbuf = jnp.pad(conv_buf, ((0, 0), (HIST - (CONV_W - 1), 0), (0, 0)))
    cw = jnp.pad(conv_w, ((0, SUBLANES - CONV_W), (0, 0)))
    cp = next(c for c in (8, 4, 2, 1) if nc % c == 0)
    rows = cp * CHUNK
    full2 = lambda shape: pl.BlockSpec(shape, lambda bi, ci: (0,) * len(shape))
    u, wq, ak, eg = pl.pallas_call(
        functools.partial(_delta_prep_body, t_valid=t if nc == 1 else rows, cp=cp),
        grid=(b, nc // cp),
        in_specs=[pl.BlockSpec((1, rows, D_CONV), lambda bi, ci: (bi, ci, 0)),
                  pl.BlockSpec((1, HIST, D_CONV), lambda bi, ci: (bi, jnp.maximum(ci * (rows // HIST) - 1, 0), 0)),
                  pl.BlockSpec((1, rows, LANES), lambda bi, ci: (bi, ci, 0)),
                  pl.BlockSpec((1, HIST, D_CONV), lambda bi, ci: (bi, 0, 0)),
                  full2((SUBLANES, D_CONV)), full2((1, LANES)), full2((1, LANES))],
        out_specs=[pl.BlockSpec((1, rows, D_DV), lambda bi, ci: (bi, ci, 0)),
                   pl.BlockSpec((1, cp, H_D, 2 * CHUNK, DK), lambda bi, ci: (bi, ci, 0, 0, 0)),
                   pl.BlockSpec((1, cp, H_D, CHUNK + DK, CHUNK), lambda bi, ci: (bi, ci, 0, 0, 0)),
                   pl.BlockSpec((1, cp, SUBLANES, LANES), lambda bi, ci: (bi, ci, 0, 0))],
        out_shape=[jax.ShapeDtypeStruct((b, tp, D_DV), F32),
                   jax.ShapeDtypeStruct((b, nc, H_D, 2 * CHUNK, DK), BF16),
                   jax.ShapeDtypeStruct((b, nc, H_D, CHUNK + DK, CHUNK), BF16),
                   jax.ShapeDtypeStruct((b, nc, SUBLANES, LANES), F32)],
        scratch_shapes=[pltpu.VMEM((HIST + rows, D_CONV), F32)],
        compiler_params=pltpu.CompilerParams(dimension_semantics=("parallel", "parallel"),
                                             vmem_limit_bytes=VMEM_LIMIT),
        name="delta_prep",
    )(dqkv, dqkv, small, cbuf, cw, _lane_row(a_log, H_D), _lane_row(dt_bias, H_D))

    bb = next(c for c in (8, 4, 2, 1) if b % c == 0)
    o, s_new = pl.pallas_call(
        functools.partial(_delta_rec_body, bb=bb, nc=nc),
        grid=(b // bb, nc),
        in_specs=[pl.BlockSpec((bb, CHUNK, D_DV), lambda bi, ci: (bi, ci, 0)),
                  pl.BlockSpec((bb, 1, H_D, 2 * CHUNK, DK), lambda bi, ci: (bi, ci, 0, 0, 0)),
                  pl.BlockSpec((bb, 1, H_D, CHUNK + DK, CHUNK), lambda bi, ci: (bi, ci, 0, 0, 0)),
                  pl.BlockSpec((bb, 1, SUBLANES, LANES), lambda bi, ci: (bi, ci, 0, 0)),
                  pl.BlockSpec((bb, CHUNK, D_DV), lambda bi, ci: (bi, ci, 0)),
                  pl.BlockSpec((bb, H_D, DK, DV), lambda bi, ci: (bi, 0, 0, 0)),
                  full2((1, DV))],
        out_specs=[pl.BlockSpec((bb, CHUNK, D_DV), lambda bi, ci: (bi, ci, 0)),
                   pl.BlockSpec((bb, H_D, DK, DV), lambda bi, ci: (bi, 0, 0, 0))],
        out_shape=[jax.ShapeDtypeStruct((b, tp, D_DV), F32), jax.ShapeDtypeStruct((b, H_D, DK, DV), F32)],
        scratch_shapes=[pltpu.VMEM((bb, H_D, DK, DV), F32)],
        compiler_params=pltpu.CompilerParams(dimension_semantics=("parallel", "arbitrary"),
                                             vmem_limit_bytes=VMEM_LIMIT),
        name="delta_recurrence",
    )(u, wq, ak, eg, z, s0, delta_norm_w.reshape(1, DV).astype(F32))
    return o[:, :t].reshape(b * t, D_DV), s_new


def _lane_expand(cols, width):
    tm = cols[0].shape[0]
    seg = lax.broadcasted_iota(jnp.int32, (tm, len(cols) * width), 1) // width
    out = jnp.broadcast_to(cols[-1], seg.shape)
    for i in range(len(cols) - 2, -1, -1):
        out = jnp.where(seg == i, cols[i], out)
    return out


def _token_order(ref, scr):
    d, nc = ref.shape[1], ref.shape[3] // LANES
    if d == 1:
        return ref[0, 0]
    for r in range(d):
        for c in range(nc):
            scr[c, pl.ds(r, ref.shape[2], stride=d), :] = ref[0, r, :, c * LANES:(c + 1) * LANES]
    return jnp.concatenate([scr[c] for c in range(nc)], axis=-1)


def _mix_body(x_ref, gate_ref, o0_ref, o1_ref, o2_ref, l0_ref, l1_ref, l2_ref, od_ref,
              wa_ref, wd_ref, wo_ref, bg_ref, g1_ref, b1_ref, rw_ref, rb_ref,
              h_ref, ti_ref, tg_ref, cnt_ref, *scratch):
    o_scr, l_scr = scratch[:3], scratch[3:]
    lses = [_token_order(l, s) for l, s in zip((l0_ref, l1_ref, l2_ref), l_scr)]
    mx = jnp.maximum(jnp.maximum(lses[0], lses[1]), lses[2])
    es = [jnp.exp(l - mx) for l in lses]
    inv = 1.0 / (es[0] + es[1] + es[2])
    o_att = None
    for e_g, o_g, s in zip(es, (o0_ref, o1_ref, o2_ref), o_scr):
        wgt = e_g * inv
        term = _lane_expand([wgt[:, h:h + 1] for h in range(H_G)], HD) * _token_order(o_g, s)
        o_att = term if o_att is None else o_att + term
    ga = _sigmoid(gate_ref[:, :D_MODEL] + bg_ref[:, :D_MODEL])
    gd = _sigmoid(gate_ref[:, D_MODEL:] + bg_ref[:, D_MODEL:])
    merged = (ga * jnp.dot(o_att.astype(BF16), wa_ref[...], preferred_element_type=F32)
              + gd * jnp.dot(od_ref[...].astype(BF16), wd_ref[...], preferred_element_type=F32))
    mix = jnp.dot(merged.astype(BF16), wo_ref[...], preferred_element_type=F32)
    h = _layer_norm(DN_ALPHA * x_ref[...] + mix, g1_ref[...], b1_ref[...])
    _rows_to_tiles(h_ref, h)

    h_hi = h.astype(BF16)
    h_lo = (h - h_hi.astype(F32)).astype(BF16)
    part = jnp.dot(h_hi, rw_ref[...], preferred_element_type=F32)
    logits = (part[:, :LANES] + part[:, LANES:]
              + jnp.dot(h_lo, rw_ref[:, :LANES], preferred_element_type=F32) + rb_ref[...])
    lane = lax.broadcasted_iota(jnp.int32, logits.shape, 1)
    lane_f = lane.astype(F32)
    ti = jnp.zeros(logits.shape, F32)
    tv = jnp.zeros(logits.shape, F32)
    hits = jnp.zeros(logits.shape, F32)
    top = None
    for kk in range(TOP_K):
        m = jnp.max(logits, axis=-1, keepdims=True)
        idx = jnp.min(jnp.where(logits == m, lane_f, float(LANES)), axis=-1, keepdims=True)
        if top is None:
            top = m
        ti = jnp.where(lane == kk, idx, ti)
        tv = jnp.where(lane == kk, jnp.exp(m - top), tv)
        chosen = lane_f == idx
        hits = jnp.where(chosen, 1.0, hits)
        logits = jnp.where(chosen, -jnp.inf, logits)
    ti_ref[...] = ti.astype(jnp.int32)
    tg_ref[...] = tv / jnp.sum(tv, axis=-1, keepdims=True)

    @pl.when(pl.program_id(0) == 0)
    def _():
        cnt_ref[...] = jnp.zeros_like(cnt_ref)

    cnt_ref[...] += jnp.sum(hits, axis=0, keepdims=True)


def _mix_and_route(x2d, gates, o_groups, lse_groups, o_del, wa, wd, wo, b_gate, ln_g, ln_b, rw, rb, tm):
    n = x2d.shape[0]
    tpb = o_groups[0].shape[1] * o_groups[0].shape[2] // tm
    row = lambda c: pl.BlockSpec((tm, c), lambda i: (i, 0))
    full = lambda a: pl.BlockSpec(a.shape, lambda i: (0, 0))

    def res(a):
        d = a.shape[1]
        return pl.BlockSpec((1, d, tm // d, a.shape[3]), lambda i: (i // tpb, 0, i % tpb, 0))

    consts = (wa, wd, wo, b_gate, ln_g, ln_b, rw, rb)
    return pl.pallas_call(
        _mix_body,
        grid=(n // tm,),
        in_specs=[row(D_MODEL), row(2 * D_MODEL)] + [res(a) for a in o_groups] + [res(a) for a in lse_groups]
        + [row(D_DV)] + [full(a) for a in consts],
        out_specs=[pl.BlockSpec((tm * TOK_ROWS, LANES), lambda i: (i, 0)), row(LANES), row(LANES),
                   pl.BlockSpec((1, LANES), lambda i: (0, 0))],
        out_shape=[jax.ShapeDtypeStruct((n * TOK_ROWS, LANES), F32), jax.ShapeDtypeStruct((n, LANES), jnp.int32),
                   jax.ShapeDtypeStruct((n, LANES), F32), jax.ShapeDtypeStruct((1, LANES), F32)],
        scratch_shapes=[pltpu.VMEM((GW // LANES, tm, LANES), F32)] * len(o_groups)
        + [pltpu.VMEM((1, tm, LANES), F32)] * len(lse_groups),
        compiler_params=pltpu.CompilerParams(dimension_semantics=("arbitrary",), vmem_limit_bytes=VMEM_LIMIT),
        name="mix_and_route",
    )(x2d, gates, *o_groups, *lse_groups, o_del, *consts)


def _rank_body(ti_ref, base_ref, dest_ref, carry):
    i = pl.program_id(0)

    @pl.when(i == 0)
    def _():
        carry[...] = base_ref[...]

    ti = ti_ref[...]
    tm = ti.shape[0]
    lane = lax.broadcasted_iota(jnp.int32, ti.shape, 1)
    sel = [lane == ti[:, kk:kk + 1] for kk in range(TOP_K)]
    hit = sel[0]
    for kk in range(1, TOP_K):
        hit = hit | sel[kk]
    cnt = hit.astype(F32)
    ri = lax.broadcasted_iota(jnp.int32, (tm, tm), 0)
    ci = lax.broadcasted_iota(jnp.int32, (tm, tm), 1)
    incl = jnp.dot((ri >= ci).astype(BF16), cnt.astype(BF16), preferred_element_type=F32)
    before = incl - cnt + carry[...]
    slot = jnp.zeros(ti.shape, F32)
    for kk in range(TOP_K):
        r = jnp.sum(jnp.where(sel[kk], before, 0.0), axis=-1, keepdims=True)
        slot = jnp.where(lane == kk, r, slot)
    dest_ref[...] = slot.T[0:SUBLANES].astype(jnp.int32)
    carry[...] = carry[...] + incl[tm - 1:tm, :]


def _route_slots(top_i, base, tm):
    n = top_i.shape[0]
    return pl.pallas_call(
        _rank_body,
        grid=(n // tm,),
        in_specs=[pl.BlockSpec((tm, LANES), lambda i: (i, 0)), pl.BlockSpec((1, LANES), lambda i: (0, 0))],
        out_specs=pl.BlockSpec((SUBLANES, tm), lambda i: (i, 0)),
        out_shape=jax.ShapeDtypeStruct((n // tm * SUBLANES, tm), jnp.int32),
        scratch_shapes=[pltpu.VMEM((1, LANES), F32)],
        compiler_params=pltpu.CompilerParams(dimension_semantics=("arbitrary",), vmem_limit_bytes=VMEM_LIMIT),
        name="route_slots",
    )(top_i, base)


def _rows_to_tiles(ref, val):
    m = val.shape[0]
    for s in range(TOK_ROWS):
        ref[pl.ds(s, m, stride=TOK_ROWS), :] = val[:, s * LANES:(s + 1) * LANES]


def _tiles_to_rows(ref, m):
    return jnp.concatenate([ref[pl.ds(s, m, stride=TOK_ROWS), :] for s in range(TOK_ROWS)], axis=-1)


def _dispatch_body(pend_ref, padded_ref, dest_ref, h_ref, *refs, max_tail):
    tm = h_ref.shape[0] // TOK_ROWS
    blk = MOE_BM * TOK_ROWS
    first_group = len(refs) == 4
    if first_group:
        xs_out, zero, sem, zsem = refs
    else:
        _, xs_out, sem = refs

    def zero_fill():
        zero[...] = jnp.zeros_like(zero)

        def last_block(e):
            start = pl.multiple_of((pend_ref[e] - MOE_BM) * TOK_ROWS, blk)
            return pltpu.make_async_copy(zero, xs_out.at[pl.ds(start, blk)], zsem)

        def tail_block(j):
            start = pl.multiple_of((pend_ref[N_EXP - 1] + j * MOE_BM) * TOK_ROWS, blk)
            return pltpu.make_async_copy(zero, xs_out.at[pl.ds(start, blk)], zsem)

        n_slot = xs_out.shape[0] // TOK_ROWS
        for e in range(N_EXP):
            @pl.when(padded_ref[e] > 0)
            def _():
                last_block(e).start()
        for j in range(max_tail):
            @pl.when(pend_ref[N_EXP - 1] + j * MOE_BM < n_slot)
            def _():
                tail_block(j).start()
        for e in range(N_EXP):
            @pl.when(padded_ref[e] > 0)
            def _():
                last_block(e).wait()
        for j in range(max_tail):
            @pl.when(pend_ref[N_EXP - 1] + j * MOE_BM < n_slot)
            def _():
                tail_block(j).wait()

    if first_group:
        pl.when(pl.program_id(0) == 0)(zero_fill)

    def issue(t, carry):
        src = h_ref.at[pl.ds(pl.multiple_of(t * TOK_ROWS, TOK_ROWS), TOK_ROWS)]
        for kk in range(TOP_K):
            d = pl.multiple_of(dest_ref[kk * tm + t] * TOK_ROWS, TOK_ROWS)
            pltpu.make_async_copy(src, xs_out.at[pl.ds(d, TOK_ROWS)], sem).start(priority=kk % 2)
        return carry

    lax.fori_loop(0, tm, issue, 0, unroll=8)
    for kk in range(TOP_K):
        pltpu.make_async_copy(h_ref, xs_out.at[pl.ds(0, tm * TOK_ROWS)], sem).wait()


def _dispatch(h_tiles, dest_flat, pad_end, padded, n_slot, tm, xs_prev=None):
    n = h_tiles.shape[0] // TOK_ROWS
    first_group = xs_prev is None
    in_specs = [pl.BlockSpec((SUBLANES * tm,), lambda i, pe, pd: (i,), memory_space=pltpu.SMEM),
                pl.BlockSpec((tm * TOK_ROWS, LANES), lambda i, pe, pd: (i, 0))]
    scratch = [pltpu.SemaphoreType.DMA(())]
    if first_group:
        scratch = [pltpu.VMEM((MOE_BM * TOK_ROWS, LANES), F32)] + scratch + [pltpu.SemaphoreType.DMA(())]
    else:
        in_specs.append(pl.BlockSpec(memory_space=pl.ANY))
    grid_spec = pltpu.PrefetchScalarGridSpec(
        num_scalar_prefetch=2, grid=(n // tm,), in_specs=in_specs,
        out_specs=pl.BlockSpec(memory_space=pl.ANY), scratch_shapes=scratch)
    max_tail = n_slot // MOE_BM - (n * TOP_K) // MOE_BM
    return pl.pallas_call(
        functools.partial(_dispatch_body, max_tail=max_tail),
        grid_spec=grid_spec,
        out_shape=jax.ShapeDtypeStruct((n_slot * TOK_ROWS, LANES), F32),
        input_output_aliases={} if first_group else {4: 0},
        compiler_params=pltpu.CompilerParams(dimension_semantics=("arbitrary",), vmem_limit_bytes=VMEM_LIMIT,
                                             has_side_effects=True, disable_bounds_checks=True),
        name="moe_dispatch",
    )(pad_end, padded, dest_flat, h_tiles, *(() if first_group else (xs_prev,)))


def _expert_body(be_ref, nu_ref, nxt_ref, par_ref, x_ref, bgu_ref, bd_ref, wgu_hbm, wd_hbm, o_ref,
                 wgu_f32, wd_f32, wgu_bf, wd_bf, sem):
    i = pl.program_id(0)
    e = be_ref[i]
    prev = be_ref[jnp.maximum(i - 1, 0)]
    live = i < nu_ref[0]

    def fetch(expert, slot):
        return (pltpu.make_async_copy(wgu_hbm.at[expert], wgu_f32.at[slot], sem.at[0, slot]),
                pltpu.make_async_copy(wd_hbm.at[expert], wd_f32.at[slot], sem.at[1, slot]))

    @pl.when(live & ((i == 0) | (e != prev)))
    def _():
        slot = par_ref[i]

        @pl.when(i == 0)
        def _():
            for cp in fetch(e, slot):
                cp.start()

        for cp in fetch(e, slot):
            cp.wait()
        wgu_bf[...] = wgu_f32[slot].astype(BF16)
        wd_bf[...] = wd_f32[slot].astype(BF16)

        @pl.when(nxt_ref[i] >= 0)
        def _():
            for cp in fetch(nxt_ref[i], 1 - slot):
                cp.start()

    @pl.when(live)
    def _():
        x = _tiles_to_rows(x_ref, MOE_BM)
        gu = jnp.dot(x.astype(BF16), wgu_bf[...], preferred_element_type=F32) + bgu_ref[0]
        gt = jnp.minimum(gu[:, :D_FF], SWIGLU_LIMIT)
        up = jnp.clip(gu[:, D_FF:], -SWIGLU_LIMIT, SWIGLU_LIMIT)
        act = (up + 1.0) * gt * _sigmoid(SWIGLU_ALPHA * gt)
        _rows_to_tiles(o_ref, jnp.dot(act.astype(BF16), wd_bf[...], preferred_element_type=F32) + bd_ref[0])

    @pl.when(i >= nu_ref[0])
    def _():
        o_ref[...] = jnp.zeros_like(o_ref)


def _experts(xs, blk_exp, n_used, padded, w_gu, b_gu, w_down, b_down):
    n_slot = xs.shape[0] // TOK_ROWS
    n_blk = n_slot // MOE_BM
    ids = jnp.arange(N_EXP, dtype=jnp.int32)
    has = padded > 0
    later = jnp.where(has[None, :] & (ids[None, :] > ids[:, None]), ids[None, :], N_EXP)
    nxt_e = jnp.min(later, axis=1)
    nxt_e = jnp.where(nxt_e < N_EXP, nxt_e, -1).astype(jnp.int32)
    par_e = ((jnp.cumsum(has.astype(jnp.int32)) - 1) % 2).astype(jnp.int32)
    rows = lambda i, be, nu, nx, pa: (jnp.minimum(i, nu[0] - 1), 0)
    grid_spec = pltpu.PrefetchScalarGridSpec(
        num_scalar_prefetch=4,
        grid=(n_blk,),
        in_specs=[pl.BlockSpec((MOE_BM * TOK_ROWS, LANES), rows),
                  pl.BlockSpec((1, 1, 2 * D_FF), lambda i, be, nu, nx, pa: (be[i], 0, 0)),
                  pl.BlockSpec((1, 1, D_MODEL), lambda i, be, nu, nx, pa: (be[i], 0, 0)),
                  pl.BlockSpec(memory_space=pl.ANY), pl.BlockSpec(memory_space=pl.ANY)],
        out_specs=pl.BlockSpec((MOE_BM * TOK_ROWS, LANES), lambda i, be, nu, nx, pa: (i, 0)),
        scratch_shapes=[pltpu.VMEM((2, D_MODEL, 2 * D_FF), F32), pltpu.VMEM((2, D_FF, D_MODEL), F32),
                        pltpu.VMEM((D_MODEL, 2 * D_FF), BF16), pltpu.VMEM((D_FF, D_MODEL), BF16),
                        pltpu.SemaphoreType.DMA((2, 2))],
    )
    return pl.pallas_call(
        _expert_body,
        grid_spec=grid_spec,
        out_shape=jax.ShapeDtypeStruct((n_slot * TOK_ROWS, LANES), F32),
        compiler_params=pltpu.CompilerParams(dimension_semantics=("arbitrary",), vmem_limit_bytes=VMEM_LIMIT),
        name="moe_experts",
    )(blk_exp, n_used, nxt_e[blk_exp], par_e[blk_exp], xs, b_gu.reshape(N_EXP, 1, 2 * D_FF),
      b_down.reshape(N_EXP, 1, D_MODEL), w_gu, w_down)


def _combine_body(dest_ref, dest_next_ref, gate_ref, h_ref, g_ref, b_ref, ys_ref, y_ref, buf, sem):
    tm = h_ref.shape[0] // TOK_ROWS
    i = pl.program_id(0)
    slot = i % 2

    def start_tile(idx_ref, s):
        def issue(t, carry):
            row = pl.multiple_of(t * TOK_ROWS, TOK_ROWS)
            for kk in range(TOP_K):
                d = pl.multiple_of(idx_ref[kk * tm + t] * TOK_ROWS, TOK_ROWS)
                pltpu.make_async_copy(ys_ref.at[pl.ds(d, TOK_ROWS)], buf.at[s, kk, pl.ds(row, TOK_ROWS)],
                                      sem.at[s]).start(priority=kk % 2)
            return carry

        lax.fori_loop(0, tm, issue, 0, unroll=8)

    @pl.when(i == 0)
    def _():
        start_tile(dest_ref, 0)

    @pl.when(i + 1 < pl.num_programs(0))
    def _():
        start_tile(dest_next_ref, 1 - slot)

    for kk in range(TOP_K):
        pltpu.make_async_copy(ys_ref.at[pl.ds(0, tm * TOK_ROWS)], buf.at[slot, kk], sem.at[slot]).wait()
    gate = gate_ref[...]
    moe = gate[:, 0:1] * _tiles_to_rows(buf.at[slot, 0], tm)
    for kk in range(1, TOP_K):
        moe = moe + gate[:, kk:kk + 1] * _tiles_to_rows(buf.at[slot, kk], tm)
    y_ref[...] = _layer_norm(DN_ALPHA * _tiles_to_rows(h_ref, tm) + moe, g_ref[...], b_ref[...])


def _combine(ys, dest_flat, gate, h_tiles, ln_g, ln_b, tm):
    n = h_tiles.shape[0] // TOK_ROWS
    return pl.pallas_call(
        _combine_body,
        grid=(n // tm,),
        in_specs=[pl.BlockSpec((SUBLANES * tm,), lambda i: (i,), memory_space=pltpu.SMEM),
                  pl.BlockSpec((SUBLANES * tm,), lambda i: (jnp.minimum(i + 1, n // tm - 1),), memory_space=pltpu.SMEM),
                  pl.BlockSpec((tm, LANES), lambda i: (i, 0)),
                  pl.BlockSpec((tm * TOK_ROWS, LANES), lambda i: (i, 0)),
                  pl.BlockSpec((1, D_MODEL), lambda i: (0, 0)),
                  pl.BlockSpec((1, D_MODEL), lambda i: (0, 0)),
                  pl.BlockSpec(memory_space=pl.ANY)],
        out_specs=pl.BlockSpec((tm, D_MODEL), lambda i: (i, 0)),
        out_shape=jax.ShapeDtypeStruct((n, D_MODEL), F32),
        scratch_shapes=[pltpu.VMEM((2, TOP_K, tm * TOK_ROWS, LANES), F32), pltpu.SemaphoreType.DMA((2,))],
        compiler_params=pltpu.CompilerParams(dimension_semantics=("arbitrary",), vmem_limit_bytes=VMEM_LIMIT,
                                             disable_bounds_checks=True),
        name="moe_combine",
    )(dest_flat, dest_flat, gate, h_tiles, ln_g, ln_b, ys)


def _moe(groups, w_gu, b_gu, w_down, b_down, ln_g, ln_b):
    counts = [g[3][0, :N_EXP].astype(jnp.int32) for g in groups]
    total = sum(counts)
    padded = (total + MOE_BM - 1) // MOE_BM * MOE_BM
    pad_end = jnp.cumsum(padded).astype(jnp.int32)
    n_assign = sum(g[1].shape[0] for g in groups) * TOP_K
    n_blk = -(-(n_assign + N_EXP * (MOE_BM - 1)) // MOE_BM)
    blk_row0 = jnp.arange(n_blk, dtype=jnp.int32) * MOE_BM
    blk_exp = jnp.minimum(jnp.sum((pad_end[None, :] <= blk_row0[:, None]).astype(jnp.int32), axis=1), N_EXP - 1)
    n_used = pad_end[-1:] // MOE_BM
    base = pad_end - padded
    xs, dests = None, []
    for (h, top_i, _, _, tm), cnt in zip(groups, counts):
        base_row = jnp.zeros((1, LANES), F32).at[0, :N_EXP].set(base.astype(F32))
        dest = _route_slots(top_i, base_row, tm).reshape(-1)
        dests.append(dest)
        xs = _dispatch(h, dest, pad_end, padded, n_blk * MOE_BM, tm, xs)
        base = base + cnt
    ys = _experts(xs, blk_exp, n_used, padded, w_gu, b_gu, w_down, b_down)
    return [_combine(ys, dest, top_g, h, ln_g, ln_b, tm) for (h, _, top_g, _, tm), dest in zip(groups, dests)]


def _pad_cols(a, width, fill=0.0):
    return jnp.pad(a, ((0, 0), (0, width - a.shape[1])), constant_values=fill)


def _token_mixers_and_route(x, caches, conv_buf, s0, w_in, b_gate, conv_w, a_log, dt_bias, delta_norm_w,
                            w_branch_attn, w_branch_delta, w_out, ln1_g, ln1_b, router_w, router_b):
    b, t, _ = x.shape
    n = b * t
    tm = min(TOK_TM, n)
    x2d = x.reshape(n, D_MODEL)

    c_att, c_dz, c_small = 3 * D_ATT, 3 * D_ATT + D_CONV, 3 * D_ATT + D_CONV + D_DV
    c_gate = c_small + 2 * H_D
    ng = len(ATT_DILS)
    w_groups = [jnp.concatenate([w_in[:, part * D_ATT + gi * GW:part * D_ATT + (gi + 1) * GW] for part in range(3)],
                                axis=1) for gi in range(ng)]
    ws = w_groups + [w_in[:, c_att:c_dz], w_in[:, c_dz:c_small],
                     _pad_cols(w_in[:, c_small:c_gate], LANES), w_in[:, c_gate:]]
    dils = (ATT_DILS if caches is None else (1,) * ng) + (1, 1, 1, 1)
    outs = _in_projection(x2d, [w.astype(BF16) for w in ws], dils, t, min(IN_TM, n))
    att_groups, (dqkv, z, small, gates) = outs[:ng], outs[ng:]
    dqkv3 = dqkv.reshape(b, t, D_CONV)

    kv_new = []
    if caches is None:
        o_groups, lse_groups = [], []
        for dil, a in zip(ATT_DILS, att_groups):
            a = a.reshape(b, dil, t // dil, 3 * GW)
            o_g, lse_g = _attn_prompt(a, dil)
            o_groups.append(o_g)
            lse_groups.append(lse_g)
            win = min(dil * STEPS, t)
            last = a[:, :, (t - win) // dil:, GW:]
            last = jnp.transpose(last, (0, 2, 1, 3)).reshape(b, win, 2, H_G, HD)
            kv_new.append(last)
        conv_buf = jnp.zeros((b, CONV_W - 1, D_CONV), F32)
        s0 = jnp.zeros((b, H_D, DK, DV), F32)
    else:
        att3 = jnp.concatenate(att_groups, axis=1).reshape(b, t, 3 * D_ATT)
        o_all, lse_all = _attn_sample(att3, caches)
        o_groups = [o_all[:, gi * GW:(gi + 1) * GW].reshape(1, 1, n, GW) for gi in range(ng)]
        lse_groups = [lse_all[:, gi * LANES:(gi + 1) * LANES].reshape(1, 1, n, LANES) for gi in range(ng)]
        for a in att_groups:
            kv_new.append(a[:, GW:].reshape(b, t, 2, H_G, HD))

    o_del, s_new = _delta_net(dqkv3, small.reshape(b, t, LANES), z.reshape(b, t, D_DV), conv_buf, s0,
                              conv_w, a_log, dt_bias, delta_norm_w)
    conv_new = jnp.concatenate([conv_buf, dqkv3], axis=1)[:, -(CONV_W - 1):] if t < CONV_W - 1 \
        else dqkv3[:, t - (CONV_W - 1):]

    rw = _pad_cols(router_w, LANES)
    rw_hi = rw.astype(BF16)
    rw = jnp.concatenate([rw_hi, (rw - rw_hi.astype(F32)).astype(BF16)], axis=1)
    rb = _pad_cols(router_b.reshape(1, N_EXP), LANES, fill=NEG)
    h, top_i, top_g, counts = _mix_and_route(
        x2d, gates, o_groups, lse_groups, o_del,
        w_branch_attn.astype(BF16), w_branch_delta.astype(BF16), w_out.astype(BF16),
        b_gate.reshape(1, 2 * D_MODEL), ln1_g.reshape(1, D_MODEL), ln1_b.reshape(1, D_MODEL), rw, rb, tm)
    return (h, top_i, top_g, counts, tm), kv_new, conv_new, s_new


def kernel(x_prompt, x_sample, cache_kv_w128, cache_kv_w512, cache_kv_w2048, state_conv, state_delta,
           w_in, b_gate, conv_w, a_log, dt_bias, delta_norm_w, w_branch_attn, w_branch_delta, w_out,
           ln1_g, ln1_b, router_w, router_b, w_gu, b_gu, w_down, b_down, ln2_g, ln2_b):
    depth = w_in.shape[0]
    assert depth == 1
    l = 0
    lw = (w_in[l], b_gate[l], conv_w[l], a_log[l], dt_bias[l], delta_norm_w[l], w_branch_attn[l],
          w_branch_delta[l], w_out[l], ln1_g[l], ln1_b[l], router_w[l], router_b[l], w_gu[l], b_gu[l],
          w_down[l], b_down[l], ln2_g[l], ln2_b[l])
    mixers = lw[:13]
    routed_p, kv_p, cv_p, s_p = _token_mixers_and_route(x_prompt, None, None, None, *mixers)
    routed_s, kv_s, cv_s, s_s = _token_mixers_and_route(
        x_sample, (cache_kv_w128[l], cache_kv_w512[l], cache_kv_w2048[l]), state_conv[l], state_delta[l], *mixers)
    w_gu_l, b_gu_l, w_down_l, b_down_l, ln2_g_l, ln2_b_l = lw[13:]
    yp, ys = _moe([routed_p, routed_s], w_gu_l, b_gu_l, w_down_l, b_down_l,
                  ln2_g_l.reshape(1, D_MODEL), ln2_b_l.reshape(1, D_MODEL))
    yp, ys = yp.reshape(x_prompt.shape), ys.reshape(x_sample.shape)
    stk = lambda a: a[None]
    return (yp, ys, stk(kv_p[0]), stk(kv_p[1]), stk(kv_p[2]), stk(cv_p), stk(s_p),
            stk(kv_s[0]), stk(kv_s[1]), stk(kv_s[2]), stk(cv_s), stk(s_s))
```

```python
import functools
import math

import numpy as np
import jax
import jax.numpy as jnp
from jax import lax
from jax.experimental import pallas as pl
from jax.experimental.pallas import tpu as pltpu

F32 = jnp.float32
BF16 = jnp.bfloat16
HIGHEST = lax.Precision.HIGHEST

D_MODEL = 1024
ATT_DILS = (1, 4, 16)
STEPS = 128
H_G = 4
HD = 64
GW = H_G * HD
D_ATT = len(ATT_DILS) * GW
H_D = 4
DK = 128
DV = 128
D_DK = H_D * DK
D_DV = H_D * DV
D_CONV = 2 * D_DK + D_DV
CONV_W = 4
CHUNK = 64
N_EXP = 32
TOP_K = 4
D_FF = D_MODEL
SWIGLU_LIMIT = 7.0
SWIGLU_ALPHA = 1.702
DN_ALPHA = 2.0 ** 0.25
LN_EPS = 1e-5
NORM_EPS = 1e-6
NEG = -1e30

LANES = 128
SUBLANES = 8
VMEM_LIMIT = 56 * 1024 * 1024
MOE_BM = 512
IN_TM = 512
TOK_TM = 512
TOK_ROWS = D_MODEL // LANES
assert TOK_ROWS == SUBLANES


def _sigmoid(x):
    return 1.0 / (1.0 + jnp.exp(-x))


def _layer_norm(v, g, b):
    mu = jnp.mean(v, axis=-1, keepdims=True)
    d = v - mu
    var = jnp.mean(d * d, axis=-1, keepdims=True)
    return d * lax.rsqrt(var + LN_EPS) * g + b


def _alibi_slopes():
    return 2.0 ** (-8.0 * np.arange(1, H_G + 1, dtype=np.float64) / H_G)


def _inproj_body(x_ref, *refs, dils, tails, tpb):
    nw = len(dils)
    tail_refs = dict(zip(sorted(tails), refs[2 * nw:2 * nw + len(tails)]))
    scr = refs[2 * nw + len(tails)] if len(refs) > 2 * nw + len(tails) else None
    x = x_ref[...].astype(BF16)
    for k, (w_ref, o_ref, d) in enumerate(zip(refs[:nw], refs[nw:2 * nw], dils)):
        y = jnp.dot(x, w_ref[...], preferred_element_type=F32)
        if k in tails:
            first_tile, rows, col0 = tails[k]

            @pl.when(pl.program_id(0) % tpb >= first_tile)
            def _():
                tail_refs[k][0] = y[y.shape[0] - rows:, col0:]

        if d == 1:
            o_ref[...] = y
        else:
            for c in range(y.shape[1] // LANES):
                scr[c] = y[:, c * LANES:(c + 1) * LANES]
            for r in range(d):
                for c in range(y.shape[1] // LANES):
                    o_ref[0, r, :, c * LANES:(c + 1) * LANES] = scr[c, pl.ds(r, y.shape[0] // d, stride=d), :]


def _in_projection(x2d, ws, dils, t, tm, tail_wins=None, tail_col0=0):
    n = x2d.shape[0]
    tpb = max(t // tm, 1)
    tail_wins = tail_wins or {}
    tails, tail_specs, tail_shapes = {}, [], []
    for k in sorted(tail_wins):
        win = tail_wins[k]
        rows = min(tm, win)
        assert t % tm == 0 and win % rows == 0 and tm % rows == 0 and (t - win) % rows == 0
        first_tile = (t - win) // tm
        tails[k] = (first_tile, rows, tail_col0)
        cw = ws[k].shape[1] - tail_col0
        tail_specs.append(pl.BlockSpec(
            (1, rows, cw), lambda i, ft=first_tile: (i // tpb, jnp.maximum(i % tpb - ft, 0), 0)))
        tail_shapes.append(jax.ShapeDtypeStruct((n // t, win, cw), F32))
    out_specs, out_shape = [], []
    for w, d in zip(ws, dils):
        c = w.shape[1]
        if d == 1:
            out_specs.append(pl.BlockSpec((tm, c), lambda i: (i, 0)))
            out_shape.append(jax.ShapeDtypeStruct((n, c), F32))
        else:
            assert tm % (d * SUBLANES) == 0 and t % tm == 0
            out_specs.append(pl.BlockSpec((1, d, tm // d, c), lambda i: (i // tpb, 0, i % tpb, 0)))
            out_shape.append(jax.ShapeDtypeStruct((n // t, d, t // d, c), F32))
    wide = max([w.shape[1] for w, d in zip(ws, dils) if d > 1], default=0)
    return pl.pallas_call(
        functools.partial(_inproj_body, dils=tuple(dils), tails=tails, tpb=tpb),
        grid=(n // tm,),
        in_specs=[pl.BlockSpec((tm, D_MODEL), lambda i: (i, 0))]
        + [pl.BlockSpec((D_MODEL, w.shape[1]), lambda i: (0, 0), pipeline_mode=pl.Buffered(1)) for w in ws],
        out_specs=out_specs + tail_specs,
        out_shape=out_shape + tail_shapes,
        scratch_shapes=[pltpu.VMEM((wide // LANES, tm, LANES), F32)] if wide else [],
        compiler_params=pltpu.CompilerParams(dimension_semantics=("arbitrary",), vmem_limit_bytes=VMEM_LIMIT),
        name="in_projection",
    )(x2d, *ws)


def _prompt_bias_table(dil):
    qi = np.arange(STEPS)[:, None]
    kj = np.arange(2 * STEPS)[None, :]
    steps = qi + STEPS - kj
    valid = (steps >= 0) & (steps <= STEPS)
    slopes = _alibi_slopes()
    bias = -slopes[:, None, None] * (dil * steps)[None].astype(np.float64)
    later = np.where(valid[None], bias, NEG)
    first = np.where((valid & (kj >= STEPS))[None], bias, NEG)
    return jnp.asarray(np.stack([first, later]).reshape(2, H_G * STEPS, 2 * STEPS), F32)


ATT_QB = 4


def _attn_prompt_body(q_ref, kp_ref, kc_ref, vp_ref, vc_ref, bias_ref, o_ref, lse_ref):
    head = lax.broadcasted_iota(jnp.int32, (STEPS, GW), 1) // HD
    for r in range(q_ref.shape[1]):
        kk = jnp.concatenate([kp_ref[0, r], kc_ref[0, r]], axis=0).astype(BF16)
        vv = jnp.concatenate([vp_ref[0, r], vc_ref[0, r]], axis=0).astype(BF16)
        lse_ref[0, r] = jnp.zeros(lse_ref.shape[2:], F32)
        for j in range(o_ref.shape[2] // STEPS):
            qrows = slice(j * STEPS, (j + 1) * STEPS)
            krows = slice(j * STEPS, (j + 2) * STEPS)
            later = jnp.minimum(pl.program_id(2), 1) if j == 0 else 1
            q = q_ref[0, r, qrows, :] * (HD ** -0.5)
            qs = jnp.concatenate([jnp.where(head == h, q, 0.0) for h in range(H_G)], axis=0).astype(BF16)
            s = lax.dot_general(qs, kk[krows], (((1,), (1,)), ((), ())), preferred_element_type=F32)
            s = s + bias_ref[later]
            mx = jnp.max(s, axis=-1, keepdims=True)
            p = jnp.exp(s - mx)
            den = jnp.sum(p, axis=-1, keepdims=True)
            pv = jnp.dot(p.astype(BF16), vv[krows], preferred_element_type=F32) / den
            lse = mx + jnp.log(den)
            o = jnp.zeros((STEPS, GW), F32)
            for h in range(H_G):
                rows = slice(h * STEPS, (h + 1) * STEPS)
                o = jnp.where(head == h, pv[rows], o)
                lse_ref[0, r, qrows, h:h + 1] = lse[rows]
            o_ref[0, r, qrows, :] = o


def _attn_prompt(a, dil):
    b, d, L, c = a.shape
    assert d == dil and c == 3 * GW and L % STEPS == 0
    qb = next(k for k in range(ATT_QB, 0, -1) if L % (k * STEPS) == 0)
    rb = next(k for k in range(ATT_QB // qb, 0, -1) if dil % k == 0)
    rows = qb * STEPS
    nb = L // rows

    def spec(col, prev):
        if prev:
            return pl.BlockSpec((1, rb, STEPS, GW), lambda bi, r, n: (bi, r, jnp.maximum(qb * n - 1, 0), col))
        return pl.BlockSpec((1, rb, rows, GW), lambda bi, r, n: (bi, r, n, col))

    return pl.pallas_call(
        _attn_prompt_body,
        grid=(b, dil // rb, nb),
        in_specs=[spec(0, False), spec(1, True), spec(1, False), spec(2, True), spec(2, False),
                  pl.BlockSpec((2, H_G * STEPS, 2 * STEPS), lambda bi, r, n: (0, 0, 0))],
        out_specs=[pl.BlockSpec((1, rb, rows, GW), lambda bi, r, n: (bi, r, n, 0)),
                   pl.BlockSpec((1, rb, rows, LANES), lambda bi, r, n: (bi, r, n, 0))],
        out_shape=[jax.ShapeDtypeStruct((b, dil, L, GW), F32),
                   jax.ShapeDtypeStruct((b, dil, L, LANES), F32)],
        compiler_params=pltpu.CompilerParams(dimension_semantics=("parallel", "parallel", "arbitrary"),
                                             vmem_limit_bytes=VMEM_LIMIT),
        name=f"attn_prompt_d{dil}",
    )(a, a, a, a, a, _prompt_bias_table(dil))


def _sample_bias_tables(s_new):
    slopes = _alibi_slopes()[:, None, None]
    s = np.arange(SUBLANES)[:, None]
    live = s < s_new
    cached = []
    for dil in ATT_DILS:
        lw = dil * STEPS
        dist = lw + s - np.arange(lw)[None, :]
        ok = live & (dist % dil == 0) & (dist <= lw)
        cached.append(jnp.asarray(np.where(ok[None], -slopes * dist[None], NEG), F32))
    s2 = np.arange(SUBLANES)[None, :]
    dist = s - s2
    new = []
    for dil in ATT_DILS:
        ok = live & (s2 < s_new) & (dist >= 0) & (dist % dil == 0)
        new.append(np.where(ok[None], -slopes * dist[None], NEG))
    return cached, jnp.asarray(np.stack(new), F32)


def _attn_sample_body(new_ref, c0_ref, c1_ref, c2_ref, bw0_ref, bw1_ref, bw2_ref, bn_ref, o_ref, lse_ref, *, s_new):
    lse_ref[0] = jnp.zeros(lse_ref.shape[1:], F32)
    for gi, (cache, bw_ref) in enumerate(zip((c0_ref, c1_ref, c2_ref), (bw0_ref, bw1_ref, bw2_ref))):
        def heads(part):
            c0 = gi * 3 * GW + part * GW
            return jnp.stack([new_ref[0, :, c0 + h * HD:c0 + (h + 1) * HD] for h in range(H_G)])

        q, k_new, v_new = heads(0) * (HD ** -0.5), heads(1), heads(2)
        kt = cache[0, 0].astype(BF16)
        vt = cache[0, 1].astype(BF16)
        sw = jnp.einsum('hqe,hek->hqk', q.astype(BF16), kt, preferred_element_type=F32) + bw_ref[...]
        sn = jnp.einsum('hqe,hse->hqs', q, k_new, preferred_element_type=F32, precision=HIGHEST) + bn_ref[gi]
        mx = jnp.maximum(jnp.max(sw, axis=-1, keepdims=True), jnp.max(sn, axis=-1, keepdims=True))
        pw = jnp.exp(sw - mx)
        pn = jnp.exp(sn - mx)
        den = jnp.sum(pw, axis=-1, keepdims=True) + jnp.sum(pn, axis=-1, keepdims=True)
        o = (jnp.einsum('hqk,hek->hqe', pw.astype(BF16), vt, preferred_element_type=F32)
             + jnp.einsum('hqs,hse->hqe', pn, v_new, preferred_element_type=F32, precision=HIGHEST)) / den
        lse = mx + jnp.log(den)
        for h in range(H_G):
            o_ref[0, :, gi * GW + h * HD:gi * GW + (h + 1) * HD] = o[h, 0:s_new]
            lse_ref[0, :, gi * LANES + h:gi * LANES + h + 1] = lse[h, 0:s_new]


def _attn_sample(att, caches):
    b, s_new, c = att.shape
    assert s_new <= min(SUBLANES, ATT_DILS[1])
    new = jnp.pad(att, ((0, 0), (0, SUBLANES - s_new), (0, 0)))
    bufs, specs = [], []
    for dil, cache in zip(ATT_DILS, caches):
        lw = cache.shape[1]
        assert lw == dil * STEPS
        bufs.append(jnp.transpose(cache, (0, 2, 3, 4, 1)))
        specs.append(pl.BlockSpec((1, 2, H_G, HD, lw), lambda bi: (bi, 0, 0, 0, 0)))
    bias_cached, bias_new = _sample_bias_tables(s_new)
    ng = len(ATT_DILS)
    o, lse = pl.pallas_call(
        functools.partial(_attn_sample_body, s_new=s_new),
        grid=(b,),
        in_specs=[pl.BlockSpec((1, SUBLANES, c), lambda bi: (bi, 0, 0))] + specs
        + [pl.BlockSpec(bc.shape, lambda bi: (0, 0, 0)) for bc in bias_cached]
        + [pl.BlockSpec(bias_new.shape, lambda bi: (0, 0, 0, 0))],
        out_specs=[pl.BlockSpec((1, s_new, D_ATT), lambda bi: (bi, 0, 0)),
                   pl.BlockSpec((1, s_new, ng * LANES), lambda bi: (bi, 0, 0))],
        out_shape=[jax.ShapeDtypeStruct((b, s_new, D_ATT), F32),
                   jax.ShapeDtypeStruct((b, s_new, ng * LANES), F32)],
        compiler_params=pltpu.CompilerParams(dimension_semantics=("parallel",), vmem_limit_bytes=VMEM_LIMIT),
        name="attn_sample",
    )(new, *bufs, *bias_cached, bias_new)
    return o.reshape(b * s_new, D_ATT), lse.reshape(b * s_new, ng * LANES)


HIST = SUBLANES


def _delta_prep_body(u_ref, prev_ref, small_ref, cbuf_ref, cw_ref, alog_ref, dtb_ref,
                     uo_ref, wq_ref, ak_ref, eg_ref, ext, *, t_valid, cp):
    c = pl.program_id(1)
    rows = cp * CHUNK

    @pl.when(c == 0)
    def _():
        ext[0:HIST, :] = cbuf_ref[0]

    @pl.when(c > 0)
    def _():
        ext[0:HIST, :] = prev_ref[0]

    ext[HIST:HIST + rows, :] = u_ref[0]
    conv = ext[HIST - 3:HIST - 3 + rows, :] * cw_ref[0:1, :]
    for i in range(1, CONV_W):
        conv = conv + ext[HIST - 3 + i:HIST - 3 + i + rows, :] * cw_ref[i:i + 1, :]
    act = conv * _sigmoid(conv)

    sm = small_ref[0]
    bmat = _sigmoid(sm)
    xg = sm + dtb_ref[...]
    gmat = -jnp.exp(alog_ref[...]) * (jnp.maximum(xg, 0.0) + jnp.log1p(jnp.exp(-jnp.abs(xg))))
    if t_valid < rows:
        live = lax.broadcasted_iota(jnp.int32, (rows, 1), 0) < t_valid
        act = jnp.where(live, act, 0.0)
        bmat = jnp.where(live, bmat, 0.0)
        gmat = jnp.where(live, gmat, 0.0)

    ri = lax.broadcasted_iota(jnp.int32, (CHUNK, CHUNK), 0)
    ci = lax.broadcasted_iota(jnp.int32, (CHUNK, CHUNK), 1)
    tril = ri >= ci
    strict = ri > ci

    pairs = [(j, h) for j in range(cp) for h in range(H_D)]
    rsl = lambda j: slice(j * CHUNK, (j + 1) * CHUNK)
    gcs, gcts = [], []
    for j in range(cp):
        gc_j = jnp.dot(tril.astype(F32), gmat[rsl(j)], preferred_element_type=F32, precision=HIGHEST)
        gcs.append(gc_j)
        gcts.append(gc_j.T)
    q = jnp.stack([act[rsl(j), h * DK:(h + 1) * DK] for j, h in pairs])
    k = jnp.stack([act[rsl(j), D_DK + h * DK:D_DK + (h + 1) * DK] for j, h in pairs])
    v = jnp.stack([act[rsl(j), 2 * D_DK + h * DV:2 * D_DK + (h + 1) * DV] for j, h in pairs])
    beta = jnp.stack([bmat[rsl(j), h:h + 1] for j, h in pairs])
    gcol = jnp.stack([gcs[j][:, H_D + h:H_D + h + 1] for j, h in pairs])
    grow = jnp.stack([gcts[j][H_D + h:H_D + h + 1, :] for j, h in pairs])
    glast = jnp.stack([gcs[j][CHUNK - 1:CHUNK, H_D + h:H_D + h + 1] for j, h in pairs])

    qn = q * lax.rsqrt(jnp.sum(q * q, axis=-1, keepdims=True) + NORM_EPS) * (DK ** -0.5)
    kn = k * lax.rsqrt(jnp.sum(k * k, axis=-1, keepdims=True) + NORM_EPS)
    decay = jnp.where(tril, jnp.exp(jnp.where(tril, gcol - grow, 0.0)), 0.0)
    eg = jnp.exp(gcol)
    kb = kn * beta
    kn_b = kn.astype(BF16)
    a_kk = jnp.einsum('bik,bjk->bij', kb.astype(BF16), kn_b, preferred_element_type=F32)
    a_qk = jnp.einsum('bik,bjk->bij', qn.astype(BF16), kn_b, preferred_element_type=F32)
    pw = jnp.where(strict, -(a_kk * decay), 0.0)
    tm1 = pw
    for _ in range(int(math.log2(CHUNK)) - 1):
        pw_b = pw.astype(BF16)
        pw = jnp.einsum('bij,bjk->bik', pw_b, pw_b, preferred_element_type=F32)
        tm1 = tm1 + pw + jnp.einsum('bij,bjk->bik', tm1.astype(BF16), pw.astype(BF16), preferred_element_type=F32)
    vb = v * beta
    kbg = kb * eg
    tm1_b = tm1.astype(BF16)
    u = vb + jnp.einsum('bij,bjk->bik', tm1_b, vb.astype(BF16), preferred_element_type=F32)
    w = (kbg + jnp.einsum('bij,bjk->bik', tm1_b, kbg.astype(BF16), preferred_element_type=F32)).astype(BF16)
    qd = (qn * eg).astype(BF16)
    aqd = (a_qk * decay).astype(BF16)
    kd = kn * jnp.exp(glast - gcol)
    egl = jnp.exp(glast)
    for i, (j, h) in enumerate(pairs):
        uo_ref[0, rsl(j), h * DV:(h + 1) * DV] = u[i]
        wq_ref[0, j, h, 0:CHUNK, :] = w[i]
        wq_ref[0, j, h, CHUNK:2 * CHUNK, :] = qd[i]
        ak_ref[0, j, h, 0:CHUNK, :] = aqd[i]
        ak_ref[0, j, h, CHUNK:CHUNK + DK, :] = kd[i].T.astype(BF16)
        eg_ref[0, j, h:h + 1, :] = jnp.broadcast_to(egl[i], (1, LANES))
    for j in range(cp):
        eg_ref[0, j, H_D:SUBLANES, :] = jnp.zeros((SUBLANES - H_D, LANES), F32)


def _delta_rec_body(u_ref, wq_ref, ak_ref, eg_ref, z_ref, s0_ref, nw_ref, o_ref, sout_ref, S, *, bb, nc):
    c = pl.program_id(1)

    @pl.when(c == 0)
    def _():
        S[...] = s0_ref[...]

    pairs = [(b, h) for b in range(bb) for h in range(H_D)]
    n = len(pairs)
    cols = lambda h: slice(h * DV, (h + 1) * DV)
    s_all = S[...].reshape(n, DK, DV)
    x = jnp.einsum('nij,njk->nik', wq_ref[:, 0].reshape(n, 2 * CHUNK, DK), s_all.astype(BF16),
                   preferred_element_type=F32)
    u = jnp.stack([u_ref[b, :, cols(h)] for b, h in pairs])
    vn = u - x[:, 0:CHUNK]
    y = jnp.einsum('nij,njk->nik', ak_ref[:, 0].reshape(n, CHUNK + DK, CHUNK), vn.astype(BF16),
                   preferred_element_type=F32)
    o = x[:, CHUNK:2 * CHUNK] + y[:, 0:CHUNK]
    eg = jnp.stack([eg_ref[b, 0, h:h + 1, :] for b, h in pairs])
    S[...] = (s_all * eg + y[:, CHUNK:CHUNK + DK]).reshape(bb, H_D, DK, DV)
    od = o * lax.rsqrt(jnp.mean(o * o, axis=-1, keepdims=True) + NORM_EPS) * nw_ref[...]
    z = jnp.stack([z_ref[b, :, cols(h)] for b, h in pairs])
    out = od * (z * _sigmoid(z))
    for i, (b, h) in enumerate(pairs):
        o_ref[b, :, cols(h)] = out[i]

    @pl.when(c == nc - 1)
    def _():
        sout_ref[...] = S[...]


def _lane_row(vals, offset):
    row = jnp.zeros((1, LANES), F32)
    return row.at[0, offset:offset + vals.shape[0]].set(vals.astype(F32))


def _delta_net(dqkv, small, z, conv_buf, s0, conv_w, a_log, dt_bias, delta_norm_w):
    b, t, _ = dqkv.shape
    tp = -(-t // CHUNK) * CHUNK
    nc = tp // CHUNK
    assert tp == t or nc == 1
    if tp != t:
        pad = ((0, 0), (0, tp - t), (0, 0))
        dqkv, small, z = jnp.pad(dqkv, pad), jnp.pad(small, pad), jnp.pad(z, pad)
    cbuf = jnp.pad(conv_buf, ((0, 0), (HIST - (CONV_W - 1), 0), (0, 0)))
    cw = jnp.pad(conv_w, ((0, SUBLANES - CONV_W), (0, 0)))
    cp = next(c for c in (8, 4, 2, 1) if nc % c == 0)
    rows = cp * CHUNK
    full2 = lambda shape: pl.BlockSpec(shape, lambda bi, ci: (0,) * len(shape))
    u, wq, ak, eg = pl.pallas_call(
        functools.partial(_delta_prep_body, t_valid=t if nc == 1 else rows, cp=cp),
        grid=(b, nc // cp),
        in_specs=[pl.BlockSpec((1, rows, D_CONV), lambda bi, ci: (bi, ci, 0)),
                  pl.BlockSpec((1, HIST, D_CONV), lambda bi, ci: (bi, jnp.maximum(ci * (rows // HIST) - 1, 0), 0)),
                  pl.BlockSpec((1, rows, LANES), lambda bi, ci: (bi, ci, 0)),
                  pl.BlockSpec((1, HIST, D_CONV), lambda bi, ci: (bi, 0, 0)),
                  full2((SUBLANES, D_CONV)), full2((1, LANES)), full2((1, LANES))],
        out_specs=[pl.BlockSpec((1, rows, D_DV), lambda bi, ci: (bi, ci, 0)),
                   pl.BlockSpec((1, cp, H_D, 2 * CHUNK, DK), lambda bi, ci: (bi, ci, 0, 0, 0)),
                   pl.BlockSpec((1, cp, H_D, CHUNK + DK, CHUNK), lambda bi, ci: (bi, ci, 0, 0, 0)),
                   pl.BlockSpec((1, cp, SUBLANES, LANES), lambda bi, ci: (bi, ci, 0, 0))],
        out_shape=[jax.ShapeDtypeStruct((b, tp, D_DV), F32),
                   jax.ShapeDtypeStruct((b, nc, H_D, 2 * CHUNK, DK), BF16),
                   jax.ShapeDtypeStruct((b, nc, H_D, CHUNK + DK, CHUNK), BF16),
                   jax.ShapeDtypeStruct((b, nc, SUBLANES, LANES), F32)],
        scratch_shapes=[pltpu.VMEM((HIST + rows, D_CONV), F32)],
        compiler_params=pltpu.CompilerParams(dimension_semantics=("parallel", "parallel"),
                                             vmem_limit_bytes=VMEM_LIMIT),
        name="delta_prep",
    )(dqkv, dqkv, small, cbuf, cw, _lane_row(a_log, H_D), _lane_row(dt_bias, H_D))

    bb = next(c for c in (8, 4, 2, 1) if b % c == 0)
    o, s_new = pl.pallas_call(
        functools.partial(_delta_rec_body, bb=bb, nc=nc),
        grid=(b // bb, nc),
        in_specs=[pl.BlockSpec((bb, CHUNK, D_DV), lambda bi, ci: (bi, ci, 0)),
                  pl.BlockSpec((bb, 1, H_D, 2 * CHUNK, DK), lambda bi, ci: (bi, ci, 0, 0, 0)),
                  pl.BlockSpec((bb, 1, H_D, CHUNK + DK, CHUNK), lambda bi, ci: (bi, ci, 0, 0, 0)),
                  pl.BlockSpec((bb, 1, SUBLANES, LANES), lambda bi, ci: (bi, ci, 0, 0)),
                  pl.BlockSpec((bb, CHUNK, D_DV), lambda bi, ci: (bi, ci, 0)),
                  pl.BlockSpec((bb, H_D, DK, DV), lambda bi, ci: (bi, 0, 0, 0)),
                  full2((1, DV))],
        out_specs=[pl.BlockSpec((bb, CHUNK, D_DV), lambda bi, ci: (bi, ci, 0)),
                   pl.BlockSpec((bb, H_D, DK, DV), lambda bi, ci: (bi, 0, 0, 0))],
        out_shape=[jax.ShapeDtypeStruct((b, tp, D_DV), F32), jax.ShapeDtypeStruct((b, H_D, DK, DV), F32)],
        scratch_shapes=[pltpu.VMEM((bb, H_D, DK, DV), F32)],
        compiler_params=pltpu.CompilerParams(dimension_semantics=("parallel", "arbitrary"),
                                             vmem_limit_bytes=VMEM_LIMIT),
        name="delta_recurrence",
    )(u, wq, ak, eg, z, s0, delta_norm_w.reshape(1, DV).astype(F32))
    return o[:, :t].reshape(b * t, D_DV), s_new


def _lane_expand(cols, width):
    tm = cols[0].shape[0]
    seg = lax.broadcasted_iota(jnp.int32, (tm, len(cols) * width), 1) // width
    out = jnp.broadcast_to(cols[-1], seg.shape)
    for i in range(len(cols) - 2, -1, -1):
        out = jnp.where(seg == i, cols[i], out)
    return out


def _token_order(ref, scr):
    d, nc = ref.shape[1], ref.shape[3] // LANES
    if d == 1:
        return ref[0, 0]
    for r in range(d):
        for c in range(nc):
            scr[c, pl.ds(r, ref.shape[2], stride=d), :] = ref[0, r, :, c * LANES:(c + 1) * LANES]
    return jnp.concatenate([scr[c] for c in range(nc)], axis=-1)


def _mix_body(x_ref, gate_ref, o0_ref, o1_ref, o2_ref, l0_ref, l1_ref, l2_ref, od_ref,
              wa_ref, wd_ref, wo_ref, bg_ref, g1_ref, b1_ref, rw_ref, rb_ref,
              h_ref, ti_ref, tg_ref, cnt_ref, *scratch):
    o_scr, l_scr = scratch[:3], scratch[3:]
    lses = [_token_order(l, s) for l, s in zip((l0_ref, l1_ref, l2_ref), l_scr)]
    mx = jnp.maximum(jnp.maximum(lses[0], lses[1]), lses[2])
    es = [jnp.exp(l - mx) for l in lses]
    inv = 1.0 / (es[0] + es[1] + es[2])
    o_att = None
    for e_g, o_g, s in zip(es, (o0_ref, o1_ref, o2_ref), o_scr):
        wgt = e_g * inv
        term = _lane_expand([wgt[:, h:h + 1] for h in range(H_G)], HD) * _token_order(o_g, s)
        o_att = term if o_att is None else o_att + term
    ga = _sigmoid(gate_ref[:, :D_MODEL] + bg_ref[:, :D_MODEL])
    gd = _sigmoid(gate_ref[:, D_MODEL:] + bg_ref[:, D_MODEL:])
    merged = (ga * jnp.dot(o_att.astype(BF16), wa_ref[...], preferred_element_type=F32)
              + gd * jnp.dot(od_ref[...].astype(BF16), wd_ref[...], preferred_element_type=F32))
    mix = jnp.dot(merged.astype(BF16), wo_ref[...], preferred_element_type=F32)
    h = _layer_norm(DN_ALPHA * x_ref[...] + mix, g1_ref[...], b1_ref[...])
    _rows_to_tiles(h_ref, h)

    h_hi = h.astype(BF16)
    h_lo = (h - h_hi.astype(F32)).astype(BF16)
    part = jnp.dot(h_hi, rw_ref[...], preferred_element_type=F32)
    logits = (part[:, :LANES] + part[:, LANES:]
              + jnp.dot(h_lo, rw_ref[:, :LANES], preferred_element_type=F32) + rb_ref[...])
    lane = lax.broadcasted_iota(jnp.int32, logits.shape, 1)
    lane_f = lane.astype(F32)
    ti = jnp.zeros(logits.shape, F32)
    tv = jnp.zeros(logits.shape, F32)
    hits = jnp.zeros(logits.shape, F32)
    top = None
    for kk in range(TOP_K):
        m = jnp.max(logits, axis=-1, keepdims=True)
        idx = jnp.min(jnp.where(logits == m, lane_f, float(LANES)), axis=-1, keepdims=True)
        if top is None:
            top = m
        ti = jnp.where(lane == kk, idx, ti)
        tv = jnp.where(lane == kk, jnp.exp(m - top), tv)
        chosen = lane_f == idx
        hits = jnp.where(chosen, 1.0, hits)
        logits = jnp.where(chosen, -jnp.inf, logits)
    ti_ref[...] = ti.astype(jnp.int32)
    tg_ref[...] = tv / jnp.sum(tv, axis=-1, keepdims=True)

    @pl.when(pl.program_id(0) == 0)
    def _():
        cnt_ref[...] = jnp.zeros_like(cnt_ref)

    cnt_ref[...] += jnp.sum(hits, axis=0, keepdims=True)


def _mix_and_route(x2d, gates, o_groups, lse_groups, o_del, wa, wd, wo, b_gate, ln_g, ln_b, rw, rb, tm):
    n = x2d.shape[0]
    tpb = o_groups[0].shape[1] * o_groups[0].shape[2] // tm
    row = lambda c: pl.BlockSpec((tm, c), lambda i: (i, 0))
    full = lambda a: pl.BlockSpec(a.shape, lambda i: (0, 0))

    def res(a):
        d = a.shape[1]
        return pl.BlockSpec((1, d, tm // d, a.shape[3]), lambda i: (i // tpb, 0, i % tpb, 0))

    consts = (wa, wd, wo, b_gate, ln_g, ln_b, rw, rb)
    return pl.pallas_call(
        _mix_body,
        grid=(n // tm,),
        in_specs=[row(D_MODEL), row(2 * D_MODEL)] + [res(a) for a in o_groups] + [res(a) for a in lse_groups]
        + [row(D_DV)] + [full(a) for a in consts],
        out_specs=[pl.BlockSpec((tm * TOK_ROWS, LANES), lambda i: (i, 0)), row(LANES), row(LANES),
                   pl.BlockSpec((1, LANES), lambda i: (0, 0))],
        out_shape=[jax.ShapeDtypeStruct((n * TOK_ROWS, LANES), F32), jax.ShapeDtypeStruct((n, LANES), jnp.int32),
                   jax.ShapeDtypeStruct((n, LANES), F32), jax.ShapeDtypeStruct((1, LANES), F32)],
        scratch_shapes=[pltpu.VMEM((GW // LANES, tm, LANES), F32)] * len(o_groups)
        + [pltpu.VMEM((1, tm, LANES), F32)] * len(lse_groups),
        compiler_params=pltpu.CompilerParams(dimension_semantics=("arbitrary",), vmem_limit_bytes=VMEM_LIMIT),
        name="mix_and_route",
    )(x2d, gates, *o_groups, *lse_groups, o_del, *consts)


def _rank_body(ti_ref, base_ref, dest_ref, carry):
    i = pl.program_id(0)

    @pl.when(i == 0)
    def _():
        carry[...] = base_ref[...]

    ti = ti_ref[...]
    tm = ti.shape[0]
    lane = lax.broadcasted_iota(jnp.int32, ti.shape, 1)
    sel = [lane == ti[:, kk:kk + 1] for kk in range(TOP_K)]
    hit = sel[0]
    for kk in range(1, TOP_K):
        hit = hit | sel[kk]
    cnt = hit.astype(F32)
    ri = lax.broadcasted_iota(jnp.int32, (tm, tm), 0)
    ci = lax.broadcasted_iota(jnp.int32, (tm, tm), 1)
    incl = jnp.dot((ri >= ci).astype(BF16), cnt.astype(BF16), preferred_element_type=F32)
    before = incl - cnt + carry[...]
    slot = jnp.zeros(ti.shape, F32)
    for kk in range(TOP_K):
        r = jnp.sum(jnp.where(sel[kk], before, 0.0), axis=-1, keepdims=True)
        slot = jnp.where(lane == kk, r, slot)
    dest_ref[...] = slot.T[0:SUBLANES].astype(jnp.int32)
    carry[...] = carry[...] + incl[tm - 1:tm, :]


def _route_slots(top_i, base, tm):
    n = top_i.shape[0]
    return pl.pallas_call(
        _rank_body,
        grid=(n // tm,),
        in_specs=[pl.BlockSpec((tm, LANES), lambda i: (i, 0)), pl.BlockSpec((1, LANES), lambda i: (0, 0))],
        out_specs=pl.BlockSpec((SUBLANES, tm), lambda i: (i, 0)),
        out_shape=jax.ShapeDtypeStruct((n // tm * SUBLANES, tm), jnp.int32),
        scratch_shapes=[pltpu.VMEM((1, LANES), F32)],
        compiler_params=pltpu.CompilerParams(dimension_semantics=("arbitrary",), vmem_limit_bytes=VMEM_LIMIT),
        name="route_slots",
    )(top_i, base)


def _rows_to_tiles(ref, val):
    m = val.shape[0]
    for s in range(TOK_ROWS):
        ref[pl.ds(s, m, stride=TOK_ROWS), :] = val[:, s * LANES:(s + 1) * LANES]


def _tiles_to_rows(ref, m):
    return jnp.concatenate([ref[pl.ds(s, m, stride=TOK_ROWS), :] for s in range(TOK_ROWS)], axis=-1)


def _dispatch_body(pend_ref, padded_ref, dest_ref, h_ref, *refs, max_tail):
    tm = h_ref.shape[0] // TOK_ROWS
    blk = MOE_BM * TOK_ROWS
    first_group = len(refs) == 4
    if first_group:
        xs_out, zero, sem, zsem = refs
    else:
        _, xs_out, sem = refs

    def zero_fill():
        zero[...] = jnp.zeros_like(zero)

        def last_block(e):
            start = pl.multiple_of((pend_ref[e] - MOE_BM) * TOK_ROWS, blk)
            return pltpu.make_async_copy(zero, xs_out.at[pl.ds(start, blk)], zsem)

        def tail_block(j):
            start = pl.multiple_of((pend_ref[N_EXP - 1] + j * MOE_BM) * TOK_ROWS, blk)
            return pltpu.make_async_copy(zero, xs_out.at[pl.ds(start, blk)], zsem)

        n_slot = xs_out.shape[0] // TOK_ROWS
        for e in range(N_EXP):
            @pl.when(padded_ref[e] > 0)
            def _():
                last_block(e).start()
        for j in range(max_tail):
            @pl.when(pend_ref[N_EXP - 1] + j * MOE_BM < n_slot)
            def _():
                tail_block(j).start()
        for e in range(N_EXP):
            @pl.when(padded_ref[e] > 0)
            def _():
                last_block(e).wait()
        for j in range(max_tail):
            @pl.when(pend_ref[N_EXP - 1] + j * MOE_BM < n_slot)
            def _():
                tail_block(j).wait()

    if first_group:
        pl.when(pl.program_id(0) == 0)(zero_fill)

    def issue(t, carry):
        src = h_ref.at[pl.ds(pl.multiple_of(t * TOK_ROWS, TOK_ROWS), TOK_ROWS)]
        for kk in range(TOP_K):
            d = pl.multiple_of(dest_ref[kk * tm + t] * TOK_ROWS, TOK_ROWS)
            pltpu.make_async_copy(src, xs_out.at[pl.ds(d, TOK_ROWS)], sem).start(priority=kk % 2)
        return carry

    lax.fori_loop(0, tm, issue, 0, unroll=8)
    for kk in range(TOP_K):
        pltpu.make_async_copy(h_ref, xs_out.at[pl.ds(0, tm * TOK_ROWS)], sem).wait()


def _dispatch(h_tiles, dest_flat, pad_end, padded, n_slot, tm, xs_prev=None):
    n = h_tiles.shape[0] // TOK_ROWS
    first_group = xs_prev is None
    in_specs = [pl.BlockSpec((SUBLANES * tm,), lambda i, pe, pd: (i,), memory_space=pltpu.SMEM),
                pl.BlockSpec((tm * TOK_ROWS, LANES), lambda i, pe, pd: (i, 0))]
    scratch = [pltpu.SemaphoreType.DMA(())]
    if first_group:
        scratch = [pltpu.VMEM((MOE_BM * TOK_ROWS, LANES), F32)] + scratch + [pltpu.SemaphoreType.DMA(())]
    else:
        in_specs.append(pl.BlockSpec(memory_space=pl.ANY))
    grid_spec = pltpu.PrefetchScalarGridSpec(
        num_scalar_prefetch=2, grid=(n // tm,), in_specs=in_specs,
        out_specs=pl.BlockSpec(memory_space=pl.ANY), scratch_shapes=scratch)
    max_tail = n_slot // MOE_BM - (n * TOP_K) // MOE_BM
    return pl.pallas_call(
        functools.partial(_dispatch_body, max_tail=max_tail),
        grid_spec=grid_spec,
        out_shape=jax.ShapeDtypeStruct((n_slot * TOK_ROWS, LANES), F32),
        input_output_aliases={} if first_group else {4: 0},
        compiler_params=pltpu.CompilerParams(dimension_semantics=("arbitrary",), vmem_limit_bytes=VMEM_LIMIT,
                                             has_side_effects=True, disable_bounds_checks=True),
        name="moe_dispatch",
    )(pad_end, padded, dest_flat, h_tiles, *(() if first_group else (xs_prev,)))


def _expert_body(be_ref, nu_ref, nxt_ref, par_ref, x_ref, bgu_ref, bd_ref, wgu_hbm, wd_hbm, o_ref,
                 wgu_f32, wd_f32, wgu_bf, wd_bf, sem):
    i = pl.program_id(0)
    e = be_ref[i]
    prev = be_ref[jnp.maximum(i - 1, 0)]
    live = i < nu_ref[0]

    def fetch(expert, slot):
        return (pltpu.make_async_copy(wgu_hbm.at[expert], wgu_f32.at[slot], sem.at[0, slot]),
                pltpu.make_async_copy(wd_hbm.at[expert], wd_f32.at[slot], sem.at[1, slot]))

    @pl.when(live & ((i == 0) | (e != prev)))
    def _():
        slot = par_ref[i]

        @pl.when(i == 0)
        def _():
            for cp in fetch(e, slot):
                cp.start()

        for cp in fetch(e, slot):
            cp.wait()
        wgu_bf[...] = wgu_f32[slot].astype(BF16)
        wd_bf[...] = wd_f32[slot].astype(BF16)

        @pl.when(nxt_ref[i] >= 0)
        def _():
            for cp in fetch(nxt_ref[i], 1 - slot):
                cp.start()

    @pl.when(live)
    def _():
        x = _tiles_to_rows(x_ref, MOE_BM)
        gu = jnp.dot(x.astype(BF16), wgu_bf[...], preferred_element_type=F32) + bgu_ref[0]
        gt = jnp.minimum(gu[:, :D_FF], SWIGLU_LIMIT)
        up = jnp.clip(gu[:, D_FF:], -SWIGLU_LIMIT, SWIGLU_LIMIT)
        act = (up + 1.0) * gt * _sigmoid(SWIGLU_ALPHA * gt)
        _rows_to_tiles(o_ref, jnp.dot(act.astype(BF16), wd_bf[...], preferred_element_type=F32) + bd_ref[0])

    @pl.when(i >= nu_ref[0])
    def _():
        o_ref[...] = jnp.zeros_like(o_ref)


def _experts(xs, blk_exp, n_used, padded, w_gu, b_gu, w_down, b_down):
    n_slot = xs.shape[0] // TOK_ROWS
    n_blk = n_slot // MOE_BM
    ids = jnp.arange(N_EXP, dtype=jnp.int32)
    has = padded > 0
    later = jnp.where(has[None, :] & (ids[None, :] > ids[:, None]), ids[None, :], N_EXP)
    nxt_e = jnp.min(later, axis=1)
    nxt_e = jnp.where(nxt_e < N_EXP, nxt_e, -1).astype(jnp.int32)
    par_e = ((jnp.cumsum(has.astype(jnp.int32)) - 1) % 2).astype(jnp.int32)
    rows = lambda i, be, nu, nx, pa: (jnp.minimum(i, nu[0] - 1), 0)
    grid_spec = pltpu.PrefetchScalarGridSpec(
        num_scalar_prefetch=4,
        grid=(n_blk,),
        in_specs=[pl.BlockSpec((MOE_BM * TOK_ROWS, LANES), rows),
                  pl.BlockSpec((1, 1, 2 * D_FF), lambda i, be, nu, nx, pa: (be[i], 0, 0)),
                  pl.BlockSpec((1, 1, D_MODEL), lambda i, be, nu, nx, pa: (be[i], 0, 0)),
                  pl.BlockSpec(memory_space=pl.ANY), pl.BlockSpec(memory_space=pl.ANY)],
        out_specs=pl.BlockSpec((MOE_BM * TOK_ROWS, LANES), lambda i, be, nu, nx, pa: (i, 0)),
        scratch_shapes=[pltpu.VMEM((2, D_MODEL, 2 * D_FF), F32), pltpu.VMEM((2, D_FF, D_MODEL), F32),
                        pltpu.VMEM((D_MODEL, 2 * D_FF), BF16), pltpu.VMEM((D_FF, D_MODEL), BF16),
                        pltpu.SemaphoreType.DMA((2, 2))],
    )
    return pl.pallas_call(
        _expert_body,
        grid_spec=grid_spec,
        out_shape=jax.ShapeDtypeStruct((n_slot * TOK_ROWS, LANES), F32),
        compiler_params=pltpu.CompilerParams(dimension_semantics=("arbitrary",), vmem_limit_bytes=VMEM_LIMIT),
        name="moe_experts",
    )(blk_exp, n_used, nxt_e[blk_exp], par_e[blk_exp], xs, b_gu.reshape(N_EXP, 1, 2 * D_FF),
      b_down.reshape(N_EXP, 1, D_MODEL), w_gu, w_down)


def _combine_body(dest_ref, dest_next_ref, gate_ref, h_ref, g_ref, b_ref, ys_ref, y_ref, buf, sem):
    tm = h_ref.shape[0] // TOK_ROWS
    i = pl.program_id(0)
    slot = i % 2

    def start_tile(idx_ref, s):
        def issue(t, carry):
            row = pl.multiple_of(t * TOK_ROWS, TOK_ROWS)
            for kk in range(TOP_K):
                d = pl.multiple_of(idx_ref[kk * tm + t] * TOK_ROWS, TOK_ROWS)
                pltpu.make_async_copy(ys_ref.at[pl.ds(d, TOK_ROWS)], buf.at[s, kk, pl.ds(row, TOK_ROWS)],
                                      sem.at[s]).start(priority=kk % 2)
            return carry

        lax.fori_loop(0, tm, issue, 0, unroll=8)

    @pl.when(i == 0)
    def _():
        start_tile(dest_ref, 0)

    @pl.when(i + 1 < pl.num_programs(0))
    def _():
        start_tile(dest_next_ref, 1 - slot)

    for kk in range(TOP_K):
        pltpu.make_async_copy(ys_ref.at[pl.ds(0, tm * TOK_ROWS)], buf.at[slot, kk], sem.at[slot]).wait()
    gate = gate_ref[...]
    moe = gate[:, 0:1] * _tiles_to_rows(buf.at[slot, 0], tm)
    for kk in range(1, TOP_K):
        moe = moe + gate[:, kk:kk + 1] * _tiles_to_rows(buf.at[slot, kk], tm)
    y_ref[...] = _layer_norm(DN_ALPHA * _tiles_to_rows(h_ref, tm) + moe, g_ref[...], b_ref[...])


def _combine(ys, dest_flat, gate, h_tiles, ln_g, ln_b, tm):
    n = h_tiles.shape[0] // TOK_ROWS
    return pl.pallas_call(
        _combine_body,
        grid=(n // tm,),
        in_specs=[pl.BlockSpec((SUBLANES * tm,), lambda i: (i,), memory_space=pltpu.SMEM),
                  pl.BlockSpec((SUBLANES * tm,), lambda i: (jnp.minimum(i + 1, n // tm - 1),), memory_space=pltpu.SMEM),
                  pl.BlockSpec((tm, LANES), lambda i: (i, 0)),
                  pl.BlockSpec((tm * TOK_ROWS, LANES), lambda i: (i, 0)),
                  pl.BlockSpec((1, D_MODEL), lambda i: (0, 0)),
                  pl.BlockSpec((1, D_MODEL), lambda i: (0, 0)),
                  pl.BlockSpec(memory_space=pl.ANY)],
        out_specs=pl.BlockSpec((tm, D_MODEL), lambda i: (i, 0)),
        out_shape=jax.ShapeDtypeStruct((n, D_MODEL), F32),
        scratch_shapes=[pltpu.VMEM((2, TOP_K, tm * TOK_ROWS, LANES), F32), pltpu.SemaphoreType.DMA((2,))],
        compiler_params=pltpu.CompilerParams(dimension_semantics=("arbitrary",), vmem_limit_bytes=VMEM_LIMIT,
                                             disable_bounds_checks=True),
        name="moe_combine",
    )(dest_flat, dest_flat, gate, h_tiles, ln_g, ln_b, ys)


def _moe(groups, w_gu, b_gu, w_down, b_down, ln_g, ln_b):
    counts = [g[3][0, :N_EXP].astype(jnp.int32) for g in groups]
    total = sum(counts)
    padded = (total + MOE_BM - 1) // MOE_BM * MOE_BM
    pad_end = jnp.cumsum(padded).astype(jnp.int32)
    n_assign = sum(g[1].shape[0] for g in groups) * TOP_K
    n_blk = -(-(n_assign + N_EXP * (MOE_BM - 1)) // MOE_BM)
    blk_row0 = jnp.arange(n_blk, dtype=jnp.int32) * MOE_BM
    blk_exp = jnp.minimum(jnp.sum((pad_end[None, :] <= blk_row0[:, None]).astype(jnp.int32), axis=1), N_EXP - 1)
    n_used = pad_end[-1:] // MOE_BM
    base = pad_end - padded
    xs, dests = None, []
    for (h, top_i, _, _, tm), cnt in zip(groups, counts):
        base_row = jnp.zeros((1, LANES), F32).at[0, :N_EXP].set(base.astype(F32))
        dest = _route_slots(top_i, base_row, tm).reshape(-1)
        dests.append(dest)
        xs = _dispatch(h, dest, pad_end, padded, n_blk * MOE_BM, tm, xs)
        base = base + cnt
    ys = _experts(xs, blk_exp, n_used, padded, w_gu, b_gu, w_down, b_down)
    return [_combine(ys, dest, top_g, h, ln_g, ln_b, tm) for (h, _, top_g, _, tm), dest in zip(groups, dests)]


def _pad_cols(a, width, fill=0.0):
    return jnp.pad(a, ((0, 0), (0, width - a.shape[1])), constant_values=fill)


def _token_mixers_and_route(x, caches, conv_buf, s0, w_in, b_gate, conv_w, a_log, dt_bias, delta_norm_w,
                            w_branch_attn, w_branch_delta, w_out, ln1_g, ln1_b, router_w, router_b):
    b, t, _ = x.shape
    n = b * t
    tm = min(TOK_TM, n)
    x2d = x.reshape(n, D_MODEL)

    c_att, c_dz, c_small = 3 * D_ATT, 3 * D_ATT + D_CONV, 3 * D_ATT + D_CONV + D_DV
    c_gate = c_small + 2 * H_D
    ng = len(ATT_DILS)
    w_groups = [jnp.concatenate([w_in[:, part * D_ATT + gi * GW:part * D_ATT + (gi + 1) * GW] for part in range(3)],
                                axis=1) for gi in range(ng)]
    ws = w_groups + [w_in[:, c_att:c_dz], w_in[:, c_dz:c_small],
                     _pad_cols(w_in[:, c_small:c_gate], LANES), w_in[:, c_gate:]]
    dils = (ATT_DILS if caches is None else (1,) * ng) + (1, 1, 1, 1)
    wins = {gi: min(dil * STEPS, t) for gi, dil in enumerate(ATT_DILS)} if caches is None else None
    outs = _in_projection(x2d, [w.astype(BF16) for w in ws], dils, t, min(IN_TM, n), tail_wins=wins, tail_col0=GW)
    att_groups, (dqkv, z, small, gates) = outs[:ng], outs[ng:ng + 4]
    dqkv3 = dqkv.reshape(b, t, D_CONV)

    kv_new = []
    if caches is None:
        o_groups, lse_groups = [], []
        for gi, (dil, a) in enumerate(zip(ATT_DILS, att_groups)):
            a = a.reshape(b, dil, t // dil, 3 * GW)
            o_g, lse_g = _attn_prompt(a, dil)
            o_groups.append(o_g)
            lse_groups.append(lse_g)
            kv_new.append(outs[ng + 4 + gi].reshape(b, wins[gi], 2, H_G, HD))
        conv_buf = jnp.zeros((b, CONV_W - 1, D_CONV), F32)
        s0 = jnp.zeros((b, H_D, DK, DV), F32)
    else:
        att3 = jnp.concatenate(att_groups, axis=1).reshape(b, t, 3 * D_ATT)
        o_all, lse_all = _attn_sample(att3, caches)
        o_groups = [o_all[:, gi * GW:(gi + 1) * GW].reshape(1, 1, n, GW) for gi in range(ng)]
        lse_groups = [lse_all[:, gi * LANES:(gi + 1) * LANES].reshape(1, 1, n, LANES) for gi in range(ng)]
        for a in att_groups:
            kv_new.append(a[:, GW:].reshape(b, t, 2, H_G, HD))

    o_del, s_new = _delta_net(dqkv3, small.reshape(b, t, LANES), z.reshape(b, t, D_DV), conv_buf, s0,
                              conv_w, a_log, dt_bias, delta_norm_w)
    conv_new = jnp.concatenate([conv_buf, dqkv3], axis=1)[:, -(CONV_W - 1):] if t < CONV_W - 1 \
        else dqkv3[:, t - (CONV_W - 1):]

    rw = _pad_cols(router_w, LANES)
    rw_hi = rw.astype(BF16)
    rw = jnp.concatenate([rw_hi, (rw - rw_hi.astype(F32)).astype(BF16)], axis=1)
    rb = _pad_cols(router_b.reshape(1, N_EXP), LANES, fill=NEG)
    h, top_i, top_g, counts = _mix_and_route(
        x2d, gates, o_groups, lse_groups, o_del,
        w_branch_attn.astype(BF16), w_branch_delta.astype(BF16), w_out.astype(BF16),
        b_gate.reshape(1, 2 * D_MODEL), ln1_g.reshape(1, D_MODEL), ln1_b.reshape(1, D_MODEL), rw, rb, tm)
    return (h, top_i, top_g, counts, tm), kv_new, conv_new, s_new


def kernel(x_prompt, x_sample, cache_kv_w128, cache_kv_w512, cache_kv_w2048, state_conv, state_delta,
           w_in, b_gate, conv_w, a_log, dt_bias, delta_norm_w, w_branch_attn, w_branch_delta, w_out,
           ln1_g, ln1_b, router_w, router_b, w_gu, b_gu, w_down, b_down, ln2_g, ln2_b):
    depth = w_in.shape[0]
    assert depth == 1
    l = 0
    lw = (w_in[l], b_gate[l], conv_w[l], a_log[l], dt_bias[l], delta_norm_w[l], w_branch_attn[l],
          w_branch_delta[l], w_out[l], ln1_g[l], ln1_b[l], router_w[l], router_b[l], w_gu[l], b_gu[l],
          w_down[l], b_down[l], ln2_g[l], ln2_b[l])
    mixers = lw[:13]
    routed_p, kv_p, cv_p, s_p = _token_mixers_and_route(x_prompt, None, None, None, *mixers)
    routed_s, kv_s, cv_s, s_s = _token_mixers_and_route(
        x_sample, (cache_kv_w128[l], cache_kv_w512[l], cache_kv_w2048[l]), state_conv[l], state_delta[l], *mixers)
    w_gu_l, b_gu_l, w_down_l, b_down_l, ln2_g_l, ln2_b_l = lw[13:]
    yp, ys = _moe([routed_p, routed_s], w_gu_l, b_gu_l, w_down_l, b_down_l,
                  ln2_g_l.reshape(1, D_MODEL), ln2_b_l.reshape(1, D_MODEL))
    yp, ys = yp.reshape(x_prompt.shape), ys.reshape(x_sample.shape)
    stk = lambda a: a[None]
    return (yp, ys, stk(kv_p[0]), stk(kv_p[1]), stk(kv_p[2]), stk(cv_p), stk(s_p),
            stk(kv_s[0]), stk(kv_s[1]), stk(kv_s[2]), stk(cv_s), stk(s_s))
```

```python
import functools
import math

import numpy as np
import jax
import jax.numpy as jnp
from jax import lax
from jax.experimental import pallas as pl
from jax.experimental.pallas import tpu as pltpu

F32 = jnp.float32
BF16 = jnp.bfloat16
HIGHEST = lax.Precision.HIGHEST

D_MODEL = 1024
ATT_DILS = (1, 4, 16)
STEPS = 128
H_G = 4
HD = 64
GW = H_G * HD
D_ATT = len(ATT_DILS) * GW
H_D = 4
DK = 128
DV = 128
D_DK = H_D * DK
D_DV = H_D * DV
D_CONV = 2 * D_DK + D_DV
CONV_W = 4
CHUNK = 64
N_EXP = 32
TOP_K = 4
D_FF = D_MODEL
SWIGLU_LIMIT = 7.0
SWIGLU_ALPHA = 1.702
DN_ALPHA = 2.0 ** 0.25
LN_EPS = 1e-5
NORM_EPS = 1e-6
NEG = -1e30

LANES = 128
SUBLANES = 8
VMEM_LIMIT = 56 * 1024 * 1024
MOE_BM = 512
IN_TM = 512
TOK_TM = 512
TOK_ROWS = D_MODEL // LANES
assert TOK_ROWS == SUBLANES


def _sigmoid(x):
    return 1.0 / (1.0 + jnp.exp(-x))


def _layer_norm(v, g, b):
    mu = jnp.mean(v, axis=-1, keepdims=True)
    d = v - mu
    var = jnp.mean(d * d, axis=-1, keepdims=True)
    return d * lax.rsqrt(var + LN_EPS) * g + b


def _alibi_slopes():
    return 2.0 ** (-8.0 * np.arange(1, H_G + 1, dtype=np.float64) / H_G)


def _inproj_body(x_ref, *refs, dils):
    nw = len(dils)
    scr = refs[2 * nw] if len(refs) > 2 * nw else None
    x = x_ref[...].astype(BF16)
    for w_ref, o_ref, d in zip(refs[:nw], refs[nw:2 * nw], dils):
        y = jnp.dot(x, w_ref[...], preferred_element_type=F32)
        if d == 1:
            o_ref[...] = y
        else:
            for c in range(y.shape[1] // LANES):
                scr[c] = y[:, c * LANES:(c + 1) * LANES]
            for r in range(d):
                for c in range(y.shape[1] // LANES):
                    o_ref[0, r, :, c * LANES:(c + 1) * LANES] = scr[c, pl.ds(r, y.shape[0] // d, stride=d), :]


def _in_projection(x2d, ws, dils, t, tm):
    n = x2d.shape[0]
    tpb = t // tm
    out_specs, out_shape = [], []
    for w, d in zip(ws, dils):
        c = w.shape[1]
        if d == 1:
            out_specs.append(pl.BlockSpec((tm, c), lambda i: (i, 0)))
            out_shape.append(jax.ShapeDtypeStruct((n, c), F32))
        else:
            assert tm % (d * SUBLANES) == 0 and t % tm == 0
            out_specs.append(pl.BlockSpec((1, d, tm // d, c), lambda i: (i // tpb, 0, i % tpb, 0)))
            out_shape.append(jax.ShapeDtypeStruct((n // t, d, t // d, c), F32))
    wide = max([w.shape[1] for w, d in zip(ws, dils) if d > 1], default=0)
    return pl.pallas_call(
        functools.partial(_inproj_body, dils=tuple(dils)),
        grid=(n // tm,),
        in_specs=[pl.BlockSpec((tm, D_MODEL), lambda i: (i, 0))]
        + [pl.BlockSpec((D_MODEL, w.shape[1]), lambda i: (0, 0), pipeline_mode=pl.Buffered(1)) for w in ws],
        out_specs=out_specs,
        out_shape=out_shape,
        scratch_shapes=[pltpu.VMEM((wide // LANES, tm, LANES), F32)] if wide else [],
        compiler_params=pltpu.CompilerParams(dimension_semantics=("parallel",), vmem_limit_bytes=VMEM_LIMIT),
        name="in_projection",
    )(x2d, *ws)


def _tail_proj_body(x_ref, w_ref, o_ref):
    o_ref[0] = jnp.dot(x_ref[...].astype(BF16), w_ref[...], preferred_element_type=F32)


def _tail_projection(x2d, w, b, t, win):
    tm = min(IN_TM, win)
    assert t % tm == 0 and win % tm == 0
    c = w.shape[1]
    return pl.pallas_call(
        _tail_proj_body,
        grid=(b, win // tm),
        in_specs=[pl.BlockSpec((tm, D_MODEL), lambda bi, j: (bi * (t // tm) + (t - win) // tm + j, 0)),
                  pl.BlockSpec((D_MODEL, c), lambda bi, j: (0, 0))],
        out_specs=pl.BlockSpec((1, tm, c), lambda bi, j: (bi, j, 0)),
        out_shape=jax.ShapeDtypeStruct((b, win, c), F32),
        compiler_params=pltpu.CompilerParams(dimension_semantics=("parallel", "parallel"),
                                             vmem_limit_bytes=VMEM_LIMIT),
        name="tail_projection",
    )(x2d, w)


def _prompt_bias_table(dil):
    qi = np.arange(STEPS)[:, None]
    kj = np.arange(2 * STEPS)[None, :]
    steps = qi + STEPS - kj
    valid = (steps >= 0) & (steps <= STEPS)
    slopes = _alibi_slopes()
    bias = -slopes[:, None, None] * (dil * steps)[None].astype(np.float64)
    later = np.where(valid[None], bias, NEG)
    first = np.where((valid & (kj >= STEPS))[None], bias, NEG)
    return jnp.asarray(np.stack([first, later]).reshape(2, H_G * STEPS, 2 * STEPS), F32)


ATT_QB = 4


def _attn_prompt_body(q_ref, kp_ref, kc_ref, vp_ref, vc_ref, bias_ref, o_ref, lse_ref):
    head = lax.broadcasted_iota(jnp.int32, (STEPS, GW), 1) // HD
    for r in range(q_ref.shape[1]):
        kk = jnp.concatenate([kp_ref[0, r], kc_ref[0, r]], axis=0).astype(BF16)
        vv = jnp.concatenate([vp_ref[0, r], vc_ref[0, r]], axis=0).astype(BF16)
        lse_ref[0, r] = jnp.zeros(lse_ref.shape[2:], F32)
        for j in range(o_ref.shape[2] // STEPS):
            qrows = slice(j * STEPS, (j + 1) * STEPS)
            krows = slice(j * STEPS, (j + 2) * STEPS)
            later = jnp.minimum(pl.program_id(2), 1) if j == 0 else 1
            q = q_ref[0, r, qrows, :] * (HD ** -0.5)
            qs = jnp.concatenate([jnp.where(head == h, q, 0.0) for h in range(H_G)], axis=0).astype(BF16)
            s = lax.dot_general(qs, kk[krows], (((1,), (1,)), ((), ())), preferred_element_type=F32)
            s = s + bias_ref[later]
            mx = jnp.max(s, axis=-1, keepdims=True)
            p = jnp.exp(s - mx)
            den = jnp.sum(p, axis=-1, keepdims=True)
            pv = jnp.dot(p.astype(BF16), vv[krows], preferred_element_type=F32) / den
            lse = mx + jnp.log(den)
            o = jnp.zeros((STEPS, GW), F32)
            for h in range(H_G):
                rows = slice(h * STEPS, (h + 1) * STEPS)
                o = jnp.where(head == h, pv[rows], o)
                lse_ref[0, r, qrows, h:h + 1] = lse[rows]
            o_ref[0, r, qrows, :] = o


def _attn_prompt(a, dil):
    b, d, L, c = a.shape
    assert d == dil and c == 3 * GW and L % STEPS == 0
    qb = next(k for k in range(ATT_QB, 0, -1) if L % (k * STEPS) == 0)
    rb = next(k for k in range(ATT_QB // qb, 0, -1) if dil % k == 0)
    rows = qb * STEPS
    nb = L // rows

    def spec(col, prev):
        if prev:
            return pl.BlockSpec((1, rb, STEPS, GW), lambda bi, r, n: (bi, r, jnp.maximum(qb * n - 1, 0), col))
        return pl.BlockSpec((1, rb, rows, GW), lambda bi, r, n: (bi, r, n, col))

    return pl.pallas_call(
        _attn_prompt_body,
        grid=(b, dil // rb, nb),
        in_specs=[spec(0, False), spec(1, True), spec(1, False), spec(2, True), spec(2, False),
                  pl.BlockSpec((2, H_G * STEPS, 2 * STEPS), lambda bi, r, n: (0, 0, 0))],
        out_specs=[pl.BlockSpec((1, rb, rows, GW), lambda bi, r, n: (bi, r, n, 0)),
                   pl.BlockSpec((1, rb, rows, LANES), lambda bi, r, n: (bi, r, n, 0))],
        out_shape=[jax.ShapeDtypeStruct((b, dil, L, GW), F32),
                   jax.ShapeDtypeStruct((b, dil, L, LANES), F32)],
        compiler_params=pltpu.CompilerParams(dimension_semantics=("parallel", "parallel", "arbitrary"),
                                             vmem_limit_bytes=VMEM_LIMIT),
        name=f"attn_prompt_d{dil}",
    )(a, a, a, a, a, _prompt_bias_table(dil))


def _sample_bias_tables(s_new):
    slopes = _alibi_slopes()[:, None, None]
    s = np.arange(SUBLANES)[:, None]
    live = s < s_new
    cached = []
    for dil in ATT_DILS:
        lw = dil * STEPS
        dist = lw + s - np.arange(lw)[None, :]
        ok = live & (dist % dil == 0) & (dist <= lw)
        cached.append(jnp.asarray(np.where(ok[None], -slopes * dist[None], NEG), F32))
    s2 = np.arange(SUBLANES)[None, :]
    dist = s - s2
    new = []
    for dil in ATT_DILS:
        ok = live & (s2 < s_new) & (dist >= 0) & (dist % dil == 0)
        new.append(np.where(ok[None], -slopes * dist[None], NEG))
    return cached, jnp.asarray(np.stack(new), F32)


def _attn_sample_body(new_ref, c0_ref, c1_ref, c2_ref, bw0_ref, bw1_ref, bw2_ref, bn_ref, o_ref, lse_ref, *, s_new):
    lse_ref[0] = jnp.zeros(lse_ref.shape[1:], F32)
    for gi, (cache, bw_ref) in enumerate(zip((c0_ref, c1_ref, c2_ref), (bw0_ref, bw1_ref, bw2_ref))):
        def heads(part):
            c0 = gi * 3 * GW + part * GW
            return jnp.stack([new_ref[0, :, c0 + h * HD:c0 + (h + 1) * HD] for h in range(H_G)])

        q, k_new, v_new = heads(0) * (HD ** -0.5), heads(1), heads(2)
        kt = cache[0, 0].astype(BF16)
        vt = cache[0, 1].astype(BF16)
        sw = jnp.einsum('hqe,hek->hqk', q.astype(BF16), kt, preferred_element_type=F32) + bw_ref[...]
        sn = jnp.einsum('hqe,hse->hqs', q, k_new, preferred_element_type=F32, precision=HIGHEST) + bn_ref[gi]
        mx = jnp.maximum(jnp.max(sw, axis=-1, keepdims=True), jnp.max(sn, axis=-1, keepdims=True))
        pw = jnp.exp(sw - mx)
        pn = jnp.exp(sn - mx)
        den = jnp.sum(pw, axis=-1, keepdims=True) + jnp.sum(pn, axis=-1, keepdims=True)
        o = (jnp.einsum('hqk,hek->hqe', pw.astype(BF16), vt, preferred_element_type=F32)
             + jnp.einsum('hqs,hse->hqe', pn, v_new, preferred_element_type=F32, precision=HIGHEST)) / den
        lse = mx + jnp.log(den)
        for h in range(H_G):
            o_ref[0, :, gi * GW + h * HD:gi * GW + (h + 1) * HD] = o[h, 0:s_new]
            lse_ref[0, :, gi * LANES + h:gi * LANES + h + 1] = lse[h, 0:s_new]


def _attn_sample(att, caches):
    b, s_new, c = att.shape
    assert s_new <= min(SUBLANES, ATT_DILS[1])
    new = jnp.pad(att, ((0, 0), (0, SUBLANES - s_new), (0, 0)))
    bufs, specs = [], []
    for dil, cache in zip(ATT_DILS, caches):
        lw = cache.shape[1]
        assert lw == dil * STEPS
        bufs.append(jnp.transpose(cache, (0, 2, 3, 4, 1)))
        specs.append(pl.BlockSpec((1, 2, H_G, HD, lw), lambda bi: (bi, 0, 0, 0, 0)))
    bias_cached, bias_new = _sample_bias_tables(s_new)
    ng = len(ATT_DILS)
    o, lse = pl.pallas_call(
        functools.partial(_attn_sample_body, s_new=s_new),
        grid=(b,),
        in_specs=[pl.BlockSpec((1, SUBLANES, c), lambda bi: (bi, 0, 0))] + specs
        + [pl.BlockSpec(bc.shape, lambda bi: (0, 0, 0)) for bc in bias_cached]
        + [pl.BlockSpec(bias_new.shape, lambda bi: (0, 0, 0, 0))],
        out_specs=[pl.BlockSpec((1, s_new, D_ATT), lambda bi: (bi, 0, 0)),
                   pl.BlockSpec((1, s_new, ng * LANES), lambda bi: (bi, 0, 0))],
        out_shape=[jax.ShapeDtypeStruct((b, s_new, D_ATT), F32),
                   jax.ShapeDtypeStruct((b, s_new, ng * LANES), F32)],
        compiler_params=pltpu.CompilerParams(dimension_semantics=("parallel",), vmem_limit_bytes=VMEM_LIMIT),
        name="attn_sample",
    )(new, *bufs, *bias_cached, bias_new)
    return o.reshape(b * s_new, D_ATT), lse.reshape(b * s_new, ng * LANES)


HIST = SUBLANES


def _delta_prep_body(u_ref, prev_ref, small_ref, cbuf_ref, cw_ref, alog_ref, dtb_ref,
                     uo_ref, wq_ref, ak_ref, eg_ref, ext, *, t_valid, cp):
    c = pl.program_id(1)
    rows = cp * CHUNK

    @pl.when(c == 0)
    def _():
        ext[0:HIST, :] = cbuf_ref[0]

    @pl.when(c > 0)
    def _():
        ext[0:HIST, :] = prev_ref[0]

    ext[HIST:HIST + rows, :] = u_ref[0]
    conv = ext[HIST - 3:HIST - 3 + rows, :] * cw_ref[0:1, :]
    for i in range(1, CONV_W):
        conv = conv + ext[HIST - 3 + i:HIST - 3 + i + rows, :] * cw_ref[i:i + 1, :]
    act = conv * _sigmoid(conv)

    sm = small_ref[0]
    bmat = _sigmoid(sm)
    xg = sm + dtb_ref[...]
    gmat = -jnp.exp(alog_ref[...]) * (jnp.maximum(xg, 0.0) + jnp.log1p(jnp.exp(-jnp.abs(xg))))
    if t_valid < rows:
        live = lax.broadcasted_iota(jnp.int32, (rows, 1), 0) < t_valid
        act = jnp.where(live, act, 0.0)
        bmat = jnp.where(live, bmat, 0.0)
        gmat = jnp.where(live, gmat, 0.0)

    ri = lax.broadcasted_iota(jnp.int32, (CHUNK, CHUNK), 0)
    ci = lax.broadcasted_iota(jnp.int32, (CHUNK, CHUNK), 1)
    tril = ri >= ci
    strict = ri > ci

    pairs = [(j, h) for j in range(cp) for h in range(H_D)]
    rsl = lambda j: slice(j * CHUNK, (j + 1) * CHUNK)
    gcs, gcts = [], []
    for j in range(cp):
        gc_j = jnp.dot(tril.astype(F32), gmat[rsl(j)], preferred_element_type=F32, precision=HIGHEST)
        gcs.append(gc_j)
        gcts.append(gc_j.T)
    q = jnp.stack([act[rsl(j), h * DK:(h + 1) * DK] for j, h in pairs])
    k = jnp.stack([act[rsl(j), D_DK + h * DK:D_DK + (h + 1) * DK] for j, h in pairs])
    v = jnp.stack([act[rsl(j), 2 * D_DK + h * DV:2 * D_DK + (h + 1) * DV] for j, h in pairs])
    beta = jnp.stack([bmat[rsl(j), h:h + 1] for j, h in pairs])
    gcol = jnp.stack([gcs[j][:, H_D + h:H_D + h + 1] for j, h in pairs])
    grow = jnp.stack([gcts[j][H_D + h:H_D + h + 1, :] for j, h in pairs])
    glast = jnp.stack([gcs[j][CHUNK - 1:CHUNK, H_D + h:H_D + h + 1] for j, h in pairs])

    qn = q * lax.rsqrt(jnp.sum(q * q, axis=-1, keepdims=True) + NORM_EPS) * (DK ** -0.5)
    kn = k * lax.rsqrt(jnp.sum(k * k, axis=-1, keepdims=True) + NORM_EPS)
    decay = jnp.where(tril, jnp.exp(jnp.where(tril, gcol - grow, 0.0)), 0.0)
    eg = jnp.exp(gcol)
    kb = kn * beta
    kn_b = kn.astype(BF16)
    a_kk = jnp.einsum('bik,bjk->bij', kb.astype(BF16), kn_b, preferred_element_type=F32)
    a_qk = jnp.einsum('bik,bjk->bij', qn.astype(BF16), kn_b, preferred_element_type=F32)
    pw = jnp.where(strict, -(a_kk * decay), 0.0)
    tm1 = pw
    for _ in range(int(math.log2(CHUNK)) - 1):
        pw_b = pw.astype(BF16)
        pw = jnp.einsum('bij,bjk->bik', pw_b, pw_b, preferred_element_type=F32)
        tm1 = tm1 + pw + jnp.einsum('bij,bjk->bik', tm1.astype(BF16), pw.astype(BF16), preferred_element_type=F32)
    vb = v * beta
    kbg = kb * eg
    tm1_b = tm1.astype(BF16)
    u = vb + jnp.einsum('bij,bjk->bik', tm1_b, vb.astype(BF16), preferred_element_type=F32)
    w = (kbg + jnp.einsum('bij,bjk->bik', tm1_b, kbg.astype(BF16), preferred_element_type=F32)).astype(BF16)
    qd = (qn * eg).astype(BF16)
    aqd = (a_qk * decay).astype(BF16)
    kd = kn * jnp.exp(glast - gcol)
    egl = jnp.exp(glast)
    for i, (j, h) in enumerate(pairs):
        uo_ref[0, rsl(j), h * DV:(h + 1) * DV] = u[i]
        wq_ref[0, j, h, 0:CHUNK, :] = w[i]
        wq_ref[0, j, h, CHUNK:2 * CHUNK, :] = qd[i]
        ak_ref[0, j, h, 0:CHUNK, :] = aqd[i]
        ak_ref[0, j, h, CHUNK:CHUNK + DK, :] = kd[i].T.astype(BF16)
        eg_ref[0, j, h:h + 1, :] = jnp.broadcast_to(egl[i], (1, LANES))
    for j in range(cp):
        eg_ref[0, j, H_D:SUBLANES, :] = jnp.zeros((SUBLANES - H_D, LANES), F32)


def _delta_rec_body(u_ref, wq_ref, ak_ref, eg_ref, z_ref, s0_ref, nw_ref, o_ref, sout_ref, S, *, bb, nc):
    c = pl.program_id(1)

    @pl.when(c == 0)
    def _():
        S[...] = s0_ref[...]

    pairs = [(b, h) for b in range(bb) for h in range(H_D)]
    n = len(pairs)
    cols = lambda h: slice(h * DV, (h + 1) * DV)
    s_all = S[...].reshape(n, DK, DV)
    x = jnp.einsum('nij,njk->nik', wq_ref[:, 0].reshape(n, 2 * CHUNK, DK), s_all.astype(BF16),
                   preferred_element_type=F32)
    u = jnp.stack([u_ref[b, :, cols(h)] for b, h in pairs])
    vn = u - x[:, 0:CHUNK]
    y = jnp.einsum('nij,njk->nik', ak_ref[:, 0].reshape(n, CHUNK + DK, CHUNK), vn.astype(BF16),
                   preferred_element_type=F32)
    o = x[:, CHUNK:2 * CHUNK] + y[:, 0:CHUNK]
    eg = jnp.stack([eg_ref[b, 0, h:h + 1, :] for b, h in pairs])
    S[...] = (s_all * eg + y[:, CHUNK:CHUNK + DK]).reshape(bb, H_D, DK, DV)
    od = o * lax.rsqrt(jnp.mean(o * o, axis=-1, keepdims=True) + NORM_EPS) * nw_ref[...]
    z = jnp.stack([z_ref[b, :, cols(h)] for b, h in pairs])
    out = od * (z * _sigmoid(z))
    for i, (b, h) in enumerate(pairs):
        o_ref[b, :, cols(h)] = out[i]

    @pl.when(c == nc - 1)
    def _():
        sout_ref[...] = S[...]


def _lane_row(vals, offset):
    row = jnp.zeros((1, LANES), F32)
    return row.at[0, offset:offset + vals.shape[0]].set(vals.astype(F32))


def _delta_net(dqkv, small, z, conv_buf, s0, conv_w, a_log, dt_bias, delta_norm_w):
    b, t, _ = dqkv.shape
    tp = -(-t // CHUNK) * CHUNK
    nc = tp // CHUNK
    assert tp == t or nc == 1
    if tp != t:
        pad = ((0, 0), (0, tp - t), (0, 0))
        dqkv, small, z = jnp.pad(dqkv, pad), jnp.pad(small, pad), jnp.pad(z, pad)
    cbuf = jnp.pad(conv_buf, ((0, 0), (HIST - (CONV_W - 1), 0), (0, 0)))
    cw = jnp.pad(conv_w, ((0, SUBLANES - CONV_W), (0, 0)))
    cp = next(c for c in (8, 4, 2, 1) if nc % c == 0)
    rows = cp * CHUNK
    full2 = lambda shape: pl.BlockSpec(shape, lambda bi, ci: (0,) * len(shape))
    u, wq, ak, eg = pl.pallas_call(
        functools.partial(_delta_prep_body, t_valid=t if nc == 1 else rows, cp=cp),
        grid=(b, nc // cp),
        in_specs=[pl.BlockSpec((1, rows, D_CONV), lambda bi, ci: (bi, ci, 0)),
                  pl.BlockSpec((1, HIST, D_CONV), lambda bi, ci: (bi, jnp.maximum(ci * (rows // HIST) - 1, 0), 0)),
                  pl.BlockSpec((1, rows, LANES), lambda bi, ci: (bi, ci, 0)),
                  pl.BlockSpec((1, HIST, D_CONV), lambda bi, ci: (bi, 0, 0)),
                  full2((SUBLANES, D_CONV)), full2((1, LANES)), full2((1, LANES))],
        out_specs=[pl.BlockSpec((1, rows, D_DV), lambda bi, ci: (bi, ci, 0)),
                   pl.BlockSpec((1, cp, H_D, 2 * CHUNK, DK), lambda bi, ci: (bi, ci, 0, 0, 0)),
                   pl.BlockSpec((1, cp, H_D, CHUNK + DK, CHUNK), lambda bi, ci: (bi, ci, 0, 0, 0)),
                   pl.BlockSpec((1, cp, SUBLANES, LANES), lambda bi, ci: (bi, ci, 0, 0))],
        out_shape=[jax.ShapeDtypeStruct((b, tp, D_DV), F32),
                   jax.ShapeDtypeStruct((b, nc, H_D, 2 * CHUNK, DK), BF16),
                   jax.ShapeDtypeStruct((b, nc, H_D, CHUNK + DK, CHUNK), BF16),
                   jax.ShapeDtypeStruct((b, nc, SUBLANES, LANES), F32)],
        scratch_shapes=[pltpu.VMEM((HIST + rows, D_CONV), F32)],
        compiler_params=pltpu.CompilerParams(dimension_semantics=("parallel", "parallel"),
                                             vmem_limit_bytes=VMEM_LIMIT),
        name="delta_prep",
    )(dqkv, dqkv, small, cbuf, cw, _lane_row(a_log, H_D), _lane_row(dt_bias, H_D))

    bb = next(c for c in (8, 4, 2, 1) if b % c == 0)
    o, s_new = pl.pallas_call(
        functools.partial(_delta_rec_body, bb=bb, nc=nc),
        grid=(b // bb, nc),
        in_specs=[pl.BlockSpec((bb, CHUNK, D_DV), lambda bi, ci: (bi, ci, 0)),
                  pl.BlockSpec((bb, 1, H_D, 2 * CHUNK, DK), lambda bi, ci: (bi, ci, 0, 0, 0)),
                  pl.BlockSpec((bb, 1, H_D, CHUNK + DK, CHUNK), lambda bi, ci: (bi, ci, 0, 0, 0)),
                  pl.BlockSpec((bb, 1, SUBLANES, LANES), lambda bi, ci: (bi, ci, 0, 0)),
                  pl.BlockSpec((bb, CHUNK, D_DV), lambda bi, ci: (bi, ci, 0)),
                  pl.BlockSpec((bb, H_D, DK, DV), lambda bi, ci: (bi, 0, 0, 0)),
                  full2((1, DV))],
        out_specs=[pl.BlockSpec((bb, CHUNK, D_DV), lambda bi, ci: (bi, ci, 0)),
                   pl.BlockSpec((bb, H_D, DK, DV), lambda bi, ci: (bi, 0, 0, 0))],
        out_shape=[jax.ShapeDtypeStruct((b, tp, D_DV), F32), jax.ShapeDtypeStruct((b, H_D, DK, DV), F32)],
        scratch_shapes=[pltpu.VMEM((bb, H_D, DK, DV), F32)],
        compiler_params=pltpu.CompilerParams(dimension_semantics=("parallel", "arbitrary"),
                                             vmem_limit_bytes=VMEM_LIMIT),
        name="delta_recurrence",
    )(u, wq, ak, eg, z, s0, delta_norm_w.reshape(1, DV).astype(F32))
    return o[:, :t].reshape(b * t, D_DV), s_new


def _lane_expand(cols, width):
    tm = cols[0].shape[0]
    seg = lax.broadcasted_iota(jnp.int32, (tm, len(cols) * width), 1) // width
    out = jnp.broadcast_to(cols[-1], seg.shape)
    for i in range(len(cols) - 2, -1, -1):
        out = jnp.where(seg == i, cols[i], out)
    return out


def _token_order(ref, scr):
    d, nc = ref.shape[1], ref.shape[3] // LANES
    if d == 1:
        return ref[0, 0]
    for r in range(d):
        for c in range(nc):
            scr[c, pl.ds(r, ref.shape[2], stride=d), :] = ref[0, r, :, c * LANES:(c + 1) * LANES]
    return jnp.concatenate([scr[c] for c in range(nc)], axis=-1)


def _mix_body(x_ref, gate_ref, o0_ref, o1_ref, o2_ref, l0_ref, l1_ref, l2_ref, od_ref,
              wa_ref, wd_ref, wo_ref, bg_ref, g1_ref, b1_ref, rw_ref, rb_ref,
              h_ref, ti_ref, tg_ref, cnt_ref, *scratch):
    o_scr, l_scr = scratch[:3], scratch[3:]
    lses = [_token_order(l, s) for l, s in zip((l0_ref, l1_ref, l2_ref), l_scr)]
    mx = jnp.maximum(jnp.maximum(lses[0], lses[1]), lses[2])
    es = [jnp.exp(l - mx) for l in lses]
    inv = 1.0 / (es[0] + es[1] + es[2])
    o_att = None
    for e_g, o_g, s in zip(es, (o0_ref, o1_ref, o2_ref), o_scr):
        wgt = e_g * inv
        term = _lane_expand([wgt[:, h:h + 1] for h in range(H_G)], HD) * _token_order(o_g, s)
        o_att = term if o_att is None else o_att + term
    ga = _sigmoid(gate_ref[:, :D_MODEL] + bg_ref[:, :D_MODEL])
    gd = _sigmoid(gate_ref[:, D_MODEL:] + bg_ref[:, D_MODEL:])
    merged = (ga * jnp.dot(o_att.astype(BF16), wa_ref[...], preferred_element_type=F32)
              + gd * jnp.dot(od_ref[...].astype(BF16), wd_ref[...], preferred_element_type=F32))
    mix = jnp.dot(merged.astype(BF16), wo_ref[...], preferred_element_type=F32)
    h = _layer_norm(DN_ALPHA * x_ref[...] + mix, g1_ref[...], b1_ref[...])
    _rows_to_tiles(h_ref, h)

    h_hi = h.astype(BF16)
    h_lo = (h - h_hi.astype(F32)).astype(BF16)
    part = jnp.dot(h_hi, rw_ref[...], preferred_element_type=F32)
    logits = (part[:, :LANES] + part[:, LANES:]
              + jnp.dot(h_lo, rw_ref[:, :LANES], preferred_element_type=F32) + rb_ref[...])
    lane = lax.broadcasted_iota(jnp.int32, logits.shape, 1)
    lane_f = lane.astype(F32)
    ti = jnp.zeros(logits.shape, F32)
    tv = jnp.zeros(logits.shape, F32)
    hits = jnp.zeros(logits.shape, F32)
    top = None
    for kk in range(TOP_K):
        m = jnp.max(logits, axis=-1, keepdims=True)
        idx = jnp.min(jnp.where(logits == m, lane_f, float(LANES)), axis=-1, keepdims=True)
        if top is None:
            top = m
        ti = jnp.where(lane == kk, idx, ti)
        tv = jnp.where(lane == kk, jnp.exp(m - top), tv)
        chosen = lane_f == idx
        hits = jnp.where(chosen, 1.0, hits)
        logits = jnp.where(chosen, -jnp.inf, logits)
    ti_ref[...] = ti.astype(jnp.int32)
    tg_ref[...] = tv / jnp.sum(tv, axis=-1, keepdims=True)

    @pl.when(pl.program_id(0) == 0)
    def _():
        cnt_ref[...] = jnp.zeros_like(cnt_ref)

    cnt_ref[...] += jnp.sum(hits, axis=0, keepdims=True)


def _mix_and_route(x2d, gates, o_groups, lse_groups, o_del, wa, wd, wo, b_gate, ln_g, ln_b, rw, rb, tm):
    n = x2d.shape[0]
    tpb = o_groups[0].shape[1] * o_groups[0].shape[2] // tm
    row = lambda c: pl.BlockSpec((tm, c), lambda i: (i, 0))
    full = lambda a: pl.BlockSpec(a.shape, lambda i: (0, 0))

    def res(a):
        d = a.shape[1]
        return pl.BlockSpec((1, d, tm // d, a.shape[3]), lambda i: (i // tpb, 0, i % tpb, 0))

    consts = (wa, wd, wo, b_gate, ln_g, ln_b, rw, rb)
    return pl.pallas_call(
        _mix_body,
        grid=(n // tm,),
        in_specs=[row(D_MODEL), row(2 * D_MODEL)] + [res(a) for a in o_groups] + [res(a) for a in lse_groups]
        + [row(D_DV)] + [full(a) for a in consts],
        out_specs=[pl.BlockSpec((tm * TOK_ROWS, LANES), lambda i: (i, 0)), row(LANES), row(LANES),
                   pl.BlockSpec((1, LANES), lambda i: (0, 0))],
        out_shape=[jax.ShapeDtypeStruct((n * TOK_ROWS, LANES), F32), jax.ShapeDtypeStruct((n, LANES), jnp.int32),
                   jax.ShapeDtypeStruct((n, LANES), F32), jax.ShapeDtypeStruct((1, LANES), F32)],
        scratch_shapes=[pltpu.VMEM((GW // LANES, tm, LANES), F32)] * len(o_groups)
        + [pltpu.VMEM((1, tm, LANES), F32)] * len(lse_groups),
        compiler_params=pltpu.CompilerParams(dimension_semantics=("arbitrary",), vmem_limit_bytes=VMEM_LIMIT),
        name="mix_and_route",
    )(x2d, gates, *o_groups, *lse_groups, o_del, *consts)


def _rank_body(ti_ref, base_ref, dest_ref, carry):
    i = pl.program_id(0)

    @pl.when(i == 0)
    def _():
        carry[...] = base_ref[...]

    ti = ti_ref[...]
    tm = ti.shape[0]
    lane = lax.broadcasted_iota(jnp.int32, ti.shape, 1)
    sel = [lane == ti[:, kk:kk + 1] for kk in range(TOP_K)]
    hit = sel[0]
    for kk in range(1, TOP_K):
        hit = hit | sel[kk]
    cnt = hit.astype(F32)
    ri = lax.broadcasted_iota(jnp.int32, (tm, tm), 0)
    ci = lax.broadcasted_iota(jnp.int32, (tm, tm), 1)
    incl = jnp.dot((ri >= ci).astype(BF16), cnt.astype(BF16), preferred_element_type=F32)
    before = incl - cnt + carry[...]
    slot = jnp.zeros(ti.shape, F32)
    for kk in range(TOP_K):
        r = jnp.sum(jnp.where(sel[kk], before, 0.0), axis=-1, keepdims=True)
        slot = jnp.where(lane == kk, r, slot)
    dest_ref[...] = slot.T[0:SUBLANES].astype(jnp.int32)
    carry[...] = carry[...] + incl[tm - 1:tm, :]


def _route_slots(top_i, base, tm):
    n = top_i.shape[0]
    return pl.pallas_call(
        _rank_body,
        grid=(n // tm,),
        in_specs=[pl.BlockSpec((tm, LANES), lambda i: (i, 0)), pl.BlockSpec((1, LANES), lambda i: (0, 0))],
        out_specs=pl.BlockSpec((SUBLANES, tm), lambda i: (i, 0)),
        out_shape=jax.ShapeDtypeStruct((n // tm * SUBLANES, tm), jnp.int32),
        scratch_shapes=[pltpu.VMEM((1, LANES), F32)],
        compiler_params=pltpu.CompilerParams(dimension_semantics=("arbitrary",), vmem_limit_bytes=VMEM_LIMIT),
        name="route_slots",
    )(top_i, base)


def _rows_to_tiles(ref, val):
    m = val.shape[0]
    for s in range(TOK_ROWS):
        ref[pl.ds(s, m, stride=TOK_ROWS), :] = val[:, s * LANES:(s + 1) * LANES]


def _tiles_to_rows(ref, m):
    return jnp.concatenate([ref[pl.ds(s, m, stride=TOK_ROWS), :] for s in range(TOK_ROWS)], axis=-1)


def _dispatch_body(pend_ref, padded_ref, dest_ref, h_ref, *refs, max_tail):
    tm = h_ref.shape[0] // TOK_ROWS
    blk = MOE_BM * TOK_ROWS
    first_group = len(refs) == 4
    if first_group:
        xs_out, zero, sem, zsem = refs
    else:
        _, xs_out, sem = refs

    def zero_fill():
        zero[...] = jnp.zeros_like(zero)

        def last_block(e):
            start = pl.multiple_of((pend_ref[e] - MOE_BM) * TOK_ROWS, blk)
            return pltpu.make_async_copy(zero, xs_out.at[pl.ds(start, blk)], zsem)

        def tail_block(j):
            start = pl.multiple_of((pend_ref[N_EXP - 1] + j * MOE_BM) * TOK_ROWS, blk)
            return pltpu.make_async_copy(zero, xs_out.at[pl.ds(start, blk)], zsem)

        n_slot = xs_out.shape[0] // TOK_ROWS
        for e in range(N_EXP):
            @pl.when(padded_ref[e] > 0)
            def _():
                last_block(e).start()
        for j in range(max_tail):
            @pl.when(pend_ref[N_EXP - 1] + j * MOE_BM < n_slot)
            def _():
                tail_block(j).start()
        for e in range(N_EXP):
            @pl.when(padded_ref[e] > 0)
            def _():
                last_block(e).wait()
        for j in range(max_tail):
            @pl.when(pend_ref[N_EXP - 1] + j * MOE_BM < n_slot)
            def _():
                tail_block(j).wait()

    if first_group:
        pl.when(pl.program_id(0) == 0)(zero_fill)

    def issue(t, carry):
        src = h_ref.at[pl.ds(pl.multiple_of(t * TOK_ROWS, TOK_ROWS), TOK_ROWS)]
        for kk in range(TOP_K):
            d = pl.multiple_of(dest_ref[kk * tm + t] * TOK_ROWS, TOK_ROWS)
            pltpu.make_async_copy(src, xs_out.at[pl.ds(d, TOK_ROWS)], sem).start(priority=kk % 2)
        return carry

    lax.fori_loop(0, tm, issue, 0, unroll=8)
    for kk in range(TOP_K):
        pltpu.make_async_copy(h_ref, xs_out.at[pl.ds(0, tm * TOK_ROWS)], sem).wait()


def _dispatch(h_tiles, dest_flat, pad_end, padded, n_slot, tm, xs_prev=None):
    n = h_tiles.shape[0] // TOK_ROWS
    first_group = xs_prev is None
    in_specs = [pl.BlockSpec((SUBLANES * tm,), lambda i, pe, pd: (i,), memory_space=pltpu.SMEM),
                pl.BlockSpec((tm * TOK_ROWS, LANES), lambda i, pe, pd: (i, 0))]
    scratch = [pltpu.SemaphoreType.DMA(())]
    if first_group:
        scratch = [pltpu.VMEM((MOE_BM * TOK_ROWS, LANES), F32)] + scratch + [pltpu.SemaphoreType.DMA(())]
    else:
        in_specs.append(pl.BlockSpec(memory_space=pl.ANY))
    grid_spec = pltpu.PrefetchScalarGridSpec(
        num_scalar_prefetch=2, grid=(n // tm,), in_specs=in_specs,
        out_specs=pl.BlockSpec(memory_space=pl.ANY), scratch_shapes=scratch)
    max_tail = n_slot // MOE_BM - (n * TOP_K) // MOE_BM
    return pl.pallas_call(
        functools.partial(_dispatch_body, max_tail=max_tail),
        grid_spec=grid_spec,
        out_shape=jax.ShapeDtypeStruct((n_slot * TOK_ROWS, LANES), F32),
        input_output_aliases={} if first_group else {4: 0},
        compiler_params=pltpu.CompilerParams(dimension_semantics=("arbitrary",), vmem_limit_bytes=VMEM_LIMIT,
                                             has_side_effects=True, disable_bounds_checks=True),
        name="moe_dispatch",
    )(pad_end, padded, dest_flat, h_tiles, *(() if first_group else (xs_prev,)))


def _expert_body(be_ref, nu_ref, nxt_ref, par_ref, x_ref, bgu_ref, bd_ref, wgu_hbm, wd_hbm, o_ref,
                 wgu_f32, wd_f32, wgu_bf, wd_bf, sem):
    i = pl.program_id(0)
    e = be_ref[i]
    prev = be_ref[jnp.maximum(i - 1, 0)]
    live = i < nu_ref[0]

    def fetch(expert, slot):
        return (pltpu.make_async_copy(wgu_hbm.at[expert], wgu_f32.at[slot], sem.at[0, slot]),
                pltpu.make_async_copy(wd_hbm.at[expert], wd_f32.at[slot], sem.at[1, slot]))

    @pl.when(live & ((i == 0) | (e != prev)))
    def _():
        slot = par_ref[i]

        @pl.when(i == 0)
        def _():
            for cp in fetch(e, slot):
                cp.start()

        for cp in fetch(e, slot):
            cp.wait()
        wgu_bf[...] = wgu_f32[slot].astype(BF16)
        wd_bf[...] = wd_f32[slot].astype(BF16)

        @pl.when(nxt_ref[i] >= 0)
        def _():
            for cp in fetch(nxt_ref[i], 1 - slot):
                cp.start()

    @pl.when(live)
    def _():
        x = _tiles_to_rows(x_ref, MOE_BM)
        gu = jnp.dot(x.astype(BF16), wgu_bf[...], preferred_element_type=F32) + bgu_ref[0]
        gt = jnp.minimum(gu[:, :D_FF], SWIGLU_LIMIT)
        up = jnp.clip(gu[:, D_FF:], -SWIGLU_LIMIT, SWIGLU_LIMIT)
        act = (up + 1.0) * gt * _sigmoid(SWIGLU_ALPHA * gt)
        _rows_to_tiles(o_ref, jnp.dot(act.astype(BF16), wd_bf[...], preferred_element_type=F32) + bd_ref[0])

    @pl.when(i >= nu_ref[0])
    def _():
        o_ref[...] = jnp.zeros_like(o_ref)


def _experts(xs, blk_exp, n_used, padded, w_gu, b_gu, w_down, b_down):
    n_slot = xs.shape[0] // TOK_ROWS
    n_blk = n_slot // MOE_BM
    ids = jnp.arange(N_EXP, dtype=jnp.int32)
    has = padded > 0
    later = jnp.where(has[None, :] & (ids[None, :] > ids[:, None]), ids[None, :], N_EXP)
    nxt_e = jnp.min(later, axis=1)
    nxt_e = jnp.where(nxt_e < N_EXP, nxt_e, -1).astype(jnp.int32)
    par_e = ((jnp.cumsum(has.astype(jnp.int32)) - 1) % 2).astype(jnp.int32)
    rows = lambda i, be, nu, nx, pa: (jnp.minimum(i, nu[0] - 1), 0)
    grid_spec = pltpu.PrefetchScalarGridSpec(
        num_scalar_prefetch=4,
        grid=(n_blk,),
        in_specs=[pl.BlockSpec((MOE_BM * TOK_ROWS, LANES), rows),
                  pl.BlockSpec((1, 1, 2 * D_FF), lambda i, be, nu, nx, pa: (be[i], 0, 0)),
                  pl.BlockSpec((1, 1, D_MODEL), lambda i, be, nu, nx, pa: (be[i], 0, 0)),
                  pl.BlockSpec(memory_space=pl.ANY), pl.BlockSpec(memory_space=pl.ANY)],
        out_specs=pl.BlockSpec((MOE_BM * TOK_ROWS, LANES), lambda i, be, nu, nx, pa: (i, 0)),
        scratch_shapes=[pltpu.VMEM((2, D_MODEL, 2 * D_FF), F32), pltpu.VMEM((2, D_FF, D_MODEL), F32),
                        pltpu.VMEM((D_MODEL, 2 * D_FF), BF16), pltpu.VMEM((D_FF, D_MODEL), BF16),
                        pltpu.SemaphoreType.DMA((2, 2))],
    )
    return pl.pallas_call(
        _expert_body,
        grid_spec=grid_spec,
        out_shape=jax.ShapeDtypeStruct((n_slot * TOK_ROWS, LANES), F32),
        compiler_params=pltpu.CompilerParams(dimension_semantics=("arbitrary",), vmem_limit_bytes=VMEM_LIMIT),
        name="moe_experts",
    )(blk_exp, n_used, nxt_e[blk_exp], par_e[blk_exp], xs, b_gu.reshape(N_EXP, 1, 2 * D_FF),
      b_down.reshape(N_EXP, 1, D_MODEL), w_gu, w_down)


def _combine_body(dest_ref, dest_next_ref, gate_ref, h_ref, g_ref, b_ref, ys_ref, y_ref, buf, sem):
    tm = h_ref.shape[0] // TOK_ROWS
    i = pl.program_id(0)
    slot = i % 2

    def start_tile(idx_ref, s):
        def issue(t, carry):
            row = pl.multiple_of(t * TOK_ROWS, TOK_ROWS)
            for kk in range(TOP_K):
                d = pl.multiple_of(idx_ref[kk * tm + t] * TOK_ROWS, TOK_ROWS)
                pltpu.make_async_copy(ys_ref.at[pl.ds(d, TOK_ROWS)], buf.at[s, kk, pl.ds(row, TOK_ROWS)],
                                      sem.at[s]).start(priority=kk % 2)
            return carry

        lax.fori_loop(0, tm, issue, 0, unroll=8)

    @pl.when(i == 0)
    def _():
        start_tile(dest_ref, 0)

    @pl.when(i + 1 < pl.num_programs(0))
    def _():
        start_tile(dest_next_ref, 1 - slot)

    for kk in range(TOP_K):
        pltpu.make_async_copy(ys_ref.at[pl.ds(0, tm * TOK_ROWS)], buf.at[slot, kk], sem.at[slot]).wait()
    gate = gate_ref[...]
    moe = gate[:, 0:1] * _tiles_to_rows(buf.at[slot, 0], tm)
    for kk in range(1, TOP_K):
        moe = moe + gate[:, kk:kk + 1] * _tiles_to_rows(buf.at[slot, kk], tm)
    y_ref[...] = _layer_norm(DN_ALPHA * _tiles_to_rows(h_ref, tm) + moe, g_ref[...], b_ref[...])


def _combine(ys, dest_flat, gate, h_tiles, ln_g, ln_b, tm):
    n = h_tiles.shape[0] // TOK_ROWS
    return pl.pallas_call(
        _combine_body,
        grid=(n // tm,),
        in_specs=[pl.BlockSpec((SUBLANES * tm,), lambda i: (i,), memory_space=pltpu.SMEM),
                  pl.BlockSpec((SUBLANES * tm,), lambda i: (jnp.minimum(i + 1, n // tm - 1),), memory_space=pltpu.SMEM),
                  pl.BlockSpec((tm, LANES), lambda i: (i, 0)),
                  pl.BlockSpec((tm * TOK_ROWS, LANES), lambda i: (i, 0)),
                  pl.BlockSpec((1, D_MODEL), lambda i: (0, 0)),
                  pl.BlockSpec((1, D_MODEL), lambda i: (0, 0)),
                  pl.BlockSpec(memory_space=pl.ANY)],
        out_specs=pl.BlockSpec((tm, D_MODEL), lambda i: (i, 0)),
        out_shape=jax.ShapeDtypeStruct((n, D_MODEL), F32),
        scratch_shapes=[pltpu.VMEM((2, TOP_K, tm * TOK_ROWS, LANES), F32), pltpu.SemaphoreType.DMA((2,))],
        compiler_params=pltpu.CompilerParams(dimension_semantics=("arbitrary",), vmem_limit_bytes=VMEM_LIMIT,
                                             disable_bounds_checks=True),
        name="moe_combine",
    )(dest_flat, dest_flat, gate, h_tiles, ln_g, ln_b, ys)


def _moe(groups, w_gu, b_gu, w_down, b_down, ln_g, ln_b):
    counts = [g[3][0, :N_EXP].astype(jnp.int32) for g in groups]
    total = sum(counts)
    padded = (total + MOE_BM - 1) // MOE_BM * MOE_BM
    pad_end = jnp.cumsum(padded).astype(jnp.int32)
    n_assign = sum(g[1].shape[0] for g in groups) * TOP_K
    n_blk = -(-(n_assign + N_EXP * (MOE_BM - 1)) // MOE_BM)
    blk_row0 = jnp.arange(n_blk, dtype=jnp.int32) * MOE_BM
    blk_exp = jnp.minimum(jnp.sum((pad_end[None, :] <= blk_row0[:, None]).astype(jnp.int32), axis=1), N_EXP - 1)
    n_used = pad_end[-1:] // MOE_BM
    base = pad_end - padded
    xs, dests = None, []
    for (h, top_i, _, _, tm), cnt in zip(groups, counts):
        base_row = jnp.zeros((1, LANES), F32).at[0, :N_EXP].set(base.astype(F32))
        dest = _route_slots(top_i, base_row, tm).reshape(-1)
        dests.append(dest)
        xs = _dispatch(h, dest, pad_end, padded, n_blk * MOE_BM, tm, xs)
        base = base + cnt
    ys = _experts(xs, blk_exp, n_used, padded, w_gu, b_gu, w_down, b_down)
    return [_combine(ys, dest, top_g, h, ln_g, ln_b, tm) for (h, _, top_g, _, tm), dest in zip(groups, dests)]


def _pad_cols(a, width, fill=0.0):
    return jnp.pad(a, ((0, 0), (0, width - a.shape[1])), constant_values=fill)


def _token_mixers_and_route(x, caches, conv_buf, s0, w_in, b_gate, conv_w, a_log, dt_bias, delta_norm_w,
                            w_branch_attn, w_branch_delta, w_out, ln1_g, ln1_b, router_w, router_b):
    b, t, _ = x.shape
    n = b * t
    tm = min(TOK_TM, n)
    x2d = x.reshape(n, D_MODEL)

    c_att, c_dz, c_small = 3 * D_ATT, 3 * D_ATT + D_CONV, 3 * D_ATT + D_CONV + D_DV
    c_gate = c_small + 2 * H_D
    ng = len(ATT_DILS)
    w_groups = [jnp.concatenate([w_in[:, part * D_ATT + gi * GW:part * D_ATT + (gi + 1) * GW] for part in range(3)],
                                axis=1) for gi in range(ng)]
    ws = w_groups + [w_in[:, c_att:c_dz], w_in[:, c_dz:c_small],
                     _pad_cols(w_in[:, c_small:c_gate], LANES), w_in[:, c_gate:]]
    dils = (ATT_DILS if caches is None else (1,) * ng) + (1, 1, 1, 1)
    outs = _in_projection(x2d, [w.astype(BF16) for w in ws], dils, t, min(IN_TM, n))
    att_groups, (dqkv, z, small, gates) = outs[:ng], outs[ng:]
    dqkv3 = dqkv.reshape(b, t, D_CONV)

    kv_new = []
    if caches is None:
        o_groups, lse_groups = [], []
        for dil, a, w_g in zip(ATT_DILS, att_groups, w_groups):
            a = a.reshape(b, dil, t // dil, 3 * GW)
            o_g, lse_g = _attn_prompt(a, dil)
            o_groups.append(o_g)
            lse_groups.append(lse_g)
            win = min(dil * STEPS, t)
            last = _tail_projection(x2d, w_g[:, GW:].astype(BF16), b, t, win)
            kv_new.append(last.reshape(b, win, 2, H_G, HD))
        conv_buf = jnp.zeros((b, CONV_W - 1, D_CONV), F32)
        s0 = jnp.zeros((b, H_D, DK, DV), F32)
    else:
        att3 = jnp.concatenate(att_groups, axis=1).reshape(b, t, 3 * D_ATT)
        o_all, lse_all = _attn_sample(att3, caches)
        o_groups = [o_all[:, gi * GW:(gi + 1) * GW].reshape(1, 1, n, GW) for gi in range(ng)]
        lse_groups = [lse_all[:, gi * LANES:(gi + 1) * LANES].reshape(1, 1, n, LANES) for gi in range(ng)]
        for a in att_groups:
            kv_new.append(a[:, GW:].reshape(b, t, 2, H_G, HD))

    o_del, s_new = _delta_net(dqkv3, small.reshape(b, t, LANES), z.reshape(b, t, D_DV), conv_buf, s0,
                              conv_w, a_log, dt_bias, delta_norm_w)
    conv_new = jnp.concatenate([conv_buf, dqkv3], axis=1)[:, -(CONV_W - 1):] if t < CONV_W - 1 \
        else dqkv3[:, t - (CONV_W - 1):]

    rw = _pad_cols(router_w, LANES)
    rw_hi = rw.astype(BF16)
    rw = jnp.concatenate([rw_hi, (rw - rw_hi.astype(F32)).astype(BF16)], axis=1)
    rb = _pad_cols(router_b.reshape(1, N_EXP), LANES, fill=NEG)
    h, top_i, top_g, counts = _mix_and_route(
        x2d, gates, o_groups, lse_groups, o_del,
        w_branch_attn.astype(BF16), w_branch_delta.astype(BF16), w_out.astype(BF16),
        b_gate.reshape(1, 2 * D_MODEL), ln1_g.reshape(1, D_MODEL), ln1_b.reshape(1, D_MODEL), rw, rb, tm)
    return (h, top_i, top_g, counts, tm), kv_new, conv_new, s_new


def kernel(x_prompt, x_sample, cache_kv_w128, cache_kv_w512, cache_kv_w2048, state_conv, state_delta,
           w_in, b_gate, conv_w, a_log, dt_bias, delta_norm_w, w_branch_attn, w_branch_delta, w_out,
           ln1_g, ln1_b, router_w, router_b, w_gu, b_gu, w_down, b_down, ln2_g, ln2_b):
    depth = w_in.shape[0]
    assert depth == 1
    l = 0
    lw = (w_in[l], b_gate[l], conv_w[l], a_log[l], dt_bias[l], delta_norm_w[l], w_branch_attn[l],
          w_branch_delta[l], w_out[l], ln1_g[l], ln1_b[l], router_w[l], router_b[l], w_gu[l], b_gu[l],
          w_down[l], b_down[l], ln2_g[l], ln2_b[l])
    mixers = lw[:13]
    routed_p, kv_p, cv_p, s_p = _token_mixers_and_route(x_prompt, None, None, None, *mixers)
    routed_s, kv_s, cv_s, s_s = _token_mixers_and_route(
        x_sample, (cache_kv_w128[l], cache_kv_w512[l], cache_kv_w2048[l]), state_conv[l], state_delta[l], *mixers)
    w_gu_l, b_gu_l, w_down_l, b_down_l, ln2_g_l, ln2_b_l = lw[13:]
    yp, ys = _moe([routed_p, routed_s], w_gu_l, b_gu_l, w_down_l, b_down_l,
                  ln2_g_l.reshape(1, D_MODEL), ln2_b_l.reshape(1, D_MODEL))
    yp, ys = yp.reshape(x_prompt.shape), ys.reshape(x_sample.shape)
    stk = lambda a: a[None]
    return (yp, ys, stk(kv_p[0]), stk(kv_p[1]), stk(kv_p[2]), stk(cv_p), stk(s_p),
            stk(kv_s[0]), stk(kv_s[1]), stk(kv_s[2]), stk(cv_s), stk(s_s))
```
